```python
import math
import jax
import jax.numpy as jnp
from jax import lax
import numpy as np

D_MODEL = 2048
BATCH = 1
SEQ = 16384
DEPTH = 2

N_META = 16
ATTN_HEADS = 8
ATTN_KV_HEADS = 2
ATTN_GROUP = ATTN_HEADS // ATTN_KV_HEADS
HEAD_DIM = 128
WINDOW = 128
BLOCK = 128
ROPE_THETA = 10000.0
Q_DIM = ATTN_HEADS * HEAD_DIM
KV_DIM = ATTN_KV_HEADS * HEAD_DIM
S5_WIDTH = D_MODEL // 2
S5_GROUP = 16
S5_GROUPS = S5_WIDTH // S5_GROUP
S5_STATE = 64
AB_IN_DIM = Q_DIM + 2 * KV_DIM + S5_WIDTH
AB_OUT_DIM = Q_DIM + S5_WIDTH
HG_HEADS = 16
HG_DK = D_MODEL // HG_HEADS
HG_DV = D_MODEL // HG_HEADS
HG_CHUNK = 64
C_IN_DIM = 5 * D_MODEL
N_GROUPS = 8
EXPERTS_PER_GROUP = 8
N_EXPERTS = N_GROUPS * EXPERTS_PER_GROUP
TOP_K = 2
D_EXPERT = 512
MOE_BLOCK = 128
N_AB_LAYERS = (DEPTH + 1) // 2
N_C_LAYERS = DEPTH // 2
DN_ALPHA = (2.0 * DEPTH) ** 0.25
DN_BETA = (8.0 * DEPTH) ** -0.25
LN_EPS = 1e-5
RMS_EPS = 1e-6
NEG_INF = -1e30

kernel_name = 'hybrid_swa_s5_hgrn2_hmoe_encoder'


def layer_norm(x, g, b):
    xf = x.astype(jnp.float32)
    mu = jnp.mean(xf, axis=-1, keepdims=True)
    var = jnp.mean(jnp.square(xf - mu), axis=-1, keepdims=True)
    return ((xf - mu) * lax.rsqrt(var + LN_EPS) * g + b).astype(x.dtype)


def rope(x, pos):
    half = HEAD_DIM // 2
    inv = ROPE_THETA ** (-jnp.arange(half, dtype=jnp.float32) * 2.0 / HEAD_DIM)
    ang = pos.astype(jnp.float32)[:, None] * inv[None, :]
    cos = jnp.cos(ang)[None, :, None, :]
    sin = jnp.sin(ang)[None, :, None, :]
    xf = x.astype(jnp.float32)
    x1, x2 = xf[..., :half], xf[..., half:]
    return jnp.concatenate([x1 * cos - x2 * sin, x2 * cos + x1 * sin], axis=-1).astype(x.dtype)


def to_padded(t, front):
    pad = jnp.zeros((t.shape[0], front - N_META) + t.shape[2:], t.dtype)
    return jnp.concatenate([t[:, :N_META], pad, t[:, N_META:]], axis=1)


def from_padded(t, front):
    return jnp.concatenate([t[:, :N_META], t[:, front:]], axis=1)


def window_attention(q, k, v, sinks):
    B = q.shape[0]
    qp, kp, vp = to_padded(q, BLOCK), to_padded(k, BLOCK), to_padded(v, BLOCK)
    lp = qp.shape[1]
    nb = lp // BLOCK
    slot = jnp.arange(lp)
    lpos = jnp.where(slot < N_META, slot, slot - (BLOCK - N_META))
    is_real = slot >= BLOCK

    def band(t, axis):
        widths = [(0, 0)] * t.ndim
        widths[axis] = (BLOCK, BLOCK)
        tb = jnp.pad(t, widths).reshape(t.shape[:axis] + (nb + 2, BLOCK) + t.shape[axis + 1:])
        parts = [lax.slice_in_dim(tb, s, s + nb, axis=axis) for s in range(3)]
        return jnp.concatenate(parts, axis=axis + 1)

    kn, vn = band(kp, 1), band(vp, 1)
    kpos, kreal = band(lpos, 0), band(is_real, 0)
    qpos = lpos.reshape(nb, BLOCK)
    visible = kreal[:, None, :] & (jnp.abs(qpos[:, :, None] - kpos[:, None, :]) <= WINDOW)

    qb = qp.reshape(B, nb, BLOCK, ATTN_KV_HEADS, ATTN_GROUP, HEAD_DIM)
    km, vm = k[:, :N_META], v[:, :N_META]
    scale = HEAD_DIM ** -0.5
    s_meta = jnp.einsum('bnqgrd,bmgd->bngrqm', qb, km).astype(jnp.float32) * scale
    s_band = jnp.einsum('bnqgrd,bnkgd->bngrqk', qb, kn).astype(jnp.float32) * scale
    s_band = jnp.where(visible[None, :, None, None], s_band, NEG_INF)
    sink = jnp.broadcast_to(sinks.astype(jnp.float32).reshape(1, 1, ATTN_KV_HEADS, ATTN_GROUP, 1, 1),
                            s_meta.shape[:-1] + (1,))
    p = jax.nn.softmax(jnp.concatenate([s_meta, s_band, sink], axis=-1), axis=-1)
    p_meta = p[..., :N_META].astype(v.dtype)
    p_band = p[..., N_META:N_META + 3 * BLOCK].astype(v.dtype)
    o = (jnp.einsum('bngrqm,bmgd->bnqgrd', p_meta, vm)
         + jnp.einsum('bngrqk,bnkgd->bnqgrd', p_band, vn))
    return from_padded(o.reshape(B, lp, Q_DIM), BLOCK)


def _linear_recurrence_op(e1, e2):
    a1, b1 = e1
    a2, b2 = e2
    return a1 * a2, a2 * b1 + b2


def s5_mixer(u, lam_re, lam_im, log_step, b_re, b_im, c_re, c_im, d_skip, w_glu, b_glu):
    f32 = jnp.float32
    B, L, _ = u.shape
    uf = u.astype(f32)
    ug = uf.reshape(B, L, S5_GROUPS, S5_GROUP)
    y = uf * d_skip.astype(f32)
    for direction in range(2):
        lam = lax.complex(jnp.minimum(lam_re[direction].astype(f32), -1e-4), lam_im[direction].astype(f32))
        step = jnp.exp(log_step[direction].astype(f32))[:, None]
        a_bar = jnp.exp(lam * step)
        b_bar = ((a_bar - 1.0) / lam)[..., None] * lax.complex(b_re[direction].astype(f32),
                                                                  b_im[direction].astype(f32))
        bu = lax.complex(jnp.einsum('blgc,gpc->blgp', ug, b_bar.real),
                         jnp.einsum('blgc,gpc->blgp', ug, b_bar.imag))
        _, xs = lax.associative_scan(_linear_recurrence_op, (jnp.broadcast_to(a_bar, bu.shape), bu),
                                     reverse=(direction == 1), axis=1)
        y_dir = (jnp.einsum('blgp,gcp->blgc', xs.real, c_re[direction].astype(f32))
                 - jnp.einsum('blgp,gcp->blgc', xs.imag, c_im[direction].astype(f32)))
        y = y + y_dir.reshape(B, L, S5_WIDTH)
    z = jax.nn.gelu(y)
    out = z * jax.nn.sigmoid(z @ w_glu.astype(f32) + b_glu.astype(f32))
    return out.astype(u.dtype)


def chunk_recurrence(q, k, v, g):
    B, lp = q.shape[:2]
    n = lp // HG_CHUNK

    def to_chunks(t):
        return t.reshape(B, n, HG_CHUNK, HG_HEADS, t.shape[-1]).transpose(1, 0, 3, 2, 4)

    tri = jnp.tril(jnp.ones((HG_CHUNK, HG_CHUNK), dtype=bool))[:, :, None]

    def step(state, chunk):
        qc, kc, vc, gc = chunk
        b = jnp.cumsum(gc, axis=2)
        rel = jnp.where(tri, b[:, :, :, None, :] - b[:, :, None, :, :], -jnp.inf)
        scores = jnp.sum(qc[:, :, :, None, :] * kc[:, :, None, :, :] * jnp.exp(rel), axis=-1)
        out = (jnp.einsum('bhts,bhsv->bhtv', scores, vc)
               + jnp.einsum('bhtk,bhkv->bhtv', qc * jnp.exp(b), state))
        b_end = b[:, :, -1:, :]
        state = (jnp.exp(b_end[:, :, 0, :, None]) * state
                 + jnp.einsum('bhsk,bhsv->bhkv', kc * jnp.exp(b_end - b), vc))
        return state, out

    state0 = jnp.zeros((B, HG_HEADS, HG_DK, HG_DV), q.dtype)
    _, out = lax.scan(step, state0, (to_chunks(q), to_chunks(k), to_chunks(v), to_chunks(g)))
    return out.transpose(1, 0, 3, 2, 4).reshape(B, lp, HG_HEADS, HG_DV)


def hgrn2_mixer(q, i_in, f_fw, f_bw, gate, lower_bound, norm_g):
    f32 = jnp.float32
    B, L, _ = q.shape
    lb = lower_bound.astype(f32).reshape(HG_HEADS, HG_DK)

    def heads(t):
        return t.astype(f32).reshape(B, L, HG_HEADS, -1)

    def forget(fz):
        f = lb + (1.0 - lb) * jax.nn.sigmoid(heads(fz))
        return jnp.log(f), 1.0 - f

    qh, vh = heads(q), heads(i_in)
    g1, k1 = forget(f_fw)
    g2, k2 = forget(f_bw)
    pad = lambda t: to_padded(t, HG_CHUNK)
    rev = lambda t: jnp.flip(t, axis=1)
    qp, vp = pad(qh), pad(vh)
    o_fw = chunk_recurrence(qp, pad(k1), vp, pad(g1))
    o_bw = rev(chunk_recurrence(rev(qp), rev(pad(k2)), rev(vp), rev(pad(g2))))
    o = from_padded(o_fw + o_bw, HG_CHUNK)
    o = o * lax.rsqrt(jnp.mean(o * o, axis=-1, keepdims=True) + RMS_EPS)
    o = o.reshape(B, L, HG_HEADS * HG_DV) * norm_g.astype(f32)
    return (o * jax.nn.sigmoid(gate.astype(f32))).astype(q.dtype)


def hier_moe(x, w_group, b_group, w_expert, b_expert, w_gate_up, w_down):
    f32 = jnp.float32
    B, L, D = x.shape
    T = B * L
    xt = x.reshape(T, D)
    xf = xt.astype(f32)
    g_logits = xf @ w_group.astype(f32) + b_group.astype(f32)
    grp = jnp.argmax(g_logits, axis=-1)
    p_grp = jnp.take_along_axis(jax.nn.softmax(g_logits, axis=-1), grp[:, None], axis=-1)
    e_logits = (xf @ w_expert.astype(f32) + b_expert.astype(f32)).reshape(T, N_GROUPS, EXPERTS_PER_GROUP)
    e_logits = jnp.take_along_axis(e_logits, grp[:, None, None], axis=1)[:, 0]
    top_v, top_i = lax.top_k(e_logits, TOP_K)
    weights = p_grp * jax.nn.softmax(top_v, axis=-1)
    expert = grp[:, None] * EXPERTS_PER_GROUP + top_i

    M = T * TOP_K
    flat_e = expert.reshape(M)
    flat_tok = jnp.repeat(jnp.arange(T), TOP_K)
    flat_w = weights.reshape(M)
    order = jnp.argsort(flat_e)
    e_sorted, tok_sorted, w_sorted = flat_e[order], flat_tok[order], flat_w[order]
    counts = jnp.bincount(flat_e, length=N_EXPERTS)
    padded = (counts + MOE_BLOCK - 1) // MOE_BLOCK * MOE_BLOCK
    start = jnp.cumsum(counts) - counts
    pend = jnp.cumsum(padded)
    pstart = pend - padded
    dest = pstart[e_sorted] + (jnp.arange(M) - start[e_sorted])
    n_blocks = -(-(M + N_EXPERTS * (MOE_BLOCK - 1)) // MOE_BLOCK)
    rows = n_blocks * MOE_BLOCK
    buf = jnp.zeros((rows, D), xt.dtype).at[dest].set(xt[tok_sorted])
    block_expert = jnp.minimum(jnp.searchsorted(pend, jnp.arange(n_blocks) * MOE_BLOCK, side='right'),
                               N_EXPERTS - 1)

    def expert_block(args):
        xb, e = args
        h = xb @ w_gate_up[e]
        return (jax.nn.silu(h[:, :D_EXPERT]) * h[:, D_EXPERT:]) @ w_down[e]

    ybuf = lax.map(expert_block, (buf.reshape(n_blocks, MOE_BLOCK, D), block_expert)).reshape(rows, D)
    y = ybuf[dest].astype(f32) * w_sorted[:, None]
    out = jnp.zeros((T, D), f32).at[tok_sorted].add(y)
    return out.reshape(B, L, D).astype(x.dtype)


def setup_inputs(seed: int = 0) -> dict:
    key = jax.random.key(seed)
    keys = iter(jax.random.split(key, 40))
    f32 = jnp.float32

    def nrm(shape, scale):
        return jax.random.normal(next(keys), shape, f32) * scale

    n_state = jnp.arange(S5_STATE, dtype=f32)
    s5_shape = (N_AB_LAYERS, 2, S5_GROUPS, S5_STATE)
    return {
        'x': nrm((BATCH, SEQ, D_MODEL), 1.0),
        'meta_tokens': nrm((N_META, D_MODEL), 1.0),
        'w_in_ab': nrm((N_AB_LAYERS, D_MODEL, AB_IN_DIM), D_MODEL ** -0.5),
        'w_out_ab': nrm((N_AB_LAYERS, AB_OUT_DIM, D_MODEL), AB_OUT_DIM ** -0.5 * DN_BETA),
        'attn_sinks': nrm((N_AB_LAYERS, ATTN_HEADS), 1.0),
        's5_lam_re': -0.5 + nrm(s5_shape, 0.01),
        's5_lam_im': math.pi * n_state + nrm(s5_shape, 0.01),
        's5_log_step': jax.random.uniform(next(keys), (N_AB_LAYERS, 2, S5_GROUPS), f32,
                                          math.log(1e-3), math.log(1e-1)),
        's5_b_re': nrm((N_AB_LAYERS, 2, S5_GROUPS, S5_STATE, S5_GROUP), (2 * S5_GROUP) ** -0.5),
        's5_b_im': nrm((N_AB_LAYERS, 2, S5_GROUPS, S5_STATE, S5_GROUP), (2 * S5_GROUP) ** -0.5),
        's5_c_re': nrm((N_AB_LAYERS, 2, S5_GROUPS, S5_GROUP, S5_STATE), S5_STATE ** -0.5),
        's5_c_im': nrm((N_AB_LAYERS, 2, S5_GROUPS, S5_GROUP, S5_STATE), S5_STATE ** -0.5),
        's5_d': nrm((N_AB_LAYERS, S5_WIDTH), 1.0),
        's5_w_glu': nrm((N_AB_LAYERS, S5_WIDTH, S5_WIDTH), S5_WIDTH ** -0.5),
        's5_b_glu': nrm((N_AB_LAYERS, S5_WIDTH), 0.01),
        'w_in_c': nrm((N_C_LAYERS, D_MODEL, C_IN_DIM), D_MODEL ** -0.5),
        'w_out_c': nrm((N_C_LAYERS, HG_HEADS * HG_DV, D_MODEL), (HG_HEADS * HG_DV) ** -0.5 * DN_BETA),
        'hgrn_lb_logits': nrm((DEPTH, HG_HEADS * HG_DK), 0.1),
        'hgrn_norm_g': 1.0 + nrm((N_C_LAYERS, HG_HEADS * HG_DV), 0.02),
        'ln_mix_g': 1.0 + nrm((DEPTH, D_MODEL), 0.02),
        'ln_mix_b': nrm((DEPTH, D_MODEL), 0.02),
        'ln_ffn_g': 1.0 + nrm((DEPTH, D_MODEL), 0.02),
        'ln_ffn_b': nrm((DEPTH, D_MODEL), 0.02),
        'moe_w_group': nrm((DEPTH, D_MODEL, N_GROUPS), D_MODEL ** -0.5),
        'moe_b_group': nrm((DEPTH, N_GROUPS), 0.01),
        'moe_w_expert': nrm((DEPTH, D_MODEL, N_EXPERTS), D_MODEL ** -0.5),
        'moe_b_expert': nrm((DEPTH, N_EXPERTS), 0.01),
        'moe_w_gate_up': nrm((DEPTH, N_EXPERTS, D_MODEL, 2 * D_EXPERT), D_MODEL ** -0.5),
        'moe_w_down': nrm((DEPTH, N_EXPERTS, D_EXPERT, D_MODEL), D_EXPERT ** -0.5 * DN_BETA),
    }


def reference(x, meta_tokens, w_in_ab, w_out_ab, attn_sinks, s5_lam_re, s5_lam_im, s5_log_step,
              s5_b_re, s5_b_im, s5_c_re, s5_c_im, s5_d, s5_w_glu, s5_b_glu, w_in_c, w_out_c,
              hgrn_lb_logits, hgrn_norm_g, ln_mix_g, ln_mix_b, ln_ffn_g, ln_ffn_b,
              moe_w_group, moe_b_group, moe_w_expert, moe_b_expert, moe_w_gate_up, moe_w_down):
    B = x.shape[0]
    meta = jnp.broadcast_to(meta_tokens.astype(x.dtype)[None], (B, N_META, D_MODEL))
    h = jnp.concatenate([meta, x], axis=1)
    L = h.shape[1]
    pos = jnp.arange(L)
    lb_probs = jax.nn.softmax(hgrn_lb_logits.astype(jnp.float32), axis=0)
    lb_table = jnp.cumsum(lb_probs, axis=0) - lb_probs[0]
    for layer in range(DEPTH):
        j = layer // 2
        if layer % 2 == 0:
            proj = h @ w_in_ab[j]
            q, k, v, u = jnp.split(proj, [Q_DIM, Q_DIM + KV_DIM, Q_DIM + 2 * KV_DIM], axis=-1)
            q = rope(q.reshape(B, L, ATTN_HEADS, HEAD_DIM), pos)
            k = rope(k.reshape(B, L, ATTN_KV_HEADS, HEAD_DIM), pos)
            v = v.reshape(B, L, ATTN_KV_HEADS, HEAD_DIM)
            a_out = window_attention(q, k, v, attn_sinks[j])
            s_out = s5_mixer(u, s5_lam_re[j], s5_lam_im[j], s5_log_step[j], s5_b_re[j], s5_b_im[j],
                             s5_c_re[j], s5_c_im[j], s5_d[j], s5_w_glu[j], s5_b_glu[j])
            mix = jnp.concatenate([a_out, s_out], axis=-1) @ w_out_ab[j]
        else:
            proj = h @ w_in_c[j]
            q, i_in, f_fw, f_bw, gate = jnp.split(proj, 5, axis=-1)
            mix = hgrn2_mixer(q, i_in, f_fw, f_bw, gate, lb_table[layer], hgrn_norm_g[j]) @ w_out_c[j]
        h = layer_norm(DN_ALPHA * h + mix, ln_mix_g[layer], ln_mix_b[layer])
        ffn = hier_moe(h, moe_w_group[layer], moe_b_group[layer], moe_w_expert[layer],
                       moe_b_expert[layer], moe_w_gate_up[layer], moe_w_down[layer])
        h = layer_norm(DN_ALPHA * h + ffn, ln_ffn_g[layer], ln_ffn_b[layer])
    return h[:, N_META:]
```

```python
import functools
import math

import jax
import jax.numpy as jnp
from jax import lax
from jax.experimental import pallas as pl
from jax.experimental.pallas import tpu as pltpu

F32 = jnp.float32
BF16 = jnp.bfloat16

N_META = 16
FRONT = 512
META_ROW0 = FRONT - N_META

ATTN_HEADS = 8
ATTN_KV_HEADS = 2
ATTN_GROUP = ATTN_HEADS // ATTN_KV_HEADS
HEAD_DIM = 128
WINDOW = 128
ATTN_BLOCK = 128
ROPE_THETA = 10000.0
Q_DIM = ATTN_HEADS * HEAD_DIM
KV_DIM = ATTN_KV_HEADS * HEAD_DIM

S5_GROUP = 16
S5_STATE = 64
S5_CHUNK = 16
S5_ROW = S5_CHUNK * S5_GROUP
S5_SCAN_BLOCK = 32

HG_HEADS = 16
HG_DK = 128
HG_CHUNK = 64
HG_SUB = 16
HG_NSUB = HG_CHUNK // HG_SUB
HG_EXP_CLAMP = 80.0

N_GROUPS = 8
EXPERTS_PER_GROUP = 8
N_EXPERTS = N_GROUPS * EXPERTS_PER_GROUP
TOP_K = 2
D_EXPERT = 512
MOE_TM = 256
ROUTE_LANES = 128

LN_EPS = 1e-5
RMS_EPS = 1e-6
NEG_INF = -1e30

ROW_TILE = 512
MIX_TILE = 256

VMEM_LIMIT = 56 * 1024 * 1024


def _cparams(*sem):
    return pltpu.CompilerParams(dimension_semantics=sem, vmem_limit_bytes=VMEM_LIMIT)


def _full(shape):
    nd = len(shape)
    return pl.BlockSpec(shape, lambda *_: (0,) * nd)


def _layer_norm(x, g, b):
    mu = jnp.mean(x, axis=-1, keepdims=True)
    xc = x - mu
    var = jnp.mean(xc * xc, axis=-1, keepdims=True)
    return xc * lax.rsqrt(var + LN_EPS) * g + b


def _inproj_ab_kernel(x_ref, wq_ref, wk_ref, wv_ref, wu_ref, cos_ref, sin_ref,
                      q_ref, k_ref, v_ref, u_ref):
    xb = x_ref[...].astype(BF16)
    cos = cos_ref[...]
    sin = sin_ref[...]

    def rope(t):
        return t * cos + pltpu.roll(t, HEAD_DIM // 2, axis=1) * sin

    q = jnp.dot(xb, wq_ref[...], preferred_element_type=F32)
    for h in range(ATTN_HEADS):
        sl = slice(h * HEAD_DIM, (h + 1) * HEAD_DIM)
        q_ref[:, sl] = rope(q[:, sl]).astype(BF16)
    k = jnp.dot(xb, wk_ref[...], preferred_element_type=F32)
    for h in range(ATTN_KV_HEADS):
        sl = slice(h * HEAD_DIM, (h + 1) * HEAD_DIM)
        k_ref[:, sl] = rope(k[:, sl]).astype(BF16)
    v_ref[...] = jnp.dot(xb, wv_ref[...], preferred_element_type=F32).astype(BF16)
    u_ref[...] = jnp.dot(xb, wu_ref[...], preferred_element_type=F32)


def _inproj_ab(h, wq, wk, wv, wu, cos2, sin2):
    lp, d = h.shape
    tm = ROW_TILE
    s5w = wu.shape[1]
    row = lambda w: pl.BlockSpec((tm, w), lambda i: (i, 0))
    return pl.pallas_call(
        _inproj_ab_kernel,
        grid=(lp // tm,),
        in_specs=[row(d), _full(wq.shape), _full(wk.shape), _full(wv.shape), _full(wu.shape),
                  row(HEAD_DIM), row(HEAD_DIM)],
        out_specs=[row(Q_DIM), row(KV_DIM), row(KV_DIM), row(s5w)],
        out_shape=[jax.ShapeDtypeStruct((lp, Q_DIM), BF16),
                   jax.ShapeDtypeStruct((lp, KV_DIM), BF16),
                   jax.ShapeDtypeStruct((lp, KV_DIM), BF16),
                   jax.ShapeDtypeStruct((lp, s5w), F32)],
        compiler_params=_cparams("parallel"),
        name="inproj_ab",
    )(h, wq, wk, wv, wu, cos2, sin2)


def _attn_kernel(sink_ref, q_ref, k0_ref, k1_ref, k2_ref, v0_ref, v1_ref, v2_ref, km_ref, vm_ref,
                 o_ref, *, lp):
    qb = pl.program_id(0)
    blk = ATTN_BLOCK
    rows = ATTN_GROUP * blk
    scale = HEAD_DIM ** -0.5
    q_row = qb * blk + lax.broadcasted_iota(jnp.int32, (rows, 3 * blk), 0) % blk
    k_row = (qb - 1) * blk + lax.broadcasted_iota(jnp.int32, (rows, 3 * blk), 1)
    vis = (k_row >= FRONT) & (k_row < lp) & (jnp.abs(q_row - k_row) <= WINDOW)
    head_of_row = lax.broadcasted_iota(jnp.int32, (rows, 1), 0) // blk
    for g in range(ATTN_KV_HEADS):
        gs = slice(g * HEAD_DIM, (g + 1) * HEAD_DIM)
        qs = jnp.concatenate(
            [q_ref[:, (g * ATTN_GROUP + r) * HEAD_DIM:(g * ATTN_GROUP + r + 1) * HEAD_DIM]
             for r in range(ATTN_GROUP)], axis=0)
        kband = jnp.concatenate([k0_ref[:, gs], k1_ref[:, gs], k2_ref[:, gs]], axis=0)
        vband = jnp.concatenate([v0_ref[:, gs], v1_ref[:, gs], v2_ref[:, gs]], axis=0)
        nt = (((1,), (1,)), ((), ()))
        s_band = lax.dot_general(qs, kband, nt, preferred_element_type=F32) * scale
        s_band = jnp.where(vis, s_band, NEG_INF)
        s_meta = lax.dot_general(qs, km_ref[:, gs], nt, preferred_element_type=F32) * scale
        sink = jnp.zeros((rows, 1), F32)
        for r in range(ATTN_GROUP):
            sink = jnp.where(head_of_row == r, sink_ref[g * ATTN_GROUP + r], sink)
        m = jnp.maximum(jnp.maximum(jnp.max(s_band, axis=-1, keepdims=True),
                                    jnp.max(s_meta, axis=-1, keepdims=True)), sink)
        e_band = jnp.exp(s_band - m)
        e_meta = jnp.exp(s_meta - m)
        denom = (jnp.sum(e_band, axis=-1, keepdims=True) + jnp.sum(e_meta, axis=-1, keepdims=True)
                 + jnp.exp(sink - m))
        o = (jnp.dot(e_band.astype(BF16), vband, preferred_element_type=F32)
             + jnp.dot(e_meta.astype(BF16), vm_ref[:, gs], preferred_element_type=F32)) / denom
        for r in range(ATTN_GROUP):
            hh = g * ATTN_GROUP + r
            o_ref[:, hh * HEAD_DIM:(hh + 1) * HEAD_DIM] = o[r * blk:(r + 1) * blk].astype(BF16)


def _window_attention(q, k, v, sinks):
    lp = q.shape[0]
    blk = ATTN_BLOCK
    nb = lp // blk
    qspec = pl.BlockSpec((blk, Q_DIM), lambda i, s: (i, 0))
    kv = lambda off: pl.BlockSpec((blk, KV_DIM), lambda i, s: (jnp.clip(i + off, 0, nb - 1), 0))
    meta = pl.BlockSpec((N_META, KV_DIM), lambda i, s: (META_ROW0 // N_META, 0))
    return pl.pallas_call(
        functools.partial(_attn_kernel, lp=lp),
        grid_spec=pltpu.PrefetchScalarGridSpec(
            num_scalar_prefetch=1,
            grid=(nb,),
            in_specs=[qspec, kv(-1), kv(0), kv(1), kv(-1), kv(0), kv(1), meta, meta],
            out_specs=qspec,
        ),
        out_shape=jax.ShapeDtypeStruct((lp, Q_DIM), BF16),
        compiler_params=_cparams("parallel"),
        name="window_attention",
    )(sinks, q, k, k, k, v, v, v, k, v)


def _s5_discretise(lam_re, lam_im, log_step, b_re, b_im, c_re, c_im):
    t = S5_CHUNK
    lam = lax.complex(jnp.minimum(lam_re, -1e-4), lam_im)
    step = jnp.exp(log_step)[..., None]
    lam_dt = lam * step
    a_bar = jnp.exp(lam_dt)
    b_bar = ((a_bar - 1.0) / lam)[..., None] * lax.complex(b_re, b_im)
    c_cplx = lax.complex(c_re, c_im)
    lags = jnp.arange(t + 1, dtype=F32)
    a_pow = jnp.exp(lam_dt[..., None, :] * lags[:, None])
    kern = jnp.einsum('dgop,dglp,dgpi->dgloi', c_cplx, a_pow[:, :, :t], b_bar).real
    s_idx = jnp.arange(t)[:, None]
    t_idx = jnp.arange(t)[None, :]
    lag_fw = t_idx - s_idx
    m_fw = jnp.where((lag_fw >= 0)[None, :, :, None, None],
                     kern[0][:, jnp.clip(lag_fw, 0, t - 1)], 0.0)
    m_bw = jnp.where((lag_fw <= 0)[None, :, :, None, None],
                     kern[1][:, jnp.clip(-lag_fw, 0, t - 1)], 0.0)
    g = lam.shape[1]
    m_mat = (m_fw + m_bw).transpose(0, 1, 4, 2, 3).reshape(g, t * S5_GROUP, t * S5_GROUP)
    e_fw = a_pow[0][:, ::-1][:, 1:][..., None] * b_bar[0][:, None]
    e_bw = a_pow[1][:, :t][..., None] * b_bar[1][:, None]
    def flat_e(e):
        e = e.transpose(0, 1, 3, 2).reshape(g, t * S5_GROUP, S5_STATE)
        return jnp.concatenate([e.real, e.imag], axis=-1)
    e_mat = jnp.concatenate([flat_e(e_fw), flat_e(e_bw)], axis=-1)
    w_fw = c_cplx[0][:, None] * a_pow[0][:, 1:][:, :, None, :]
    w_bw = c_cplx[1][:, None] * a_pow[1][:, ::-1][:, :t][:, :, None, :]
    def flat_f(w):
        w = w.reshape(g, t * S5_GROUP, S5_STATE).transpose(0, 2, 1)
        return jnp.concatenate([w.real, -w.imag], axis=1)
    a_t = a_pow[:, :, t]
    coef1 = jnp.concatenate([a_t.real, a_t.real], axis=-1)
    coef2 = jnp.concatenate([-a_t.imag, a_t.imag], axis=-1)
    return (m_mat.astype(BF16), e_mat.astype(BF16), flat_f(w_fw).astype(BF16),
            flat_f(w_bw).astype(BF16), coef1, coef2)


def _s5_local_kernel(u_ref, e_ref, s_ref):
    s_ref[0] = jnp.dot(u_ref[0], e_ref[0], preferred_element_type=F32)


def _s5_scan_kernel(sfw_ref, sbw_ref, c1_ref, c2_ref, xfw_ref, xbw_ref, st_ref):
    @pl.when(pl.program_id(0) == 0)
    def _():
        st_ref[...] = jnp.zeros_like(st_ref)

    nb = sfw_ref.shape[0]
    c1f, c1b = c1_ref[0], c1_ref[1]
    c2f, c2b = c2_ref[0], c2_ref[1]

    def body(i, carry):
        xf, xb = carry
        j = nb - 1 - i
        xfw_ref[i] = xf
        xbw_ref[j] = xb
        xf = c1f * xf + c2f * pltpu.roll(xf, S5_STATE, axis=1) + sfw_ref[i]
        xb = c1b * xb + c2b * pltpu.roll(xb, S5_STATE, axis=1) + sbw_ref[j]
        return xf, xb

    xf, xb = lax.fori_loop(0, nb, body, (st_ref[0], st_ref[1]))
    st_ref[0] = xf
    st_ref[1] = xb


def _s5_combine_kernel(u_ref, m_ref, xf_ref, xb_ref, ff_ref, fb_ref, y_ref):
    y = jnp.dot(u_ref[0], m_ref[0], preferred_element_type=F32)
    y += jnp.dot(xf_ref[0], ff_ref[0], preferred_element_type=F32)
    y += jnp.dot(xb_ref[0], fb_ref[0], preferred_element_type=F32)
    y_ref[0] = y


def _s5_conv(u, mats):
    m_mat, e_mat, f_fw, f_bw, coef1, coef2 = mats
    lp, w = u.shape
    g = w // S5_GROUP
    nc = lp // S5_CHUNK
    sb = S5_SCAN_BLOCK
    u_flat = (u.astype(BF16).reshape(nc, S5_CHUNK, g, S5_GROUP).transpose(2, 0, 1, 3)
              .reshape(g, nc, S5_ROW))
    grp = lambda r, c: pl.BlockSpec((1, r, c), lambda i: (i, 0, 0))
    s_loc = pl.pallas_call(
        _s5_local_kernel,
        grid=(g,),
        in_specs=[grp(nc, S5_ROW), grp(S5_ROW, 4 * S5_STATE)],
        out_specs=grp(nc, 4 * S5_STATE),
        out_shape=jax.ShapeDtypeStruct((g, nc, 4 * S5_STATE), F32),
        compiler_params=_cparams("parallel"),
        name="s5_local_states",
    )(u_flat, e_mat)
    s_dir = s_loc.reshape(g, nc, 2, 2 * S5_STATE).transpose(2, 1, 0, 3)
    nblk = nc // sb
    fwd = pl.BlockSpec((sb, g, 2 * S5_STATE), lambda i: (i, 0, 0))
    bwd = pl.BlockSpec((sb, g, 2 * S5_STATE), lambda i: (nblk - 1 - i, 0, 0))
    x_fw, x_bw = pl.pallas_call(
        _s5_scan_kernel,
        grid=(nblk,),
        in_specs=[fwd, bwd, _full(coef1.shape), _full(coef2.shape)],
        out_specs=[fwd, bwd],
        out_shape=[jax.ShapeDtypeStruct((nc, g, 2 * S5_STATE), F32)] * 2,
        scratch_shapes=[pltpu.VMEM((2, g, 2 * S5_STATE), F32)],
        compiler_params=_cparams("arbitrary"),
        name="s5_chunk_scan",
    )(s_dir[0], s_dir[1], coef1, coef2)
    x_fw = x_fw.astype(BF16).transpose(1, 0, 2)
    x_bw = x_bw.astype(BF16).transpose(1, 0, 2)
    y_flat = pl.pallas_call(
        _s5_combine_kernel,
        grid=(g,),
        in_specs=[grp(nc, S5_ROW), grp(S5_ROW, S5_ROW), grp(nc, 2 * S5_STATE), grp(nc, 2 * S5_STATE),
                  grp(2 * S5_STATE, S5_ROW), grp(2 * S5_STATE, S5_ROW)],
        out_specs=grp(nc, S5_ROW),
        out_shape=jax.ShapeDtypeStruct((g, nc, S5_ROW), F32),
        compiler_params=_cparams("parallel"),
        name="s5_combine",
    )(u_flat, m_mat, x_fw, x_bw, f_fw, f_bw)
    return y_flat.reshape(g, nc, S5_CHUNK, S5_GROUP).transpose(1, 2, 0, 3).reshape(lp, w)


def _route(hn, wr_ref, br_ref, route_ref):
    logits = jnp.dot(hn, wr_ref[...], preferred_element_type=F32,
                     precision=lax.Precision.HIGHEST) + br_ref[...]
    lane_i = lax.broadcasted_iota(jnp.int32, logits.shape, 1)
    lane = lane_i.astype(F32)
    lane_grp = ((lane_i - N_GROUPS) // EXPERTS_PER_GROUP).astype(F32)
    big = float(ROUTE_LANES)
    g_log = jnp.where(lane_i < N_GROUPS, logits, -jnp.inf)
    g_max = jnp.max(g_log, axis=-1, keepdims=True)
    grp = jnp.min(jnp.where(g_log == g_max, lane, big), axis=-1, keepdims=True)
    p_grp = 1.0 / jnp.sum(jnp.exp(g_log - g_max), axis=-1, keepdims=True)
    in_grp = (lane_i >= N_GROUPS) & (lane_i < N_GROUPS + N_EXPERTS) & (lane_grp == grp)
    e_log = jnp.where(in_grp, logits, -jnp.inf)
    v1 = jnp.max(e_log, axis=-1, keepdims=True)
    i1 = jnp.min(jnp.where(e_log == v1, lane, big), axis=-1, keepdims=True)
    e_log2 = jnp.where(lane == i1, -jnp.inf, e_log)
    v2 = jnp.max(e_log2, axis=-1, keepdims=True)
    i2 = jnp.min(jnp.where(e_log2 == v2, lane, big), axis=-1, keepdims=True)
    e21 = jnp.exp(v2 - v1)
    w1 = p_grp / (1.0 + e21)
    w2 = p_grp * e21 / (1.0 + e21)
    route = jnp.where(lane_i == 0, w1, 0.0)
    route = jnp.where(lane_i == 1, w2, route)
    route = jnp.where(lane_i == 2, i1 - N_GROUPS, route)
    route = jnp.where(lane_i == 3, i2 - N_GROUPS, route)
    route_ref[...] = route


def _mix_ab_kernel(h_ref, a_ref, u_ref, y_ref, d_ref, wglu_ref, bglu_ref, woa_ref, wos_ref,
                   g_ref, b_ref, wr_ref, br_ref, hn_ref, route_ref, *, alpha):
    y = u_ref[...] * d_ref[...] + y_ref[...]
    z = 0.5 * y * (1.0 + jnp.tanh(math.sqrt(2.0 / math.pi) * (y + 0.044715 * (y * y * y))))
    gate = jnp.dot(z.astype(BF16), wglu_ref[...], preferred_element_type=F32) + bglu_ref[...]
    s_out = z * jax.nn.sigmoid(gate)
    mix = (jnp.dot(a_ref[...], woa_ref[...], preferred_element_type=F32)
           + jnp.dot(s_out.astype(BF16), wos_ref[...], preferred_element_type=F32))
    hn = _layer_norm(alpha * h_ref[...] + mix, g_ref[...], b_ref[...])
    hn_ref[...] = hn
    _route(hn, wr_ref, br_ref, route_ref)


def _mix_ab(h, a_out, u, y_conv, d_skip, w_glu, b_glu, wo_a, wo_s, ln_g, ln_b, w_route, b_route, alpha):
    lp, d = h.shape
    tm = MIX_TILE
    w = u.shape[1]
    row = lambda c: pl.BlockSpec((tm, c), lambda i: (i, 0))
    return pl.pallas_call(
        functools.partial(_mix_ab_kernel, alpha=alpha),
        grid=(lp // tm,),
        in_specs=[row(d), row(Q_DIM), row(w), row(w), _full(d_skip.shape), _full(w_glu.shape),
                  _full(b_glu.shape), _full(wo_a.shape), _full(wo_s.shape), _full(ln_g.shape),
                  _full(ln_b.shape), _full(w_route.shape), _full(b_route.shape)],
        out_specs=[row(d), row(ROUTE_LANES)],
        out_shape=[jax.ShapeDtypeStruct((lp, d), F32), jax.ShapeDtypeStruct((lp, ROUTE_LANES), F32)],
        compiler_params=_cparams("parallel"),
        name="mix_ab",
    )(h, a_out, u, y_conv, d_skip, w_glu, b_glu, wo_a, wo_s, ln_g, ln_b, w_route, b_route)


def _mix_c_kernel(h_ref, ofw_ref, obw_ref, gate_ref, ng_ref, wo_ref, g_ref, b_ref, wr_ref, br_ref,
                  hn_ref, route_ref, *, alpha):
    parts = []
    for hh in range(HG_HEADS):
        o = ofw_ref[hh] + obw_ref[hh]
        o = o * lax.rsqrt(jnp.mean(o * o, axis=-1, keepdims=True) + RMS_EPS)
        o = o * ng_ref[hh] * jax.nn.sigmoid(gate_ref[hh])
        parts.append(o.astype(BF16))
    mix = jnp.dot(jnp.concatenate(parts, axis=1), wo_ref[...], preferred_element_type=F32)
    hn = _layer_norm(alpha * h_ref[...] + mix, g_ref[...], b_ref[...])
    hn_ref[...] = hn
    _route(hn, wr_ref, br_ref, route_ref)


def _mix_c(h, o_fw, o_bw, gate, norm_g, wo, ln_g, ln_b, w_route, b_route, alpha):
    lp, d = h.shape
    tm = MIX_TILE
    row = lambda c: pl.BlockSpec((tm, c), lambda i: (i, 0))
    hm = pl.BlockSpec((HG_HEADS, tm, HG_DK), lambda i: (0, i, 0))
    return pl.pallas_call(
        functools.partial(_mix_c_kernel, alpha=alpha),
        grid=(lp // tm,),
        in_specs=[row(d), hm, hm, hm, _full(norm_g.shape), _full(wo.shape), _full(ln_g.shape),
                  _full(ln_b.shape), _full(w_route.shape), _full(b_route.shape)],
        out_specs=[row(d), row(ROUTE_LANES)],
        out_shape=[jax.ShapeDtypeStruct((lp, d), F32), jax.ShapeDtypeStruct((lp, ROUTE_LANES), F32)],
        compiler_params=_cparams("parallel"),
        name="mix_c",
    )(h, o_fw, o_bw, gate, norm_g, wo, ln_g, ln_b, w_route, b_route)


def _moe_plan(route, n_rows_valid_from):
    lp = route.shape[0]
    tm = MOE_TM
    valid = (jnp.arange(lp) >= n_rows_valid_from)
    expert = route[:, 2:4].astype(jnp.int32)
    flat_e = jnp.where(valid[:, None], expert, N_EXPERTS).reshape(-1)
    onehot = (flat_e[:, None] == jnp.arange(N_EXPERTS)[None, :]).astype(jnp.int32)
    csum = jnp.cumsum(onehot, axis=0)
    counts = csum[-1]
    rank = jnp.sum(jnp.where(onehot > 0, csum - 1, 0), axis=1)
    padded = (counts + tm - 1) // tm * tm
    pend = jnp.cumsum(padded)
    pstart = pend - padded
    e_safe = jnp.minimum(flat_e, N_EXPERTS - 1)
    dest = jnp.where(flat_e < N_EXPERTS, pstart[e_safe] + rank, 0)
    n_tokens = lp - n_rows_valid_from
    n_blocks = -(-(n_tokens * TOP_K + N_EXPERTS * (tm - 1)) // tm)
    rows = n_blocks * tm
    tok = jnp.repeat(jnp.arange(lp, dtype=jnp.int32), TOP_K)
    scatter_to = jnp.where(flat_e < N_EXPERTS, dest, rows)
    tok_of_row = jnp.zeros((rows,), jnp.int32).at[scatter_to].set(tok, mode='drop')
    n_used = (pend[-1] // tm).astype(jnp.int32)
    blk = jnp.minimum(jnp.arange(n_blocks, dtype=jnp.int32), jnp.maximum(n_used - 1, 0))
    block_expert = jnp.minimum(jnp.searchsorted(pend, blk * tm, side='right'),
                               N_EXPERTS - 1).astype(jnp.int32)
    weights = jnp.where(valid[:, None], route[:, 0:2], 0.0)
    return (block_expert, n_used.reshape(1), tok_of_row.reshape(n_blocks, 1, tm),
            dest.reshape(lp, TOP_K).astype(jnp.int32), weights)


def _moe_expert_kernel(be_ref, nu_ref, tok_ref, h_hbm, wgu_ref, wd_ref, y_ref,
                       xbuf, sem, wgu_bf, wd_bf):
    i = pl.program_id(0)
    tm = xbuf.shape[0]

    @pl.when(i < nu_ref[0])
    def _():
        def issue(r, carry):
            pltpu.make_async_copy(h_hbm.at[tok_ref[0, 0, r]], xbuf.at[r], sem).start()
            return carry
        lax.fori_loop(0, tm, issue, 0)

        changed = jnp.logical_or(i == 0, be_ref[i] != be_ref[jnp.maximum(i - 1, 0)])

        @pl.when(changed)
        def _():
            wgu_bf[...] = wgu_ref[0].astype(BF16)
            wd_bf[...] = wd_ref[0].astype(BF16)

        pltpu.make_async_copy(h_hbm.at[pl.ds(0, tm)], xbuf, sem).wait()
        hgu = jnp.dot(xbuf[...].astype(BF16), wgu_bf[...], preferred_element_type=F32)
        act = jax.nn.silu(hgu[:, :D_EXPERT]) * hgu[:, D_EXPERT:]
        y_ref[...] = jnp.dot(act.astype(BF16), wd_bf[...], preferred_element_type=F32)

    @pl.when(i >= nu_ref[0])
    def _():
        y_ref[...] = jnp.zeros_like(y_ref)


def _moe_experts(h, plan, w_gate_up, w_down):
    block_expert, n_used, tok_of_row, _, _ = plan
    lp, d = h.shape
    tm = MOE_TM
    n_blocks = tok_of_row.shape[0]
    last = lambda i, nu: jnp.minimum(i, jnp.maximum(nu[0] - 1, 0))
    return pl.pallas_call(
        _moe_expert_kernel,
        grid_spec=pltpu.PrefetchScalarGridSpec(
            num_scalar_prefetch=2,
            grid=(n_blocks,),
            in_specs=[
                pl.BlockSpec((1, 1, tm), lambda i, be, nu: (last(i, nu), 0, 0),
                             memory_space=pltpu.SMEM),
                pl.BlockSpec(memory_space=pl.ANY),
                pl.BlockSpec((1, d, 2 * D_EXPERT), lambda i, be, nu: (be[i], 0, 0)),
                pl.BlockSpec((1, D_EXPERT, d), lambda i, be, nu: (be[i], 0, 0)),
            ],
            out_specs=pl.BlockSpec((tm, d), lambda i, be, nu: (i, 0)),
            scratch_shapes=[pltpu.VMEM((tm, d), F32), pltpu.SemaphoreType.DMA(()),
                            pltpu.VMEM((d, 2 * D_EXPERT), BF16), pltpu.VMEM((D_EXPERT, d), BF16)],
        ),
        out_shape=jax.ShapeDtypeStruct((n_blocks * tm, d), F32),
        compiler_params=_cparams("arbitrary"),
        name="moe_experts",
    )(block_expert, n_used, tok_of_row, h, w_gate_up, w_down)


def _moe_combine_kernel(d0_ref, d1_ref, h_ref, w_ref, y_hbm, g_ref, b_ref, o_ref,
                        buf0, buf1, sem, *, alpha):
    tm = buf0.shape[0]

    def issue(r, carry):
        pltpu.make_async_copy(y_hbm.at[d0_ref[0, 0, r]], buf0.at[r], sem.at[0]).start()
        pltpu.make_async_copy(y_hbm.at[d1_ref[0, 0, r]], buf1.at[r], sem.at[1]).start()
        return carry
    lax.fori_loop(0, tm, issue, 0)
    pltpu.make_async_copy(y_hbm.at[pl.ds(0, tm)], buf0, sem.at[0]).wait()
    pltpu.make_async_copy(y_hbm.at[pl.ds(0, tm)], buf1, sem.at[1]).wait()
    w = w_ref[...]
    ffn = w[:, 0:1] * buf0[...] + w[:, 1:2] * buf1[...]
    o_ref[...] = _layer_norm(alpha * h_ref[...] + ffn, g_ref[...], b_ref[...])


def _moe_combine(h, plan, y_buf, ln_g, ln_b, alpha, first_row):
    _, _, _, dest, weights = plan
    lp, d = h.shape
    tm = MIX_TILE
    off = first_row // tm
    nt = lp // tm
    d0 = dest[:, 0].reshape(nt, 1, tm)
    d1 = dest[:, 1].reshape(nt, 1, tm)
    idx = pl.BlockSpec((1, 1, tm), lambda i: (i + off, 0, 0), memory_space=pltpu.SMEM)
    return pl.pallas_call(
        functools.partial(_moe_combine_kernel, alpha=alpha),
        grid=(nt - off,),
        in_specs=[idx, idx, pl.BlockSpec((tm, d), lambda i: (i + off, 0)),
                  pl.BlockSpec((tm, TOP_K), lambda i: (i + off, 0)),
                  pl.BlockSpec(memory_space=pl.ANY), _full(ln_g.shape), _full(ln_b.shape)],
        out_specs=pl.BlockSpec((tm, d), lambda i: (i, 0)),
        out_shape=jax.ShapeDtypeStruct((lp - first_row, d), F32),
        scratch_shapes=[pltpu.VMEM((tm, d), F32), pltpu.VMEM((tm, d), F32),
                        pltpu.SemaphoreType.DMA((2,))],
        compiler_params=_cparams("arbitrary"),
        name="moe_combine",
    )(d0, d1, h, weights, y_buf, ln_g, ln_b)


def _inproj_c_kernel(x_ref, w_ref, lb_ref, q_ref, v_ref, k1_ref, g1_ref, k2_ref, g2_ref, gate_ref,
                     xb_ref, *, tiles_per_seg):
    i = pl.program_id(0)
    j = pl.program_id(1)
    tm = x_ref.shape[0]

    @pl.when(j == 0)
    def _():
        xb_ref[...] = x_ref[...].astype(BF16)

    seg = j // tiles_per_seg
    acc = jnp.dot(xb_ref[...], w_ref[...], preferred_element_type=F32)
    heads = acc.shape[1] // HG_DK

    def store(ref, val):
        for hh in range(heads):
            ref[hh] = val[:, hh * HG_DK:(hh + 1) * HG_DK].astype(ref.dtype)

    def forget(k_ref, g_ref):
        lb = lb_ref[...]
        f = lb + (1.0 - lb) * jax.nn.sigmoid(acc)
        real = (i * tm + lax.broadcasted_iota(jnp.int32, (tm, 1), 0)) >= META_ROW0
        store(k_ref, jnp.where(real, 1.0 - f, 0.0))
        store(g_ref, jnp.where(real, jnp.log(f), 0.0))

    pl.when(seg == 0)(lambda: store(q_ref, acc))
    pl.when(seg == 1)(lambda: store(v_ref, acc))
    pl.when(seg == 2)(lambda: forget(k1_ref, g1_ref))
    pl.when(seg == 3)(lambda: forget(k2_ref, g2_ref))
    pl.when(seg == 4)(lambda: store(gate_ref, acc))


def _inproj_c(h, w, lb):
    lp, d = h.shape
    tm = ROW_TILE
    tn = 512
    tps = d // tn
    hpt = tn // HG_DK
    def out_spec(seg):
        return pl.BlockSpec((hpt, tm, HG_DK), lambda i, j: (jnp.clip(j - seg * tps, 0, tps - 1), i, 0))
    hm = lambda dt: jax.ShapeDtypeStruct((HG_HEADS, lp, HG_DK), dt)
    return pl.pallas_call(
        functools.partial(_inproj_c_kernel, tiles_per_seg=tps),
        grid=(lp // tm, 5 * tps),
        in_specs=[pl.BlockSpec((tm, d), lambda i, j: (i, 0)),
                  pl.BlockSpec((d, tn), lambda i, j: (0, j)),
                  pl.BlockSpec((1, tn), lambda i, j: (0, j % tps))],
        out_specs=[out_spec(0), out_spec(1), out_spec(2), out_spec(2), out_spec(3), out_spec(3),
                   out_spec(4)],
        out_shape=[hm(BF16), hm(BF16), hm(BF16), hm(F32), hm(BF16), hm(F32), hm(F32)],
        scratch_shapes=[pltpu.VMEM((tm, d), BF16)],
        compiler_params=_cparams("parallel", "arbitrary"),
        name="inproj_c",
    )(h, w, lb)


def _hgrn_chunk(q, k, v, g, state_t, reverse):
    c = HG_CHUNK
    row = lax.broadcasted_iota(jnp.int32, (c, HG_DK), 0)
    b = g
    sh = 1
    while sh < c:
        if not reverse:
            b = b + jnp.where(row >= sh, pltpu.roll(b, sh, axis=0), 0.0)
        else:
            b = b + jnp.where(row < c - sh, pltpu.roll(b, c - sh, axis=0), 0.0)
        sh *= 2
    blk = row // HG_SUB
    mid = HG_SUB // 2
    refs = [jnp.broadcast_to(b[j * HG_SUB + mid:j * HG_SUB + mid + 1, :], (c, HG_DK))
            for j in range(HG_NSUB)]
    own_ref = refs[0]
    for j in range(1, HG_NSUB):
        own_ref = jnp.where(blk == j, refs[j], own_ref)
    qd = q * jnp.exp(jnp.minimum(b - own_ref, HG_EXP_CLAMP))
    q_ext = jnp.concatenate([jnp.where(blk == j, qd, 0.0) for j in range(HG_NSUB)], axis=1)
    k_parts = []
    for j in range(HG_NSUB):
        earlier = (blk >= j) if reverse else (blk <= j)
        kd = k * jnp.exp(jnp.minimum(refs[j] - b, HG_EXP_CLAMP))
        k_parts.append(jnp.where(earlier, kd, 0.0))
    k_ext = jnp.concatenate(k_parts, axis=1)
    nt = (((1,), (1,)), ((), ()))
    scores = lax.dot_general(q_ext.astype(BF16), k_ext.astype(BF16), nt, preferred_element_type=F32)
    r_i = lax.broadcasted_iota(jnp.int32, (c, c), 0)
    c_i = lax.broadcasted_iota(jnp.int32, (c, c), 1)
    causal = (c_i >= r_i) if reverse else (c_i <= r_i)
    scores = jnp.where(causal, scores, 0.0)
    vb = v.astype(BF16)
    o = jnp.dot(scores.astype(BF16), vb, preferred_element_type=F32)
    o += lax.dot_general((q * jnp.exp(b)).astype(BF16), state_t.astype(BF16), nt,
                         preferred_element_type=F32)
    b_end = b[0:1, :] if reverse else b[c - 1:c, :]
    k_end = (k * jnp.exp(b_end - b)).astype(BF16)
    new_state = jnp.exp(b_end) * state_t + jnp.dot(v.T.astype(BF16), k_end, preferred_element_type=F32)
    return o, new_state


def _hgrn_kernel(qf_ref, vf_ref, k1_ref, g1_ref, qb_ref, vb_ref, k2_ref, g2_ref,
                 ofw_ref, obw_ref, st_ref):
    @pl.when(pl.program_id(0) == 0)
    def _():
        st_ref[...] = jnp.zeros_like(st_ref)

    def head(hh, carry):
        o, s = _hgrn_chunk(qf_ref[hh].astype(F32), k1_ref[hh].astype(F32), vf_ref[hh].astype(F32),
                           g1_ref[hh], st_ref[0, hh], reverse=False)
        ofw_ref[hh] = o
        st_ref[0, hh] = s
        o, s = _hgrn_chunk(qb_ref[hh].astype(F32), k2_ref[hh].astype(F32), vb_ref[hh].astype(F32),
                           g2_ref[hh], st_ref[1, hh], reverse=True)
        obw_ref[hh] = o
        st_ref[1, hh] = s
        return carry

    lax.fori_loop(0, HG_HEADS, head, 0)


def _hgrn(q, v, k1, g1, k2, g2):
    _, lp, dk = q.shape
    c = HG_CHUNK
    nc = lp // c
    fwd = pl.BlockSpec((HG_HEADS, c, dk), lambda i: (0, i, 0))
    bwd = pl.BlockSpec((HG_HEADS, c, dk), lambda i: (0, nc - 1 - i, 0))
    return pl.pallas_call(
        _hgrn_kernel,
        grid=(nc,),
        in_specs=[fwd, fwd, fwd, fwd, bwd, bwd, bwd, bwd],
        out_specs=[fwd, bwd],
        out_shape=[jax.ShapeDtypeStruct((HG_HEADS, lp, dk), F32)] * 2,
        scratch_shapes=[pltpu.VMEM((2, HG_HEADS, dk, dk), F32)],
        compiler_params=_cparams("arbitrary"),
        name="hgrn2_recurrence",
    )(q, v, k1, g1, q, v, k2, g2)


def _route_params(w_group, b_group, w_expert, b_expert):
    d = w_group.shape[0]
    pad = ROUTE_LANES - N_GROUPS - N_EXPERTS
    w = jnp.concatenate([w_group, w_expert, jnp.zeros((d, pad), F32)], axis=1)
    b = jnp.concatenate([b_group, b_expert, jnp.zeros((pad,), F32)])[None, :]
    return w, b


def _rope_tables(lp):
    half = HEAD_DIM // 2
    pos = jnp.maximum(jnp.arange(lp) - META_ROW0, 0).astype(F32)
    inv = ROPE_THETA ** (-jnp.arange(half, dtype=F32) * 2.0 / HEAD_DIM)
    ang = pos[:, None] * inv[None, :]
    cos, sin = jnp.cos(ang), jnp.sin(ang)
    return jnp.concatenate([cos, cos], axis=1), jnp.concatenate([-sin, sin], axis=1)


def kernel(x, meta_tokens, w_in_ab, w_out_ab, attn_sinks, s5_lam_re, s5_lam_im, s5_log_step, s5_b_re, s5_b_im, s5_c_re, s5_c_im, s5_d, s5_w_glu, s5_b_glu, w_in_c, w_out_c, hgrn_lb_logits, hgrn_norm_g, ln_mix_g, ln_mix_b, ln_ffn_g, ln_ffn_b, moe_w_group, moe_b_group, moe_w_expert, moe_b_expert, moe_w_gate_up, moe_w_down):
    batch, seq, d = x.shape
    assert batch == 1 and seq % FRONT == 0
    depth = ln_mix_g.shape[0]
    assert depth == 2
    alpha = (2.0 * depth) ** 0.25
    lp = FRONT + seq
    row2 = lambda t: t[None, :]

    h = jnp.concatenate([jnp.zeros((META_ROW0, d), F32), meta_tokens.astype(F32), x[0]], axis=0)

    w_in = w_in_ab[0].astype(BF16)
    s5w = s5_d.shape[1]
    wq, wk, wv, wu = (w_in[:, :Q_DIM], w_in[:, Q_DIM:Q_DIM + KV_DIM],
                      w_in[:, Q_DIM + KV_DIM:Q_DIM + 2 * KV_DIM], w_in[:, Q_DIM + 2 * KV_DIM:])
    cos2, sin2 = _rope_tables(lp)
    q, k, v, u = _inproj_ab(h, wq, wk, wv, wu, cos2, sin2)
    a_out = _window_attention(q, k, v, attn_sinks[0])
    mats = _s5_discretise(s5_lam_re[0], s5_lam_im[0], s5_log_step[0], s5_b_re[0], s5_b_im[0],
                          s5_c_re[0], s5_c_im[0])
    y_conv = _s5_conv(u, mats)
    wo = w_out_ab[0].astype(BF16)
    w_route, b_route = _route_params(moe_w_group[0], moe_b_group[0], moe_w_expert[0], moe_b_expert[0])
    h, route = _mix_ab(h, a_out, u, y_conv, row2(s5_d[0]), s5_w_glu[0].astype(BF16), row2(s5_b_glu[0]),
                       wo[:Q_DIM], wo[Q_DIM:], row2(ln_mix_g[0]), row2(ln_mix_b[0]), w_route, b_route,
                       alpha)
    plan = _moe_plan(route, META_ROW0)
    y_buf = _moe_experts(h, plan, moe_w_gate_up[0], moe_w_down[0])
    h = _moe_combine(h, plan, y_buf, row2(ln_ffn_g[0]), row2(ln_ffn_b[0]), alpha, 0)

    lb_probs = jax.nn.softmax(hgrn_lb_logits.astype(F32), axis=0)
    lb = (jnp.cumsum(lb_probs, axis=0) - lb_probs[0])[1]
    qh, vh, k1, g1, k2, g2, gate = _inproj_c(h, w_in_c[0].astype(BF16), row2(lb))
    o_fw, o_bw = _hgrn(qh, vh, k1, g1, k2, g2)
    w_route, b_route = _route_params(moe_w_group[1], moe_b_group[1], moe_w_expert[1], moe_b_expert[1])
    h, route = _mix_c(h, o_fw, o_bw, gate, hgrn_norm_g[0].reshape(HG_HEADS, 1, HG_DK),
                      w_out_c[0].astype(BF16), row2(ln_mix_g[1]), row2(ln_mix_b[1]), w_route, b_route,
                      alpha)
    plan = _moe_plan(route, META_ROW0)
    y_buf = _moe_experts(h, plan, moe_w_gate_up[1], moe_w_down[1])
    out = _moe_combine(h, plan, y_buf, row2(ln_ffn_g[1]), row2(ln_ffn_b[1]), alpha, FRONT)
    return out[None]
```

```python
import functools
import math

import jax
import jax.numpy as jnp
from jax import lax
from jax.experimental import pallas as pl
from jax.experimental.pallas import tpu as pltpu

F32 = jnp.float32
BF16 = jnp.bfloat16

N_META = 16
FRONT = 512
META_ROW0 = FRONT - N_META

ATTN_HEADS = 8
ATTN_KV_HEADS = 2
ATTN_GROUP = ATTN_HEADS // ATTN_KV_HEADS
HEAD_DIM = 128
WINDOW = 128
ATTN_BLOCK = 128
ROPE_THETA = 10000.0
Q_DIM = ATTN_HEADS * HEAD_DIM
KV_DIM = ATTN_KV_HEADS * HEAD_DIM

S5_GROUP = 16
S5_STATE = 64
S5_CHUNK = 16
S5_ROW = S5_CHUNK * S5_GROUP
S5_SCAN_BLOCK = 32

HG_HEADS = 16
HG_DK = 128
HG_CHUNK = 64
HG_SUB = 16
HG_NSUB = HG_CHUNK // HG_SUB
HG_EXP_CLAMP = 80.0

N_GROUPS = 8
EXPERTS_PER_GROUP = 8
N_EXPERTS = N_GROUPS * EXPERTS_PER_GROUP
TOP_K = 2
D_EXPERT = 512
MOE_TM = 256
ROUTE_LANES = 128

LN_EPS = 1e-5
RMS_EPS = 1e-6
NEG_INF = -1e30

ROW_TILE = 512
MIX_TILE = 256

VMEM_LIMIT = 56 * 1024 * 1024


def _cparams(*sem):
    return pltpu.CompilerParams(dimension_semantics=sem, vmem_limit_bytes=VMEM_LIMIT)


def _full(shape):
    nd = len(shape)
    return pl.BlockSpec(shape, lambda *_: (0,) * nd)


def _layer_norm(x, g, b):
    mu = jnp.mean(x, axis=-1, keepdims=True)
    xc = x - mu
    var = jnp.mean(xc * xc, axis=-1, keepdims=True)
    return xc * lax.rsqrt(var + LN_EPS) * g + b


def _inproj_ab_kernel(x_ref, wq_ref, wk_ref, wv_ref, wu_ref, cos_ref, sin_ref,
                      q_ref, k_ref, v_ref, u_ref):
    xb = x_ref[...].astype(BF16)
    cos = cos_ref[...]
    sin = sin_ref[...]

    def rope(t):
        return t * cos + pltpu.roll(t, HEAD_DIM // 2, axis=1) * sin

    q = jnp.dot(xb, wq_ref[...], preferred_element_type=F32)
    for h in range(ATTN_HEADS):
        sl = slice(h * HEAD_DIM, (h + 1) * HEAD_DIM)
        q_ref[:, sl] = rope(q[:, sl]).astype(BF16)
    k = jnp.dot(xb, wk_ref[...], preferred_element_type=F32)
    for h in range(ATTN_KV_HEADS):
        sl = slice(h * HEAD_DIM, (h + 1) * HEAD_DIM)
        k_ref[:, sl] = rope(k[:, sl]).astype(BF16)
    v_ref[...] = jnp.dot(xb, wv_ref[...], preferred_element_type=F32).astype(BF16)
    u_ref[...] = jnp.dot(xb, wu_ref[...], preferred_element_type=F32)


def _inproj_ab(h, wq, wk, wv, wu, cos2, sin2):
    lp, d = h.shape
    tm = ROW_TILE
    s5w = wu.shape[1]
    row = lambda w: pl.BlockSpec((tm, w), lambda i: (i, 0))
    return pl.pallas_call(
        _inproj_ab_kernel,
        grid=(lp // tm,),
        in_specs=[row(d), _full(wq.shape), _full(wk.shape), _full(wv.shape), _full(wu.shape),
                  row(HEAD_DIM), row(HEAD_DIM)],
        out_specs=[row(Q_DIM), row(KV_DIM), row(KV_DIM), row(s5w)],
        out_shape=[jax.ShapeDtypeStruct((lp, Q_DIM), BF16),
                   jax.ShapeDtypeStruct((lp, KV_DIM), BF16),
                   jax.ShapeDtypeStruct((lp, KV_DIM), BF16),
                   jax.ShapeDtypeStruct((lp, s5w), F32)],
        compiler_params=_cparams("parallel"),
        name="inproj_ab",
    )(h, wq, wk, wv, wu, cos2, sin2)


def _attn_kernel(sink_ref, q_ref, k0_ref, k1_ref, k2_ref, v0_ref, v1_ref, v2_ref, km_ref, vm_ref,
                 o_ref, *, lp):
    qb = pl.program_id(0)
    blk = ATTN_BLOCK
    rows = ATTN_GROUP * blk
    scale = HEAD_DIM ** -0.5
    q_row = qb * blk + lax.broadcasted_iota(jnp.int32, (rows, 3 * blk), 0) % blk
    k_row = (qb - 1) * blk + lax.broadcasted_iota(jnp.int32, (rows, 3 * blk), 1)
    vis = (k_row >= FRONT) & (k_row < lp) & (jnp.abs(q_row - k_row) <= WINDOW)
    head_of_row = lax.broadcasted_iota(jnp.int32, (rows, 1), 0) // blk
    for g in range(ATTN_KV_HEADS):
        gs = slice(g * HEAD_DIM, (g + 1) * HEAD_DIM)
        qs = jnp.concatenate(
            [q_ref[:, (g * ATTN_GROUP + r) * HEAD_DIM:(g * ATTN_GROUP + r + 1) * HEAD_DIM]
             for r in range(ATTN_GROUP)], axis=0)
        kband = jnp.concatenate([k0_ref[:, gs], k1_ref[:, gs], k2_ref[:, gs]], axis=0)
        vband = jnp.concatenate([v0_ref[:, gs], v1_ref[:, gs], v2_ref[:, gs]], axis=0)
        nt = (((1,), (1,)), ((), ()))
        s_band = lax.dot_general(qs, kband, nt, preferred_element_type=F32) * scale
        s_band = jnp.where(vis, s_band, NEG_INF)
        s_meta = lax.dot_general(qs, km_ref[:, gs], nt, preferred_element_type=F32) * scale
        sink = jnp.zeros((rows, 1), F32)
        for r in range(ATTN_GROUP):
            sink = jnp.where(head_of_row == r, sink_ref[g * ATTN_GROUP + r], sink)
        m = jnp.maximum(jnp.maximum(jnp.max(s_band, axis=-1, keepdims=True),
                                    jnp.max(s_meta, axis=-1, keepdims=True)), sink)
        e_band = jnp.exp(s_band - m)
        e_meta = jnp.exp(s_meta - m)
        denom = (jnp.sum(e_band, axis=-1, keepdims=True) + jnp.sum(e_meta, axis=-1, keepdims=True)
                 + jnp.exp(sink - m))
        o = (jnp.dot(e_band.astype(BF16), vband, preferred_element_type=F32)
             + jnp.dot(e_meta.astype(BF16), vm_ref[:, gs], preferred_element_type=F32)) / denom
        for r in range(ATTN_GROUP):
            hh = g * ATTN_GROUP + r
            o_ref[:, hh * HEAD_DIM:(hh + 1) * HEAD_DIM] = o[r * blk:(r + 1) * blk].astype(BF16)


def _window_attention(q, k, v, sinks):
    lp = q.shape[0]
    blk = ATTN_BLOCK
    nb = lp // blk
    qspec = pl.BlockSpec((blk, Q_DIM), lambda i, s: (i, 0))
    kv = lambda off: pl.BlockSpec((blk, KV_DIM), lambda i, s: (jnp.clip(i + off, 0, nb - 1), 0))
    meta = pl.BlockSpec((N_META, KV_DIM), lambda i, s: (META_ROW0 // N_META, 0))
    return pl.pallas_call(
        functools.partial(_attn_kernel, lp=lp),
        grid_spec=pltpu.PrefetchScalarGridSpec(
            num_scalar_prefetch=1,
            grid=(nb,),
            in_specs=[qspec, kv(-1), kv(0), kv(1), kv(-1), kv(0), kv(1), meta, meta],
            out_specs=qspec,
        ),
        out_shape=jax.ShapeDtypeStruct((lp, Q_DIM), BF16),
        compiler_params=_cparams("parallel"),
        name="window_attention",
    )(sinks, q, k, k, k, v, v, v, k, v)


def _s5_discretise(lam_re, lam_im, log_step, b_re, b_im, c_re, c_im):
    t = S5_CHUNK
    hi = lax.Precision.HIGHEST
    g = lam_re.shape[1]
    lr = jnp.minimum(lam_re, -1e-4)
    li = lam_im
    step = jnp.exp(log_step)[..., None]
    dr, di = lr * step, li * step
    lags = jnp.arange(t + 1, dtype=F32)[:, None]
    mag = jnp.exp(dr[..., None, :] * lags)
    pr, pi = mag * jnp.cos(di[..., None, :] * lags), mag * jnp.sin(di[..., None, :] * lags)
    ar, ai = pr[:, :, 1], pi[:, :, 1]
    den = lr * lr + li * li
    zr = ((ar - 1.0) * lr + ai * li) / den
    zi = (ai * lr - (ar - 1.0) * li) / den
    br = zr[..., None] * b_re - zi[..., None] * b_im
    bi = zr[..., None] * b_im + zi[..., None] * b_re
    xr = c_re[:, :, None] * pr[:, :, :t, None] - c_im[:, :, None] * pi[:, :, :t, None]
    xi = c_re[:, :, None] * pi[:, :, :t, None] + c_im[:, :, None] * pr[:, :, :t, None]
    kern = (jnp.einsum('dglop,dgpi->dgloi', xr, br, precision=hi)
            - jnp.einsum('dglop,dgpi->dgloi', xi, bi, precision=hi))
    s_idx = jnp.arange(t)[:, None]
    t_idx = jnp.arange(t)[None, :]
    lag_fw = t_idx - s_idx
    m_fw = jnp.where((lag_fw >= 0)[None, :, :, None, None],
                     kern[0][:, jnp.clip(lag_fw, 0, t - 1)], 0.0)
    m_bw = jnp.where((lag_fw <= 0)[None, :, :, None, None],
                     kern[1][:, jnp.clip(-lag_fw, 0, t - 1)], 0.0)
    m_mat = (m_fw + m_bw).transpose(0, 1, 4, 2, 3).reshape(g, t * S5_GROUP, t * S5_GROUP)

    def flat_e(p_r, p_i, d):
        e_r = p_r[..., None] * br[d][:, None] - p_i[..., None] * bi[d][:, None]
        e_i = p_r[..., None] * bi[d][:, None] + p_i[..., None] * br[d][:, None]
        flat = lambda e: e.transpose(0, 1, 3, 2).reshape(g, t * S5_GROUP, S5_STATE)
        return jnp.concatenate([flat(e_r), flat(e_i)], axis=-1)
    e_mat = jnp.concatenate([flat_e(pr[0][:, ::-1][:, 1:], pi[0][:, ::-1][:, 1:], 0),
                             flat_e(pr[1][:, :t], pi[1][:, :t], 1)], axis=-1)

    def flat_f(p_r, p_i, d):
        w_r = c_re[d][:, None] * p_r[:, :, None, :] - c_im[d][:, None] * p_i[:, :, None, :]
        w_i = c_re[d][:, None] * p_i[:, :, None, :] + c_im[d][:, None] * p_r[:, :, None, :]
        flat = lambda w: w.reshape(g, t * S5_GROUP, S5_STATE).transpose(0, 2, 1)
        return jnp.concatenate([flat(w_r), -flat(w_i)], axis=1)
    f_fw = flat_f(pr[0][:, 1:], pi[0][:, 1:], 0)
    f_bw = flat_f(pr[1][:, ::-1][:, :t], pi[1][:, ::-1][:, :t], 1)
    coef1 = jnp.concatenate([pr[:, :, t], pr[:, :, t]], axis=-1)
    coef2 = jnp.concatenate([-pi[:, :, t], pi[:, :, t]], axis=-1)
    return (m_mat.astype(BF16), e_mat.astype(BF16), f_fw.astype(BF16), f_bw.astype(BF16), coef1, coef2)


def _s5_local_kernel(u_ref, e_ref, s_ref):
    s_ref[0] = jnp.dot(u_ref[0], e_ref[0], preferred_element_type=F32)


def _s5_scan_kernel(sfw_ref, sbw_ref, c1_ref, c2_ref, xfw_ref, xbw_ref, st_ref):
    @pl.when(pl.program_id(0) == 0)
    def _():
        st_ref[...] = jnp.zeros_like(st_ref)

    nb = sfw_ref.shape[0]
    c1f, c1b = c1_ref[0], c1_ref[1]
    c2f, c2b = c2_ref[0], c2_ref[1]

    def body(i, carry):
        xf, xb = carry
        j = nb - 1 - i
        xfw_ref[i] = xf
        xbw_ref[j] = xb
        xf = c1f * xf + c2f * pltpu.roll(xf, S5_STATE, axis=1) + sfw_ref[i]
        xb = c1b * xb + c2b * pltpu.roll(xb, S5_STATE, axis=1) + sbw_ref[j]
        return xf, xb

    xf, xb = lax.fori_loop(0, nb, body, (st_ref[0], st_ref[1]))
    st_ref[0] = xf
    st_ref[1] = xb


def _s5_combine_kernel(u_ref, m_ref, xf_ref, xb_ref, ff_ref, fb_ref, y_ref):
    y = jnp.dot(u_ref[0], m_ref[0], preferred_element_type=F32)
    y += jnp.dot(xf_ref[0], ff_ref[0], preferred_element_type=F32)
    y += jnp.dot(xb_ref[0], fb_ref[0], preferred_element_type=F32)
    y_ref[0] = y


def _s5_conv(u, mats):
    m_mat, e_mat, f_fw, f_bw, coef1, coef2 = mats
    lp, w = u.shape
    g = w // S5_GROUP
    nc = lp // S5_CHUNK
    sb = S5_SCAN_BLOCK
    u_flat = (u.astype(BF16).reshape(nc, S5_CHUNK, g, S5_GROUP).transpose(2, 0, 1, 3)
              .reshape(g, nc, S5_ROW))
    grp = lambda r, c: pl.BlockSpec((1, r, c), lambda i: (i, 0, 0))
    s_loc = pl.pallas_call(
        _s5_local_kernel,
        grid=(g,),
        in_specs=[grp(nc, S5_ROW), grp(S5_ROW, 4 * S5_STATE)],
        out_specs=grp(nc, 4 * S5_STATE),
        out_shape=jax.ShapeDtypeStruct((g, nc, 4 * S5_STATE), F32),
        compiler_params=_cparams("parallel"),
        name="s5_local_states",
    )(u_flat, e_mat)
    s_dir = s_loc.reshape(g, nc, 2, 2 * S5_STATE).transpose(2, 1, 0, 3)
    nblk = nc // sb
    fwd = pl.BlockSpec((sb, g, 2 * S5_STATE), lambda i: (i, 0, 0))
    bwd = pl.BlockSpec((sb, g, 2 * S5_STATE), lambda i: (nblk - 1 - i, 0, 0))
    x_fw, x_bw = pl.pallas_call(
        _s5_scan_kernel,
        grid=(nblk,),
        in_specs=[fwd, bwd, _full(coef1.shape), _full(coef2.shape)],
        out_specs=[fwd, bwd],
        out_shape=[jax.ShapeDtypeStruct((nc, g, 2 * S5_STATE), F32)] * 2,
        scratch_shapes=[pltpu.VMEM((2, g, 2 * S5_STATE), F32)],
        compiler_params=_cparams("arbitrary"),
        name="s5_chunk_scan",
    )(s_dir[0], s_dir[1], coef1, coef2)
    x_fw = x_fw.astype(BF16).transpose(1, 0, 2)
    x_bw = x_bw.astype(BF16).transpose(1, 0, 2)
    y_flat = pl.pallas_call(
        _s5_combine_kernel,
        grid=(g,),
        in_specs=[grp(nc, S5_ROW), grp(S5_ROW, S5_ROW), grp(nc, 2 * S5_STATE), grp(nc, 2 * S5_STATE),
                  grp(2 * S5_STATE, S5_ROW), grp(2 * S5_STATE, S5_ROW)],
        out_specs=grp(nc, S5_ROW),
        out_shape=jax.ShapeDtypeStruct((g, nc, S5_ROW), F32),
        compiler_params=_cparams("parallel"),
        name="s5_combine",
    )(u_flat, m_mat, x_fw, x_bw, f_fw, f_bw)
    return y_flat.reshape(g, nc, S5_CHUNK, S5_GROUP).transpose(1, 2, 0, 3).reshape(lp, w)


def _route(hn, wr_ref, br_ref, route_ref):
    logits = jnp.dot(hn, wr_ref[...], preferred_element_type=F32,
                     precision=lax.Precision.HIGHEST) + br_ref[...]
    lane_i = lax.broadcasted_iota(jnp.int32, logits.shape, 1)
    lane = lane_i.astype(F32)
    lane_grp = ((lane_i - N_GROUPS) // EXPERTS_PER_GROUP).astype(F32)
    big = float(ROUTE_LANES)
    g_log = jnp.where(lane_i < N_GROUPS, logits, -jnp.inf)
    g_max = jnp.max(g_log, axis=-1, keepdims=True)
    grp = jnp.min(jnp.where(g_log == g_max, lane, big), axis=-1, keepdims=True)
    p_grp = 1.0 / jnp.sum(jnp.exp(g_log - g_max), axis=-1, keepdims=True)
    in_grp = (lane_i >= N_GROUPS) & (lane_i < N_GROUPS + N_EXPERTS) & (lane_grp == grp)
    e_log = jnp.where(in_grp, logits, -jnp.inf)
    v1 = jnp.max(e_log, axis=-1, keepdims=True)
    i1 = jnp.min(jnp.where(e_log == v1, lane, big), axis=-1, keepdims=True)
    e_log2 = jnp.where(lane == i1, -jnp.inf, e_log)
    v2 = jnp.max(e_log2, axis=-1, keepdims=True)
    i2 = jnp.min(jnp.where(e_log2 == v2, lane, big), axis=-1, keepdims=True)
    e21 = jnp.exp(v2 - v1)
    w1 = p_grp / (1.0 + e21)
    w2 = p_grp * e21 / (1.0 + e21)
    route = jnp.where(lane_i == 0, w1, 0.0)
    route = jnp.where(lane_i == 1, w2, route)
    route = jnp.where(lane_i == 2, i1 - N_GROUPS, route)
    route = jnp.where(lane_i == 3, i2 - N_GROUPS, route)
    route_ref[...] = route


def _mix_ab_kernel(h_ref, a_ref, u_ref, y_ref, d_ref, wglu_ref, bglu_ref, woa_ref, wos_ref,
                   g_ref, b_ref, wr_ref, br_ref, hn_ref, route_ref, *, alpha):
    y = u_ref[...] * d_ref[...] + y_ref[...]
    z = 0.5 * y * (1.0 + jnp.tanh(math.sqrt(2.0 / math.pi) * (y + 0.044715 * (y * y * y))))
    gate = jnp.dot(z.astype(BF16), wglu_ref[...], preferred_element_type=F32) + bglu_ref[...]
    s_out = z * jax.nn.sigmoid(gate)
    mix = (jnp.dot(a_ref[...], woa_ref[...], preferred_element_type=F32)
           + jnp.dot(s_out.astype(BF16), wos_ref[...], preferred_element_type=F32))
    hn = _layer_norm(alpha * h_ref[...] + mix, g_ref[...], b_ref[...])
    hn_ref[...] = hn
    _route(hn, wr_ref, br_ref, route_ref)


def _mix_ab(h, a_out, u, y_conv, d_skip, w_glu, b_glu, wo_a, wo_s, ln_g, ln_b, w_route, b_route, alpha):
    lp, d = h.shape
    tm = MIX_TILE
    w = u.shape[1]
    row = lambda c: pl.BlockSpec((tm, c), lambda i: (i, 0))
    return pl.pallas_call(
        functools.partial(_mix_ab_kernel, alpha=alpha),
        grid=(lp // tm,),
        in_specs=[row(d), row(Q_DIM), row(w), row(w), _full(d_skip.shape), _full(w_glu.shape),
                  _full(b_glu.shape), _full(wo_a.shape), _full(wo_s.shape), _full(ln_g.shape),
                  _full(ln_b.shape), _full(w_route.shape), _full(b_route.shape)],
        out_specs=[row(d), row(ROUTE_LANES)],
        out_shape=[jax.ShapeDtypeStruct((lp, d), F32), jax.ShapeDtypeStruct((lp, ROUTE_LANES), F32)],
        compiler_params=_cparams("parallel"),
        name="mix_ab",
    )(h, a_out, u, y_conv, d_skip, w_glu, b_glu, wo_a, wo_s, ln_g, ln_b, w_route, b_route)


def _mix_c_kernel(h_ref, ofw_ref, obw_ref, gate_ref, ng_ref, wo_ref, g_ref, b_ref, wr_ref, br_ref,
                  hn_ref, route_ref, *, alpha):
    parts = []
    for hh in range(HG_HEADS):
        o = ofw_ref[hh] + obw_ref[hh]
        o = o * lax.rsqrt(jnp.mean(o * o, axis=-1, keepdims=True) + RMS_EPS)
        o = o * ng_ref[hh] * jax.nn.sigmoid(gate_ref[hh])
        parts.append(o.astype(BF16))
    mix = jnp.dot(jnp.concatenate(parts, axis=1), wo_ref[...], preferred_element_type=F32)
    hn = _layer_norm(alpha * h_ref[...] + mix, g_ref[...], b_ref[...])
    hn_ref[...] = hn
    _route(hn, wr_ref, br_ref, route_ref)


def _mix_c(h, o_fw, o_bw, gate, norm_g, wo, ln_g, ln_b, w_route, b_route, alpha):
    lp, d = h.shape
    tm = MIX_TILE
    row = lambda c: pl.BlockSpec((tm, c), lambda i: (i, 0))
    hm = pl.BlockSpec((HG_HEADS, tm, HG_DK), lambda i: (0, i, 0))
    return pl.pallas_call(
        functools.partial(_mix_c_kernel, alpha=alpha),
        grid=(lp // tm,),
        in_specs=[row(d), hm, hm, hm, _full(norm_g.shape), _full(wo.shape), _full(ln_g.shape),
                  _full(ln_b.shape), _full(w_route.shape), _full(b_route.shape)],
        out_specs=[row(d), row(ROUTE_LANES)],
        out_shape=[jax.ShapeDtypeStruct((lp, d), F32), jax.ShapeDtypeStruct((lp, ROUTE_LANES), F32)],
        compiler_params=_cparams("parallel"),
        name="mix_c",
    )(h, o_fw, o_bw, gate, norm_g, wo, ln_g, ln_b, w_route, b_route)


def _moe_plan(route, n_rows_valid_from):
    lp = route.shape[0]
    tm = MOE_TM
    valid = (jnp.arange(lp) >= n_rows_valid_from)
    expert = route[:, 2:4].astype(jnp.int32)
    flat_e = jnp.where(valid[:, None], expert, N_EXPERTS).reshape(-1)
    onehot = (flat_e[:, None] == jnp.arange(N_EXPERTS)[None, :]).astype(jnp.int32)
    csum = jnp.cumsum(onehot, axis=0)
    counts = csum[-1]
    rank = jnp.sum(jnp.where(onehot > 0, csum - 1, 0), axis=1)
    padded = (counts + tm - 1) // tm * tm
    pend = jnp.cumsum(padded)
    pstart = pend - padded
    e_safe = jnp.minimum(flat_e, N_EXPERTS - 1)
    dest = jnp.where(flat_e < N_EXPERTS, pstart[e_safe] + rank, 0)
    n_tokens = lp - n_rows_valid_from
    n_blocks = -(-(n_tokens * TOP_K + N_EXPERTS * (tm - 1)) // tm)
    rows = n_blocks * tm
    tok = jnp.repeat(jnp.arange(lp, dtype=jnp.int32), TOP_K)
    scatter_to = jnp.where(flat_e < N_EXPERTS, dest, rows)
    tok_of_row = jnp.zeros((rows,), jnp.int32).at[scatter_to].set(tok, mode='drop')
    n_used = (pend[-1] // tm).astype(jnp.int32)
    blk = jnp.minimum(jnp.arange(n_blocks, dtype=jnp.int32), jnp.maximum(n_used - 1, 0))
    block_expert = jnp.minimum(jnp.searchsorted(pend, blk * tm, side='right'),
                               N_EXPERTS - 1).astype(jnp.int32)
    used = counts > 0
    pos = jnp.cumsum(used.astype(jnp.int32)) - 1
    eid = jnp.arange(N_EXPERTS, dtype=jnp.int32)
    later = jnp.where(used[None, :] & (eid[None, :] > eid[:, None]), eid[None, :], N_EXPERTS)
    nxt_e = jnp.min(later, axis=1)
    nxt_e = jnp.where(nxt_e == N_EXPERTS, -1, nxt_e).astype(jnp.int32)
    w_slot = (pos[block_expert] % 2).astype(jnp.int32)
    nxt = nxt_e[block_expert]
    weights = jnp.where(valid[:, None], route[:, 0:2], 0.0)
    return (block_expert, n_used.reshape(1), w_slot, nxt, tok_of_row.reshape(n_blocks, 1, tm),
            dest.reshape(lp, TOP_K).astype(jnp.int32), weights)


def _moe_expert_kernel(be_ref, nu_ref, ws_ref, nx_ref, tok0_ref, tokn_ref, h_hbm, wgu_hbm, wd_hbm,
                       y_ref, xbuf, xsem, wgu_f, wd_f, wsem, wgu_bf, wd_bf, *, layer):
    i = pl.program_id(0)
    tm = xbuf.shape[1]
    n_used = nu_ref[0]

    def gather(tok_ref, slot):
        def issue(r, carry):
            pltpu.make_async_copy(h_hbm.at[tok_ref[0, 0, r]], xbuf.at[slot, r], xsem.at[slot]).start()
            return carry
        lax.fori_loop(0, tm, issue, 0, unroll=8)

    def gather_wait(slot):
        pltpu.make_async_copy(h_hbm.at[pl.ds(0, tm)], xbuf.at[slot], xsem.at[slot]).wait()

    def weight_copies(e, s):
        return (pltpu.make_async_copy(wgu_hbm.at[layer, e], wgu_f.at[s], wsem.at[0, s]),
                pltpu.make_async_copy(wd_hbm.at[layer, e], wd_f.at[s], wsem.at[1, s]))

    @pl.when(i == 0)
    def _():
        for cp in weight_copies(be_ref[0], 0):
            cp.start()
        gather(tok0_ref, 0)

    @pl.when(i < n_used)
    def _():
        slot = i % 2
        e = be_ref[i]
        first = jnp.logical_or(i == 0, be_ref[jnp.maximum(i - 1, 0)] != e)

        @pl.when(first)
        def _():
            s = ws_ref[i]
            for cp in weight_copies(e, s):
                cp.wait()
            nxt = nx_ref[i]

            @pl.when(nxt >= 0)
            def _():
                for cp in weight_copies(nxt, 1 - s):
                    cp.start()

            wgu_bf[...] = wgu_f[s].astype(BF16)
            wd_bf[...] = wd_f[s].astype(BF16)

        gather_wait(slot)
        gather(tokn_ref, 1 - slot)
        hgu = jnp.dot(xbuf[slot].astype(BF16), wgu_bf[...], preferred_element_type=F32)
        act = jax.nn.silu(hgu[:, :D_EXPERT]) * hgu[:, D_EXPERT:]
        y_ref[...] = jnp.dot(act.astype(BF16), wd_bf[...], preferred_element_type=F32)

        @pl.when(i == n_used - 1)
        def _():
            gather_wait(1 - slot)

    @pl.when(i >= n_used)
    def _():
        y_ref[...] = jnp.zeros_like(y_ref)


def _moe_experts(h, plan, w_gate_up, w_down, layer):
    block_expert, n_used, w_slot, nxt, tok_of_row, _, _ = plan
    lp, d = h.shape
    tm = MOE_TM
    n_blocks = tok_of_row.shape[0]
    nxt_blk = lambda i, be, nu, ws, nx: (jnp.minimum(i + 1, jnp.maximum(nu[0] - 1, 0)), 0, 0)
    smem = lambda imap: pl.BlockSpec((1, 1, tm), imap, memory_space=pltpu.SMEM)
    anyspace = pl.BlockSpec(memory_space=pl.ANY)
    return pl.pallas_call(
        functools.partial(_moe_expert_kernel, layer=layer),
        grid_spec=pltpu.PrefetchScalarGridSpec(
            num_scalar_prefetch=4,
            grid=(n_blocks,),
            in_specs=[smem(lambda i, be, nu, ws, nx: (0, 0, 0)), smem(nxt_blk),
                      anyspace, anyspace, anyspace],
            out_specs=pl.BlockSpec((tm, d), lambda i, be, nu, ws, nx: (i, 0)),
            scratch_shapes=[pltpu.VMEM((2, tm, d), F32), pltpu.SemaphoreType.DMA((2,)),
                            pltpu.VMEM((2, d, 2 * D_EXPERT), F32), pltpu.VMEM((2, D_EXPERT, d), F32),
                            pltpu.SemaphoreType.DMA((2, 2)),
                            pltpu.VMEM((d, 2 * D_EXPERT), BF16), pltpu.VMEM((D_EXPERT, d), BF16)],
        ),
        out_shape=jax.ShapeDtypeStruct((n_blocks * tm, d), F32),
        compiler_params=_cparams("arbitrary"),
        name="moe_experts",
    )(block_expert, n_used, w_slot, nxt, tok_of_row, tok_of_row, h, w_gate_up, w_down)


def _moe_combine_kernel(d0_ref, d1_ref, d0n_ref, d1n_ref, h_ref, w_ref, y_hbm, g_ref, b_ref, o_ref,
                        buf, sem, *, alpha):
    i = pl.program_id(0)
    tm = buf.shape[2]
    slot = i % 2

    def gather(a_ref, b_ref_, s):
        def issue(r, carry):
            pltpu.make_async_copy(y_hbm.at[a_ref[0, 0, r]], buf.at[s, 0, r], sem.at[s, 0]).start()
            pltpu.make_async_copy(y_hbm.at[b_ref_[0, 0, r]], buf.at[s, 1, r], sem.at[s, 1]).start()
            return carry
        lax.fori_loop(0, tm, issue, 0, unroll=8)

    @pl.when(i == 0)
    def _():
        gather(d0_ref, d1_ref, 0)

    for k in range(TOP_K):
        pltpu.make_async_copy(y_hbm.at[pl.ds(0, tm)], buf.at[slot, k], sem.at[slot, k]).wait()

    @pl.when(i + 1 < pl.num_programs(0))
    def _():
        gather(d0n_ref, d1n_ref, 1 - slot)

    w = w_ref[...]
    ffn = w[:, 0:1] * buf[slot, 0] + w[:, 1:2] * buf[slot, 1]
    o_ref[...] = _layer_norm(alpha * h_ref[...] + ffn, g_ref[...], b_ref[...])


def _moe_combine(h, plan, y_buf, ln_g, ln_b, alpha, first_row):
    dest, weights = plan[-2], plan[-1]
    lp, d = h.shape
    tm = MIX_TILE
    off = first_row // tm
    nt = lp // tm
    d0 = dest[:, 0].reshape(nt, 1, tm)
    d1 = dest[:, 1].reshape(nt, 1, tm)
    cur = pl.BlockSpec((1, 1, tm), lambda i: (off, 0, 0), memory_space=pltpu.SMEM)
    nxt = pl.BlockSpec((1, 1, tm), lambda i: (jnp.minimum(i + 1 + off, nt - 1), 0, 0),
                       memory_space=pltpu.SMEM)
    return pl.pallas_call(
        functools.partial(_moe_combine_kernel, alpha=alpha),
        grid=(nt - off,),
        in_specs=[cur, cur, nxt, nxt, pl.BlockSpec((tm, d), lambda i: (i + off, 0)),
                  pl.BlockSpec((tm, TOP_K), lambda i: (i + off, 0)),
                  pl.BlockSpec(memory_space=pl.ANY), _full(ln_g.shape), _full(ln_b.shape)],
        out_specs=pl.BlockSpec((tm, d), lambda i: (i, 0)),
        out_shape=jax.ShapeDtypeStruct((lp - first_row, d), F32),
        scratch_shapes=[pltpu.VMEM((2, TOP_K, tm, d), F32), pltpu.SemaphoreType.DMA((2, TOP_K))],
        compiler_params=_cparams("arbitrary"),
        name="moe_combine",
    )(d0, d1, d0, d1, h, weights, y_buf, ln_g, ln_b)


def _inproj_c_kernel(x_ref, w_ref, lb_ref, q_ref, v_ref, k1_ref, g1_ref, k2_ref, g2_ref, gate_ref,
                     xb_ref, *, tiles_per_seg):
    i = pl.program_id(0)
    j = pl.program_id(1)
    tm = x_ref.shape[0]

    @pl.when(j == 0)
    def _():
        xb_ref[...] = x_ref[...].astype(BF16)

    seg = j // tiles_per_seg
    acc = jnp.dot(xb_ref[...], w_ref[...], preferred_element_type=F32)
    heads = acc.shape[1] // HG_DK

    def store(ref, val):
        for hh in range(heads):
            ref[hh] = val[:, hh * HG_DK:(hh + 1) * HG_DK].astype(ref.dtype)

    def forget(k_ref, g_ref):
        lb = lb_ref[...]
        f = lb + (1.0 - lb) * jax.nn.sigmoid(acc)
        real = (i * tm + lax.broadcasted_iota(jnp.int32, (tm, 1), 0)) >= META_ROW0
        store(k_ref, jnp.where(real, 1.0 - f, 0.0))
        store(g_ref, jnp.where(real, jnp.log(f), 0.0))

    pl.when(seg == 0)(lambda: store(q_ref, acc))
    pl.when(seg == 1)(lambda: store(v_ref, acc))
    pl.when(seg == 2)(lambda: forget(k1_ref, g1_ref))
    pl.when(seg == 3)(lambda: forget(k2_ref, g2_ref))
    pl.when(seg == 4)(lambda: store(gate_ref, acc))


def _inproj_c(h, w, lb):
    lp, d = h.shape
    tm = ROW_TILE
    tn = 512
    tps = d // tn
    hpt = tn // HG_DK
    def out_spec(seg):
        return pl.BlockSpec((hpt, tm, HG_DK), lambda i, j: (jnp.clip(j - seg * tps, 0, tps - 1), i, 0))
    hm = lambda dt: jax.ShapeDtypeStruct((HG_HEADS, lp, HG_DK), dt)
    return pl.pallas_call(
        functools.partial(_inproj_c_kernel, tiles_per_seg=tps),
        grid=(lp // tm, 5 * tps),
        in_specs=[pl.BlockSpec((tm, d), lambda i, j: (i, 0)),
                  pl.BlockSpec((d, tn), lambda i, j: (0, j)),
                  pl.BlockSpec((1, tn), lambda i, j: (0, j % tps))],
        out_specs=[out_spec(0), out_spec(1), out_spec(2), out_spec(2), out_spec(3), out_spec(3),
                   out_spec(4)],
        out_shape=[hm(BF16), hm(BF16), hm(BF16), hm(F32), hm(BF16), hm(F32), hm(F32)],
        scratch_shapes=[pltpu.VMEM((tm, d), BF16)],
        compiler_params=_cparams("parallel", "arbitrary"),
        name="inproj_c",
    )(h, w, lb)


def _hgrn_chunk(q, k, v, g, state_t, reverse):
    c = HG_CHUNK
    row = lax.broadcasted_iota(jnp.int32, (c, HG_DK), 0)
    b = g
    sh = 1
    while sh < c:
        if not reverse:
            b = b + jnp.where(row >= sh, pltpu.roll(b, sh, axis=0), 0.0)
        else:
            b = b + jnp.where(row < c - sh, pltpu.roll(b, c - sh, axis=0), 0.0)
        sh *= 2
    sub = HG_SUB
    mid = sub // 2
    blocks = lambda x: [x[j * sub:(j + 1) * sub] for j in range(HG_NSUB)]
    ref = [b[j * sub + mid:j * sub + mid + 1, :] for j in range(HG_NSUB)]
    b_end = b[0:1, :] if reverse else b[c - 1:c, :]
    qd, ku, q0, k_end = [], [], [], []
    for j, (qj, kj, bj) in enumerate(zip(blocks(q), blocks(k), blocks(b))):
        dj = bj - ref[j]
        qd_j = qj * jnp.exp(jnp.minimum(dj, HG_EXP_CLAMP))
        ku_j = kj * jnp.exp(jnp.minimum(-dj, HG_EXP_CLAMP))
        qd.append(qd_j.astype(BF16))
        ku.append(ku_j)
        q0.append(qd_j * jnp.exp(ref[j]))
        k_end.append(ku_j * jnp.exp(b_end - ref[j]))
    nt = (((1,), (1,)), ((), ()))
    zero = jnp.zeros((sub, HG_DK), BF16)
    score_rows = []
    for j in range(HG_NSUB):
        parts = []
        for i in range(HG_NSUB):
            if i == j:
                parts.append(ku[i].astype(BF16))
            elif (i > j) if reverse else (i < j):
                parts.append((ku[i] * jnp.exp(ref[j] - ref[i])).astype(BF16))
            else:
                parts.append(zero)
        k_ext = jnp.concatenate(parts, axis=0)
        score_rows.append(lax.dot_general(qd[j], k_ext, nt, preferred_element_type=F32))
    scores = jnp.concatenate(score_rows, axis=0)
    r_i = lax.broadcasted_iota(jnp.int32, (c, c), 0)
    c_i = lax.broadcasted_iota(jnp.int32, (c, c), 1)
    causal = (c_i >= r_i) if reverse else (c_i <= r_i)
    scores = jnp.where(causal, scores, 0.0)
    o = jnp.dot(scores.astype(BF16), v.astype(BF16), preferred_element_type=F32)
    o += lax.dot_general(jnp.concatenate(q0, axis=0).astype(BF16), state_t.astype(BF16), nt,
                         preferred_element_type=F32)
    new_state = jnp.exp(b_end) * state_t + jnp.dot(
        v.T.astype(BF16), jnp.concatenate(k_end, axis=0).astype(BF16), preferred_element_type=F32)
    return o, new_state


def _hgrn_kernel(qf_ref, vf_ref, k1_ref, g1_ref, qb_ref, vb_ref, k2_ref, g2_ref,
                 ofw_ref, obw_ref, st_ref):
    @pl.when(pl.program_id(0) == 0)
    def _():
        st_ref[...] = jnp.zeros_like(st_ref)

    def head(hh, carry):
        o, s = _hgrn_chunk(qf_ref[hh].astype(F32), k1_ref[hh].astype(F32), vf_ref[hh].astype(F32),
                           g1_ref[hh], st_ref[0, hh], reverse=False)
        ofw_ref[hh] = o
        st_ref[0, hh] = s
        o, s = _hgrn_chunk(qb_ref[hh].astype(F32), k2_ref[hh].astype(F32), vb_ref[hh].astype(F32),
                           g2_ref[hh], st_ref[1, hh], reverse=True)
        obw_ref[hh] = o
        st_ref[1, hh] = s
        return carry

    lax.fori_loop(0, HG_HEADS, head, 0, unroll=2)


def _hgrn(q, v, k1, g1, k2, g2):
    _, lp, dk = q.shape
    c = HG_CHUNK
    nc = lp // c
    fwd = pl.BlockSpec((HG_HEADS, c, dk), lambda i: (0, i, 0))
    bwd = pl.BlockSpec((HG_HEADS, c, dk), lambda i: (0, nc - 1 - i, 0))
    return pl.pallas_call(
        _hgrn_kernel,
        grid=(nc,),
        in_specs=[fwd, fwd, fwd, fwd, bwd, bwd, bwd, bwd],
        out_specs=[fwd, bwd],
        out_shape=[jax.ShapeDtypeStruct((HG_HEADS, lp, dk), F32)] * 2,
        scratch_shapes=[pltpu.VMEM((2, HG_HEADS, dk, dk), F32)],
        compiler_params=_cparams("arbitrary"),
        name="hgrn2_recurrence",
    )(q, v, k1, g1, q, v, k2, g2)


def _route_params(w_group, b_group, w_expert, b_expert):
    d = w_group.shape[0]
    pad = ROUTE_LANES - N_GROUPS - N_EXPERTS
    w = jnp.concatenate([w_group, w_expert, jnp.zeros((d, pad), F32)], axis=1)
    b = jnp.concatenate([b_group, b_expert, jnp.zeros((pad,), F32)])[None, :]
    return w, b


def _rope_tables(lp):
    half = HEAD_DIM // 2
    pos = jnp.maximum(jnp.arange(lp) - META_ROW0, 0).astype(F32)
    inv = ROPE_THETA ** (-jnp.arange(half, dtype=F32) * 2.0 / HEAD_DIM)
    ang = pos[:, None] * inv[None, :]
    cos, sin = jnp.cos(ang), jnp.sin(ang)
    return jnp.concatenate([cos, cos], axis=1), jnp.concatenate([-sin, sin], axis=1)


def kernel(x, meta_tokens, w_in_ab, w_out_ab, attn_sinks, s5_lam_re, s5_lam_im, s5_log_step, s5_b_re, s5_b_im, s5_c_re, s5_c_im, s5_d, s5_w_glu, s5_b_glu, w_in_c, w_out_c, hgrn_lb_logits, hgrn_norm_g, ln_mix_g, ln_mix_b, ln_ffn_g, ln_ffn_b, moe_w_group, moe_b_group, moe_w_expert, moe_b_expert, moe_w_gate_up, moe_w_down):
    batch, seq, d = x.shape
    assert batch == 1 and seq % FRONT == 0
    depth = ln_mix_g.shape[0]
    assert depth == 2
    alpha = (2.0 * depth) ** 0.25
    lp = FRONT + seq
    row2 = lambda t: t[None, :]

    h = jnp.concatenate([jnp.zeros((META_ROW0, d), F32), meta_tokens.astype(F32), x[0]], axis=0)

    w_in = w_in_ab[0].astype(BF16)
    s5w = s5_d.shape[1]
    wq, wk, wv, wu = (w_in[:, :Q_DIM], w_in[:, Q_DIM:Q_DIM + KV_DIM],
                      w_in[:, Q_DIM + KV_DIM:Q_DIM + 2 * KV_DIM], w_in[:, Q_DIM + 2 * KV_DIM:])
    cos2, sin2 = _rope_tables(lp)
    q, k, v, u = _inproj_ab(h, wq, wk, wv, wu, cos2, sin2)
    a_out = _window_attention(q, k, v, attn_sinks[0])
    mats = _s5_discretise(s5_lam_re[0], s5_lam_im[0], s5_log_step[0], s5_b_re[0], s5_b_im[0],
                          s5_c_re[0], s5_c_im[0])
    y_conv = _s5_conv(u, mats)
    wo = w_out_ab[0].astype(BF16)
    w_route, b_route = _route_params(moe_w_group[0], moe_b_group[0], moe_w_expert[0], moe_b_expert[0])
    h, route = _mix_ab(h, a_out, u, y_conv, row2(s5_d[0]), s5_w_glu[0].astype(BF16), row2(s5_b_glu[0]),
                       wo[:Q_DIM], wo[Q_DIM:], row2(ln_mix_g[0]), row2(ln_mix_b[0]), w_route, b_route,
                       alpha)
    plan = _moe_plan(route, META_ROW0)
    y_buf = _moe_experts(h, plan, moe_w_gate_up, moe_w_down, 0)
    h = _moe_combine(h, plan, y_buf, row2(ln_ffn_g[0]), row2(ln_ffn_b[0]), alpha, 0)

    lb_probs = jax.nn.softmax(hgrn_lb_logits.astype(F32), axis=0)
    lb = (jnp.cumsum(lb_probs, axis=0) - lb_probs[0])[1]
    qh, vh, k1, g1, k2, g2, gate = _inproj_c(h, w_in_c[0].astype(BF16), row2(lb))
    o_fw, o_bw = _hgrn(qh, vh, k1, g1, k2, g2)
    w_route, b_route = _route_params(moe_w_group[1], moe_b_group[1], moe_w_expert[1], moe_b_expert[1])
    h, route = _mix_c(h, o_fw, o_bw, gate, hgrn_norm_g[0].reshape(HG_HEADS, 1, HG_DK),
                      w_out_c[0].astype(BF16), row2(ln_mix_g[1]), row2(ln_mix_b[1]), w_route, b_route,
                      alpha)
    plan = _moe_plan(route, META_ROW0)
    y_buf = _moe_experts(h, plan, moe_w_gate_up, moe_w_down, 1)
    out = _moe_combine(h, plan, y_buf, row2(ln_ffn_g[1]), row2(ln_ffn_b[1]), alpha, FRONT)
    return out[None]
```

```python
import functools
import math

import jax
import jax.numpy as jnp
from jax import lax
from jax.experimental import pallas as pl
from jax.experimental.pallas import tpu as pltpu

F32 = jnp.float32
BF16 = jnp.bfloat16

N_META = 16
FRONT = 512
META_ROW0 = FRONT - N_META

ATTN_HEADS = 8
ATTN_KV_HEADS = 2
ATTN_GROUP = ATTN_HEADS // ATTN_KV_HEADS
HEAD_DIM = 128
WINDOW = 128
ATTN_BLOCK = 128
ROPE_THETA = 10000.0
Q_DIM = ATTN_HEADS * HEAD_DIM
KV_DIM = ATTN_KV_HEADS * HEAD_DIM

S5_GROUP = 16
S5_STATE = 64
S5_CHUNK = 16
S5_ROW = S5_CHUNK * S5_GROUP
S5_SCAN_BLOCK = 32

HG_HEADS = 16
HG_DK = 128
HG_CHUNK = 64
HG_SUB = 16
HG_NSUB = HG_CHUNK // HG_SUB
HG_EXP_CLAMP = 80.0

N_GROUPS = 8
EXPERTS_PER_GROUP = 8
N_EXPERTS = N_GROUPS * EXPERTS_PER_GROUP
TOP_K = 2
D_EXPERT = 512
MOE_TM = 256
ROUTE_LANES = 128

LN_EPS = 1e-5
RMS_EPS = 1e-6
NEG_INF = -1e30

ROW_TILE = 512
MIX_TILE = 256
MIX_SUB = 128
INPROJ_C_TILE = 1536
INPROJ_C_SUB = 512

VMEM_LIMIT = 56 * 1024 * 1024


def _cparams(*sem):
    return pltpu.CompilerParams(dimension_semantics=sem, vmem_limit_bytes=VMEM_LIMIT)


def _full(shape):
    nd = len(shape)
    return pl.BlockSpec(shape, lambda *_: (0,) * nd)


def _layer_norm(x, g, b):
    mu = jnp.mean(x, axis=-1, keepdims=True)
    xc = x - mu
    var = jnp.mean(xc * xc, axis=-1, keepdims=True)
    return xc * lax.rsqrt(var + LN_EPS) * g + b


def _inproj_ab_kernel(x_ref, wq_ref, wk_ref, wv_ref, wu_ref, cos_ref, sin_ref,
                      q_ref, k_ref, v_ref, u_ref):
    xb = x_ref[...].astype(BF16)
    cos = cos_ref[...]
    sin = sin_ref[...]

    def rope(t):
        return t * cos + pltpu.roll(t, HEAD_DIM // 2, axis=1) * sin

    q = jnp.dot(xb, wq_ref[...], preferred_element_type=F32)
    for h in range(ATTN_HEADS):
        sl = slice(h * HEAD_DIM, (h + 1) * HEAD_DIM)
        q_ref[:, sl] = rope(q[:, sl]).astype(BF16)
    k = jnp.dot(xb, wk_ref[...], preferred_element_type=F32)
    for h in range(ATTN_KV_HEADS):
        sl = slice(h * HEAD_DIM, (h + 1) * HEAD_DIM)
        k_ref[:, sl] = rope(k[:, sl]).astype(BF16)
    v_ref[...] = jnp.dot(xb, wv_ref[...], preferred_element_type=F32).astype(BF16)
    u_ref[...] = jnp.dot(xb, wu_ref[...], preferred_element_type=F32)


def _inproj_ab(h, wq, wk, wv, wu, cos2, sin2):
    lp, d = h.shape
    tm = ROW_TILE
    s5w = wu.shape[1]
    row = lambda w: pl.BlockSpec((tm, w), lambda i: (i, 0))
    return pl.pallas_call(
        _inproj_ab_kernel,
        grid=(lp // tm,),
        in_specs=[row(d), _full(wq.shape), _full(wk.shape), _full(wv.shape), _full(wu.shape),
                  row(HEAD_DIM), row(HEAD_DIM)],
        out_specs=[row(Q_DIM), row(KV_DIM), row(KV_DIM), row(s5w)],
        out_shape=[jax.ShapeDtypeStruct((lp, Q_DIM), BF16),
                   jax.ShapeDtypeStruct((lp, KV_DIM), BF16),
                   jax.ShapeDtypeStruct((lp, KV_DIM), BF16),
                   jax.ShapeDtypeStruct((lp, s5w), F32)],
        compiler_params=_cparams("parallel"),
        name="inproj_ab",
    )(h, wq, wk, wv, wu, cos2, sin2)


def _attn_kernel(sink_ref, q_ref, k0_ref, k1_ref, k2_ref, v0_ref, v1_ref, v2_ref, km_ref, vm_ref,
                 o_ref, *, lp):
    qb = pl.program_id(0)
    blk = ATTN_BLOCK
    rows = ATTN_GROUP * blk
    scale = HEAD_DIM ** -0.5
    q_row = qb * blk + lax.broadcasted_iota(jnp.int32, (rows, 3 * blk), 0) % blk
    k_row = (qb - 1) * blk + lax.broadcasted_iota(jnp.int32, (rows, 3 * blk), 1)
    vis = (k_row >= FRONT) & (k_row < lp) & (jnp.abs(q_row - k_row) <= WINDOW)
    head_of_row = lax.broadcasted_iota(jnp.int32, (rows, 1), 0) // blk
    for g in range(ATTN_KV_HEADS):
        gs = slice(g * HEAD_DIM, (g + 1) * HEAD_DIM)
        qs = jnp.concatenate(
            [q_ref[:, (g * ATTN_GROUP + r) * HEAD_DIM:(g * ATTN_GROUP + r + 1) * HEAD_DIM]
             for r in range(ATTN_GROUP)], axis=0)
        kband = jnp.concatenate([k0_ref[:, gs], k1_ref[:, gs], k2_ref[:, gs]], axis=0)
        vband = jnp.concatenate([v0_ref[:, gs], v1_ref[:, gs], v2_ref[:, gs]], axis=0)
        nt = (((1,), (1,)), ((), ()))
        s_band = lax.dot_general(qs, kband, nt, preferred_element_type=F32) * scale
        s_band = jnp.where(vis, s_band, NEG_INF)
        s_meta = lax.dot_general(qs, km_ref[:, gs], nt, preferred_element_type=F32) * scale
        sink = jnp.zeros((rows, 1), F32)
        for r in range(ATTN_GROUP):
            sink = jnp.where(head_of_row == r, sink_ref[g * ATTN_GROUP + r], sink)
        m = jnp.maximum(jnp.maximum(jnp.max(s_band, axis=-1, keepdims=True),
                                    jnp.max(s_meta, axis=-1, keepdims=True)), sink)
        e_band = jnp.exp(s_band - m)
        e_meta = jnp.exp(s_meta - m)
        denom = (jnp.sum(e_band, axis=-1, keepdims=True) + jnp.sum(e_meta, axis=-1, keepdims=True)
                 + jnp.exp(sink - m))
        o = (jnp.dot(e_band.astype(BF16), vband, preferred_element_type=F32)
             + jnp.dot(e_meta.astype(BF16), vm_ref[:, gs], preferred_element_type=F32)) / denom
        for r in range(ATTN_GROUP):
            hh = g * ATTN_GROUP + r
            o_ref[:, hh * HEAD_DIM:(hh + 1) * HEAD_DIM] = o[r * blk:(r + 1) * blk].astype(BF16)


def _window_attention(q, k, v, sinks):
    lp = q.shape[0]
    blk = ATTN_BLOCK
    nb = lp // blk
    qspec = pl.BlockSpec((blk, Q_DIM), lambda i, s: (i, 0))
    kv = lambda off: pl.BlockSpec((blk, KV_DIM), lambda i, s: (jnp.clip(i + off, 0, nb - 1), 0))
    meta = pl.BlockSpec((N_META, KV_DIM), lambda i, s: (META_ROW0 // N_META, 0))
    return pl.pallas_call(
        functools.partial(_attn_kernel, lp=lp),
        grid_spec=pltpu.PrefetchScalarGridSpec(
            num_scalar_prefetch=1,
            grid=(nb,),
            in_specs=[qspec, kv(-1), kv(0), kv(1), kv(-1), kv(0), kv(1), meta, meta],
            out_specs=qspec,
        ),
        out_shape=jax.ShapeDtypeStruct((lp, Q_DIM), BF16),
        compiler_params=_cparams("parallel"),
        name="window_attention",
    )(sinks, q, k, k, k, v, v, v, k, v)


def _s5_discretise(lam_re, lam_im, log_step, b_re, b_im, c_re, c_im):
    t = S5_CHUNK
    hi = lax.Precision.HIGHEST
    g = lam_re.shape[1]
    lr = jnp.minimum(lam_re, -1e-4)
    li = lam_im
    step = jnp.exp(log_step)[..., None]
    dr, di = lr * step, li * step
    lags = jnp.arange(t + 1, dtype=F32)[:, None]
    mag = jnp.exp(dr[..., None, :] * lags)
    pr, pi = mag * jnp.cos(di[..., None, :] * lags), mag * jnp.sin(di[..., None, :] * lags)
    ar, ai = pr[:, :, 1], pi[:, :, 1]
    den = lr * lr + li * li
    zr = ((ar - 1.0) * lr + ai * li) / den
    zi = (ai * lr - (ar - 1.0) * li) / den
    br = zr[..., None] * b_re - zi[..., None] * b_im
    bi = zr[..., None] * b_im + zi[..., None] * b_re
    xr = c_re[:, :, None] * pr[:, :, :t, None] - c_im[:, :, None] * pi[:, :, :t, None]
    xi = c_re[:, :, None] * pi[:, :, :t, None] + c_im[:, :, None] * pr[:, :, :t, None]
    kern = (jnp.einsum('dglop,dgpi->dgloi', xr, br, precision=hi)
            - jnp.einsum('dglop,dgpi->dgloi', xi, bi, precision=hi))
    s_idx = jnp.arange(t)[:, None]
    t_idx = jnp.arange(t)[None, :]
    lag_fw = t_idx - s_idx
    m_fw = jnp.where((lag_fw >= 0)[None, :, :, None, None],
                     kern[0][:, jnp.clip(lag_fw, 0, t - 1)], 0.0)
    m_bw = jnp.where((lag_fw <= 0)[None, :, :, None, None],
                     kern[1][:, jnp.clip(-lag_fw, 0, t - 1)], 0.0)
    m_mat = (m_fw + m_bw).transpose(0, 1, 4, 2, 3).reshape(g, t * S5_GROUP, t * S5_GROUP)

    def flat_e(p_r, p_i, d):
        e_r = p_r[..., None] * br[d][:, None] - p_i[..., None] * bi[d][:, None]
        e_i = p_r[..., None] * bi[d][:, None] + p_i[..., None] * br[d][:, None]
        flat = lambda e: e.transpose(0, 1, 3, 2).reshape(g, t * S5_GROUP, S5_STATE)
        return jnp.concatenate([flat(e_r), flat(e_i)], axis=-1)
    e_mat = jnp.concatenate([flat_e(pr[0][:, ::-1][:, 1:], pi[0][:, ::-1][:, 1:], 0),
                             flat_e(pr[1][:, :t], pi[1][:, :t], 1)], axis=-1)

    def flat_f(p_r, p_i, d):
        w_r = c_re[d][:, None] * p_r[:, :, None, :] - c_im[d][:, None] * p_i[:, :, None, :]
        w_i = c_re[d][:, None] * p_i[:, :, None, :] + c_im[d][:, None] * p_r[:, :, None, :]
        flat = lambda w: w.reshape(g, t * S5_GROUP, S5_STATE).transpose(0, 2, 1)
        return jnp.concatenate([flat(w_r), -flat(w_i)], axis=1)
    f_fw = flat_f(pr[0][:, 1:], pi[0][:, 1:], 0)
    f_bw = flat_f(pr[1][:, ::-1][:, :t], pi[1][:, ::-1][:, :t], 1)
    coef1 = jnp.concatenate([pr[:, :, t], pr[:, :, t]], axis=-1)
    coef2 = jnp.concatenate([-pi[:, :, t], pi[:, :, t]], axis=-1)
    return (m_mat.astype(BF16), e_mat.astype(BF16), f_fw.astype(BF16), f_bw.astype(BF16), coef1, coef2)


def _s5_local_kernel(u_ref, e_ref, s_ref):
    s_ref[0] = jnp.dot(u_ref[0], e_ref[0], preferred_element_type=F32)


def _s5_scan_kernel(sfw_ref, sbw_ref, c1_ref, c2_ref, xfw_ref, xbw_ref, st_ref):
    @pl.when(pl.program_id(0) == 0)
    def _():
        st_ref[...] = jnp.zeros_like(st_ref)

    nb = sfw_ref.shape[0]
    c1f, c1b = c1_ref[0], c1_ref[1]
    c2f, c2b = c2_ref[0], c2_ref[1]

    def body(i, carry):
        xf, xb = carry
        j = nb - 1 - i
        xfw_ref[i] = xf
        xbw_ref[j] = xb
        xf = c1f * xf + c2f * pltpu.roll(xf, S5_STATE, axis=1) + sfw_ref[i]
        xb = c1b * xb + c2b * pltpu.roll(xb, S5_STATE, axis=1) + sbw_ref[j]
        return xf, xb

    xf, xb = lax.fori_loop(0, nb, body, (st_ref[0], st_ref[1]))
    st_ref[0] = xf
    st_ref[1] = xb


def _s5_combine_kernel(u_ref, m_ref, xf_ref, xb_ref, ff_ref, fb_ref, y_ref):
    y = jnp.dot(u_ref[0], m_ref[0], preferred_element_type=F32)
    y += jnp.dot(xf_ref[0], ff_ref[0], preferred_element_type=F32)
    y += jnp.dot(xb_ref[0], fb_ref[0], preferred_element_type=F32)
    y_ref[0] = y


def _s5_conv(u, mats):
    m_mat, e_mat, f_fw, f_bw, coef1, coef2 = mats
    lp, w = u.shape
    g = w // S5_GROUP
    nc = lp // S5_CHUNK
    sb = S5_SCAN_BLOCK
    u_flat = (u.astype(BF16).reshape(nc, S5_CHUNK, g, S5_GROUP).transpose(2, 0, 1, 3)
              .reshape(g, nc, S5_ROW))
    grp = lambda r, c: pl.BlockSpec((1, r, c), lambda i: (i, 0, 0))
    s_loc = pl.pallas_call(
        _s5_local_kernel,
        grid=(g,),
        in_specs=[grp(nc, S5_ROW), grp(S5_ROW, 4 * S5_STATE)],
        out_specs=grp(nc, 4 * S5_STATE),
        out_shape=jax.ShapeDtypeStruct((g, nc, 4 * S5_STATE), F32),
        compiler_params=_cparams("parallel"),
        name="s5_local_states",
    )(u_flat, e_mat)
    s_dir = s_loc.reshape(g, nc, 2, 2 * S5_STATE).transpose(2, 1, 0, 3)
    nblk = nc // sb
    fwd = pl.BlockSpec((sb, g, 2 * S5_STATE), lambda i: (i, 0, 0))
    bwd = pl.BlockSpec((sb, g, 2 * S5_STATE), lambda i: (nblk - 1 - i, 0, 0))
    x_fw, x_bw = pl.pallas_call(
        _s5_scan_kernel,
        grid=(nblk,),
        in_specs=[fwd, bwd, _full(coef1.shape), _full(coef2.shape)],
        out_specs=[fwd, bwd],
        out_shape=[jax.ShapeDtypeStruct((nc, g, 2 * S5_STATE), F32)] * 2,
        scratch_shapes=[pltpu.VMEM((2, g, 2 * S5_STATE), F32)],
        compiler_params=_cparams("arbitrary"),
        name="s5_chunk_scan",
    )(s_dir[0], s_dir[1], coef1, coef2)
    x_fw = x_fw.astype(BF16).transpose(1, 0, 2)
    x_bw = x_bw.astype(BF16).transpose(1, 0, 2)
    y_flat = pl.pallas_call(
        _s5_combine_kernel,
        grid=(g,),
        in_specs=[grp(nc, S5_ROW), grp(S5_ROW, S5_ROW), grp(nc, 2 * S5_STATE), grp(nc, 2 * S5_STATE),
                  grp(2 * S5_STATE, S5_ROW), grp(2 * S5_STATE, S5_ROW)],
        out_specs=grp(nc, S5_ROW),
        out_shape=jax.ShapeDtypeStruct((g, nc, S5_ROW), F32),
        compiler_params=_cparams("parallel"),
        name="s5_combine",
    )(u_flat, m_mat, x_fw, x_bw, f_fw, f_bw)
    return y_flat.reshape(g, nc, S5_CHUNK, S5_GROUP).transpose(1, 2, 0, 3).reshape(lp, w)


def _route(hn, wr_ref, br_ref):
    hn_hi = hn.astype(BF16)
    hn_lo = (hn - hn_hi.astype(F32)).astype(BF16)
    p_hi = jnp.dot(hn_hi, wr_ref[...], preferred_element_type=F32)
    p_lo = jnp.dot(hn_lo, wr_ref[:, :ROUTE_LANES], preferred_element_type=F32)
    logits = p_hi[:, :ROUTE_LANES] + (p_hi[:, ROUTE_LANES:] + p_lo) + br_ref[...]
    lane_i = lax.broadcasted_iota(jnp.int32, logits.shape, 1)
    lane = lane_i.astype(F32)
    lane_grp = ((lane_i - N_GROUPS) // EXPERTS_PER_GROUP).astype(F32)
    big = float(ROUTE_LANES)
    g_log = jnp.where(lane_i < N_GROUPS, logits, -jnp.inf)
    g_max = jnp.max(g_log, axis=-1, keepdims=True)
    grp = jnp.min(jnp.where(g_log == g_max, lane, big), axis=-1, keepdims=True)
    p_grp = 1.0 / jnp.sum(jnp.exp(g_log - g_max), axis=-1, keepdims=True)
    in_grp = (lane_i >= N_GROUPS) & (lane_i < N_GROUPS + N_EXPERTS) & (lane_grp == grp)
    e_log = jnp.where(in_grp, logits, -jnp.inf)
    v1 = jnp.max(e_log, axis=-1, keepdims=True)
    i1 = jnp.min(jnp.where(e_log == v1, lane, big), axis=-1, keepdims=True)
    e_log2 = jnp.where(lane == i1, -jnp.inf, e_log)
    v2 = jnp.max(e_log2, axis=-1, keepdims=True)
    i2 = jnp.min(jnp.where(e_log2 == v2, lane, big), axis=-1, keepdims=True)
    e21 = jnp.exp(v2 - v1)
    w1 = p_grp / (1.0 + e21)
    w2 = p_grp * e21 / (1.0 + e21)
    route = jnp.where(lane_i == 0, w1, 0.0)
    route = jnp.where(lane_i == 1, w2, route)
    route = jnp.where(lane_i == 2, i1 - N_GROUPS, route)
    route = jnp.where(lane_i == 3, i2 - N_GROUPS, route)
    return route


def _sub_rows(tm):
    return [slice(r, r + MIX_SUB) for r in range(0, tm, MIX_SUB)]


def _mix_ab_kernel(h_ref, a_ref, u_ref, y_ref, d_ref, wglu_ref, bglu_ref, woa_ref, wos_ref,
                   g_ref, b_ref, wr_ref, br_ref, hn_ref, route_ref, *, alpha):
    for rows in _sub_rows(h_ref.shape[0]):
        y = u_ref[rows, :] * d_ref[...] + y_ref[rows, :]
        z = 0.5 * y * (1.0 + jnp.tanh(math.sqrt(2.0 / math.pi) * (y + 0.044715 * (y * y * y))))
        gate = jnp.dot(z.astype(BF16), wglu_ref[...], preferred_element_type=F32) + bglu_ref[...]
        s_out = z * jax.nn.sigmoid(gate)
        mix = (jnp.dot(a_ref[rows, :], woa_ref[...], preferred_element_type=F32)
               + jnp.dot(s_out.astype(BF16), wos_ref[...], preferred_element_type=F32))
        hn = _layer_norm(alpha * h_ref[rows, :] + mix, g_ref[...], b_ref[...])
        hn_ref[rows, :] = hn
        route_ref[rows, :] = _route(hn, wr_ref, br_ref)


def _mix_ab(h, a_out, u, y_conv, d_skip, w_glu, b_glu, wo_a, wo_s, ln_g, ln_b, w_route, b_route, alpha):
    lp, d = h.shape
    tm = MIX_TILE
    w = u.shape[1]
    row = lambda c: pl.BlockSpec((tm, c), lambda i: (i, 0))
    return pl.pallas_call(
        functools.partial(_mix_ab_kernel, alpha=alpha),
        grid=(lp // tm,),
        in_specs=[row(d), row(Q_DIM), row(w), row(w), _full(d_skip.shape), _full(w_glu.shape),
                  _full(b_glu.shape), _full(wo_a.shape), _full(wo_s.shape), _full(ln_g.shape),
                  _full(ln_b.shape), _full(w_route.shape), _full(b_route.shape)],
        out_specs=[row(d), row(ROUTE_LANES)],
        out_shape=[jax.ShapeDtypeStruct((lp, d), F32), jax.ShapeDtypeStruct((lp, ROUTE_LANES), F32)],
        compiler_params=_cparams("parallel"),
        name="mix_ab",
    )(h, a_out, u, y_conv, d_skip, w_glu, b_glu, wo_a, wo_s, ln_g, ln_b, w_route, b_route)


def _mix_c_kernel(h_ref, ofw_ref, obw_ref, gate_ref, ng_ref, wo_ref, g_ref, b_ref, wr_ref, br_ref,
                  hn_ref, route_ref, *, alpha):
    for rows in _sub_rows(h_ref.shape[0]):
        parts = []
        for hh in range(HG_HEADS):
            o = ofw_ref[hh, rows, :] + obw_ref[hh, rows, :]
            o = o * lax.rsqrt(jnp.mean(o * o, axis=-1, keepdims=True) + RMS_EPS)
            o = o * ng_ref[hh] * jax.nn.sigmoid(gate_ref[0, hh, rows, :].astype(F32))
            parts.append(o.astype(BF16))
        mix = jnp.dot(jnp.concatenate(parts, axis=1), wo_ref[...], preferred_element_type=F32)
        hn = _layer_norm(alpha * h_ref[rows, :] + mix, g_ref[...], b_ref[...])
        hn_ref[rows, :] = hn
        route_ref[rows, :] = _route(hn, wr_ref, br_ref)


def _mix_c(h, o_fw, o_bw, plain, norm_g, wo, ln_g, ln_b, w_route, b_route, alpha):
    lp, d = h.shape
    tm = MIX_TILE
    row = lambda c: pl.BlockSpec((tm, c), lambda i: (i, 0))
    hm = pl.BlockSpec((HG_HEADS, tm, HG_DK), lambda i: (0, i, 0))
    gate = pl.BlockSpec((1, HG_HEADS, tm, HG_DK), lambda i: (2, 0, i, 0))
    return pl.pallas_call(
        functools.partial(_mix_c_kernel, alpha=alpha),
        grid=(lp // tm,),
        in_specs=[row(d), hm, hm, gate, _full(norm_g.shape), _full(wo.shape), _full(ln_g.shape),
                  _full(ln_b.shape), _full(w_route.shape), _full(b_route.shape)],
        out_specs=[row(d), row(ROUTE_LANES)],
        out_shape=[jax.ShapeDtypeStruct((lp, d), F32), jax.ShapeDtypeStruct((lp, ROUTE_LANES), F32)],
        compiler_params=_cparams("parallel"),
        name="mix_c",
    )(h, o_fw, o_bw, plain, norm_g, wo, ln_g, ln_b, w_route, b_route)


def _moe_plan(route, n_rows_valid_from):
    lp = route.shape[0]
    tm = MOE_TM
    valid = (jnp.arange(lp) >= n_rows_valid_from)
    expert = route[:, 2:4].astype(jnp.int32)
    flat_e = jnp.where(valid[:, None], expert, N_EXPERTS).reshape(-1)
    onehot = (flat_e[:, None] == jnp.arange(N_EXPERTS)[None, :]).astype(jnp.int32)
    csum = jnp.cumsum(onehot, axis=0)
    counts = csum[-1]
    rank = jnp.sum(jnp.where(onehot > 0, csum - 1, 0), axis=1)
    padded = (counts + tm - 1) // tm * tm
    pend = jnp.cumsum(padded)
    pstart = pend - padded
    e_safe = jnp.minimum(flat_e, N_EXPERTS - 1)
    dest = jnp.where(flat_e < N_EXPERTS, pstart[e_safe] + rank, 0)
    n_tokens = lp - n_rows_valid_from
    n_blocks = -(-(n_tokens * TOP_K + N_EXPERTS * (tm - 1)) // tm)
    rows = n_blocks * tm
    tok = jnp.repeat(jnp.arange(lp, dtype=jnp.int32), TOP_K)
    scatter_to = jnp.where(flat_e < N_EXPERTS, dest, rows)
    tok_of_row = jnp.zeros((rows,), jnp.int32).at[scatter_to].set(tok, mode='drop')
    n_used = (pend[-1] // tm).astype(jnp.int32)
    blk = jnp.minimum(jnp.arange(n_blocks, dtype=jnp.int32), jnp.maximum(n_used - 1, 0))
    block_expert = jnp.minimum(jnp.searchsorted(pend, blk * tm, side='right'),
                               N_EXPERTS - 1).astype(jnp.int32)
    used = counts > 0
    pos = jnp.cumsum(used.astype(jnp.int32)) - 1
    eid = jnp.arange(N_EXPERTS, dtype=jnp.int32)
    later = jnp.where(used[None, :] & (eid[None, :] > eid[:, None]), eid[None, :], N_EXPERTS)
    nxt_e = jnp.min(later, axis=1)
    nxt_e = jnp.where(nxt_e == N_EXPERTS, -1, nxt_e).astype(jnp.int32)
    w_slot = (pos[block_expert] % 2).astype(jnp.int32)
    nxt = nxt_e[block_expert]
    weights = jnp.where(valid[:, None], route[:, 0:2], 0.0)
    return (block_expert, n_used.reshape(1), w_slot, nxt, tok_of_row.reshape(n_blocks, 1, tm),
            dest.reshape(lp, TOP_K).astype(jnp.int32), weights)


def _moe_expert_kernel(be_ref, nu_ref, ws_ref, nx_ref, tok0_ref, tokn_ref, h_hbm, wgu_hbm, wd_hbm,
                       y_ref, xbuf, xsem, wgu_f, wd_f, wsem, wgu_bf, wd_bf, *, layer):
    i = pl.program_id(0)
    tm = xbuf.shape[1]
    n_used = nu_ref[0]

    def gather(tok_ref, slot):
        def issue(r, carry):
            pltpu.make_async_copy(h_hbm.at[tok_ref[0, 0, r]], xbuf.at[slot, r], xsem.at[slot]).start()
            return carry
        lax.fori_loop(0, tm, issue, 0, unroll=8)

    def gather_wait(slot):
        pltpu.make_async_copy(h_hbm.at[pl.ds(0, tm)], xbuf.at[slot], xsem.at[slot]).wait()

    def weight_copies(e, s):
        return (pltpu.make_async_copy(wgu_hbm.at[layer, e], wgu_f.at[s], wsem.at[0, s]),
                pltpu.make_async_copy(wd_hbm.at[layer, e], wd_f.at[s], wsem.at[1, s]))

    @pl.when(i == 0)
    def _():
        for cp in weight_copies(be_ref[0], 0):
            cp.start()
        gather(tok0_ref, 0)

    @pl.when(i < n_used)
    def _():
        slot = i % 2
        e = be_ref[i]
        first = jnp.logical_or(i == 0, be_ref[jnp.maximum(i - 1, 0)] != e)

        @pl.when(first)
        def _():
            s = ws_ref[i]
            for cp in weight_copies(e, s):
                cp.wait()
            nxt = nx_ref[i]

            @pl.when(nxt >= 0)
            def _():
                for cp in weight_copies(nxt, 1 - s):
                    cp.start()

            wgu_bf[...] = wgu_f[s].astype(BF16)
            wd_bf[...] = wd_f[s].astype(BF16)

        gather_wait(slot)
        gather(tokn_ref, 1 - slot)
        hgu = jnp.dot(xbuf[slot].astype(BF16), wgu_bf[...], preferred_element_type=F32)
        act = jax.nn.silu(hgu[:, :D_EXPERT]) * hgu[:, D_EXPERT:]
        y_ref[...] = jnp.dot(act.astype(BF16), wd_bf[...], preferred_element_type=F32)

        @pl.when(i == n_used - 1)
        def _():
            gather_wait(1 - slot)

    @pl.when(i >= n_used)
    def _():
        y_ref[...] = jnp.zeros_like(y_ref)


def _moe_experts(h, plan, w_gate_up, w_down, layer):
    block_expert, n_used, w_slot, nxt, tok_of_row, _, _ = plan
    lp, d = h.shape
    tm = MOE_TM
    n_blocks = tok_of_row.shape[0]
    nxt_blk = lambda i, be, nu, ws, nx: (jnp.minimum(i + 1, jnp.maximum(nu[0] - 1, 0)), 0, 0)
    smem = lambda imap: pl.BlockSpec((1, 1, tm), imap, memory_space=pltpu.SMEM)
    anyspace = pl.BlockSpec(memory_space=pl.ANY)
    return pl.pallas_call(
        functools.partial(_moe_expert_kernel, layer=layer),
        grid_spec=pltpu.PrefetchScalarGridSpec(
            num_scalar_prefetch=4,
            grid=(n_blocks,),
            in_specs=[smem(lambda i, be, nu, ws, nx: (0, 0, 0)), smem(nxt_blk),
                      anyspace, anyspace, anyspace],
            out_specs=pl.BlockSpec((tm, d), lambda i, be, nu, ws, nx: (i, 0)),
            scratch_shapes=[pltpu.VMEM((2, tm, d), F32), pltpu.SemaphoreType.DMA((2,)),
                            pltpu.VMEM((2, d, 2 * D_EXPERT), F32), pltpu.VMEM((2, D_EXPERT, d), F32),
                            pltpu.SemaphoreType.DMA((2, 2)),
                            pltpu.VMEM((d, 2 * D_EXPERT), BF16), pltpu.VMEM((D_EXPERT, d), BF16)],
        ),
        out_shape=jax.ShapeDtypeStruct((n_blocks * tm, d), F32),
        compiler_params=_cparams("arbitrary"),
        name="moe_experts",
    )(block_expert, n_used, w_slot, nxt, tok_of_row, tok_of_row, h, w_gate_up, w_down)


def _moe_combine_kernel(d0_ref, d1_ref, d0n_ref, d1n_ref, h_ref, w_ref, y_hbm, g_ref, b_ref, *rest,
                        alpha, with_bf16):
    if with_bf16:
        o_ref, ob_ref, buf, sem = rest
    else:
        o_ref, buf, sem = rest
    i = pl.program_id(0)
    tm = buf.shape[2]
    slot = i % 2

    def gather(a_ref, b_ref_, s):
        def issue(r, carry):
            pltpu.make_async_copy(y_hbm.at[a_ref[0, 0, r]], buf.at[s, 0, r], sem.at[s, 0]).start()
            pltpu.make_async_copy(y_hbm.at[b_ref_[0, 0, r]], buf.at[s, 1, r], sem.at[s, 1]).start()
            return carry
        lax.fori_loop(0, tm, issue, 0, unroll=8)

    @pl.when(i == 0)
    def _():
        gather(d0_ref, d1_ref, 0)

    for k in range(TOP_K):
        pltpu.make_async_copy(y_hbm.at[pl.ds(0, tm)], buf.at[slot, k], sem.at[slot, k]).wait()

    @pl.when(i + 1 < pl.num_programs(0))
    def _():
        gather(d0n_ref, d1n_ref, 1 - slot)

    w = w_ref[...]
    ffn = w[:, 0:1] * buf[slot, 0] + w[:, 1:2] * buf[slot, 1]
    out = _layer_norm(alpha * h_ref[...] + ffn, g_ref[...], b_ref[...])
    o_ref[...] = out
    if with_bf16:
        ob_ref[...] = out.astype(BF16)


def _moe_combine(h, plan, y_buf, ln_g, ln_b, alpha, first_row, with_bf16=False):
    dest, weights = plan[-2], plan[-1]
    lp, d = h.shape
    tm = MIX_TILE
    off = first_row // tm
    nt = lp // tm
    d0 = dest[:, 0].reshape(nt, 1, tm)
    d1 = dest[:, 1].reshape(nt, 1, tm)
    cur = pl.BlockSpec((1, 1, tm), lambda i: (off, 0, 0), memory_space=pltpu.SMEM)
    nxt = pl.BlockSpec((1, 1, tm), lambda i: (jnp.minimum(i + 1 + off, nt - 1), 0, 0),
                       memory_space=pltpu.SMEM)
    out_spec = pl.BlockSpec((tm, d), lambda i: (i, 0))
    out_shape = jax.ShapeDtypeStruct((lp - first_row, d), F32)
    if with_bf16:
        out_spec = [out_spec, out_spec]
        out_shape = [out_shape, jax.ShapeDtypeStruct((lp - first_row, d), BF16)]
    return pl.pallas_call(
        functools.partial(_moe_combine_kernel, alpha=alpha, with_bf16=with_bf16),
        grid=(nt - off,),
        in_specs=[cur, cur, nxt, nxt, pl.BlockSpec((tm, d), lambda i: (i + off, 0)),
                  pl.BlockSpec((tm, TOP_K), lambda i: (i + off, 0)),
                  pl.BlockSpec(memory_space=pl.ANY), _full(ln_g.shape), _full(ln_b.shape)],
        out_specs=out_spec,
        out_shape=out_shape,
        scratch_shapes=[pltpu.VMEM((2, TOP_K, tm, d), F32), pltpu.SemaphoreType.DMA((2, TOP_K))],
        compiler_params=_cparams("arbitrary"),
        name="moe_combine",
    )(d0, d1, d0, d1, h, weights, y_buf, ln_g, ln_b)


def _store_heads(ref, seg_rows, val):
    for hh in range(val.shape[1] // HG_DK):
        ref[0, hh, seg_rows, :] = val[:, hh * HG_DK:(hh + 1) * HG_DK].astype(ref.dtype)


def _inproj_plain_kernel(x_ref, w_ref, o_ref):
    for r in range(0, x_ref.shape[0], INPROJ_C_SUB):
        rows = slice(r, r + INPROJ_C_SUB)
        _store_heads(o_ref, rows, jnp.dot(x_ref[rows, :], w_ref[...], preferred_element_type=F32))


def _inproj_forget_kernel(x_ref, w_ref, lb_ref, k_ref, g_ref):
    i = pl.program_id(0)
    tm = x_ref.shape[0]
    lb = lb_ref[...]
    for r in range(0, tm, INPROJ_C_SUB):
        rows = slice(r, r + INPROJ_C_SUB)
        acc = jnp.dot(x_ref[rows, :], w_ref[...], preferred_element_type=F32)
        f = lb + (1.0 - lb) * jax.nn.sigmoid(acc)
        real = (i * tm + r + lax.broadcasted_iota(jnp.int32, (INPROJ_C_SUB, 1), 0)) >= META_ROW0
        _store_heads(k_ref, rows, jnp.where(real, 1.0 - f, 0.0))
        _store_heads(g_ref, rows, jnp.where(real, jnp.log(f), 0.0))


def _inproj_c(hb, w, lb):
    lp, d = hb.shape
    tm = INPROJ_C_TILE
    tn = 512
    tps = d // tn
    hpt = tn // HG_DK
    x_spec = pl.BlockSpec((tm, d), lambda i, j: (i, 0))
    out_spec = pl.BlockSpec((1, hpt, tm, HG_DK), lambda i, j: (j // tps, j % tps, i, 0))
    hm = lambda n, dt: jax.ShapeDtypeStruct((n, HG_HEADS, lp, HG_DK), dt)
    plain_col = lambda i, j: (0, jnp.where(j >= 2 * tps, j + 2 * tps, j))
    plain = pl.pallas_call(
        _inproj_plain_kernel,
        grid=(lp // tm, 3 * tps),
        in_specs=[x_spec, pl.BlockSpec((d, tn), plain_col)],
        out_specs=out_spec,
        out_shape=hm(3, BF16),
        compiler_params=_cparams("parallel", "arbitrary"),
        name="inproj_c_plain",
    )(hb, w)
    k, g = pl.pallas_call(
        _inproj_forget_kernel,
        grid=(lp // tm, 2 * tps),
        in_specs=[x_spec, pl.BlockSpec((d, tn), lambda i, j: (0, j + 2 * tps)),
                  pl.BlockSpec((1, tn), lambda i, j: (0, j % tps))],
        out_specs=[out_spec, out_spec],
        out_shape=[hm(2, BF16), hm(2, F32)],
        compiler_params=_cparams("parallel", "arbitrary"),
        name="inproj_c_forget",
    )(hb, w, lb)
    return plain, k, g


def _hgrn_chunk(q, k, v, g, state_t, reverse):
    c = HG_CHUNK
    row = lax.broadcasted_iota(jnp.int32, (c, HG_DK), 0)
    b = g
    sh = 1
    while sh < c:
        if not reverse:
            b = b + jnp.where(row >= sh, pltpu.roll(b, sh, axis=0), 0.0)
        else:
            b = b + jnp.where(row < c - sh, pltpu.roll(b, c - sh, axis=0), 0.0)
        sh *= 2
    sub = HG_SUB
    mid = sub // 2
    blocks = lambda x: [x[j * sub:(j + 1) * sub] for j in range(HG_NSUB)]
    ref = [b[j * sub + mid:j * sub + mid + 1, :] for j in range(HG_NSUB)]
    b_end = b[0:1, :] if reverse else b[c - 1:c, :]
    qd, ku, q0, k_end = [], [], [], []
    for j, (qj, kj, bj) in enumerate(zip(blocks(q), blocks(k), blocks(b))):
        dj = bj - ref[j]
        qd_j = qj * jnp.exp(jnp.minimum(dj, HG_EXP_CLAMP))
        ku_j = kj * jnp.exp(jnp.minimum(-dj, HG_EXP_CLAMP))
        qd.append(qd_j.astype(BF16))
        ku.append(ku_j)
        q0.append(qd_j * jnp.exp(ref[j]))
        k_end.append(ku_j * jnp.exp(b_end - ref[j]))
    nt = (((1,), (1,)), ((), ()))
    zero = jnp.zeros((sub, HG_DK), BF16)
    score_rows = []
    for j in range(HG_NSUB):
        parts = []
        for i in range(HG_NSUB):
            if i == j:
                parts.append(ku[i].astype(BF16))
            elif (i > j) if reverse else (i < j):
                parts.append((ku[i] * jnp.exp(ref[j] - ref[i])).astype(BF16))
            else:
                parts.append(zero)
        k_ext = jnp.concatenate(parts, axis=0)
        score_rows.append(lax.dot_general(qd[j], k_ext, nt, preferred_element_type=F32))
    scores = jnp.concatenate(score_rows, axis=0)
    r_i = lax.broadcasted_iota(jnp.int32, (c, c), 0)
    c_i = lax.broadcasted_iota(jnp.int32, (c, c), 1)
    causal = (c_i >= r_i) if reverse else (c_i <= r_i)
    scores = jnp.where(causal, scores, 0.0)
    o = jnp.dot(scores.astype(BF16), v.astype(BF16), preferred_element_type=F32)
    o += lax.dot_general(jnp.concatenate(q0, axis=0).astype(BF16), state_t.astype(BF16), nt,
                         preferred_element_type=F32)
    new_state = jnp.exp(b_end) * state_t + jnp.dot(
        v.T.astype(BF16), jnp.concatenate(k_end, axis=0).astype(BF16), preferred_element_type=F32)
    return o, new_state


def _hgrn_kernel(qf_ref, vf_ref, k1_ref, g1_ref, qb_ref, vb_ref, k2_ref, g2_ref,
                 ofw_ref, obw_ref, st_ref):
    @pl.when(pl.program_id(0) == 0)
    def _():
        st_ref[...] = jnp.zeros_like(st_ref)

    def head(hh, carry):
        o, s = _hgrn_chunk(qf_ref[0, hh].astype(F32), k1_ref[0, hh].astype(F32),
                           vf_ref[0, hh].astype(F32), g1_ref[0, hh], st_ref[0, hh], reverse=False)
        ofw_ref[hh] = o
        st_ref[0, hh] = s
        o, s = _hgrn_chunk(qb_ref[0, hh].astype(F32), k2_ref[0, hh].astype(F32),
                           vb_ref[0, hh].astype(F32), g2_ref[0, hh], st_ref[1, hh], reverse=True)
        obw_ref[hh] = o
        st_ref[1, hh] = s
        return carry

    lax.fori_loop(0, HG_HEADS, head, 0, unroll=2)


def _hgrn(plain, k, g):
    _, _, lp, dk = plain.shape
    c = HG_CHUNK
    nc = lp // c
    fwd = lambda seg: pl.BlockSpec((1, HG_HEADS, c, dk), lambda i: (seg, 0, i, 0))
    bwd = lambda seg: pl.BlockSpec((1, HG_HEADS, c, dk), lambda i: (seg, 0, nc - 1 - i, 0))
    out_f = pl.BlockSpec((HG_HEADS, c, dk), lambda i: (0, i, 0))
    out_b = pl.BlockSpec((HG_HEADS, c, dk), lambda i: (0, nc - 1 - i, 0))
    return pl.pallas_call(
        _hgrn_kernel,
        grid=(nc,),
        in_specs=[fwd(0), fwd(1), fwd(0), fwd(0), bwd(0), bwd(1), bwd(1), bwd(1)],
        out_specs=[out_f, out_b],
        out_shape=[jax.ShapeDtypeStruct((HG_HEADS, lp, dk), F32)] * 2,
        scratch_shapes=[pltpu.VMEM((2, HG_HEADS, dk, dk), F32)],
        compiler_params=_cparams("arbitrary"),
        name="hgrn2_recurrence",
    )(plain, plain, k, g, plain, plain, k, g)


def _route_params(w_group, b_group, w_expert, b_expert):
    d = w_group.shape[0]
    pad = ROUTE_LANES - N_GROUPS - N_EXPERTS
    w = jnp.concatenate([w_group, w_expert, jnp.zeros((d, pad), F32)], axis=1)
    b = jnp.concatenate([b_group, b_expert, jnp.zeros((pad,), F32)])[None, :]
    w_hi = w.astype(BF16)
    w_lo = (w - w_hi.astype(F32)).astype(BF16)
    return jnp.concatenate([w_hi, w_lo], axis=1), b


def _rope_tables(lp):
    half = HEAD_DIM // 2
    pos = jnp.maximum(jnp.arange(lp) - META_ROW0, 0).astype(F32)
    inv = ROPE_THETA ** (-jnp.arange(half, dtype=F32) * 2.0 / HEAD_DIM)
    ang = pos[:, None] * inv[None, :]
    cos, sin = jnp.cos(ang), jnp.sin(ang)
    return jnp.concatenate([cos, cos], axis=1), jnp.concatenate([-sin, sin], axis=1)


def kernel(x, meta_tokens, w_in_ab, w_out_ab, attn_sinks, s5_lam_re, s5_lam_im, s5_log_step, s5_b_re, s5_b_im, s5_c_re, s5_c_im, s5_d, s5_w_glu, s5_b_glu, w_in_c, w_out_c, hgrn_lb_logits, hgrn_norm_g, ln_mix_g, ln_mix_b, ln_ffn_g, ln_ffn_b, moe_w_group, moe_b_group, moe_w_expert, moe_b_expert, moe_w_gate_up, moe_w_down):
    batch, seq, d = x.shape
    assert batch == 1 and seq % FRONT == 0
    depth = ln_mix_g.shape[0]
    assert depth == 2
    alpha = (2.0 * depth) ** 0.25
    lp = FRONT + seq
    row2 = lambda t: t[None, :]

    h = jnp.concatenate([jnp.zeros((META_ROW0, d), F32), meta_tokens.astype(F32), x[0]], axis=0)

    w_in = w_in_ab[0].astype(BF16)
    s5w = s5_d.shape[1]
    wq, wk, wv, wu = (w_in[:, :Q_DIM], w_in[:, Q_DIM:Q_DIM + KV_DIM],
                      w_in[:, Q_DIM + KV_DIM:Q_DIM + 2 * KV_DIM], w_in[:, Q_DIM + 2 * KV_DIM:])
    cos2, sin2 = _rope_tables(lp)
    q, k, v, u = _inproj_ab(h, wq, wk, wv, wu, cos2, sin2)
    a_out = _window_attention(q, k, v, attn_sinks[0])
    mats = _s5_discretise(s5_lam_re[0], s5_lam_im[0], s5_log_step[0], s5_b_re[0], s5_b_im[0],
                          s5_c_re[0], s5_c_im[0])
    y_conv = _s5_conv(u, mats)
    wo = w_out_ab[0].astype(BF16)
    w_route, b_route = _route_params(moe_w_group[0], moe_b_group[0], moe_w_expert[0], moe_b_expert[0])
    h, route = _mix_ab(h, a_out, u, y_conv, row2(s5_d[0]), s5_w_glu[0].astype(BF16), row2(s5_b_glu[0]),
                       wo[:Q_DIM], wo[Q_DIM:], row2(ln_mix_g[0]), row2(ln_mix_b[0]), w_route, b_route,
                       alpha)
    plan = _moe_plan(route, META_ROW0)
    y_buf = _moe_experts(h, plan, moe_w_gate_up, moe_w_down, 0)
    h, hb = _moe_combine(h, plan, y_buf, row2(ln_ffn_g[0]), row2(ln_ffn_b[0]), alpha, 0, with_bf16=True)

    lb_probs = jax.nn.softmax(hgrn_lb_logits.astype(F32), axis=0)
    lb = (jnp.cumsum(lb_probs, axis=0) - lb_probs[0])[1]
    plain, hk, hg = _inproj_c(hb, w_in_c[0].astype(BF16), row2(lb))
    o_fw, o_bw = _hgrn(plain, hk, hg)
    w_route, b_route = _route_params(moe_w_group[1], moe_b_group[1], moe_w_expert[1], moe_b_expert[1])
    h, route = _mix_c(h, o_fw, o_bw, plain, hgrn_norm_g[0].reshape(HG_HEADS, 1, HG_DK),
                      w_out_c[0].astype(BF16), row2(ln_mix_g[1]), row2(ln_mix_b[1]), w_route, b_route,
                      alpha)
    plan = _moe_plan(route, META_ROW0)
    y_buf = _moe_experts(h, plan, moe_w_gate_up, moe_w_down, 1)
    out = _moe_combine(h, plan, y_buf, row2(ln_ffn_g[1]), row2(ln_ffn_b[1]), alpha, FRONT)
    return out[None]
```

```python
import functools
import math

import jax
import jax.numpy as jnp
from jax import lax
from jax.experimental import pallas as pl
from jax.experimental.pallas import tpu as pltpu

F32 = jnp.float32
BF16 = jnp.bfloat16

N_META = 16
FRONT = 512
META_ROW0 = FRONT - N_META

ATTN_HEADS = 8
ATTN_KV_HEADS = 2
ATTN_GROUP = ATTN_HEADS // ATTN_KV_HEADS
HEAD_DIM = 128
WINDOW = 128
ATTN_BLOCK = 128
ROPE_THETA = 10000.0
Q_DIM = ATTN_HEADS * HEAD_DIM
KV_DIM = ATTN_KV_HEADS * HEAD_DIM

S5_GROUP = 16
S5_STATE = 64
S5_CHUNK = 16
S5_ROW = S5_CHUNK * S5_GROUP
S5_SCAN_BLOCK = 32
S5_CHUNK_BLOCKS = (176, 96, 48, 32, 16)

HG_HEADS = 16
HG_DK = 128
HG_CHUNK = 128
HG_SUB = 16
HG_NSUB = HG_CHUNK // HG_SUB
HG_EXP_CLAMP = 80.0

N_GROUPS = 8
EXPERTS_PER_GROUP = 8
N_EXPERTS = N_GROUPS * EXPERTS_PER_GROUP
TOP_K = 2
D_EXPERT = 512
MOE_TM = 256
ROUTE_LANES = 128

LN_EPS = 1e-5
RMS_EPS = 1e-6
NEG_INF = -1e30

ROW_TILE = 512
MIX_TILE = 256
MIX_SUB = 128
INPROJ_C_TILES = (1536, 1024, 512)
INPROJ_C_SUB = 512

VMEM_LIMIT = 56 * 1024 * 1024


def _cparams(*sem):
    return pltpu.CompilerParams(dimension_semantics=sem, vmem_limit_bytes=VMEM_LIMIT)


def _pick(n, candidates):
    return next(c for c in candidates if n % c == 0)


def _full(shape):
    nd = len(shape)
    return pl.BlockSpec(shape, lambda *_: (0,) * nd)


def _layer_norm(x, g, b):
    mu = jnp.mean(x, axis=-1, keepdims=True)
    xc = x - mu
    var = jnp.mean(xc * xc, axis=-1, keepdims=True)
    return xc * lax.rsqrt(var + LN_EPS) * g + b


def _inproj_ab_kernel(x_ref, wq_ref, wk_ref, wv_ref, wu_ref, cos_ref, sin_ref,
                      q_ref, k_ref, v_ref, u_ref):
    xb = x_ref[...].astype(BF16)
    cos = cos_ref[...]
    sin = sin_ref[...]

    def rope(t):
        return t * cos + pltpu.roll(t, HEAD_DIM // 2, axis=1) * sin

    q = jnp.dot(xb, wq_ref[...], preferred_element_type=F32)
    for h in range(ATTN_HEADS):
        sl = slice(h * HEAD_DIM, (h + 1) * HEAD_DIM)
        q_ref[:, sl] = rope(q[:, sl]).astype(BF16)
    k = jnp.dot(xb, wk_ref[...], preferred_element_type=F32)
    for h in range(ATTN_KV_HEADS):
        sl = slice(h * HEAD_DIM, (h + 1) * HEAD_DIM)
        k_ref[:, sl] = rope(k[:, sl]).astype(BF16)
    v_ref[...] = jnp.dot(xb, wv_ref[...], preferred_element_type=F32).astype(BF16)
    u_ref[...] = jnp.dot(xb, wu_ref[...], preferred_element_type=F32)


def _inproj_ab(h, wq, wk, wv, wu, cos2, sin2):
    lp, d = h.shape
    tm = ROW_TILE
    s5w = wu.shape[1]
    row = lambda w: pl.BlockSpec((tm, w), lambda i: (i, 0))
    return pl.pallas_call(
        _inproj_ab_kernel,
        grid=(lp // tm,),
        in_specs=[row(d), _full(wq.shape), _full(wk.shape), _full(wv.shape), _full(wu.shape),
                  row(HEAD_DIM), row(HEAD_DIM)],
        out_specs=[row(Q_DIM), row(KV_DIM), row(KV_DIM), row(s5w)],
        out_shape=[jax.ShapeDtypeStruct((lp, Q_DIM), BF16),
                   jax.ShapeDtypeStruct((lp, KV_DIM), BF16),
                   jax.ShapeDtypeStruct((lp, KV_DIM), BF16),
                   jax.ShapeDtypeStruct((lp, s5w), F32)],
        compiler_params=_cparams("parallel"),
        name="inproj_ab",
    )(h, wq, wk, wv, wu, cos2, sin2)


def _attn_kernel(sink_ref, q_ref, k0_ref, k1_ref, k2_ref, v0_ref, v1_ref, v2_ref, km_ref, vm_ref,
                 o_ref, *, lp):
    qb = pl.program_id(0)
    blk = ATTN_BLOCK
    rows = ATTN_GROUP * blk
    scale = HEAD_DIM ** -0.5
    q_row = qb * blk + lax.broadcasted_iota(jnp.int32, (rows, 3 * blk), 0) % blk
    k_row = (qb - 1) * blk + lax.broadcasted_iota(jnp.int32, (rows, 3 * blk), 1)
    vis = (k_row >= FRONT) & (k_row < lp) & (jnp.abs(q_row - k_row) <= WINDOW)
    head_of_row = lax.broadcasted_iota(jnp.int32, (rows, 1), 0) // blk
    for g in range(ATTN_KV_HEADS):
        gs = slice(g * HEAD_DIM, (g + 1) * HEAD_DIM)
        qs = jnp.concatenate(
            [q_ref[:, (g * ATTN_GROUP + r) * HEAD_DIM:(g * ATTN_GROUP + r + 1) * HEAD_DIM]
             for r in range(ATTN_GROUP)], axis=0)
        kband = jnp.concatenate([k0_ref[:, gs], k1_ref[:, gs], k2_ref[:, gs]], axis=0)
        vband = jnp.concatenate([v0_ref[:, gs], v1_ref[:, gs], v2_ref[:, gs]], axis=0)
        nt = (((1,), (1,)), ((), ()))
        s_band = lax.dot_general(qs, kband, nt, preferred_element_type=F32) * scale
        s_band = jnp.where(vis, s_band, NEG_INF)
        s_meta = lax.dot_general(qs, km_ref[:, gs], nt, preferred_element_type=F32) * scale
        sink = jnp.zeros((rows, 1), F32)
        for r in range(ATTN_GROUP):
            sink = jnp.where(head_of_row == r, sink_ref[g * ATTN_GROUP + r], sink)
        m = jnp.maximum(jnp.maximum(jnp.max(s_band, axis=-1, keepdims=True),
                                    jnp.max(s_meta, axis=-1, keepdims=True)), sink)
        e_band = jnp.exp(s_band - m)
        e_meta = jnp.exp(s_meta - m)
        denom = (jnp.sum(e_band, axis=-1, keepdims=True) + jnp.sum(e_meta, axis=-1, keepdims=True)
                 + jnp.exp(sink - m))
        o = (jnp.dot(e_band.astype(BF16), vband, preferred_element_type=F32)
             + jnp.dot(e_meta.astype(BF16), vm_ref[:, gs], preferred_element_type=F32)) / denom
        for r in range(ATTN_GROUP):
            hh = g * ATTN_GROUP + r
            o_ref[:, hh * HEAD_DIM:(hh + 1) * HEAD_DIM] = o[r * blk:(r + 1) * blk].astype(BF16)


def _window_attention(q, k, v, sinks):
    lp = q.shape[0]
    blk = ATTN_BLOCK
    nb = lp // blk
    qspec = pl.BlockSpec((blk, Q_DIM), lambda i, s: (i, 0))
    kv = lambda off: pl.BlockSpec((blk, KV_DIM), lambda i, s: (jnp.clip(i + off, 0, nb - 1), 0))
    meta = pl.BlockSpec((N_META, KV_DIM), lambda i, s: (META_ROW0 // N_META, 0))
    return pl.pallas_call(
        functools.partial(_attn_kernel, lp=lp),
        grid_spec=pltpu.PrefetchScalarGridSpec(
            num_scalar_prefetch=1,
            grid=(nb,),
            in_specs=[qspec, kv(-1), kv(0), kv(1), kv(-1), kv(0), kv(1), meta, meta],
            out_specs=qspec,
        ),
        out_shape=jax.ShapeDtypeStruct((lp, Q_DIM), BF16),
        compiler_params=_cparams("parallel"),
        name="window_attention",
    )(sinks, q, k, k, k, v, v, v, k, v)


def _s5_discretise(lam_re, lam_im, log_step, b_re, b_im, c_re, c_im):
    t = S5_CHUNK
    hi = lax.Precision.HIGHEST
    g = lam_re.shape[1]
    lr = jnp.minimum(lam_re, -1e-4)
    li = lam_im
    step = jnp.exp(log_step)[..., None]
    dr, di = lr * step, li * step
    lags = jnp.arange(t + 1, dtype=F32)[:, None]
    mag = jnp.exp(dr[..., None, :] * lags)
    pr, pi = mag * jnp.cos(di[..., None, :] * lags), mag * jnp.sin(di[..., None, :] * lags)
    ar, ai = pr[:, :, 1], pi[:, :, 1]
    den = lr * lr + li * li
    zr = ((ar - 1.0) * lr + ai * li) / den
    zi = (ai * lr - (ar - 1.0) * li) / den
    br = zr[..., None] * b_re - zi[..., None] * b_im
    bi = zr[..., None] * b_im + zi[..., None] * b_re
    xr = c_re[:, :, None] * pr[:, :, :t, None] - c_im[:, :, None] * pi[:, :, :t, None]
    xi = c_re[:, :, None] * pi[:, :, :t, None] + c_im[:, :, None] * pr[:, :, :t, None]
    kern = (jnp.einsum('dglop,dgpi->dgloi', xr, br, precision=hi)
            - jnp.einsum('dglop,dgpi->dgloi', xi, bi, precision=hi))
    s_idx = jnp.arange(t)[:, None]
    t_idx = jnp.arange(t)[None, :]
    lag_fw = t_idx - s_idx
    m_fw = jnp.where((lag_fw >= 0)[None, :, :, None, None],
                     kern[0][:, jnp.clip(lag_fw, 0, t - 1)], 0.0)
    m_bw = jnp.where((lag_fw <= 0)[None, :, :, None, None],
                     kern[1][:, jnp.clip(-lag_fw, 0, t - 1)], 0.0)
    m_mat = (m_fw + m_bw).transpose(0, 1, 4, 2, 3).reshape(g, t * S5_GROUP, t * S5_GROUP)

    def flat_e(p_r, p_i, d):
        e_r = p_r[..., None] * br[d][:, None] - p_i[..., None] * bi[d][:, None]
        e_i = p_r[..., None] * bi[d][:, None] + p_i[..., None] * br[d][:, None]
        flat = lambda e: e.transpose(0, 1, 3, 2).reshape(g, t * S5_GROUP, S5_STATE)
        return jnp.concatenate([flat(e_r), flat(e_i)], axis=-1)
    e_mat = jnp.concatenate([flat_e(pr[0][:, ::-1][:, 1:], pi[0][:, ::-1][:, 1:], 0),
                             flat_e(pr[1][:, :t], pi[1][:, :t], 1)], axis=-1)

    def flat_f(p_r, p_i, d):
        w_r = c_re[d][:, None] * p_r[:, :, None, :] - c_im[d][:, None] * p_i[:, :, None, :]
        w_i = c_re[d][:, None] * p_i[:, :, None, :] + c_im[d][:, None] * p_r[:, :, None, :]
        flat = lambda w: w.reshape(g, t * S5_GROUP, S5_STATE).transpose(0, 2, 1)
        return jnp.concatenate([flat(w_r), -flat(w_i)], axis=1)
    f_fw = flat_f(pr[0][:, 1:], pi[0][:, 1:], 0)
    f_bw = flat_f(pr[1][:, ::-1][:, :t], pi[1][:, ::-1][:, :t], 1)
    gh = g // 2
    upper = (jnp.arange(g) >= gh)[:, None, None]
    def place(m):
        z = jnp.zeros_like(m)
        return jnp.concatenate([jnp.where(upper, z, m), jnp.where(upper, m, z)], axis=1)
    p = S5_STATE
    f_mat = jnp.concatenate([place(f_fw[:, :p]), place(f_fw[:, p:]), place(f_bw[:, :p]), place(f_bw[:, p:])],
                            axis=1)
    pack = lambda v: jnp.concatenate([v[:gh], v[gh:]], axis=-1)
    coef = jnp.stack([pack(pr[0][:, t]), pack(pi[0][:, t]), pack(pr[1][:, t]), pack(pi[1][:, t])])
    split = lambda m: m.astype(BF16).reshape((2, gh) + m.shape[1:])
    return split(m_mat), split(e_mat), split(f_mat), coef


def _chunk_rows_to_lanes(u_ref, ncb):
    lane_seg = lax.broadcasted_iota(jnp.int32, (ncb, 128), 1) // S5_GROUP
    rows = [u_ref[pl.ds(s, ncb, stride=S5_CHUNK), :] for s in range(S5_CHUNK)]
    segs = 128 // S5_GROUP
    out = []
    for j in range(segs):
        halves = []
        for h in range(S5_CHUNK // segs):
            acc = None
            for s8 in range(segs):
                r = rows[h * segs + s8]
                k = (s8 - j) % segs
                if k:
                    r = pltpu.roll(r, S5_GROUP * k, axis=1)
                acc = r if acc is None else jnp.where(lane_seg == s8, r, acc)
            halves.append(acc)
        out.append(jnp.concatenate(halves, axis=1))
    return out


def _lanes_to_chunk_rows(ys, y_ref, ncb, add_ref, scale_ref):
    lane_seg = lax.broadcasted_iota(jnp.int32, (ncb, 128), 1) // S5_GROUP
    segs = 128 // S5_GROUP
    for t in range(S5_CHUNK):
        h, t8 = divmod(t, segs)
        acc = None
        for j in range(segs):
            r = ys[j][:, h * 128:(h + 1) * 128]
            k = (j - t8) % segs
            if k:
                r = pltpu.roll(r, S5_GROUP * k, axis=1)
            acc = r if acc is None else jnp.where(lane_seg == j, r, acc)
        idx = pl.ds(t, ncb, stride=S5_CHUNK)
        y_ref[idx, :] = acc + add_ref[idx, :] * scale_ref[...]


def _pair_halves(a, b):
    lane = lax.broadcasted_iota(jnp.int32, a.shape, 1)
    lo = jnp.where(lane < S5_STATE, a, pltpu.roll(b, S5_STATE, axis=1))
    hi = jnp.where(lane < S5_STATE, pltpu.roll(a, S5_STATE, axis=1), b)
    return lo, hi


def _s5_local_kernel(ua_ref, ub_ref, e_ref, u2_ref, f1_ref, f2_ref, b1_ref, b2_ref):
    ncb = u2_ref.shape[2]
    p2 = 2 * S5_STATE
    chunks = [_chunk_rows_to_lanes(ua_ref, ncb), _chunk_rows_to_lanes(ub_ref, ncb)]
    for j in range(u2_ref.shape[1]):
        st = []
        for hh in range(2):
            u2 = chunks[hh][j].astype(BF16)
            u2_ref[hh, j] = u2
            st.append(jnp.dot(u2, e_ref[hh, j], preferred_element_type=F32))
        f1_ref[:, j, :], f2_ref[:, j, :] = _pair_halves(st[0][:, :p2], st[1][:, :p2])
        b1_ref[:, j, :], b2_ref[:, j, :] = _pair_halves(st[0][:, p2:], st[1][:, p2:])


def _s5_scan_kernel(sf1_ref, sf2_ref, sb1_ref, sb2_ref, coef_ref, xf1_ref, xf2_ref, xb1_ref, xb2_ref,
                    st_ref):
    @pl.when(pl.program_id(0) == 0)
    def _():
        st_ref[...] = jnp.zeros_like(st_ref)

    nb = sf1_ref.shape[0]
    crf, cif, crb, cib = coef_ref[0], coef_ref[1], coef_ref[2], coef_ref[3]

    def body(i, carry):
        f1, f2, b1, b2 = carry
        j = nb - 1 - i
        xf1_ref[i] = f1
        xf2_ref[i] = f2
        xb1_ref[j] = b1
        xb2_ref[j] = b2
        return (crf * f1 - cif * f2 + sf1_ref[i], crf * f2 + cif * f1 + sf2_ref[i],
                crb * b1 - cib * b2 + sb1_ref[j], crb * b2 + cib * b1 + sb2_ref[j])

    out = lax.fori_loop(0, nb, body, (st_ref[0], st_ref[1], st_ref[2], st_ref[3]))
    for q in range(4):
        st_ref[q] = out[q]


def _s5_readout_kernel(ua_ref, ub_ref, d_ref, u2_ref, m_ref, f_ref, xf1_ref, xf2_ref, xb1_ref, xb2_ref,
                       ya_ref, yb_ref):
    ncb = u2_ref.shape[2]
    ys = [[], []]
    for j in range(u2_ref.shape[1]):
        x = jnp.concatenate([xf1_ref[:, j, :], xf2_ref[:, j, :], xb1_ref[:, j, :], xb2_ref[:, j, :]],
                            axis=1).astype(BF16)
        for hh in range(2):
            y = jnp.dot(u2_ref[hh, j], m_ref[hh, j], preferred_element_type=F32)
            ys[hh].append(y + jnp.dot(x, f_ref[hh, j], preferred_element_type=F32))
    _lanes_to_chunk_rows(ys[0], ya_ref, ncb, ua_ref, d_ref.at[0])
    _lanes_to_chunk_rows(ys[1], yb_ref, ncb, ub_ref, d_ref.at[1])


def _s5_mixer_pre_glu(u, d_skip, mats):
    m_mat, e_mat, f_mat, coef = mats
    lp, w = u.shape
    g = w // S5_GROUP
    gh = g // 2
    gpb = 128 // S5_GROUP
    nc = lp // S5_CHUNK
    ncb = _pick(nc, S5_CHUNK_BLOCKS)
    rows = ncb * S5_CHUNK
    nlb = w // 128
    grid = (nlb // 2, nc // ncb)
    u_lo = pl.BlockSpec((rows, 128), lambda b, r: (r, b))
    u_hi = pl.BlockSpec((rows, 128), lambda b, r: (r, b + nlb // 2))
    per_group = lambda k, n: pl.BlockSpec((2, gpb, k, n), lambda b, r: (0, b, 0, 0))
    u2_spec = pl.BlockSpec((2, gpb, ncb, S5_ROW), lambda b, r: (0, b, r, 0))
    st_spec = pl.BlockSpec((ncb, gpb, 2 * S5_STATE), lambda b, r: (r, b, 0))
    st_shape = jax.ShapeDtypeStruct((nc, gh, 2 * S5_STATE), F32)
    u2, sf1, sf2, sb1, sb2 = pl.pallas_call(
        _s5_local_kernel,
        grid=grid,
        in_specs=[u_lo, u_hi, per_group(S5_ROW, 4 * S5_STATE)],
        out_specs=[u2_spec, st_spec, st_spec, st_spec, st_spec],
        out_shape=[jax.ShapeDtypeStruct((2, gh, nc, S5_ROW), BF16)] + [st_shape] * 4,
        compiler_params=_cparams("parallel", "parallel"),
        name="s5_local_states",
    )(u, u, e_mat)
    sb = S5_SCAN_BLOCK
    nblk = nc // sb
    fwd = pl.BlockSpec((sb, gh, 2 * S5_STATE), lambda i: (i, 0, 0))
    bwd = pl.BlockSpec((sb, gh, 2 * S5_STATE), lambda i: (nblk - 1 - i, 0, 0))
    xf1, xf2, xb1, xb2 = pl.pallas_call(
        _s5_scan_kernel,
        grid=(nblk,),
        in_specs=[fwd, fwd, bwd, bwd, _full(coef.shape)],
        out_specs=[fwd, fwd, bwd, bwd],
        out_shape=[st_shape] * 4,
        scratch_shapes=[pltpu.VMEM((4, gh, 2 * S5_STATE), F32)],
        compiler_params=_cparams("arbitrary"),
        name="s5_chunk_scan",
    )(sf1, sf2, sb1, sb2, coef)
    d2 = d_skip.reshape(2, 1, nlb // 2 * 128)
    y_lo, y_hi = pl.pallas_call(
        _s5_readout_kernel,
        grid=grid,
        in_specs=[u_lo, u_hi, pl.BlockSpec((2, 1, 128), lambda b, r: (0, 0, b)), u2_spec,
                  per_group(S5_ROW, S5_ROW), per_group(8 * S5_STATE, S5_ROW),
                  st_spec, st_spec, st_spec, st_spec],
        out_specs=[pl.BlockSpec((rows, 128), lambda b, r: (r, b))] * 2,
        out_shape=[jax.ShapeDtypeStruct((lp, w // 2), F32)] * 2,
        compiler_params=_cparams("parallel", "parallel"),
        name="s5_readout",
    )(u, u, d2, u2, m_mat, f_mat, xf1, xf2, xb1, xb2)
    return y_lo, y_hi


def _route(hn, wr_ref, br_ref):
    hn_hi = hn.astype(BF16)
    hn_lo = (hn - hn_hi.astype(F32)).astype(BF16)
    p_hi = jnp.dot(hn_hi, wr_ref[...], preferred_element_type=F32)
    p_lo = jnp.dot(hn_lo, wr_ref[:, :ROUTE_LANES], preferred_element_type=F32)
    logits = p_hi[:, :ROUTE_LANES] + (p_hi[:, ROUTE_LANES:] + p_lo) + br_ref[...]
    lane_i = lax.broadcasted_iota(jnp.int32, logits.shape, 1)
    lane = lane_i.astype(F32)
    lane_grp = ((lane_i - N_GROUPS) // EXPERTS_PER_GROUP).astype(F32)
    big = float(ROUTE_LANES)
    g_log = jnp.where(lane_i < N_GROUPS, logits, -jnp.inf)
    g_max = jnp.max(g_log, axis=-1, keepdims=True)
    grp = jnp.min(jnp.where(g_log == g_max, lane, big), axis=-1, keepdims=True)
    p_grp = 1.0 / jnp.sum(jnp.exp(g_log - g_max), axis=-1, keepdims=True)
    in_grp = (lane_i >= N_GROUPS) & (lane_i < N_GROUPS + N_EXPERTS) & (lane_grp == grp)
    e_log = jnp.where(in_grp, logits, -jnp.inf)
    v1 = jnp.max(e_log, axis=-1, keepdims=True)
    i1 = jnp.min(jnp.where(e_log == v1, lane, big), axis=-1, keepdims=True)
    e_log2 = jnp.where(lane == i1, -jnp.inf, e_log)
    v2 = jnp.max(e_log2, axis=-1, keepdims=True)
    i2 = jnp.min(jnp.where(e_log2 == v2, lane, big), axis=-1, keepdims=True)
    e21 = jnp.exp(v2 - v1)
    w1 = p_grp / (1.0 + e21)
    w2 = p_grp * e21 / (1.0 + e21)
    route = jnp.where(lane_i == 0, w1, 0.0)
    route = jnp.where(lane_i == 1, w2, route)
    route = jnp.where(lane_i == 2, i1 - N_GROUPS, route)
    route = jnp.where(lane_i == 3, i2 - N_GROUPS, route)
    return route


def _sub_rows(tm):
    return [slice(r, r + MIX_SUB) for r in range(0, tm, MIX_SUB)]


def _mix_ab_kernel(h_ref, a_ref, ylo_ref, yhi_ref, wglu_ref, bglu_ref, woa_ref, wos_ref,
                   g_ref, b_ref, wr_ref, br_ref, hn_ref, route_ref, *, alpha):
    for rows in _sub_rows(h_ref.shape[0]):
        y = jnp.concatenate([ylo_ref[rows, :], yhi_ref[rows, :]], axis=1)
        z = 0.5 * y * (1.0 + jnp.tanh(math.sqrt(2.0 / math.pi) * (y + 0.044715 * (y * y * y))))
        gate = jnp.dot(z.astype(BF16), wglu_ref[...], preferred_element_type=F32) + bglu_ref[...]
        s_out = z * jax.nn.sigmoid(gate)
        mix = (jnp.dot(a_ref[rows, :], woa_ref[...], preferred_element_type=F32)
               + jnp.dot(s_out.astype(BF16), wos_ref[...], preferred_element_type=F32))
        hn = _layer_norm(alpha * h_ref[rows, :] + mix, g_ref[...], b_ref[...])
        hn_ref[rows, :] = hn
        route_ref[rows, :] = _route(hn, wr_ref, br_ref)


def _mix_ab(h, a_out, y_lo, y_hi, w_glu, b_glu, wo_a, wo_s, ln_g, ln_b, w_route, b_route, alpha):
    lp, d = h.shape
    tm = MIX_TILE
    row = lambda c: pl.BlockSpec((tm, c), lambda i: (i, 0))
    return pl.pallas_call(
        functools.partial(_mix_ab_kernel, alpha=alpha),
        grid=(lp // tm,),
        in_specs=[row(d), row(Q_DIM), row(y_lo.shape[1]), row(y_hi.shape[1]), _full(w_glu.shape),
                  _full(b_glu.shape), _full(wo_a.shape), _full(wo_s.shape), _full(ln_g.shape),
                  _full(ln_b.shape), _full(w_route.shape), _full(b_route.shape)],
        out_specs=[row(d), row(ROUTE_LANES)],
        out_shape=[jax.ShapeDtypeStruct((lp, d), F32), jax.ShapeDtypeStruct((lp, ROUTE_LANES), F32)],
        compiler_params=_cparams("parallel"),
        name="mix_ab",
    )(h, a_out, y_lo, y_hi, w_glu, b_glu, wo_a, wo_s, ln_g, ln_b, w_route, b_route)


def _mix_c_kernel(h_ref, ofw_ref, obw_ref, gate_ref, ng_ref, wo_ref, g_ref, b_ref, wr_ref, br_ref,
                  hn_ref, route_ref, *, alpha):
    for rows in _sub_rows(h_ref.shape[0]):
        parts = []
        for hh in range(HG_HEADS):
            o = ofw_ref[hh, rows, :] + obw_ref[hh, rows, :]
            o = o * lax.rsqrt(jnp.mean(o * o, axis=-1, keepdims=True) + RMS_EPS)
            o = o * ng_ref[hh] * jax.nn.sigmoid(gate_ref[0, hh, rows, :].astype(F32))
            parts.append(o.astype(BF16))
        mix = jnp.dot(jnp.concatenate(parts, axis=1), wo_ref[...], preferred_element_type=F32)
        hn = _layer_norm(alpha * h_ref[rows, :] + mix, g_ref[...], b_ref[...])
        hn_ref[rows, :] = hn
        route_ref[rows, :] = _route(hn, wr_ref, br_ref)


def _mix_c(h, o_fw, o_bw, plain, norm_g, wo, ln_g, ln_b, w_route, b_route, alpha):
    lp, d = h.shape
    tm = MIX_TILE
    row = lambda c: pl.BlockSpec((tm, c), lambda i: (i, 0))
    hm = pl.BlockSpec((HG_HEADS, tm, HG_DK), lambda i: (0, i, 0))
    gate = pl.BlockSpec((1, HG_HEADS, tm, HG_DK), lambda i: (2, 0, i, 0))
    return pl.pallas_call(
        functools.partial(_mix_c_kernel, alpha=alpha),
        grid=(lp // tm,),
        in_specs=[row(d), hm, hm, gate, _full(norm_g.shape), _full(wo.shape), _full(ln_g.shape),
                  _full(ln_b.shape), _full(w_route.shape), _full(b_route.shape)],
        out_specs=[row(d), row(ROUTE_LANES)],
        out_shape=[jax.ShapeDtypeStruct((lp, d), F32), jax.ShapeDtypeStruct((lp, ROUTE_LANES), F32)],
        compiler_params=_cparams("parallel"),
        name="mix_c",
    )(h, o_fw, o_bw, plain, norm_g, wo, ln_g, ln_b, w_route, b_route)


def _moe_plan(route, n_rows_valid_from):
    lp = route.shape[0]
    tm = MOE_TM
    valid = (jnp.arange(lp) >= n_rows_valid_from)
    expert = route[:, 2:4].astype(jnp.int32)
    flat_e = jnp.where(valid[:, None], expert, N_EXPERTS).reshape(-1)
    onehot = (flat_e[:, None] == jnp.arange(N_EXPERTS)[None, :]).astype(jnp.int32)
    csum = jnp.cumsum(onehot, axis=0)
    counts = csum[-1]
    rank = jnp.sum(jnp.where(onehot > 0, csum - 1, 0), axis=1)
    padded = (counts + tm - 1) // tm * tm
    pend = jnp.cumsum(padded)
    pstart = pend - padded
    e_safe = jnp.minimum(flat_e, N_EXPERTS - 1)
    dest = jnp.where(flat_e < N_EXPERTS, pstart[e_safe] + rank, 0)
    n_tokens = lp - n_rows_valid_from
    n_blocks = -(-(n_tokens * TOP_K + N_EXPERTS * (tm - 1)) // tm)
    rows = n_blocks * tm
    tok = jnp.repeat(jnp.arange(lp, dtype=jnp.int32), TOP_K)
    scatter_to = jnp.where(flat_e < N_EXPERTS, dest, rows)
    tok_of_row = jnp.zeros((rows,), jnp.int32).at[scatter_to].set(tok, mode='drop')
    n_used = (pend[-1] // tm).astype(jnp.int32)
    blk = jnp.minimum(jnp.arange(n_blocks, dtype=jnp.int32), jnp.maximum(n_used - 1, 0))
    block_expert = jnp.minimum(jnp.searchsorted(pend, blk * tm, side='right'),
                               N_EXPERTS - 1).astype(jnp.int32)
    used = counts > 0
    pos = jnp.cumsum(used.astype(jnp.int32)) - 1
    eid = jnp.arange(N_EXPERTS, dtype=jnp.int32)
    later = jnp.where(used[None, :] & (eid[None, :] > eid[:, None]), eid[None, :], N_EXPERTS)
    nxt_e = jnp.min(later, axis=1)
    nxt_e = jnp.where(nxt_e == N_EXPERTS, -1, nxt_e).astype(jnp.int32)
    w_slot = (pos[block_expert] % 2).astype(jnp.int32)
    nxt = nxt_e[block_expert]
    weights = jnp.where(valid[:, None], route[:, 0:2], 0.0)
    return (block_expert, n_used.reshape(1), w_slot, nxt, tok_of_row.reshape(n_blocks, 1, tm),
            dest.reshape(lp, TOP_K).astype(jnp.int32), weights)


def _moe_expert_kernel(be_ref, nu_ref, ws_ref, nx_ref, tok0_ref, tokn_ref, h_hbm, wgu_hbm, wd_hbm,
                       y_ref, xbuf, xsem, wgu_f, wd_f, wsem, wgu_bf, wd_bf, *, layer):
    i = pl.program_id(0)
    tm = xbuf.shape[1]
    n_used = nu_ref[0]

    def gather(tok_ref, slot):
        def issue(r, carry):
            pltpu.make_async_copy(h_hbm.at[tok_ref[0, 0, r]], xbuf.at[slot, r], xsem.at[slot]).start()
            return carry
        lax.fori_loop(0, tm, issue, 0, unroll=8)

    def gather_wait(slot):
        pltpu.make_async_copy(h_hbm.at[pl.ds(0, tm)], xbuf.at[slot], xsem.at[slot]).wait()

    def weight_copies(e, s):
        return (pltpu.make_async_copy(wgu_hbm.at[layer, e], wgu_f.at[s], wsem.at[0, s]),
                pltpu.make_async_copy(wd_hbm.at[layer, e], wd_f.at[s], wsem.at[1, s]))

    @pl.when(i == 0)
    def _():
        for cp in weight_copies(be_ref[0], 0):
            cp.start()
        gather(tok0_ref, 0)

    @pl.when(i < n_used)
    def _():
        slot = i % 2
        e = be_ref[i]
        first = jnp.logical_or(i == 0, be_ref[jnp.maximum(i - 1, 0)] != e)

        @pl.when(first)
        def _():
            s = ws_ref[i]
            for cp in weight_copies(e, s):
                cp.wait()
            nxt = nx_ref[i]

            @pl.when(nxt >= 0)
            def _():
                for cp in weight_copies(nxt, 1 - s):
                    cp.start()

            wgu_bf[...] = wgu_f[s].astype(BF16)
            wd_bf[...] = wd_f[s].astype(BF16)

        gather_wait(slot)
        gather(tokn_ref, 1 - slot)
        hgu = jnp.dot(xbuf[slot].astype(BF16), wgu_bf[...], preferred_element_type=F32)
        act = jax.nn.silu(hgu[:, :D_EXPERT]) * hgu[:, D_EXPERT:]
        y_ref[...] = jnp.dot(act.astype(BF16), wd_bf[...], preferred_element_type=F32)

        @pl.when(i == n_used - 1)
        def _():
            gather_wait(1 - slot)

    @pl.when(i >= n_used)
    def _():
        y_ref[...] = jnp.zeros_like(y_ref)


def _moe_experts(h, plan, w_gate_up, w_down, layer):
    block_expert, n_used, w_slot, nxt, tok_of_row, _, _ = plan
    lp, d = h.shape
    tm = MOE_TM
    n_blocks = tok_of_row.shape[0]
    nxt_blk = lambda i, be, nu, ws, nx: (jnp.minimum(i + 1, jnp.maximum(nu[0] - 1, 0)), 0, 0)
    smem = lambda imap: pl.BlockSpec((1, 1, tm), imap, memory_space=pltpu.SMEM)
    anyspace = pl.BlockSpec(memory_space=pl.ANY)
    return pl.pallas_call(
        functools.partial(_moe_expert_kernel, layer=layer),
        grid_spec=pltpu.PrefetchScalarGridSpec(
            num_scalar_prefetch=4,
            grid=(n_blocks,),
            in_specs=[smem(lambda i, be, nu, ws, nx: (0, 0, 0)), smem(nxt_blk),
                      anyspace, anyspace, anyspace],
            out_specs=pl.BlockSpec((tm, d), lambda i, be, nu, ws, nx: (i, 0)),
            scratch_shapes=[pltpu.VMEM((2, tm, d), F32), pltpu.SemaphoreType.DMA((2,)),
                            pltpu.VMEM((2, d, 2 * D_EXPERT), F32), pltpu.VMEM((2, D_EXPERT, d), F32),
                            pltpu.SemaphoreType.DMA((2, 2)),
                            pltpu.VMEM((d, 2 * D_EXPERT), BF16), pltpu.VMEM((D_EXPERT, d), BF16)],
        ),
        out_shape=jax.ShapeDtypeStruct((n_blocks * tm, d), F32),
        compiler_params=_cparams("arbitrary"),
        name="moe_experts",
    )(block_expert, n_used, w_slot, nxt, tok_of_row, tok_of_row, h, w_gate_up, w_down)


def _moe_combine_kernel(d0_ref, d1_ref, d0n_ref, d1n_ref, h_ref, w_ref, y_hbm, g_ref, b_ref, *rest,
                        alpha, with_bf16):
    if with_bf16:
        o_ref, ob_ref, buf, sem = rest
    else:
        o_ref, buf, sem = rest
    i = pl.program_id(0)
    tm = buf.shape[2]
    slot = i % 2

    def gather(a_ref, b_ref_, s):
        def issue(r, carry):
            pltpu.make_async_copy(y_hbm.at[a_ref[0, 0, r]], buf.at[s, 0, r], sem.at[s, 0]).start()
            pltpu.make_async_copy(y_hbm.at[b_ref_[0, 0, r]], buf.at[s, 1, r], sem.at[s, 1]).start()
            return carry
        lax.fori_loop(0, tm, issue, 0, unroll=8)

    @pl.when(i == 0)
    def _():
        gather(d0_ref, d1_ref, 0)

    for k in range(TOP_K):
        pltpu.make_async_copy(y_hbm.at[pl.ds(0, tm)], buf.at[slot, k], sem.at[slot, k]).wait()

    @pl.when(i + 1 < pl.num_programs(0))
    def _():
        gather(d0n_ref, d1n_ref, 1 - slot)

    w = w_ref[...]
    ffn = w[:, 0:1] * buf[slot, 0] + w[:, 1:2] * buf[slot, 1]
    out = _layer_norm(alpha * h_ref[...] + ffn, g_ref[...], b_ref[...])
    o_ref[...] = out
    if with_bf16:
        ob_ref[...] = out.astype(BF16)


def _moe_combine(h, plan, y_buf, ln_g, ln_b, alpha, first_row, with_bf16=False):
    dest, weights = plan[-2], plan[-1]
    lp, d = h.shape
    tm = MIX_TILE
    off = first_row // tm
    nt = lp // tm
    d0 = dest[:, 0].reshape(nt, 1, tm)
    d1 = dest[:, 1].reshape(nt, 1, tm)
    cur = pl.BlockSpec((1, 1, tm), lambda i: (off, 0, 0), memory_space=pltpu.SMEM)
    nxt = pl.BlockSpec((1, 1, tm), lambda i: (jnp.minimum(i + 1 + off, nt - 1), 0, 0),
                       memory_space=pltpu.SMEM)
    out_spec = pl.BlockSpec((tm, d), lambda i: (i, 0))
    out_shape = jax.ShapeDtypeStruct((lp - first_row, d), F32)
    if with_bf16:
        out_spec = [out_spec, out_spec]
        out_shape = [out_shape, jax.ShapeDtypeStruct((lp - first_row, d), BF16)]
    return pl.pallas_call(
        functools.partial(_moe_combine_kernel, alpha=alpha, with_bf16=with_bf16),
        grid=(nt - off,),
        in_specs=[cur, cur, nxt, nxt, pl.BlockSpec((tm, d), lambda i: (i + off, 0)),
                  pl.BlockSpec((tm, TOP_K), lambda i: (i + off, 0)),
                  pl.BlockSpec(memory_space=pl.ANY), _full(ln_g.shape), _full(ln_b.shape)],
        out_specs=out_spec,
        out_shape=out_shape,
        scratch_shapes=[pltpu.VMEM((2, TOP_K, tm, d), F32), pltpu.SemaphoreType.DMA((2, TOP_K))],
        compiler_params=_cparams("arbitrary"),
        name="moe_combine",
    )(d0, d1, d0, d1, h, weights, y_buf, ln_g, ln_b)


def _store_heads(ref, seg_rows, val):
    for hh in range(val.shape[1] // HG_DK):
        ref[0, hh, seg_rows, :] = val[:, hh * HG_DK:(hh + 1) * HG_DK].astype(ref.dtype)


def _inproj_plain_kernel(x_ref, w_ref, o_ref):
    for r in range(0, x_ref.shape[0], INPROJ_C_SUB):
        rows = slice(r, r + INPROJ_C_SUB)
        _store_heads(o_ref, rows, jnp.dot(x_ref[rows, :], w_ref[...], preferred_element_type=F32))


def _inproj_forget_kernel(x_ref, w_ref, lb_ref, k_ref, g_ref):
    i = pl.program_id(0)
    tm = x_ref.shape[0]
    lb = lb_ref[...]
    for r in range(0, tm, INPROJ_C_SUB):
        rows = slice(r, r + INPROJ_C_SUB)
        acc = jnp.dot(x_ref[rows, :], w_ref[...], preferred_element_type=F32)
        f = lb + (1.0 - lb) * jax.nn.sigmoid(acc)
        real = (i * tm + r + lax.broadcasted_iota(jnp.int32, (INPROJ_C_SUB, 1), 0)) >= META_ROW0
        _store_heads(k_ref, rows, jnp.where(real, 1.0 - f, 0.0))
        _store_heads(g_ref, rows, jnp.where(real, jnp.log(f), 0.0))


def _inproj_c(hb, w, lb):
    lp, d = hb.shape
    tm = _pick(lp, INPROJ_C_TILES)
    tn = 512
    tps = d // tn
    hpt = tn // HG_DK
    x_spec = pl.BlockSpec((tm, d), lambda i, j: (i, 0))
    out_spec = pl.BlockSpec((1, hpt, tm, HG_DK), lambda i, j: (j // tps, j % tps, i, 0))
    hm = lambda n, dt: jax.ShapeDtypeStruct((n, HG_HEADS, lp, HG_DK), dt)
    plain_col = lambda i, j: (0, jnp.where(j >= 2 * tps, j + 2 * tps, j))
    plain = pl.pallas_call(
        _inproj_plain_kernel,
        grid=(lp // tm, 3 * tps),
        in_specs=[x_spec, pl.BlockSpec((d, tn), plain_col)],
        out_specs=out_spec,
        out_shape=hm(3, BF16),
        compiler_params=_cparams("parallel", "arbitrary"),
        name="inproj_c_plain",
    )(hb, w)
    k, g = pl.pallas_call(
        _inproj_forget_kernel,
        grid=(lp // tm, 2 * tps),
        in_specs=[x_spec, pl.BlockSpec((d, tn), lambda i, j: (0, j + 2 * tps)),
                  pl.BlockSpec((1, tn), lambda i, j: (0, j % tps))],
        out_specs=[out_spec, out_spec],
        out_shape=[hm(2, BF16), hm(2, F32)],
        compiler_params=_cparams("parallel", "arbitrary"),
        name="inproj_c_forget",
    )(hb, w, lb)
    return plain, k, g


def _hgrn_chunk(q, k, v, g, state_t, reverse):
    c = HG_CHUNK
    row = lax.broadcasted_iota(jnp.int32, (c, HG_DK), 0)
    b = g
    sh = 1
    while sh < c:
        if not reverse:
            b = b + jnp.where(row >= sh, pltpu.roll(b, sh, axis=0), 0.0)
        else:
            b = b + jnp.where(row < c - sh, pltpu.roll(b, c - sh, axis=0), 0.0)
        sh *= 2
    sub = HG_SUB
    mid = sub // 2
    blocks = lambda x: [x[j * sub:(j + 1) * sub] for j in range(HG_NSUB)]
    ref = [b[j * sub + mid:j * sub + mid + 1, :] for j in range(HG_NSUB)]
    b_end = b[0:1, :] if reverse else b[c - 1:c, :]
    qd, ku, q0, k_end = [], [], [], []
    for j, (qj, kj, bj) in enumerate(zip(blocks(q), blocks(k), blocks(b))):
        dj = bj - ref[j]
        qd_j = qj * jnp.exp(jnp.minimum(dj, HG_EXP_CLAMP))
        ku_j = kj * jnp.exp(jnp.minimum(-dj, HG_EXP_CLAMP))
        qd.append(qd_j.astype(BF16))
        ku.append(ku_j)
        q0.append(qd_j * jnp.exp(ref[j]))
        k_end.append(ku_j * jnp.exp(b_end - ref[j]))
    nt = (((1,), (1,)), ((), ()))
    zero = jnp.zeros((sub, HG_DK), BF16)
    score_rows = []
    for j in range(HG_NSUB):
        parts = []
        for i in range(HG_NSUB):
            if i == j:
                parts.append(ku[i].astype(BF16))
            elif (i > j) if reverse else (i < j):
                parts.append((ku[i] * jnp.exp(ref[j] - ref[i])).astype(BF16))
            else:
                parts.append(zero)
        k_ext = jnp.concatenate(parts, axis=0)
        score_rows.append(lax.dot_general(qd[j], k_ext, nt, preferred_element_type=F32))
    scores = jnp.concatenate(score_rows, axis=0)
    r_i = lax.broadcasted_iota(jnp.int32, (c, c), 0)
    c_i = lax.broadcasted_iota(jnp.int32, (c, c), 1)
    causal = (c_i >= r_i) if reverse else (c_i <= r_i)
    scores = jnp.where(causal, scores, 0.0)
    o = jnp.dot(scores.astype(BF16), v.astype(BF16), preferred_element_type=F32)
    o += lax.dot_general(jnp.concatenate(q0, axis=0).astype(BF16), state_t.astype(BF16), nt,
                         preferred_element_type=F32)
    new_state = jnp.exp(b_end) * state_t + jnp.dot(
        v.T.astype(BF16), jnp.concatenate(k_end, axis=0).astype(BF16), preferred_element_type=F32)
    return o, new_state


def _hgrn_kernel(qf_ref, vf_ref, k1_ref, g1_ref, qb_ref, vb_ref, k2_ref, g2_ref,
                 ofw_ref, obw_ref, st_ref):
    @pl.when(pl.program_id(0) == 0)
    def _():
        st_ref[...] = jnp.zeros_like(st_ref)

    def head(hh, carry):
        o, s = _hgrn_chunk(qf_ref[0, hh].astype(F32), k1_ref[0, hh].astype(F32),
                           vf_ref[0, hh].astype(F32), g1_ref[0, hh], st_ref[0, hh], reverse=False)
        ofw_ref[hh] = o
        st_ref[0, hh] = s
        o, s = _hgrn_chunk(qb_ref[0, hh].astype(F32), k2_ref[0, hh].astype(F32),
                           vb_ref[0, hh].astype(F32), g2_ref[0, hh], st_ref[1, hh], reverse=True)
        obw_ref[hh] = o
        st_ref[1, hh] = s
        return carry

    lax.fori_loop(0, HG_HEADS, head, 0, unroll=2)


def _hgrn(plain, k, g):
    _, _, lp, dk = plain.shape
    c = HG_CHUNK
    nc = lp // c
    fwd = lambda seg: pl.BlockSpec((1, HG_HEADS, c, dk), lambda i: (seg, 0, i, 0))
    bwd = lambda seg: pl.BlockSpec((1, HG_HEADS, c, dk), lambda i: (seg, 0, nc - 1 - i, 0))
    out_f = pl.BlockSpec((HG_HEADS, c, dk), lambda i: (0, i, 0))
    out_b = pl.BlockSpec((HG_HEADS, c, dk), lambda i: (0, nc - 1 - i, 0))
    return pl.pallas_call(
        _hgrn_kernel,
        grid=(nc,),
        in_specs=[fwd(0), fwd(1), fwd(0), fwd(0), bwd(0), bwd(1), bwd(1), bwd(1)],
        out_specs=[out_f, out_b],
        out_shape=[jax.ShapeDtypeStruct((HG_HEADS, lp, dk), F32)] * 2,
        scratch_shapes=[pltpu.VMEM((2, HG_HEADS, dk, dk), F32)],
        compiler_params=_cparams("arbitrary"),
        name="hgrn2_recurrence",
    )(plain, plain, k, g, plain, plain, k, g)


def _route_params(w_group, b_group, w_expert, b_expert):
    d = w_group.shape[0]
    pad = ROUTE_LANES - N_GROUPS - N_EXPERTS
    w = jnp.concatenate([w_group, w_expert, jnp.zeros((d, pad), F32)], axis=1)
    b = jnp.concatenate([b_group, b_expert, jnp.zeros((pad,), F32)])[None, :]
    w_hi = w.astype(BF16)
    w_lo = (w - w_hi.astype(F32)).astype(BF16)
    return jnp.concatenate([w_hi, w_lo], axis=1), b


def _rope_tables(lp):
    half = HEAD_DIM // 2
    pos = jnp.maximum(jnp.arange(lp) - META_ROW0, 0).astype(F32)
    inv = ROPE_THETA ** (-jnp.arange(half, dtype=F32) * 2.0 / HEAD_DIM)
    ang = pos[:, None] * inv[None, :]
    cos, sin = jnp.cos(ang), jnp.sin(ang)
    return jnp.concatenate([cos, cos], axis=1), jnp.concatenate([-sin, sin], axis=1)


def kernel(x, meta_tokens, w_in_ab, w_out_ab, attn_sinks, s5_lam_re, s5_lam_im, s5_log_step, s5_b_re, s5_b_im, s5_c_re, s5_c_im, s5_d, s5_w_glu, s5_b_glu, w_in_c, w_out_c, hgrn_lb_logits, hgrn_norm_g, ln_mix_g, ln_mix_b, ln_ffn_g, ln_ffn_b, moe_w_group, moe_b_group, moe_w_expert, moe_b_expert, moe_w_gate_up, moe_w_down):
    batch, seq, d = x.shape
    assert batch == 1 and seq % FRONT == 0
    depth = ln_mix_g.shape[0]
    assert depth == 2
    alpha = (2.0 * depth) ** 0.25
    lp = FRONT + seq
    row2 = lambda t: t[None, :]

    h = jnp.concatenate([jnp.zeros((META_ROW0, d), F32), meta_tokens.astype(F32), x[0]], axis=0)

    w_in = w_in_ab[0].astype(BF16)
    s5w = s5_d.shape[1]
    wq, wk, wv, wu = (w_in[:, :Q_DIM], w_in[:, Q_DIM:Q_DIM + KV_DIM],
                      w_in[:, Q_DIM + KV_DIM:Q_DIM + 2 * KV_DIM], w_in[:, Q_DIM + 2 * KV_DIM:])
    cos2, sin2 = _rope_tables(lp)
    q, k, v, u = _inproj_ab(h, wq, wk, wv, wu, cos2, sin2)
    a_out = _window_attention(q, k, v, attn_sinks[0])
    mats = _s5_discretise(s5_lam_re[0], s5_lam_im[0], s5_log_step[0], s5_b_re[0], s5_b_im[0],
                          s5_c_re[0], s5_c_im[0])
    y_lo, y_hi = _s5_mixer_pre_glu(u, s5_d[0], mats)
    wo = w_out_ab[0].astype(BF16)
    w_route, b_route = _route_params(moe_w_group[0], moe_b_group[0], moe_w_expert[0], moe_b_expert[0])
    h, route = _mix_ab(h, a_out, y_lo, y_hi, s5_w_glu[0].astype(BF16), row2(s5_b_glu[0]),
                       wo[:Q_DIM], wo[Q_DIM:], row2(ln_mix_g[0]), row2(ln_mix_b[0]), w_route, b_route,
                       alpha)
    plan = _moe_plan(route, META_ROW0)
    y_buf = _moe_experts(h, plan, moe_w_gate_up, moe_w_down, 0)
    h, hb = _moe_combine(h, plan, y_buf, row2(ln_ffn_g[0]), row2(ln_ffn_b[0]), alpha, 0, with_bf16=True)

    lb_probs = jax.nn.softmax(hgrn_lb_logits.astype(F32), axis=0)
    lb = (jnp.cumsum(lb_probs, axis=0) - lb_probs[0])[1]
    plain, hk, hg = _inproj_c(hb, w_in_c[0].astype(BF16), row2(lb))
    o_fw, o_bw = _hgrn(plain, hk, hg)
    w_route, b_route = _route_params(moe_w_group[1], moe_b_group[1], moe_w_expert[1], moe_b_expert[1])
    h, route = _mix_c(h, o_fw, o_bw, plain, hgrn_norm_g[0].reshape(HG_HEADS, 1, HG_DK),
                      w_out_c[0].astype(BF16), row2(ln_mix_g[1]), row2(ln_mix_b[1]), w_route, b_route,
                      alpha)
    plan = _moe_plan(route, META_ROW0)
    y_buf = _moe_experts(h, plan, moe_w_gate_up, moe_w_down, 1)
    out = _moe_combine(h, plan, y_buf, row2(ln_ffn_g[1]), row2(ln_ffn_b[1]), alpha, FRONT)
    return out[None]
```

```python
import functools
import math

import jax
import jax.numpy as jnp
from jax import lax
from jax.experimental import pallas as pl
from jax.experimental.pallas import tpu as pltpu

F32 = jnp.float32
BF16 = jnp.bfloat16

N_META = 16
FRONT = 512
META_ROW0 = FRONT - N_META

ATTN_HEADS = 8
ATTN_KV_HEADS = 2
ATTN_GROUP = ATTN_HEADS // ATTN_KV_HEADS
HEAD_DIM = 128
WINDOW = 128
ATTN_BLOCK = 128
ROPE_THETA = 10000.0
Q_DIM = ATTN_HEADS * HEAD_DIM
KV_DIM = ATTN_KV_HEADS * HEAD_DIM

S5_GROUP = 16
S5_STATE = 64
S5_CHUNK = 16
S5_ROW = S5_CHUNK * S5_GROUP
S5_SCAN_BLOCK = 32
S5_CHUNK_BLOCKS = (176, 96, 48, 32, 16)

HG_HEADS = 16
HG_DK = 128
HG_CHUNK = 128
HG_SUB = 16
HG_NSUB = HG_CHUNK // HG_SUB
HG_EXP_CLAMP = 80.0

N_GROUPS = 8
EXPERTS_PER_GROUP = 8
N_EXPERTS = N_GROUPS * EXPERTS_PER_GROUP
TOP_K = 2
D_EXPERT = 512
MOE_TM = 256
ROUTE_LANES = 128

LN_EPS = 1e-5
RMS_EPS = 1e-6
NEG_INF = -1e30

ROW_TILE = 512
MIX_TILE = 256
MIX_SUB = 128
INPROJ_C_TILES = (1536, 1024, 512)
INPROJ_C_SUB = 512

VMEM_LIMIT = 56 * 1024 * 1024


def _cparams(*sem):
    return pltpu.CompilerParams(dimension_semantics=sem, vmem_limit_bytes=VMEM_LIMIT)


def _pick(n, candidates):
    return next(c for c in candidates if n % c == 0)


def _full(shape):
    nd = len(shape)
    return pl.BlockSpec(shape, lambda *_: (0,) * nd)


def _layer_norm(x, g, b):
    mu = jnp.mean(x, axis=-1, keepdims=True)
    xc = x - mu
    var = jnp.mean(xc * xc, axis=-1, keepdims=True)
    return xc * lax.rsqrt(var + LN_EPS) * g + b


def _inproj_ab_kernel(x_ref, wq_ref, wk_ref, wv_ref, wu_ref, cos_ref, sin_ref,
                      q_ref, k_ref, v_ref, u_ref):
    xb = x_ref[...].astype(BF16)
    cos = cos_ref[...]
    sin = sin_ref[...]

    def rope(t):
        return t * cos + pltpu.roll(t, HEAD_DIM // 2, axis=1) * sin

    q = jnp.dot(xb, wq_ref[...], preferred_element_type=F32)
    for h in range(ATTN_HEADS):
        sl = slice(h * HEAD_DIM, (h + 1) * HEAD_DIM)
        q_ref[:, sl] = rope(q[:, sl]).astype(BF16)
    k = jnp.dot(xb, wk_ref[...], preferred_element_type=F32)
    for h in range(ATTN_KV_HEADS):
        sl = slice(h * HEAD_DIM, (h + 1) * HEAD_DIM)
        k_ref[:, sl] = rope(k[:, sl]).astype(BF16)
    v_ref[...] = jnp.dot(xb, wv_ref[...], preferred_element_type=F32).astype(BF16)
    u_ref[...] = jnp.dot(xb, wu_ref[...], preferred_element_type=F32)


def _inproj_ab(h, wq, wk, wv, wu, cos2, sin2):
    lp, d = h.shape
    tm = ROW_TILE
    s5w = wu.shape[1]
    row = lambda w: pl.BlockSpec((tm, w), lambda i: (i, 0))
    return pl.pallas_call(
        _inproj_ab_kernel,
        grid=(lp // tm,),
        in_specs=[row(d), _full(wq.shape), _full(wk.shape), _full(wv.shape), _full(wu.shape),
                  row(HEAD_DIM), row(HEAD_DIM)],
        out_specs=[row(Q_DIM), row(KV_DIM), row(KV_DIM), row(s5w)],
        out_shape=[jax.ShapeDtypeStruct((lp, Q_DIM), BF16),
                   jax.ShapeDtypeStruct((lp, KV_DIM), BF16),
                   jax.ShapeDtypeStruct((lp, KV_DIM), BF16),
                   jax.ShapeDtypeStruct((lp, s5w), F32)],
        compiler_params=_cparams("parallel"),
        name="inproj_ab",
    )(h, wq, wk, wv, wu, cos2, sin2)


def _attn_kernel(sink_ref, q_ref, k0_ref, k1_ref, k2_ref, v0_ref, v1_ref, v2_ref, km_ref, vm_ref,
                 o_ref, *, lp):
    qb = pl.program_id(0)
    blk = ATTN_BLOCK
    rows = ATTN_GROUP * blk
    scale = HEAD_DIM ** -0.5
    q_row = qb * blk + lax.broadcasted_iota(jnp.int32, (rows, 3 * blk), 0) % blk
    k_row = (qb - 1) * blk + lax.broadcasted_iota(jnp.int32, (rows, 3 * blk), 1)
    vis = (k_row >= FRONT) & (k_row < lp) & (jnp.abs(q_row - k_row) <= WINDOW)
    head_of_row = lax.broadcasted_iota(jnp.int32, (rows, 1), 0) // blk
    for g in range(ATTN_KV_HEADS):
        gs = slice(g * HEAD_DIM, (g + 1) * HEAD_DIM)
        qs = jnp.concatenate(
            [q_ref[:, (g * ATTN_GROUP + r) * HEAD_DIM:(g * ATTN_GROUP + r + 1) * HEAD_DIM]
             for r in range(ATTN_GROUP)], axis=0)
        kband = jnp.concatenate([k0_ref[:, gs], k1_ref[:, gs], k2_ref[:, gs]], axis=0)
        vband = jnp.concatenate([v0_ref[:, gs], v1_ref[:, gs], v2_ref[:, gs]], axis=0)
        nt = (((1,), (1,)), ((), ()))
        s_band = lax.dot_general(qs, kband, nt, preferred_element_type=F32) * scale
        s_band = jnp.where(vis, s_band, NEG_INF)
        s_meta = lax.dot_general(qs, km_ref[:, gs], nt, preferred_element_type=F32) * scale
        sink = jnp.zeros((rows, 1), F32)
        for r in range(ATTN_GROUP):
            sink = jnp.where(head_of_row == r, sink_ref[g * ATTN_GROUP + r], sink)
        m = jnp.maximum(jnp.maximum(jnp.max(s_band, axis=-1, keepdims=True),
                                    jnp.max(s_meta, axis=-1, keepdims=True)), sink)
        e_band = jnp.exp(s_band - m)
        e_meta = jnp.exp(s_meta - m)
        denom = (jnp.sum(e_band, axis=-1, keepdims=True) + jnp.sum(e_meta, axis=-1, keepdims=True)
                 + jnp.exp(sink - m))
        o = (jnp.dot(e_band.astype(BF16), vband, preferred_element_type=F32)
             + jnp.dot(e_meta.astype(BF16), vm_ref[:, gs], preferred_element_type=F32)) / denom
        for r in range(ATTN_GROUP):
            hh = g * ATTN_GROUP + r
            o_ref[:, hh * HEAD_DIM:(hh + 1) * HEAD_DIM] = o[r * blk:(r + 1) * blk].astype(BF16)


def _window_attention(q, k, v, sinks):
    lp = q.shape[0]
    blk = ATTN_BLOCK
    nb = lp // blk
    qspec = pl.BlockSpec((blk, Q_DIM), lambda i, s: (i, 0))
    kv = lambda off: pl.BlockSpec((blk, KV_DIM), lambda i, s: (jnp.clip(i + off, 0, nb - 1), 0))
    meta = pl.BlockSpec((N_META, KV_DIM), lambda i, s: (META_ROW0 // N_META, 0))
    return pl.pallas_call(
        functools.partial(_attn_kernel, lp=lp),
        grid_spec=pltpu.PrefetchScalarGridSpec(
            num_scalar_prefetch=1,
            grid=(nb,),
            in_specs=[qspec, kv(-1), kv(0), kv(1), kv(-1), kv(0), kv(1), meta, meta],
            out_specs=qspec,
        ),
        out_shape=jax.ShapeDtypeStruct((lp, Q_DIM), BF16),
        compiler_params=_cparams("parallel"),
        name="window_attention",
    )(sinks, q, k, k, k, v, v, v, k, v)


def _s5_discretise(lam_re, lam_im, log_step, b_re, b_im, c_re, c_im):
    t = S5_CHUNK
    hi = lax.Precision.HIGHEST
    g = lam_re.shape[1]
    lr = jnp.minimum(lam_re, -1e-4)
    li = lam_im
    step = jnp.exp(log_step)[..., None]
    dr, di = lr * step, li * step
    lags = jnp.arange(t + 1, dtype=F32)[:, None]
    mag = jnp.exp(dr[..., None, :] * lags)
    pr, pi = mag * jnp.cos(di[..., None, :] * lags), mag * jnp.sin(di[..., None, :] * lags)
    ar, ai = pr[:, :, 1], pi[:, :, 1]
    den = lr * lr + li * li
    zr = ((ar - 1.0) * lr + ai * li) / den
    zi = (ai * lr - (ar - 1.0) * li) / den
    br = zr[..., None] * b_re - zi[..., None] * b_im
    bi = zr[..., None] * b_im + zi[..., None] * b_re
    xr = c_re[:, :, None] * pr[:, :, :t, None] - c_im[:, :, None] * pi[:, :, :t, None]
    xi = c_re[:, :, None] * pi[:, :, :t, None] + c_im[:, :, None] * pr[:, :, :t, None]
    kern = (jnp.einsum('dglop,dgpi->dgloi', xr, br, precision=hi)
            - jnp.einsum('dglop,dgpi->dgloi', xi, bi, precision=hi))
    s_idx = jnp.arange(t)[:, None]
    t_idx = jnp.arange(t)[None, :]
    lag_fw = t_idx - s_idx
    m_fw = jnp.where((lag_fw >= 0)[None, :, :, None, None],
                     kern[0][:, jnp.clip(lag_fw, 0, t - 1)], 0.0)
    m_bw = jnp.where((lag_fw <= 0)[None, :, :, None, None],
                     kern[1][:, jnp.clip(-lag_fw, 0, t - 1)], 0.0)
    m_mat = (m_fw + m_bw).transpose(0, 1, 4, 2, 3).reshape(g, t * S5_GROUP, t * S5_GROUP)

    def flat_e(p_r, p_i, d):
        e_r = p_r[..., None] * br[d][:, None] - p_i[..., None] * bi[d][:, None]
        e_i = p_r[..., None] * bi[d][:, None] + p_i[..., None] * br[d][:, None]
        flat = lambda e: e.transpose(0, 1, 3, 2).reshape(g, t * S5_GROUP, S5_STATE)
        return jnp.concatenate([flat(e_r), flat(e_i)], axis=-1)
    e_mat = jnp.concatenate([flat_e(pr[0][:, ::-1][:, 1:], pi[0][:, ::-1][:, 1:], 0),
                             flat_e(pr[1][:, :t], pi[1][:, :t], 1)], axis=-1)

    def flat_f(p_r, p_i, d):
        w_r = c_re[d][:, None] * p_r[:, :, None, :] - c_im[d][:, None] * p_i[:, :, None, :]
        w_i = c_re[d][:, None] * p_i[:, :, None, :] + c_im[d][:, None] * p_r[:, :, None, :]
        flat = lambda w: w.reshape(g, t * S5_GROUP, S5_STATE).transpose(0, 2, 1)
        return jnp.concatenate([flat(w_r), -flat(w_i)], axis=1)
    f_fw = flat_f(pr[0][:, 1:], pi[0][:, 1:], 0)
    f_bw = flat_f(pr[1][:, ::-1][:, :t], pi[1][:, ::-1][:, :t], 1)
    gh = g // 2
    upper = (jnp.arange(g) >= gh)[:, None, None]
    def place(m):
        z = jnp.zeros_like(m)
        return jnp.concatenate([jnp.where(upper, z, m), jnp.where(upper, m, z)], axis=1)
    p = S5_STATE
    f_mat = jnp.concatenate([place(f_fw[:, :p]), place(f_fw[:, p:]), place(f_bw[:, :p]), place(f_bw[:, p:])],
                            axis=1)
    pack = lambda v: jnp.concatenate([v[:gh], v[gh:]], axis=-1)
    coef = jnp.stack([pack(pr[0][:, t]), pack(pi[0][:, t]), pack(pr[1][:, t]), pack(pi[1][:, t])])
    split = lambda m: m.astype(BF16).reshape((2, gh) + m.shape[1:])
    return split(m_mat), split(e_mat), split(f_mat), coef


def _chunk_rows_to_lanes(u_ref, ncb):
    lane_seg = lax.broadcasted_iota(jnp.int32, (ncb, 128), 1) // S5_GROUP
    rows = [u_ref[pl.ds(s, ncb, stride=S5_CHUNK), :] for s in range(S5_CHUNK)]
    segs = 128 // S5_GROUP
    out = []
    for j in range(segs):
        halves = []
        for h in range(S5_CHUNK // segs):
            acc = None
            for s8 in range(segs):
                r = rows[h * segs + s8]
                k = (s8 - j) % segs
                if k:
                    r = pltpu.roll(r, S5_GROUP * k, axis=1)
                acc = r if acc is None else jnp.where(lane_seg == s8, r, acc)
            halves.append(acc)
        out.append(jnp.concatenate(halves, axis=1))
    return out


def _lanes_to_chunk_rows(ys, y_ref, ncb, add_ref, scale_ref):
    lane_seg = lax.broadcasted_iota(jnp.int32, (ncb, 128), 1) // S5_GROUP
    segs = 128 // S5_GROUP
    for t in range(S5_CHUNK):
        h, t8 = divmod(t, segs)
        acc = None
        for j in range(segs):
            r = ys[j][:, h * 128:(h + 1) * 128]
            k = (j - t8) % segs
            if k:
                r = pltpu.roll(r, S5_GROUP * k, axis=1)
            acc = r if acc is None else jnp.where(lane_seg == j, r, acc)
        idx = pl.ds(t, ncb, stride=S5_CHUNK)
        y_ref[idx, :] = acc + add_ref[idx, :] * scale_ref[...]


def _pair_halves(a, b):
    lane = lax.broadcasted_iota(jnp.int32, a.shape, 1)
    lo = jnp.where(lane < S5_STATE, a, pltpu.roll(b, S5_STATE, axis=1))
    hi = jnp.where(lane < S5_STATE, pltpu.roll(a, S5_STATE, axis=1), b)
    return lo, hi


def _s5_local_kernel(ua_ref, ub_ref, e_ref, u2_ref, f1_ref, f2_ref, b1_ref, b2_ref):
    ncb = u2_ref.shape[2]
    p2 = 2 * S5_STATE
    chunks = [_chunk_rows_to_lanes(ua_ref, ncb), _chunk_rows_to_lanes(ub_ref, ncb)]
    for j in range(u2_ref.shape[1]):
        st = []
        for hh in range(2):
            u2 = chunks[hh][j].astype(BF16)
            u2_ref[hh, j] = u2
            st.append(jnp.dot(u2, e_ref[hh, j], preferred_element_type=F32))
        f1_ref[:, j, :], f2_ref[:, j, :] = _pair_halves(st[0][:, :p2], st[1][:, :p2])
        b1_ref[:, j, :], b2_ref[:, j, :] = _pair_halves(st[0][:, p2:], st[1][:, p2:])


def _s5_scan_kernel(sf1_ref, sf2_ref, sb1_ref, sb2_ref, coef_ref, xf1_ref, xf2_ref, xb1_ref, xb2_ref,
                    st_ref):
    @pl.when(pl.program_id(0) == 0)
    def _():
        st_ref[...] = jnp.zeros_like(st_ref)

    nb = sf1_ref.shape[0]
    crf, cif, crb, cib = coef_ref[0], coef_ref[1], coef_ref[2], coef_ref[3]

    def body(i, carry):
        f1, f2, b1, b2 = carry
        j = nb - 1 - i
        xf1_ref[i] = f1
        xf2_ref[i] = f2
        xb1_ref[j] = b1
        xb2_ref[j] = b2
        return (crf * f1 - cif * f2 + sf1_ref[i], crf * f2 + cif * f1 + sf2_ref[i],
                crb * b1 - cib * b2 + sb1_ref[j], crb * b2 + cib * b1 + sb2_ref[j])

    out = lax.fori_loop(0, nb, body, (st_ref[0], st_ref[1], st_ref[2], st_ref[3]))
    for q in range(4):
        st_ref[q] = out[q]


def _s5_readout_kernel(ua_ref, ub_ref, d_ref, u2_ref, m_ref, f_ref, xf1_ref, xf2_ref, xb1_ref, xb2_ref,
                       ya_ref, yb_ref):
    ncb = u2_ref.shape[2]
    ys = [[], []]
    for j in range(u2_ref.shape[1]):
        x = jnp.concatenate([xf1_ref[:, j, :], xf2_ref[:, j, :], xb1_ref[:, j, :], xb2_ref[:, j, :]],
                            axis=1).astype(BF16)
        for hh in range(2):
            y = jnp.dot(u2_ref[hh, j], m_ref[hh, j], preferred_element_type=F32)
            ys[hh].append(y + jnp.dot(x, f_ref[hh, j], preferred_element_type=F32))
    _lanes_to_chunk_rows(ys[0], ya_ref, ncb, ua_ref, d_ref.at[0])
    _lanes_to_chunk_rows(ys[1], yb_ref, ncb, ub_ref, d_ref.at[1])


def _s5_mixer_pre_glu(u, d_skip, mats):
    m_mat, e_mat, f_mat, coef = mats
    lp, w = u.shape
    g = w // S5_GROUP
    gh = g // 2
    gpb = 128 // S5_GROUP
    nc = lp // S5_CHUNK
    ncb = _pick(nc, S5_CHUNK_BLOCKS)
    rows = ncb * S5_CHUNK
    nlb = w // 128
    grid = (nlb // 2, nc // ncb)
    u_lo = pl.BlockSpec((rows, 128), lambda b, r: (r, b))
    u_hi = pl.BlockSpec((rows, 128), lambda b, r: (r, b + nlb // 2))
    per_group = lambda k, n: pl.BlockSpec((2, gpb, k, n), lambda b, r: (0, b, 0, 0))
    u2_spec = pl.BlockSpec((2, gpb, ncb, S5_ROW), lambda b, r: (0, b, r, 0))
    st_spec = pl.BlockSpec((ncb, gpb, 2 * S5_STATE), lambda b, r: (r, b, 0))
    st_shape = jax.ShapeDtypeStruct((nc, gh, 2 * S5_STATE), F32)
    u2, sf1, sf2, sb1, sb2 = pl.pallas_call(
        _s5_local_kernel,
        grid=grid,
        in_specs=[u_lo, u_hi, per_group(S5_ROW, 4 * S5_STATE)],
        out_specs=[u2_spec, st_spec, st_spec, st_spec, st_spec],
        out_shape=[jax.ShapeDtypeStruct((2, gh, nc, S5_ROW), BF16)] + [st_shape] * 4,
        compiler_params=_cparams("parallel", "parallel"),
        name="s5_local_states",
    )(u, u, e_mat)
    sb = S5_SCAN_BLOCK
    nblk = nc // sb
    fwd = pl.BlockSpec((sb, gh, 2 * S5_STATE), lambda i: (i, 0, 0))
    bwd = pl.BlockSpec((sb, gh, 2 * S5_STATE), lambda i: (nblk - 1 - i, 0, 0))
    xf1, xf2, xb1, xb2 = pl.pallas_call(
        _s5_scan_kernel,
        grid=(nblk,),
        in_specs=[fwd, fwd, bwd, bwd, _full(coef.shape)],
        out_specs=[fwd, fwd, bwd, bwd],
        out_shape=[st_shape] * 4,
        scratch_shapes=[pltpu.VMEM((4, gh, 2 * S5_STATE), F32)],
        compiler_params=_cparams("arbitrary"),
        name="s5_chunk_scan",
    )(sf1, sf2, sb1, sb2, coef)
    d2 = d_skip.reshape(2, 1, nlb // 2 * 128)
    y_lo, y_hi = pl.pallas_call(
        _s5_readout_kernel,
        grid=grid,
        in_specs=[u_lo, u_hi, pl.BlockSpec((2, 1, 128), lambda b, r: (0, 0, b)), u2_spec,
                  per_group(S5_ROW, S5_ROW), per_group(8 * S5_STATE, S5_ROW),
                  st_spec, st_spec, st_spec, st_spec],
        out_specs=[pl.BlockSpec((rows, 128), lambda b, r: (r, b))] * 2,
        out_shape=[jax.ShapeDtypeStruct((lp, w // 2), F32)] * 2,
        compiler_params=_cparams("parallel", "parallel"),
        name="s5_readout",
    )(u, u, d2, u2, m_mat, f_mat, xf1, xf2, xb1, xb2)
    return y_lo, y_hi


def _route(hn, wr_ref, br_ref):
    hn_hi = hn.astype(BF16)
    hn_lo = (hn - hn_hi.astype(F32)).astype(BF16)
    p_hi = jnp.dot(hn_hi, wr_ref[...], preferred_element_type=F32)
    p_lo = jnp.dot(hn_lo, wr_ref[:, :ROUTE_LANES], preferred_element_type=F32)
    logits = p_hi[:, :ROUTE_LANES] + (p_hi[:, ROUTE_LANES:] + p_lo) + br_ref[...]
    lane_i = lax.broadcasted_iota(jnp.int32, logits.shape, 1)
    lane = lane_i.astype(F32)
    lane_grp = ((lane_i - N_GROUPS) // EXPERTS_PER_GROUP).astype(F32)
    big = float(ROUTE_LANES)
    g_log = jnp.where(lane_i < N_GROUPS, logits, -jnp.inf)
    g_max = jnp.max(g_log, axis=-1, keepdims=True)
    grp = jnp.min(jnp.where(g_log == g_max, lane, big), axis=-1, keepdims=True)
    p_grp = 1.0 / jnp.sum(jnp.exp(g_log - g_max), axis=-1, keepdims=True)
    in_grp = (lane_i >= N_GROUPS) & (lane_i < N_GROUPS + N_EXPERTS) & (lane_grp == grp)
    e_log = jnp.where(in_grp, logits, -jnp.inf)
    v1 = jnp.max(e_log, axis=-1, keepdims=True)
    i1 = jnp.min(jnp.where(e_log == v1, lane, big), axis=-1, keepdims=True)
    e_log2 = jnp.where(lane == i1, -jnp.inf, e_log)
    v2 = jnp.max(e_log2, axis=-1, keepdims=True)
    i2 = jnp.min(jnp.where(e_log2 == v2, lane, big), axis=-1, keepdims=True)
    e21 = jnp.exp(v2 - v1)
    w1 = p_grp / (1.0 + e21)
    w2 = p_grp * e21 / (1.0 + e21)
    route = jnp.where(lane_i == 0, w1, 0.0)
    route = jnp.where(lane_i == 1, w2, route)
    route = jnp.where(lane_i == 2, i1 - N_GROUPS, route)
    route = jnp.where(lane_i == 3, i2 - N_GROUPS, route)
    return route


def _sub_rows(tm):
    return [slice(r, r + MIX_SUB) for r in range(0, tm, MIX_SUB)]


def _mix_ab_kernel(h_ref, a_ref, ylo_ref, yhi_ref, wglu_ref, bglu_ref, woa_ref, wos_ref,
                   g_ref, b_ref, wr_ref, br_ref, hn_ref, route_ref, *, alpha):
    for rows in _sub_rows(h_ref.shape[0]):
        y = jnp.concatenate([ylo_ref[rows, :], yhi_ref[rows, :]], axis=1)
        z = 0.5 * y * (1.0 + jnp.tanh(math.sqrt(2.0 / math.pi) * (y + 0.044715 * (y * y * y))))
        gate = jnp.dot(z.astype(BF16), wglu_ref[...], preferred_element_type=F32) + bglu_ref[...]
        s_out = z * jax.nn.sigmoid(gate)
        mix = (jnp.dot(a_ref[rows, :], woa_ref[...], preferred_element_type=F32)
               + jnp.dot(s_out.astype(BF16), wos_ref[...], preferred_element_type=F32))
        hn = _layer_norm(alpha * h_ref[rows, :] + mix, g_ref[...], b_ref[...])
        hn_ref[rows, :] = hn
        route_ref[rows, :] = _route(hn, wr_ref, br_ref)


def _mix_ab(h, a_out, y_lo, y_hi, w_glu, b_glu, wo_a, wo_s, ln_g, ln_b, w_route, b_route, alpha):
    lp, d = h.shape
    tm = MIX_TILE
    row = lambda c: pl.BlockSpec((tm, c), lambda i: (i, 0))
    return pl.pallas_call(
        functools.partial(_mix_ab_kernel, alpha=alpha),
        grid=(lp // tm,),
        in_specs=[row(d), row(Q_DIM), row(y_lo.shape[1]), row(y_hi.shape[1]), _full(w_glu.shape),
                  _full(b_glu.shape), _full(wo_a.shape), _full(wo_s.shape), _full(ln_g.shape),
                  _full(ln_b.shape), _full(w_route.shape), _full(b_route.shape)],
        out_specs=[row(d), row(ROUTE_LANES)],
        out_shape=[jax.ShapeDtypeStruct((lp, d), F32), jax.ShapeDtypeStruct((lp, ROUTE_LANES), F32)],
        compiler_params=_cparams("parallel"),
        name="mix_ab",
    )(h, a_out, y_lo, y_hi, w_glu, b_glu, wo_a, wo_s, ln_g, ln_b, w_route, b_route)


def _mix_c_kernel(h_ref, ofw_ref, obw_ref, gate_ref, ng_ref, wo_ref, g_ref, b_ref, wr_ref, br_ref,
                  hn_ref, route_ref, *, alpha):
    for rows in _sub_rows(h_ref.shape[0]):
        parts = []
        for hh in range(HG_HEADS):
            o = ofw_ref[hh, rows, :] + obw_ref[hh, rows, :]
            o = o * lax.rsqrt(jnp.mean(o * o, axis=-1, keepdims=True) + RMS_EPS)
            o = o * ng_ref[hh] * jax.nn.sigmoid(gate_ref[0, hh, rows, :].astype(F32))
            parts.append(o.astype(BF16))
        mix = jnp.dot(jnp.concatenate(parts, axis=1), wo_ref[...], preferred_element_type=F32)
        hn = _layer_norm(alpha * h_ref[rows, :] + mix, g_ref[...], b_ref[...])
        hn_ref[rows, :] = hn
        route_ref[rows, :] = _route(hn, wr_ref, br_ref)


def _mix_c(h, o_fw, o_bw, plain, norm_g, wo, ln_g, ln_b, w_route, b_route, alpha):
    lp, d = h.shape
    tm = MIX_TILE
    row = lambda c: pl.BlockSpec((tm, c), lambda i: (i, 0))
    hm = pl.BlockSpec((HG_HEADS, tm, HG_DK), lambda i: (0, i, 0))
    gate = pl.BlockSpec((1, HG_HEADS, tm, HG_DK), lambda i: (2, 0, i, 0))
    return pl.pallas_call(
        functools.partial(_mix_c_kernel, alpha=alpha),
        grid=(lp // tm,),
        in_specs=[row(d), hm, hm, gate, _full(norm_g.shape), _full(wo.shape), _full(ln_g.shape),
                  _full(ln_b.shape), _full(w_route.shape), _full(b_route.shape)],
        out_specs=[row(d), row(ROUTE_LANES)],
        out_shape=[jax.ShapeDtypeStruct((lp, d), F32), jax.ShapeDtypeStruct((lp, ROUTE_LANES), F32)],
        compiler_params=_cparams("parallel"),
        name="mix_c",
    )(h, o_fw, o_bw, plain, norm_g, wo, ln_g, ln_b, w_route, b_route)


def _moe_plan(route, n_rows_valid_from):
    lp = route.shape[0]
    tm = MOE_TM
    valid = (jnp.arange(lp) >= n_rows_valid_from)
    expert = route[:, 2:4].astype(jnp.int32)
    flat_e = jnp.where(valid[:, None], expert, N_EXPERTS).reshape(-1)
    onehot = (flat_e[:, None] == jnp.arange(N_EXPERTS)[None, :]).astype(jnp.int32)
    csum = jnp.cumsum(onehot, axis=0)
    counts = csum[-1]
    rank = jnp.sum(jnp.where(onehot > 0, csum - 1, 0), axis=1)
    padded = (counts + tm - 1) // tm * tm
    pend = jnp.cumsum(padded)
    pstart = pend - padded
    e_safe = jnp.minimum(flat_e, N_EXPERTS - 1)
    dest = jnp.where(flat_e < N_EXPERTS, pstart[e_safe] + rank, 0)
    n_tokens = lp - n_rows_valid_from
    n_blocks = -(-(n_tokens * TOP_K + N_EXPERTS * (tm - 1)) // tm)
    rows = n_blocks * tm
    tok = jnp.repeat(jnp.arange(lp, dtype=jnp.int32), TOP_K)
    scatter_to = jnp.where(flat_e < N_EXPERTS, dest, rows)
    tok_of_row = jnp.zeros((rows,), jnp.int32).at[scatter_to].set(tok, mode='drop')
    n_used = (pend[-1] // tm).astype(jnp.int32)
    blk = jnp.minimum(jnp.arange(n_blocks, dtype=jnp.int32), jnp.maximum(n_used - 1, 0))
    block_expert = jnp.minimum(jnp.searchsorted(pend, blk * tm, side='right'),
                               N_EXPERTS - 1).astype(jnp.int32)
    used = counts > 0
    pos = jnp.cumsum(used.astype(jnp.int32)) - 1
    eid = jnp.arange(N_EXPERTS, dtype=jnp.int32)
    later = jnp.where(used[None, :] & (eid[None, :] > eid[:, None]), eid[None, :], N_EXPERTS)
    nxt_e = jnp.min(later, axis=1)
    nxt_e = jnp.where(nxt_e == N_EXPERTS, -1, nxt_e).astype(jnp.int32)
    w_slot = (pos[block_expert] % 2).astype(jnp.int32)
    nxt = nxt_e[block_expert]
    weights = jnp.where(valid[:, None], route[:, 0:2], 0.0)
    return (block_expert, n_used.reshape(1), w_slot, nxt, tok_of_row.reshape(n_blocks, 1, tm),
            dest.reshape(lp, TOP_K).astype(jnp.int32), weights)


def _moe_expert_kernel(be_ref, nu_ref, ws_ref, nx_ref, tok0_ref, tokn_ref, h_hbm, wgu_hbm, wd_hbm,
                       y_ref, xbuf, xsem, wgu_f, wd_f, wsem, wgu_bf, wd_bf, *, layer):
    i = pl.program_id(0)
    tm = xbuf.shape[1]
    n_used = nu_ref[0]

    def gather(tok_ref, slot):
        def issue(r, carry):
            pltpu.make_async_copy(h_hbm.at[tok_ref[0, 0, r]], xbuf.at[slot, r], xsem.at[slot]).start()
            return carry
        lax.fori_loop(0, tm, issue, 0, unroll=8)

    def gather_wait(slot):
        pltpu.make_async_copy(h_hbm.at[pl.ds(0, tm)], xbuf.at[slot], xsem.at[slot]).wait()

    def weight_copies(e, s):
        return (pltpu.make_async_copy(wgu_hbm.at[layer, e], wgu_f.at[s], wsem.at[0, s]),
                pltpu.make_async_copy(wd_hbm.at[layer, e], wd_f.at[s], wsem.at[1, s]))

    @pl.when(i == 0)
    def _():
        for cp in weight_copies(be_ref[0], 0):
            cp.start()
        gather(tok0_ref, 0)

    @pl.when(i < n_used)
    def _():
        slot = i % 2
        e = be_ref[i]
        first = jnp.logical_or(i == 0, be_ref[jnp.maximum(i - 1, 0)] != e)

        gather_wait(slot)
        gather(tokn_ref, 1 - slot)

        @pl.when(first)
        def _():
            s = ws_ref[i]
            for cp in weight_copies(e, s):
                cp.wait()
            nxt = nx_ref[i]

            @pl.when(nxt >= 0)
            def _():
                for cp in weight_copies(nxt, 1 - s):
                    cp.start(priority=1)

            wgu_bf[...] = wgu_f[s].astype(BF16)
            wd_bf[...] = wd_f[s].astype(BF16)

        hgu = jnp.dot(xbuf[slot].astype(BF16), wgu_bf[...], preferred_element_type=F32)
        act = jax.nn.silu(hgu[:, :D_EXPERT]) * hgu[:, D_EXPERT:]
        y_ref[...] = jnp.dot(act.astype(BF16), wd_bf[...], preferred_element_type=F32)

        @pl.when(i == n_used - 1)
        def _():
            gather_wait(1 - slot)

    @pl.when(i >= n_used)
    def _():
        y_ref[...] = jnp.zeros_like(y_ref)


def _moe_experts(h, plan, w_gate_up, w_down, layer):
    block_expert, n_used, w_slot, nxt, tok_of_row, _, _ = plan
    lp, d = h.shape
    tm = MOE_TM
    n_blocks = tok_of_row.shape[0]
    nxt_blk = lambda i, be, nu, ws, nx: (jnp.minimum(i + 1, jnp.maximum(nu[0] - 1, 0)), 0, 0)
    smem = lambda imap: pl.BlockSpec((1, 1, tm), imap, memory_space=pltpu.SMEM)
    anyspace = pl.BlockSpec(memory_space=pl.ANY)
    return pl.pallas_call(
        functools.partial(_moe_expert_kernel, layer=layer),
        grid_spec=pltpu.PrefetchScalarGridSpec(
            num_scalar_prefetch=4,
            grid=(n_blocks,),
            in_specs=[smem(lambda i, be, nu, ws, nx: (0, 0, 0)), smem(nxt_blk),
                      anyspace, anyspace, anyspace],
            out_specs=pl.BlockSpec((tm, d), lambda i, be, nu, ws, nx: (i, 0)),
            scratch_shapes=[pltpu.VMEM((2, tm, d), F32), pltpu.SemaphoreType.DMA((2,)),
                            pltpu.VMEM((2, d, 2 * D_EXPERT), F32), pltpu.VMEM((2, D_EXPERT, d), F32),
                            pltpu.SemaphoreType.DMA((2, 2)),
                            pltpu.VMEM((d, 2 * D_EXPERT), BF16), pltpu.VMEM((D_EXPERT, d), BF16)],
        ),
        out_shape=jax.ShapeDtypeStruct((n_blocks * tm, d), F32),
        compiler_params=_cparams("arbitrary"),
        name="moe_experts",
    )(block_expert, n_used, w_slot, nxt, tok_of_row, tok_of_row, h, w_gate_up, w_down)


def _moe_combine_kernel(d0_ref, d1_ref, d0n_ref, d1n_ref, h_ref, w_ref, y_hbm, g_ref, b_ref, *rest,
                        alpha, with_bf16):
    if with_bf16:
        o_ref, ob_ref, buf, sem = rest
    else:
        o_ref, buf, sem = rest
    i = pl.program_id(0)
    tm = buf.shape[2]
    slot = i % 2

    def gather(a_ref, b_ref_, s):
        def issue(r, carry):
            pltpu.make_async_copy(y_hbm.at[a_ref[0, 0, r]], buf.at[s, 0, r], sem.at[s, 0]).start()
            pltpu.make_async_copy(y_hbm.at[b_ref_[0, 0, r]], buf.at[s, 1, r], sem.at[s, 1]).start()
            return carry
        lax.fori_loop(0, tm, issue, 0, unroll=8)

    @pl.when(i == 0)
    def _():
        gather(d0_ref, d1_ref, 0)

    for k in range(TOP_K):
        pltpu.make_async_copy(y_hbm.at[pl.ds(0, tm)], buf.at[slot, k], sem.at[slot, k]).wait()

    @pl.when(i + 1 < pl.num_programs(0))
    def _():
        gather(d0n_ref, d1n_ref, 1 - slot)

    w = w_ref[...]
    ffn = w[:, 0:1] * buf[slot, 0] + w[:, 1:2] * buf[slot, 1]
    out = _layer_norm(alpha * h_ref[...] + ffn, g_ref[...], b_ref[...])
    o_ref[...] = out
    if with_bf16:
        ob_ref[...] = out.astype(BF16)


def _moe_combine(h, plan, y_buf, ln_g, ln_b, alpha, first_row, with_bf16=False):
    dest, weights = plan[-2], plan[-1]
    lp, d = h.shape
    tm = MIX_TILE
    off = first_row // tm
    nt = lp // tm
    d0 = dest[:, 0].reshape(nt, 1, tm)
    d1 = dest[:, 1].reshape(nt, 1, tm)
    cur = pl.BlockSpec((1, 1, tm), lambda i: (off, 0, 0), memory_space=pltpu.SMEM)
    nxt = pl.BlockSpec((1, 1, tm), lambda i: (jnp.minimum(i + 1 + off, nt - 1), 0, 0),
                       memory_space=pltpu.SMEM)
    out_spec = pl.BlockSpec((tm, d), lambda i: (i, 0))
    out_shape = jax.ShapeDtypeStruct((lp - first_row, d), F32)
    if with_bf16:
        out_spec = [out_spec, out_spec]
        out_shape = [out_shape, jax.ShapeDtypeStruct((lp - first_row, d), BF16)]
    return pl.pallas_call(
        functools.partial(_moe_combine_kernel, alpha=alpha, with_bf16=with_bf16),
        grid=(nt - off,),
        in_specs=[cur, cur, nxt, nxt, pl.BlockSpec((tm, d), lambda i: (i + off, 0)),
                  pl.BlockSpec((tm, TOP_K), lambda i: (i + off, 0)),
                  pl.BlockSpec(memory_space=pl.ANY), _full(ln_g.shape), _full(ln_b.shape)],
        out_specs=out_spec,
        out_shape=out_shape,
        scratch_shapes=[pltpu.VMEM((2, TOP_K, tm, d), F32), pltpu.SemaphoreType.DMA((2, TOP_K))],
        compiler_params=_cparams("arbitrary"),
        name="moe_combine",
    )(d0, d1, d0, d1, h, weights, y_buf, ln_g, ln_b)


def _store_heads(ref, seg_rows, val):
    for hh in range(val.shape[1] // HG_DK):
        ref[0, hh, seg_rows, :] = val[:, hh * HG_DK:(hh + 1) * HG_DK].astype(ref.dtype)


def _inproj_plain_kernel(x_ref, w_ref, o_ref):
    for r in range(0, x_ref.shape[0], INPROJ_C_SUB):
        rows = slice(r, r + INPROJ_C_SUB)
        _store_heads(o_ref, rows, jnp.dot(x_ref[rows, :], w_ref[...], preferred_element_type=F32))


def _inproj_forget_kernel(x_ref, w_ref, lb_ref, k_ref, g_ref):
    i = pl.program_id(0)
    tm = x_ref.shape[0]
    lb = lb_ref[...]
    for r in range(0, tm, INPROJ_C_SUB):
        rows = slice(r, r + INPROJ_C_SUB)
        acc = jnp.dot(x_ref[rows, :], w_ref[...], preferred_element_type=F32)
        f = lb + (1.0 - lb) * jax.nn.sigmoid(acc)
        real = (i * tm + r + lax.broadcasted_iota(jnp.int32, (INPROJ_C_SUB, 1), 0)) >= META_ROW0
        _store_heads(k_ref, rows, jnp.where(real, 1.0 - f, 0.0))
        _store_heads(g_ref, rows, jnp.where(real, jnp.log(f), 0.0))


def _inproj_c(hb, w, lb):
    lp, d = hb.shape
    tm = _pick(lp, INPROJ_C_TILES)
    tn = 512
    tps = d // tn
    hpt = tn // HG_DK
    x_spec = pl.BlockSpec((tm, d), lambda i, j: (i, 0))
    out_spec = pl.BlockSpec((1, hpt, tm, HG_DK), lambda i, j: (j // tps, j % tps, i, 0))
    hm = lambda n, dt: jax.ShapeDtypeStruct((n, HG_HEADS, lp, HG_DK), dt)
    plain_col = lambda i, j: (0, jnp.where(j >= 2 * tps, j + 2 * tps, j))
    plain = pl.pallas_call(
        _inproj_plain_kernel,
        grid=(lp // tm, 3 * tps),
        in_specs=[x_spec, pl.BlockSpec((d, tn), plain_col)],
        out_specs=out_spec,
        out_shape=hm(3, BF16),
        compiler_params=_cparams("parallel", "arbitrary"),
        name="inproj_c_plain",
    )(hb, w)
    k, g = pl.pallas_call(
        _inproj_forget_kernel,
        grid=(lp // tm, 2 * tps),
        in_specs=[x_spec, pl.BlockSpec((d, tn), lambda i, j: (0, j + 2 * tps)),
                  pl.BlockSpec((1, tn), lambda i, j: (0, j % tps))],
        out_specs=[out_spec, out_spec],
        out_shape=[hm(2, BF16), hm(2, F32)],
        compiler_params=_cparams("parallel", "arbitrary"),
        name="inproj_c_forget",
    )(hb, w, lb)
    return plain, k, g


def _hgrn_chunk(q, k, v, g, state_t, reverse):
    c = HG_CHUNK
    row = lax.broadcasted_iota(jnp.int32, (c, HG_DK), 0)
    b = g
    sh = 1
    while sh < c:
        if not reverse:
            b = b + jnp.where(row >= sh, pltpu.roll(b, sh, axis=0), 0.0)
        else:
            b = b + jnp.where(row < c - sh, pltpu.roll(b, c - sh, axis=0), 0.0)
        sh *= 2
    r_i = lax.broadcasted_iota(jnp.int32, (c, c), 0)
    c_i = lax.broadcasted_iota(jnp.int32, (c, c), 1)
    causal = (c_i >= r_i) if reverse else (c_i <= r_i)
    sub = HG_SUB
    mid = sub // 2
    blocks = lambda x: [x[j * sub:(j + 1) * sub] for j in range(HG_NSUB)]
    ref = [b[j * sub + mid:j * sub + mid + 1, :] for j in range(HG_NSUB)]
    b_end = b[0:1, :] if reverse else b[c - 1:c, :]
    qd, ku, q0, k_end = [], [], [], []
    for j, (qj, kj, bj) in enumerate(zip(blocks(q), blocks(k), blocks(b))):
        dj = bj - ref[j]
        qd_j = qj * jnp.exp(jnp.minimum(dj, HG_EXP_CLAMP))
        ku_j = kj * jnp.exp(jnp.minimum(-dj, HG_EXP_CLAMP))
        qd.append(qd_j.astype(BF16))
        ku.append(ku_j)
        q0.append(qd_j * jnp.exp(ref[j]))
        k_end.append(ku_j * jnp.exp(b_end - ref[j]))
    nt = (((1,), (1,)), ((), ()))
    zero = jnp.zeros((sub, HG_DK), BF16)
    score_rows = []
    for j in range(HG_NSUB):
        parts = []
        for i in range(HG_NSUB):
            if i == j:
                parts.append(ku[i].astype(BF16))
            elif (i > j) if reverse else (i < j):
                parts.append((ku[i] * jnp.exp(ref[j] - ref[i])).astype(BF16))
            else:
                parts.append(zero)
        k_ext = jnp.concatenate(parts, axis=0)
        score_rows.append(lax.dot_general(qd[j], k_ext, nt, preferred_element_type=F32))
    scores = jnp.concatenate(score_rows, axis=0)
    scores = jnp.where(causal, scores, 0.0)
    o = jnp.dot(scores.astype(BF16), v.astype(BF16), preferred_element_type=F32)
    o += lax.dot_general(jnp.concatenate(q0, axis=0).astype(BF16), state_t.astype(BF16), nt,
                         preferred_element_type=F32)
    new_state = jnp.exp(b_end) * state_t + jnp.dot(
        v.T.astype(BF16), jnp.concatenate(k_end, axis=0).astype(BF16), preferred_element_type=F32)
    return o, new_state


def _hgrn_kernel(qf_ref, vf_ref, k1_ref, g1_ref, qb_ref, vb_ref, k2_ref, g2_ref,
                 ofw_ref, obw_ref, st_ref):
    @pl.when(pl.program_id(0) == 0)
    def _():
        st_ref[...] = jnp.zeros_like(st_ref)

    def head(hh, carry):
        o, s = _hgrn_chunk(qf_ref[0, hh].astype(F32), k1_ref[0, hh].astype(F32),
                           vf_ref[0, hh].astype(F32), g1_ref[0, hh], st_ref[0, hh], reverse=False)
        ofw_ref[hh] = o
        st_ref[0, hh] = s
        o, s = _hgrn_chunk(qb_ref[0, hh].astype(F32), k2_ref[0, hh].astype(F32),
                           vb_ref[0, hh].astype(F32), g2_ref[0, hh], st_ref[1, hh], reverse=True)
        obw_ref[hh] = o
        st_ref[1, hh] = s
        return carry

    lax.fori_loop(0, HG_HEADS, head, 0, unroll=4)


def _hgrn(plain, k, g):
    _, _, lp, dk = plain.shape
    c = HG_CHUNK
    nc = lp // c
    fwd = lambda seg: pl.BlockSpec((1, HG_HEADS, c, dk), lambda i: (seg, 0, i, 0))
    bwd = lambda seg: pl.BlockSpec((1, HG_HEADS, c, dk), lambda i: (seg, 0, nc - 1 - i, 0))
    out_f = pl.BlockSpec((HG_HEADS, c, dk), lambda i: (0, i, 0))
    out_b = pl.BlockSpec((HG_HEADS, c, dk), lambda i: (0, nc - 1 - i, 0))
    return pl.pallas_call(
        _hgrn_kernel,
        grid=(nc,),
        in_specs=[fwd(0), fwd(1), fwd(0), fwd(0), bwd(0), bwd(1), bwd(1), bwd(1)],
        out_specs=[out_f, out_b],
        out_shape=[jax.ShapeDtypeStruct((HG_HEADS, lp, dk), F32)] * 2,
        scratch_shapes=[pltpu.VMEM((2, HG_HEADS, dk, dk), F32)],
        compiler_params=_cparams("arbitrary"),
        name="hgrn2_recurrence",
    )(plain, plain, k, g, plain, plain, k, g)


def _route_params(w_group, b_group, w_expert, b_expert):
    d = w_group.shape[0]
    pad = ROUTE_LANES - N_GROUPS - N_EXPERTS
    w = jnp.concatenate([w_group, w_expert, jnp.zeros((d, pad), F32)], axis=1)
    b = jnp.concatenate([b_group, b_expert, jnp.zeros((pad,), F32)])[None, :]
    w_hi = w.astype(BF16)
    w_lo = (w - w_hi.astype(F32)).astype(BF16)
    return jnp.concatenate([w_hi, w_lo], axis=1), b


def _rope_tables(lp):
    half = HEAD_DIM // 2
    pos = jnp.maximum(jnp.arange(lp) - META_ROW0, 0).astype(F32)
    inv = ROPE_THETA ** (-jnp.arange(half, dtype=F32) * 2.0 / HEAD_DIM)
    ang = pos[:, None] * inv[None, :]
    cos, sin = jnp.cos(ang), jnp.sin(ang)
    return jnp.concatenate([cos, cos], axis=1), jnp.concatenate([-sin, sin], axis=1)


def kernel(x, meta_tokens, w_in_ab, w_out_ab, attn_sinks, s5_lam_re, s5_lam_im, s5_log_step, s5_b_re, s5_b_im, s5_c_re, s5_c_im, s5_d, s5_w_glu, s5_b_glu, w_in_c, w_out_c, hgrn_lb_logits, hgrn_norm_g, ln_mix_g, ln_mix_b, ln_ffn_g, ln_ffn_b, moe_w_group, moe_b_group, moe_w_expert, moe_b_expert, moe_w_gate_up, moe_w_down):
    batch, seq, d = x.shape
    assert batch == 1 and seq % FRONT == 0
    depth = ln_mix_g.shape[0]
    assert depth == 2
    alpha = (2.0 * depth) ** 0.25
    lp = FRONT + seq
    row2 = lambda t: t[None, :]

    h = jnp.concatenate([jnp.zeros((META_ROW0, d), F32), meta_tokens.astype(F32), x[0]], axis=0)

    w_in = w_in_ab[0].astype(BF16)
    s5w = s5_d.shape[1]
    wq, wk, wv, wu = (w_in[:, :Q_DIM], w_in[:, Q_DIM:Q_DIM + KV_DIM],
                      w_in[:, Q_DIM + KV_DIM:Q_DIM + 2 * KV_DIM], w_in[:, Q_DIM + 2 * KV_DIM:])
    cos2, sin2 = _rope_tables(lp)
    q, k, v, u = _inproj_ab(h, wq, wk, wv, wu, cos2, sin2)
    a_out = _window_attention(q, k, v, attn_sinks[0])
    mats = _s5_discretise(s5_lam_re[0], s5_lam_im[0], s5_log_step[0], s5_b_re[0], s5_b_im[0],
                          s5_c_re[0], s5_c_im[0])
    y_lo, y_hi = _s5_mixer_pre_glu(u, s5_d[0], mats)
    wo = w_out_ab[0].astype(BF16)
    w_route, b_route = _route_params(moe_w_group[0], moe_b_group[0], moe_w_expert[0], moe_b_expert[0])
    h, route = _mix_ab(h, a_out, y_lo, y_hi, s5_w_glu[0].astype(BF16), row2(s5_b_glu[0]),
                       wo[:Q_DIM], wo[Q_DIM:], row2(ln_mix_g[0]), row2(ln_mix_b[0]), w_route, b_route,
                       alpha)
    plan = _moe_plan(route, META_ROW0)
    y_buf = _moe_experts(h, plan, moe_w_gate_up, moe_w_down, 0)
    h, hb = _moe_combine(h, plan, y_buf, row2(ln_ffn_g[0]), row2(ln_ffn_b[0]), alpha, 0, with_bf16=True)

    lb_probs = jax.nn.softmax(hgrn_lb_logits.astype(F32), axis=0)
    lb = (jnp.cumsum(lb_probs, axis=0) - lb_probs[0])[1]
    plain, hk, hg = _inproj_c(hb, w_in_c[0].astype(BF16), row2(lb))
    o_fw, o_bw = _hgrn(plain, hk, hg)
    w_route, b_route = _route_params(moe_w_group[1], moe_b_group[1], moe_w_expert[1], moe_b_expert[1])
    h, route = _mix_c(h, o_fw, o_bw, plain, hgrn_norm_g[0].reshape(HG_HEADS, 1, HG_DK),
                      w_out_c[0].astype(BF16), row2(ln_mix_g[1]), row2(ln_mix_b[1]), w_route, b_route,
                      alpha)
    plan = _moe_plan(route, META_ROW0)
    y_buf = _moe_experts(h, plan, moe_w_gate_up, moe_w_down, 1)
    out = _moe_combine(h, plan, y_buf, row2(ln_ffn_g[1]), row2(ln_ffn_b[1]), alpha, FRONT)
    return out[None]
```

```python
import functools
import math

import jax
import jax.numpy as jnp
from jax import lax
from jax.experimental import pallas as pl
from jax.experimental.pallas import tpu as pltpu

F32 = jnp.float32
BF16 = jnp.bfloat16

N_META = 16
FRONT = 512
META_ROW0 = FRONT - N_META

ATTN_HEADS = 8
ATTN_KV_HEADS = 2
ATTN_GROUP = ATTN_HEADS // ATTN_KV_HEADS
HEAD_DIM = 128
WINDOW = 128
ATTN_BLOCK = 128
ROPE_THETA = 10000.0
Q_DIM = ATTN_HEADS * HEAD_DIM
KV_DIM = ATTN_KV_HEADS * HEAD_DIM

S5_GROUP = 16
S5_STATE = 64
S5_CHUNK = 16
S5_ROW = S5_CHUNK * S5_GROUP
S5_SCAN_BLOCK = 32
S5_CHUNK_BLOCKS = (176, 96, 48, 32, 16)

HG_HEADS = 16
HG_DK = 128
HG_CHUNK = 128
HG_SUB = 16
HG_NSUB = HG_CHUNK // HG_SUB
HG_EXP_CLAMP = 80.0

N_GROUPS = 8
EXPERTS_PER_GROUP = 8
N_EXPERTS = N_GROUPS * EXPERTS_PER_GROUP
TOP_K = 2
D_EXPERT = 512
MOE_TM = 256
ROUTE_LANES = 128

LN_EPS = 1e-5
RMS_EPS = 1e-6
NEG_INF = -1e30

ROW_TILE = 512
MIX_TILE = 256
MIX_SUB = 128
INPROJ_C_TILES = (1536, 1024, 512)
INPROJ_C_SUB = 512

VMEM_LIMIT = 56 * 1024 * 1024


def _cparams(*sem):
    return pltpu.CompilerParams(dimension_semantics=sem, vmem_limit_bytes=VMEM_LIMIT)


def _pick(n, candidates):
    return next(c for c in candidates if n % c == 0)


def _full(shape):
    nd = len(shape)
    return pl.BlockSpec(shape, lambda *_: (0,) * nd)


def _layer_norm(x, g, b):
    mu = jnp.mean(x, axis=-1, keepdims=True)
    xc = x - mu
    var = jnp.mean(xc * xc, axis=-1, keepdims=True)
    return xc * lax.rsqrt(var + LN_EPS) * g + b


def _inproj_ab_kernel(x_ref, wq_ref, wk_ref, wv_ref, wu_ref, cos_ref, sin_ref,
                      q_ref, k_ref, v_ref, u_ref):
    xb = x_ref[...].astype(BF16)
    cos = cos_ref[...]
    sin = sin_ref[...]

    def rope(t):
        return t * cos + pltpu.roll(t, HEAD_DIM // 2, axis=1) * sin

    q = jnp.dot(xb, wq_ref[...], preferred_element_type=F32)
    for h in range(ATTN_HEADS):
        sl = slice(h * HEAD_DIM, (h + 1) * HEAD_DIM)
        q_ref[:, sl] = rope(q[:, sl]).astype(BF16)
    k = jnp.dot(xb, wk_ref[...], preferred_element_type=F32)
    for h in range(ATTN_KV_HEADS):
        sl = slice(h * HEAD_DIM, (h + 1) * HEAD_DIM)
        k_ref[:, sl] = rope(k[:, sl]).astype(BF16)
    v_ref[...] = jnp.dot(xb, wv_ref[...], preferred_element_type=F32).astype(BF16)
    u_ref[...] = jnp.dot(xb, wu_ref[...], preferred_element_type=F32)


def _inproj_ab(h, wq, wk, wv, wu, cos2, sin2):
    lp, d = h.shape
    tm = ROW_TILE
    s5w = wu.shape[1]
    row = lambda w: pl.BlockSpec((tm, w), lambda i: (i, 0))
    return pl.pallas_call(
        _inproj_ab_kernel,
        grid=(lp // tm,),
        in_specs=[row(d), _full(wq.shape), _full(wk.shape), _full(wv.shape), _full(wu.shape),
                  row(HEAD_DIM), row(HEAD_DIM)],
        out_specs=[row(Q_DIM), row(KV_DIM), row(KV_DIM), row(s5w)],
        out_shape=[jax.ShapeDtypeStruct((lp, Q_DIM), BF16),
                   jax.ShapeDtypeStruct((lp, KV_DIM), BF16),
                   jax.ShapeDtypeStruct((lp, KV_DIM), BF16),
                   jax.ShapeDtypeStruct((lp, s5w), F32)],
        compiler_params=_cparams("parallel"),
        name="inproj_ab",
    )(h, wq, wk, wv, wu, cos2, sin2)


def _attn_kernel(sink_ref, q_ref, k0_ref, k1_ref, k2_ref, v0_ref, v1_ref, v2_ref, km_ref, vm_ref,
                 o_ref, *, lp):
    qb = pl.program_id(0)
    blk = ATTN_BLOCK
    rows = ATTN_GROUP * blk
    scale = HEAD_DIM ** -0.5
    q_row = qb * blk + lax.broadcasted_iota(jnp.int32, (rows, 3 * blk), 0) % blk
    k_row = (qb - 1) * blk + lax.broadcasted_iota(jnp.int32, (rows, 3 * blk), 1)
    vis = (k_row >= FRONT) & (k_row < lp) & (jnp.abs(q_row - k_row) <= WINDOW)
    head_of_row = lax.broadcasted_iota(jnp.int32, (rows, 1), 0) // blk
    for g in range(ATTN_KV_HEADS):
        gs = slice(g * HEAD_DIM, (g + 1) * HEAD_DIM)
        qs = jnp.concatenate(
            [q_ref[:, (g * ATTN_GROUP + r) * HEAD_DIM:(g * ATTN_GROUP + r + 1) * HEAD_DIM]
             for r in range(ATTN_GROUP)], axis=0)
        kband = jnp.concatenate([k0_ref[:, gs], k1_ref[:, gs], k2_ref[:, gs]], axis=0)
        vband = jnp.concatenate([v0_ref[:, gs], v1_ref[:, gs], v2_ref[:, gs]], axis=0)
        nt = (((1,), (1,)), ((), ()))
        s_band = lax.dot_general(qs, kband, nt, preferred_element_type=F32) * scale
        s_band = jnp.where(vis, s_band, NEG_INF)
        s_meta = lax.dot_general(qs, km_ref[:, gs], nt, preferred_element_type=F32) * scale
        sink = jnp.zeros((rows, 1), F32)
        for r in range(ATTN_GROUP):
            sink = jnp.where(head_of_row == r, sink_ref[g * ATTN_GROUP + r], sink)
        m = jnp.maximum(jnp.maximum(jnp.max(s_band, axis=-1, keepdims=True),
                                    jnp.max(s_meta, axis=-1, keepdims=True)), sink)
        e_band = jnp.exp(s_band - m)
        e_meta = jnp.exp(s_meta - m)
        denom = (jnp.sum(e_band, axis=-1, keepdims=True) + jnp.sum(e_meta, axis=-1, keepdims=True)
                 + jnp.exp(sink - m))
        o = (jnp.dot(e_band.astype(BF16), vband, preferred_element_type=F32)
             + jnp.dot(e_meta.astype(BF16), vm_ref[:, gs], preferred_element_type=F32)) / denom
        for r in range(ATTN_GROUP):
            hh = g * ATTN_GROUP + r
            o_ref[:, hh * HEAD_DIM:(hh + 1) * HEAD_DIM] = o[r * blk:(r + 1) * blk].astype(BF16)


def _window_attention(q, k, v, sinks):
    lp = q.shape[0]
    blk = ATTN_BLOCK
    nb = lp // blk
    qspec = pl.BlockSpec((blk, Q_DIM), lambda i, s: (i, 0))
    kv = lambda off: pl.BlockSpec((blk, KV_DIM), lambda i, s: (jnp.clip(i + off, 0, nb - 1), 0))
    meta = pl.BlockSpec((N_META, KV_DIM), lambda i, s: (META_ROW0 // N_META, 0))
    return pl.pallas_call(
        functools.partial(_attn_kernel, lp=lp),
        grid_spec=pltpu.PrefetchScalarGridSpec(
            num_scalar_prefetch=1,
            grid=(nb,),
            in_specs=[qspec, kv(-1), kv(0), kv(1), kv(-1), kv(0), kv(1), meta, meta],
            out_specs=qspec,
        ),
        out_shape=jax.ShapeDtypeStruct((lp, Q_DIM), BF16),
        compiler_params=_cparams("parallel"),
        name="window_attention",
    )(sinks, q, k, k, k, v, v, v, k, v)


def _s5_discretise(lam_re, lam_im, log_step, b_re, b_im, c_re, c_im):
    t = S5_CHUNK
    hi = lax.Precision.HIGHEST
    g = lam_re.shape[1]
    lr = jnp.minimum(lam_re, -1e-4)
    li = lam_im
    step = jnp.exp(log_step)[..., None]
    dr, di = lr * step, li * step
    lags = jnp.arange(t + 1, dtype=F32)[:, None]
    mag = jnp.exp(dr[..., None, :] * lags)
    pr, pi = mag * jnp.cos(di[..., None, :] * lags), mag * jnp.sin(di[..., None, :] * lags)
    ar, ai = pr[:, :, 1], pi[:, :, 1]
    den = lr * lr + li * li
    zr = ((ar - 1.0) * lr + ai * li) / den
    zi = (ai * lr - (ar - 1.0) * li) / den
    br = zr[..., None] * b_re - zi[..., None] * b_im
    bi = zr[..., None] * b_im + zi[..., None] * b_re
    brt, bit = br.transpose(0, 1, 3, 2), bi.transpose(0, 1, 3, 2)
    xr = c_re[:, :, None] * pr[:, :, :t, None] - c_im[:, :, None] * pi[:, :, :t, None]
    xi = c_re[:, :, None] * pi[:, :, :t, None] + c_im[:, :, None] * pr[:, :, :t, None]
    kern = (jnp.einsum('dgip,dglop->dgilo', brt, xr, precision=hi)
            - jnp.einsum('dgip,dglop->dgilo', bit, xi, precision=hi))
    k_lag = jnp.concatenate([jnp.flip(kern[1][:, :, 1:], axis=2), kern[0][:, :, :1] + kern[1][:, :, :1],
                             kern[0][:, :, 1:]], axis=2)
    m_mat = jnp.stack([k_lag[:, :, t - 1 - s:2 * t - 1 - s] for s in range(t)], axis=1)
    m_mat = m_mat.reshape(g, t * S5_GROUP, t * S5_GROUP)

    def flat_e(p_r, p_i, d):
        e_r = p_r[:, :, None] * brt[d][:, None] - p_i[:, :, None] * bit[d][:, None]
        e_i = p_r[:, :, None] * bit[d][:, None] + p_i[:, :, None] * brt[d][:, None]
        flat = lambda e: e.reshape(g, t * S5_GROUP, S5_STATE)
        return jnp.concatenate([flat(e_r), flat(e_i)], axis=-1)
    e_mat = jnp.concatenate([flat_e(pr[0][:, ::-1][:, 1:], pi[0][:, ::-1][:, 1:], 0),
                             flat_e(pr[1][:, :t], pi[1][:, :t], 1)], axis=-1)

    def flat_f(p_r, p_i, d):
        ct_r, ct_i = c_re[d].transpose(0, 2, 1), c_im[d].transpose(0, 2, 1)
        pt_r, pt_i = p_r.transpose(0, 2, 1)[..., None], p_i.transpose(0, 2, 1)[..., None]
        w_r = ct_r[:, :, None] * pt_r - ct_i[:, :, None] * pt_i
        w_i = ct_r[:, :, None] * pt_i + ct_i[:, :, None] * pt_r
        flat = lambda w: w.reshape(g, S5_STATE, t * S5_GROUP)
        return jnp.concatenate([flat(w_r), -flat(w_i)], axis=1)
    f_fw = flat_f(pr[0][:, 1:], pi[0][:, 1:], 0)
    f_bw = flat_f(pr[1][:, ::-1][:, :t], pi[1][:, ::-1][:, :t], 1)
    gh = g // 2
    upper = (jnp.arange(g) >= gh)[:, None, None]
    def place(m):
        z = jnp.zeros_like(m)
        return jnp.concatenate([jnp.where(upper, z, m), jnp.where(upper, m, z)], axis=1)
    p = S5_STATE
    f_mat = jnp.concatenate([place(f_fw[:, :p]), place(f_fw[:, p:]), place(f_bw[:, :p]), place(f_bw[:, p:])],
                            axis=1)
    pack = lambda v: jnp.concatenate([v[:gh], v[gh:]], axis=-1)
    coef = jnp.stack([pack(pr[0][:, t]), pack(pi[0][:, t]), pack(pr[1][:, t]), pack(pi[1][:, t])])
    split = lambda m: m.astype(BF16).reshape((2, gh) + m.shape[1:])
    return split(m_mat), split(e_mat), split(f_mat), coef


def _chunk_rows_to_lanes(u_ref, ncb):
    lane_seg = lax.broadcasted_iota(jnp.int32, (ncb, 128), 1) // S5_GROUP
    rows = [u_ref[pl.ds(s, ncb, stride=S5_CHUNK), :] for s in range(S5_CHUNK)]
    segs = 128 // S5_GROUP
    out = []
    for j in range(segs):
        halves = []
        for h in range(S5_CHUNK // segs):
            acc = None
            for s8 in range(segs):
                r = rows[h * segs + s8]
                k = (s8 - j) % segs
                if k:
                    r = pltpu.roll(r, S5_GROUP * k, axis=1)
                acc = r if acc is None else jnp.where(lane_seg == s8, r, acc)
            halves.append(acc)
        out.append(jnp.concatenate(halves, axis=1))
    return out


def _lanes_to_chunk_rows(ys, y_ref, ncb, add_ref, scale_ref):
    lane_seg = lax.broadcasted_iota(jnp.int32, (ncb, 128), 1) // S5_GROUP
    segs = 128 // S5_GROUP
    for t in range(S5_CHUNK):
        h, t8 = divmod(t, segs)
        acc = None
        for j in range(segs):
            r = ys[j][:, h * 128:(h + 1) * 128]
            k = (j - t8) % segs
            if k:
                r = pltpu.roll(r, S5_GROUP * k, axis=1)
            acc = r if acc is None else jnp.where(lane_seg == j, r, acc)
        idx = pl.ds(t, ncb, stride=S5_CHUNK)
        y_ref[idx, :] = acc + add_ref[idx, :] * scale_ref[...]


def _pair_halves(a, b):
    lane = lax.broadcasted_iota(jnp.int32, a.shape, 1)
    lo = jnp.where(lane < S5_STATE, a, pltpu.roll(b, S5_STATE, axis=1))
    hi = jnp.where(lane < S5_STATE, pltpu.roll(a, S5_STATE, axis=1), b)
    return lo, hi


def _s5_local_kernel(ua_ref, ub_ref, e_ref, u2_ref, f1_ref, f2_ref, b1_ref, b2_ref):
    ncb = u2_ref.shape[2]
    p2 = 2 * S5_STATE
    chunks = [_chunk_rows_to_lanes(ua_ref, ncb), _chunk_rows_to_lanes(ub_ref, ncb)]
    for j in range(u2_ref.shape[1]):
        st = []
        for hh in range(2):
            u2 = chunks[hh][j].astype(BF16)
            u2_ref[hh, j] = u2
            st.append(jnp.dot(u2, e_ref[hh, j], preferred_element_type=F32))
        f1_ref[:, j, :], f2_ref[:, j, :] = _pair_halves(st[0][:, :p2], st[1][:, :p2])
        b1_ref[:, j, :], b2_ref[:, j, :] = _pair_halves(st[0][:, p2:], st[1][:, p2:])


def _s5_scan_kernel(sf1_ref, sf2_ref, sb1_ref, sb2_ref, coef_ref, xf1_ref, xf2_ref, xb1_ref, xb2_ref,
                    st_ref):
    @pl.when(pl.program_id(0) == 0)
    def _():
        st_ref[...] = jnp.zeros_like(st_ref)

    nb = sf1_ref.shape[0]
    crf, cif, crb, cib = coef_ref[0], coef_ref[1], coef_ref[2], coef_ref[3]

    def body(i, carry):
        f1, f2, b1, b2 = carry
        j = nb - 1 - i
        xf1_ref[i] = f1
        xf2_ref[i] = f2
        xb1_ref[j] = b1
        xb2_ref[j] = b2
        return (crf * f1 - cif * f2 + sf1_ref[i], crf * f2 + cif * f1 + sf2_ref[i],
                crb * b1 - cib * b2 + sb1_ref[j], crb * b2 + cib * b1 + sb2_ref[j])

    out = lax.fori_loop(0, nb, body, (st_ref[0], st_ref[1], st_ref[2], st_ref[3]))
    for q in range(4):
        st_ref[q] = out[q]


def _s5_readout_kernel(ua_ref, ub_ref, d_ref, u2_ref, m_ref, f_ref, xf1_ref, xf2_ref, xb1_ref, xb2_ref,
                       ya_ref, yb_ref):
    ncb = u2_ref.shape[2]
    ys = [[], []]
    for j in range(u2_ref.shape[1]):
        x = jnp.concatenate([xf1_ref[:, j, :], xf2_ref[:, j, :], xb1_ref[:, j, :], xb2_ref[:, j, :]],
                            axis=1).astype(BF16)
        for hh in range(2):
            y = jnp.dot(u2_ref[hh, j], m_ref[hh, j], preferred_element_type=F32)
            ys[hh].append(y + jnp.dot(x, f_ref[hh, j], preferred_element_type=F32))
    _lanes_to_chunk_rows(ys[0], ya_ref, ncb, ua_ref, d_ref.at[0])
    _lanes_to_chunk_rows(ys[1], yb_ref, ncb, ub_ref, d_ref.at[1])


def _s5_mixer_pre_glu(u, d_skip, mats):
    m_mat, e_mat, f_mat, coef = mats
    lp, w = u.shape
    g = w // S5_GROUP
    gh = g // 2
    gpb = 128 // S5_GROUP
    nc = lp // S5_CHUNK
    ncb = _pick(nc, S5_CHUNK_BLOCKS)
    rows = ncb * S5_CHUNK
    nlb = w // 128
    grid = (nlb // 2, nc // ncb)
    u_lo = pl.BlockSpec((rows, 128), lambda b, r: (r, b))
    u_hi = pl.BlockSpec((rows, 128), lambda b, r: (r, b + nlb // 2))
    per_group = lambda k, n: pl.BlockSpec((2, gpb, k, n), lambda b, r: (0, b, 0, 0))
    u2_spec = pl.BlockSpec((2, gpb, ncb, S5_ROW), lambda b, r: (0, b, r, 0))
    st_spec = pl.BlockSpec((ncb, gpb, 2 * S5_STATE), lambda b, r: (r, b, 0))
    st_shape = jax.ShapeDtypeStruct((nc, gh, 2 * S5_STATE), F32)
    u2, sf1, sf2, sb1, sb2 = pl.pallas_call(
        _s5_local_kernel,
        grid=grid,
        in_specs=[u_lo, u_hi, per_group(S5_ROW, 4 * S5_STATE)],
        out_specs=[u2_spec, st_spec, st_spec, st_spec, st_spec],
        out_shape=[jax.ShapeDtypeStruct((2, gh, nc, S5_ROW), BF16)] + [st_shape] * 4,
        compiler_params=_cparams("parallel", "parallel"),
        name="s5_local_states",
    )(u, u, e_mat)
    sb = S5_SCAN_BLOCK
    nblk = nc // sb
    fwd = pl.BlockSpec((sb, gh, 2 * S5_STATE), lambda i: (i, 0, 0))
    bwd = pl.BlockSpec((sb, gh, 2 * S5_STATE), lambda i: (nblk - 1 - i, 0, 0))
    xf1, xf2, xb1, xb2 = pl.pallas_call(
        _s5_scan_kernel,
        grid=(nblk,),
        in_specs=[fwd, fwd, bwd, bwd, _full(coef.shape)],
        out_specs=[fwd, fwd, bwd, bwd],
        out_shape=[st_shape] * 4,
        scratch_shapes=[pltpu.VMEM((4, gh, 2 * S5_STATE), F32)],
        compiler_params=_cparams("arbitrary"),
        name="s5_chunk_scan",
    )(sf1, sf2, sb1, sb2, coef)
    d2 = d_skip.reshape(2, 1, nlb // 2 * 128)
    y_lo, y_hi = pl.pallas_call(
        _s5_readout_kernel,
        grid=grid,
        in_specs=[u_lo, u_hi, pl.BlockSpec((2, 1, 128), lambda b, r: (0, 0, b)), u2_spec,
                  per_group(S5_ROW, S5_ROW), per_group(8 * S5_STATE, S5_ROW),
                  st_spec, st_spec, st_spec, st_spec],
        out_specs=[pl.BlockSpec((rows, 128), lambda b, r: (r, b))] * 2,
        out_shape=[jax.ShapeDtypeStruct((lp, w // 2), F32)] * 2,
        compiler_params=_cparams("parallel", "parallel"),
        name="s5_readout",
    )(u, u, d2, u2, m_mat, f_mat, xf1, xf2, xb1, xb2)
    return y_lo, y_hi


def _route(hn, wr_ref, br_ref):
    hn_hi = hn.astype(BF16)
    hn_lo = (hn - hn_hi.astype(F32)).astype(BF16)
    p_hi = jnp.dot(hn_hi, wr_ref[...], preferred_element_type=F32)
    p_lo = jnp.dot(hn_lo, wr_ref[:, :ROUTE_LANES], preferred_element_type=F32)
    logits = p_hi[:, :ROUTE_LANES] + (p_hi[:, ROUTE_LANES:] + p_lo) + br_ref[...]
    lane_i = lax.broadcasted_iota(jnp.int32, logits.shape, 1)
    lane = lane_i.astype(F32)
    lane_grp = ((lane_i - N_GROUPS) // EXPERTS_PER_GROUP).astype(F32)
    big = float(ROUTE_LANES)
    g_log = jnp.where(lane_i < N_GROUPS, logits, -jnp.inf)
    g_max = jnp.max(g_log, axis=-1, keepdims=True)
    grp = jnp.min(jnp.where(g_log == g_max, lane, big), axis=-1, keepdims=True)
    p_grp = 1.0 / jnp.sum(jnp.exp(g_log - g_max), axis=-1, keepdims=True)
    in_grp = (lane_i >= N_GROUPS) & (lane_i < N_GROUPS + N_EXPERTS) & (lane_grp == grp)
    e_log = jnp.where(in_grp, logits, -jnp.inf)
    v1 = jnp.max(e_log, axis=-1, keepdims=True)
    i1 = jnp.min(jnp.where(e_log == v1, lane, big), axis=-1, keepdims=True)
    e_log2 = jnp.where(lane == i1, -jnp.inf, e_log)
    v2 = jnp.max(e_log2, axis=-1, keepdims=True)
    i2 = jnp.min(jnp.where(e_log2 == v2, lane, big), axis=-1, keepdims=True)
    e21 = jnp.exp(v2 - v1)
    w1 = p_grp / (1.0 + e21)
    w2 = p_grp * e21 / (1.0 + e21)
    route = jnp.where(lane_i == 0, w1, 0.0)
    route = jnp.where(lane_i == 1, w2, route)
    route = jnp.where(lane_i == 2, i1 - N_GROUPS, route)
    route = jnp.where(lane_i == 3, i2 - N_GROUPS, route)
    return route


def _slab_spec(tm, d):
    return pl.BlockSpec((tm * (d // 128), 128), lambda i: (i, 0))


def _slab_shape(rows, d):
    return jax.ShapeDtypeStruct((rows * (d // 128), 128), F32)


def _store_row_slabs(ref, rows, val):
    nl = val.shape[1] // 128
    for c in range(nl):
        ref[pl.ds(rows.start * nl + c, rows.stop - rows.start, stride=nl), :] = val[:, c * 128:(c + 1) * 128]


def _sub_rows(tm):
    return [slice(r, r + MIX_SUB) for r in range(0, tm, MIX_SUB)]


def _mix_ab_kernel(h_ref, a_ref, ylo_ref, yhi_ref, wglu_ref, bglu_ref, woa_ref, wos_ref,
                   g_ref, b_ref, wr_ref, br_ref, hn_ref, hs_ref, route_ref, *, alpha):
    for rows in _sub_rows(h_ref.shape[0]):
        y = jnp.concatenate([ylo_ref[rows, :], yhi_ref[rows, :]], axis=1)
        z = 0.5 * y * (1.0 + jnp.tanh(math.sqrt(2.0 / math.pi) * (y + 0.044715 * (y * y * y))))
        gate = jnp.dot(z.astype(BF16), wglu_ref[...], preferred_element_type=F32) + bglu_ref[...]
        s_out = z * jax.nn.sigmoid(gate)
        mix = (jnp.dot(a_ref[rows, :], woa_ref[...], preferred_element_type=F32)
               + jnp.dot(s_out.astype(BF16), wos_ref[...], preferred_element_type=F32))
        hn = _layer_norm(alpha * h_ref[rows, :] + mix, g_ref[...], b_ref[...])
        hn_ref[rows, :] = hn
        _store_row_slabs(hs_ref, rows, hn)
        route_ref[rows, :] = _route(hn, wr_ref, br_ref)


def _mix_ab(h, a_out, y_lo, y_hi, w_glu, b_glu, wo_a, wo_s, ln_g, ln_b, w_route, b_route, alpha):
    lp, d = h.shape
    tm = MIX_TILE
    row = lambda c: pl.BlockSpec((tm, c), lambda i: (i, 0))
    return pl.pallas_call(
        functools.partial(_mix_ab_kernel, alpha=alpha),
        grid=(lp // tm,),
        in_specs=[row(d), row(Q_DIM), row(y_lo.shape[1]), row(y_hi.shape[1]), _full(w_glu.shape),
                  _full(b_glu.shape), _full(wo_a.shape), _full(wo_s.shape), _full(ln_g.shape),
                  _full(ln_b.shape), _full(w_route.shape), _full(b_route.shape)],
        out_specs=[row(d), _slab_spec(tm, d), row(ROUTE_LANES)],
        out_shape=[jax.ShapeDtypeStruct((lp, d), F32), _slab_shape(lp, d),
                   jax.ShapeDtypeStruct((lp, ROUTE_LANES), F32)],
        compiler_params=_cparams("parallel"),
        name="mix_ab",
    )(h, a_out, y_lo, y_hi, w_glu, b_glu, wo_a, wo_s, ln_g, ln_b, w_route, b_route)


def _mix_c_kernel(h_ref, ofw_ref, obw_ref, gate_ref, ng_ref, wo_ref, g_ref, b_ref, wr_ref, br_ref,
                  hn_ref, hs_ref, route_ref, *, alpha):
    for rows in _sub_rows(h_ref.shape[0]):
        parts = []
        for hh in range(HG_HEADS):
            o = ofw_ref[hh, rows, :] + obw_ref[hh, rows, :]
            o = o * lax.rsqrt(jnp.mean(o * o, axis=-1, keepdims=True) + RMS_EPS)
            o = o * ng_ref[hh] * jax.nn.sigmoid(gate_ref[0, hh, rows, :].astype(F32))
            parts.append(o.astype(BF16))
        mix = jnp.dot(jnp.concatenate(parts, axis=1), wo_ref[...], preferred_element_type=F32)
        hn = _layer_norm(alpha * h_ref[rows, :] + mix, g_ref[...], b_ref[...])
        hn_ref[rows, :] = hn
        _store_row_slabs(hs_ref, rows, hn)
        route_ref[rows, :] = _route(hn, wr_ref, br_ref)


def _mix_c(h, o_fw, o_bw, plain, norm_g, wo, ln_g, ln_b, w_route, b_route, alpha):
    lp, d = h.shape
    tm = MIX_TILE
    row = lambda c: pl.BlockSpec((tm, c), lambda i: (i, 0))
    hm = pl.BlockSpec((HG_HEADS, tm, HG_DK), lambda i: (0, i, 0))
    gate = pl.BlockSpec((1, HG_HEADS, tm, HG_DK), lambda i: (2, 0, i, 0))
    return pl.pallas_call(
        functools.partial(_mix_c_kernel, alpha=alpha),
        grid=(lp // tm,),
        in_specs=[row(d), hm, hm, gate, _full(norm_g.shape), _full(wo.shape), _full(ln_g.shape),
                  _full(ln_b.shape), _full(w_route.shape), _full(b_route.shape)],
        out_specs=[row(d), _slab_spec(tm, d), row(ROUTE_LANES)],
        out_shape=[jax.ShapeDtypeStruct((lp, d), F32), _slab_shape(lp, d),
                   jax.ShapeDtypeStruct((lp, ROUTE_LANES), F32)],
        compiler_params=_cparams("parallel"),
        name="mix_c",
    )(h, o_fw, o_bw, plain, norm_g, wo, ln_g, ln_b, w_route, b_route)


def _moe_plan(route, n_rows_valid_from):
    lp = route.shape[0]
    tm = MOE_TM
    valid = (jnp.arange(lp) >= n_rows_valid_from)
    expert = route[:, 2:4].astype(jnp.int32)
    flat_e = jnp.where(valid[:, None], expert, N_EXPERTS).reshape(-1)
    onehot = (flat_e[:, None] == jnp.arange(N_EXPERTS)[None, :]).astype(jnp.int32)
    csum = jnp.cumsum(onehot, axis=0)
    counts = csum[-1]
    rank = jnp.sum(jnp.where(onehot > 0, csum - 1, 0), axis=1)
    padded = (counts + tm - 1) // tm * tm
    pend = jnp.cumsum(padded)
    pstart = pend - padded
    e_safe = jnp.minimum(flat_e, N_EXPERTS - 1)
    dest = jnp.where(flat_e < N_EXPERTS, pstart[e_safe] + rank, 0)
    n_tokens = lp - n_rows_valid_from
    n_blocks = -(-(n_tokens * TOP_K + N_EXPERTS * (tm - 1)) // tm)
    rows = n_blocks * tm
    tok = jnp.repeat(jnp.arange(lp, dtype=jnp.int32), TOP_K)
    scatter_to = jnp.where(flat_e < N_EXPERTS, dest, rows)
    tok_of_row = jnp.zeros((rows,), jnp.int32).at[scatter_to].set(tok, mode='drop')
    n_used = (pend[-1] // tm).astype(jnp.int32)
    blk = jnp.minimum(jnp.arange(n_blocks, dtype=jnp.int32), jnp.maximum(n_used - 1, 0))
    block_expert = jnp.minimum(jnp.searchsorted(pend, blk * tm, side='right'),
                               N_EXPERTS - 1).astype(jnp.int32)
    used = counts > 0
    pos = jnp.cumsum(used.astype(jnp.int32)) - 1
    eid = jnp.arange(N_EXPERTS, dtype=jnp.int32)
    later = jnp.where(used[None, :] & (eid[None, :] > eid[:, None]), eid[None, :], N_EXPERTS)
    nxt_e = jnp.min(later, axis=1)
    nxt_e = jnp.where(nxt_e == N_EXPERTS, -1, nxt_e).astype(jnp.int32)
    w_slot = (pos[block_expert] % 2).astype(jnp.int32)
    nxt = nxt_e[block_expert]
    weights = jnp.where(valid[:, None], route[:, 0:2], 0.0)
    return (block_expert, n_used.reshape(1), w_slot, nxt, tok_of_row.reshape(n_blocks, 1, tm),
            dest.reshape(lp, TOP_K).astype(jnp.int32), weights)


def _moe_expert_kernel(be_ref, nu_ref, ws_ref, nx_ref, tok0_ref, tokn_ref, h_hbm, wgu_hbm, wd_hbm,
                       y_ref, xbuf, xsem, wgu_f, wd_f, wsem, wgu_bf, wd_bf, *, layer):
    i = pl.program_id(0)
    nl = wgu_f.shape[1] // 128
    tm = xbuf.shape[1] // nl
    n_used = nu_ref[0]

    def gather(tok_ref, slot):
        def issue(r, carry):
            src = h_hbm.at[pl.ds(pl.multiple_of(tok_ref[0, 0, r], nl), nl)]
            pltpu.make_async_copy(src, xbuf.at[slot, pl.ds(pl.multiple_of(r * nl, nl), nl)],
                                  xsem.at[slot]).start()
            return carry
        lax.fori_loop(0, tm, issue, 0, unroll=8)

    def gather_wait(slot):
        pltpu.make_async_copy(h_hbm.at[pl.ds(0, tm * nl)], xbuf.at[slot], xsem.at[slot]).wait()

    def weight_copies(e, s):
        return (pltpu.make_async_copy(wgu_hbm.at[layer, e], wgu_f.at[s], wsem.at[0, s]),
                pltpu.make_async_copy(wd_hbm.at[layer, e], wd_f.at[s], wsem.at[1, s]))

    @pl.when(i == 0)
    def _():
        for cp in weight_copies(be_ref[0], 0):
            cp.start()
        gather(tok0_ref, 0)

    @pl.when(i < n_used)
    def _():
        slot = i % 2
        e = be_ref[i]
        first = jnp.logical_or(i == 0, be_ref[jnp.maximum(i - 1, 0)] != e)

        gather_wait(slot)
        gather(tokn_ref, 1 - slot)

        @pl.when(first)
        def _():
            s = ws_ref[i]
            for cp in weight_copies(e, s):
                cp.wait()
            nxt = nx_ref[i]

            @pl.when(nxt >= 0)
            def _():
                for cp in weight_copies(nxt, 1 - s):
                    cp.start(priority=1)

            wgu_bf[...] = wgu_f[s].astype(BF16)
            wd_bf[...] = wd_f[s].astype(BF16)

        x = jnp.concatenate([xbuf[slot, pl.ds(c, tm, stride=nl), :] for c in range(nl)], axis=1)
        hgu = jnp.dot(x.astype(BF16), wgu_bf[...], preferred_element_type=F32)
        act = jax.nn.silu(hgu[:, :D_EXPERT]) * hgu[:, D_EXPERT:]
        y = jnp.dot(act.astype(BF16), wd_bf[...], preferred_element_type=F32)
        for c in range(y.shape[1] // 128):
            y_ref[pl.ds(c, tm, stride=y.shape[1] // 128), :] = y[:, c * 128:(c + 1) * 128]

        @pl.when(i == n_used - 1)
        def _():
            gather_wait(1 - slot)

    @pl.when(i >= n_used)
    def _():
        y_ref[...] = jnp.zeros_like(y_ref)


def _moe_experts(h_slabs, plan, w_gate_up, w_down, layer):
    block_expert, n_used, w_slot, nxt, tok_of_row, _, _ = plan
    d = w_gate_up.shape[2]
    tok_of_row = tok_of_row * (d // 128)
    tm = MOE_TM
    n_blocks = tok_of_row.shape[0]
    nxt_blk = lambda i, be, nu, ws, nx: (jnp.minimum(i + 1, jnp.maximum(nu[0] - 1, 0)), 0, 0)
    smem = lambda imap: pl.BlockSpec((1, 1, tm), imap, memory_space=pltpu.SMEM)
    anyspace = pl.BlockSpec(memory_space=pl.ANY)
    return pl.pallas_call(
        functools.partial(_moe_expert_kernel, layer=layer),
        grid_spec=pltpu.PrefetchScalarGridSpec(
            num_scalar_prefetch=4,
            grid=(n_blocks,),
            in_specs=[smem(lambda i, be, nu, ws, nx: (0, 0, 0)), smem(nxt_blk),
                      anyspace, anyspace, anyspace],
            out_specs=pl.BlockSpec((tm * (d // 128), 128), lambda i, be, nu, ws, nx: (i, 0)),
            scratch_shapes=[pltpu.VMEM((2, tm * (d // 128), 128), F32), pltpu.SemaphoreType.DMA((2,)),
                            pltpu.VMEM((2, d, 2 * D_EXPERT), F32), pltpu.VMEM((2, D_EXPERT, d), F32),
                            pltpu.SemaphoreType.DMA((2, 2)),
                            pltpu.VMEM((d, 2 * D_EXPERT), BF16), pltpu.VMEM((D_EXPERT, d), BF16)],
        ),
        out_shape=jax.ShapeDtypeStruct((n_blocks * tm * (d // 128), 128), F32),
        compiler_params=_cparams("arbitrary"),
        name="moe_experts",
    )(block_expert, n_used, w_slot, nxt, tok_of_row, tok_of_row, h_slabs, w_gate_up, w_down)


def _moe_combine_kernel(d0_ref, d1_ref, d0n_ref, d1n_ref, h_ref, w_ref, y_hbm, g_ref, b_ref, *rest,
                        alpha, with_bf16):
    if with_bf16:
        o_ref, ob_ref, buf, sem = rest
    else:
        o_ref, buf, sem = rest
    i = pl.program_id(0)
    tm, d = h_ref.shape
    nl = d // 128
    slot = i % 2

    def gather(a_ref, b_ref_, s):
        def issue(r, carry):
            for k, idx_ref in enumerate((a_ref, b_ref_)):
                src = y_hbm.at[pl.ds(pl.multiple_of(idx_ref[0, 0, r], nl), nl)]
                pltpu.make_async_copy(src, buf.at[s, k, pl.ds(pl.multiple_of(r * nl, nl), nl)],
                                      sem.at[s, k]).start()
            return carry
        lax.fori_loop(0, tm, issue, 0, unroll=8)

    @pl.when(i == 0)
    def _():
        gather(d0_ref, d1_ref, 0)

    for k in range(TOP_K):
        pltpu.make_async_copy(y_hbm.at[pl.ds(0, tm * nl)], buf.at[slot, k], sem.at[slot, k]).wait()

    @pl.when(i + 1 < pl.num_programs(0))
    def _():
        gather(d0n_ref, d1n_ref, 1 - slot)

    w = w_ref[...]
    cols = []
    for c in range(nl):
        rows = pl.ds(c, tm, stride=nl)
        ffn = w[:, 0:1] * buf[slot, 0, rows, :] + w[:, 1:2] * buf[slot, 1, rows, :]
        cols.append(alpha * h_ref[:, c * 128:(c + 1) * 128] + ffn)
    out = _layer_norm(jnp.concatenate(cols, axis=1), g_ref[...], b_ref[...])
    o_ref[...] = out
    if with_bf16:
        ob_ref[...] = out.astype(BF16)


def _moe_combine(h, plan, y_buf, ln_g, ln_b, alpha, first_row, with_bf16=False):
    dest, weights = plan[-2], plan[-1]
    lp, d = h.shape
    tm = MIX_TILE
    off = first_row // tm
    nt = lp // tm
    d0 = (dest[:, 0] * (d // 128)).reshape(nt, 1, tm)
    d1 = (dest[:, 1] * (d // 128)).reshape(nt, 1, tm)
    cur = pl.BlockSpec((1, 1, tm), lambda i: (off, 0, 0), memory_space=pltpu.SMEM)
    nxt = pl.BlockSpec((1, 1, tm), lambda i: (jnp.minimum(i + 1 + off, nt - 1), 0, 0),
                       memory_space=pltpu.SMEM)
    out_spec = pl.BlockSpec((tm, d), lambda i: (i, 0))
    out_shape = jax.ShapeDtypeStruct((lp - first_row, d), F32)
    if with_bf16:
        out_spec = [out_spec, out_spec]
        out_shape = [out_shape, jax.ShapeDtypeStruct((lp - first_row, d), BF16)]
    return pl.pallas_call(
        functools.partial(_moe_combine_kernel, alpha=alpha, with_bf16=with_bf16),
        grid=(nt - off,),
        in_specs=[cur, cur, nxt, nxt, pl.BlockSpec((tm, d), lambda i: (i + off, 0)),
                  pl.BlockSpec((tm, TOP_K), lambda i: (i + off, 0)),
                  pl.BlockSpec(memory_space=pl.ANY), _full(ln_g.shape), _full(ln_b.shape)],
        out_specs=out_spec,
        out_shape=out_shape,
        scratch_shapes=[pltpu.VMEM((2, TOP_K, tm * (d // 128), 128), F32),
                        pltpu.SemaphoreType.DMA((2, TOP_K))],
        compiler_params=_cparams("arbitrary"),
        name="moe_combine",
    )(d0, d1, d0, d1, h, weights, y_buf, ln_g, ln_b)


def _store_heads(ref, seg_rows, val):
    for hh in range(val.shape[1] // HG_DK):
        ref[0, hh, seg_rows, :] = val[:, hh * HG_DK:(hh + 1) * HG_DK].astype(ref.dtype)


def _inproj_plain_kernel(x_ref, w_ref, o_ref):
    for r in range(0, x_ref.shape[0], INPROJ_C_SUB):
        rows = slice(r, r + INPROJ_C_SUB)
        _store_heads(o_ref, rows, jnp.dot(x_ref[rows, :], w_ref[...], preferred_element_type=F32))


def _inproj_forget_kernel(x_ref, w_ref, lb_ref, k_ref, g_ref):
    i = pl.program_id(0)
    tm = x_ref.shape[0]
    lb = lb_ref[...]
    for r in range(0, tm, INPROJ_C_SUB):
        rows = slice(r, r + INPROJ_C_SUB)
        acc = jnp.dot(x_ref[rows, :], w_ref[...], preferred_element_type=F32)
        f = lb + (1.0 - lb) * jax.nn.sigmoid(acc)
        real = (i * tm + r + lax.broadcasted_iota(jnp.int32, (INPROJ_C_SUB, 1), 0)) >= META_ROW0
        _store_heads(k_ref, rows, jnp.where(real, 1.0 - f, 0.0))
        _store_heads(g_ref, rows, jnp.where(real, jnp.log(f), 0.0))


def _inproj_c(hb, w, lb):
    lp, d = hb.shape
    tm = _pick(lp, INPROJ_C_TILES)
    tn = 512
    tps = d // tn
    hpt = tn // HG_DK
    x_spec = pl.BlockSpec((tm, d), lambda i, j: (i, 0))
    out_spec = pl.BlockSpec((1, hpt, tm, HG_DK), lambda i, j: (j // tps, j % tps, i, 0))
    hm = lambda n, dt: jax.ShapeDtypeStruct((n, HG_HEADS, lp, HG_DK), dt)
    plain_col = lambda i, j: (0, jnp.where(j >= 2 * tps, j + 2 * tps, j))
    plain = pl.pallas_call(
        _inproj_plain_kernel,
        grid=(lp // tm, 3 * tps),
        in_specs=[x_spec, pl.BlockSpec((d, tn), plain_col)],
        out_specs=out_spec,
        out_shape=hm(3, BF16),
        compiler_params=_cparams("parallel", "arbitrary"),
        name="inproj_c_plain",
    )(hb, w)
    k, g = pl.pallas_call(
        _inproj_forget_kernel,
        grid=(lp // tm, 2 * tps),
        in_specs=[x_spec, pl.BlockSpec((d, tn), lambda i, j: (0, j + 2 * tps)),
                  pl.BlockSpec((1, tn), lambda i, j: (0, j % tps))],
        out_specs=[out_spec, out_spec],
        out_shape=[hm(2, BF16), hm(2, F32)],
        compiler_params=_cparams("parallel", "arbitrary"),
        name="inproj_c_forget",
    )(hb, w, lb)
    return plain, k, g


def _hgrn_chunk(q, k, v, g, state_t, reverse):
    c = HG_CHUNK
    row = lax.broadcasted_iota(jnp.int32, (c, HG_DK), 0)
    b = g
    sh = 1
    while sh < c:
        if not reverse:
            b = b + jnp.where(row >= sh, pltpu.roll(b, sh, axis=0), 0.0)
        else:
            b = b + jnp.where(row < c - sh, pltpu.roll(b, c - sh, axis=0), 0.0)
        sh *= 2
    r_i = lax.broadcasted_iota(jnp.int32, (c, c), 0)
    c_i = lax.broadcasted_iota(jnp.int32, (c, c), 1)
    causal = (c_i >= r_i) if reverse else (c_i <= r_i)
    sub = HG_SUB
    mid = sub // 2
    blocks = lambda x: [x[j * sub:(j + 1) * sub] for j in range(HG_NSUB)]
    ref = [b[j * sub + mid:j * sub + mid + 1, :] for j in range(HG_NSUB)]
    b_end = b[0:1, :] if reverse else b[c - 1:c, :]
    qd, ku, q0, k_end = [], [], [], []
    for j, (qj, kj, bj) in enumerate(zip(blocks(q), blocks(k), blocks(b))):
        dj = bj - ref[j]
        qd_j = qj * jnp.exp(jnp.minimum(dj, HG_EXP_CLAMP))
        ku_j = kj * jnp.exp(jnp.minimum(-dj, HG_EXP_CLAMP))
        qd.append(qd_j.astype(BF16))
        ku.append(ku_j)
        q0.append(qd_j * jnp.exp(ref[j]))
        k_end.append(ku_j * jnp.exp(b_end - ref[j]))
    nt = (((1,), (1,)), ((), ()))
    zero = jnp.zeros((sub, HG_DK), BF16)
    score_rows = []
    for j in range(HG_NSUB):
        parts = []
        for i in range(HG_NSUB):
            if i == j:
                parts.append(ku[i].astype(BF16))
            elif (i > j) if reverse else (i < j):
                parts.append((ku[i] * jnp.exp(ref[j] - ref[i])).astype(BF16))
            else:
                parts.append(zero)
        k_ext = jnp.concatenate(parts, axis=0)
        score_rows.append(lax.dot_general(qd[j], k_ext, nt, preferred_element_type=F32))
    scores = jnp.concatenate(score_rows, axis=0)
    scores = jnp.where(causal, scores, 0.0)
    o = jnp.dot(scores.astype(BF16), v.astype(BF16), preferred_element_type=F32)
    o += lax.dot_general(jnp.concatenate(q0, axis=0).astype(BF16), state_t.astype(BF16), nt,
                         preferred_element_type=F32)
    new_state = jnp.exp(b_end) * state_t + jnp.dot(
        v.T.astype(BF16), jnp.concatenate(k_end, axis=0).astype(BF16), preferred_element_type=F32)
    return o, new_state


def _hgrn_kernel(qf_ref, vf_ref, k1_ref, g1_ref, qb_ref, vb_ref, k2_ref, g2_ref,
                 ofw_ref, obw_ref, st_ref):
    @pl.when(pl.program_id(0) == 0)
    def _():
        st_ref[...] = jnp.zeros_like(st_ref)

    def head(hh, carry):
        o, s = _hgrn_chunk(qf_ref[0, hh].astype(F32), k1_ref[0, hh].astype(F32),
                           vf_ref[0, hh].astype(F32), g1_ref[0, hh], st_ref[0, hh], reverse=False)
        ofw_ref[hh] = o
        st_ref[0, hh] = s
        o, s = _hgrn_chunk(qb_ref[0, hh].astype(F32), k2_ref[0, hh].astype(F32),
                           vb_ref[0, hh].astype(F32), g2_ref[0, hh], st_ref[1, hh], reverse=True)
        obw_ref[hh] = o
        st_ref[1, hh] = s
        return carry

    lax.fori_loop(0, HG_HEADS, head, 0, unroll=4)


def _hgrn(plain, k, g):
    _, _, lp, dk = plain.shape
    c = HG_CHUNK
    nc = lp // c
    fwd = lambda seg: pl.BlockSpec((1, HG_HEADS, c, dk), lambda i: (seg, 0, i, 0))
    bwd = lambda seg: pl.BlockSpec((1, HG_HEADS, c, dk), lambda i: (seg, 0, nc - 1 - i, 0))
    out_f = pl.BlockSpec((HG_HEADS, c, dk), lambda i: (0, i, 0))
    out_b = pl.BlockSpec((HG_HEADS, c, dk), lambda i: (0, nc - 1 - i, 0))
    return pl.pallas_call(
        _hgrn_kernel,
        grid=(nc,),
        in_specs=[fwd(0), fwd(1), fwd(0), fwd(0), bwd(0), bwd(1), bwd(1), bwd(1)],
        out_specs=[out_f, out_b],
        out_shape=[jax.ShapeDtypeStruct((HG_HEADS, lp, dk), F32)] * 2,
        scratch_shapes=[pltpu.VMEM((2, HG_HEADS, dk, dk), F32)],
        compiler_params=_cparams("arbitrary"),
        name="hgrn2_recurrence",
    )(plain, plain, k, g, plain, plain, k, g)


def _route_params(w_group, b_group, w_expert, b_expert):
    d = w_group.shape[0]
    pad = ROUTE_LANES - N_GROUPS - N_EXPERTS
    w = jnp.concatenate([w_group, w_expert, jnp.zeros((d, pad), F32)], axis=1)
    b = jnp.concatenate([b_group, b_expert, jnp.zeros((pad,), F32)])[None, :]
    w_hi = w.astype(BF16)
    w_lo = (w - w_hi.astype(F32)).astype(BF16)
    return jnp.concatenate([w_hi, w_lo], axis=1), b


def _rope_tables(lp):
    half = HEAD_DIM // 2
    pos = jnp.maximum(jnp.arange(lp) - META_ROW0, 0).astype(F32)
    inv = ROPE_THETA ** (-jnp.arange(half, dtype=F32) * 2.0 / HEAD_DIM)
    ang = pos[:, None] * inv[None, :]
    cos, sin = jnp.cos(ang), jnp.sin(ang)
    return jnp.concatenate([cos, cos], axis=1), jnp.concatenate([-sin, sin], axis=1)


def kernel(x, meta_tokens, w_in_ab, w_out_ab, attn_sinks, s5_lam_re, s5_lam_im, s5_log_step, s5_b_re, s5_b_im, s5_c_re, s5_c_im, s5_d, s5_w_glu, s5_b_glu, w_in_c, w_out_c, hgrn_lb_logits, hgrn_norm_g, ln_mix_g, ln_mix_b, ln_ffn_g, ln_ffn_b, moe_w_group, moe_b_group, moe_w_expert, moe_b_expert, moe_w_gate_up, moe_w_down):
    batch, seq, d = x.shape
    assert batch == 1 and seq % FRONT == 0
    depth = ln_mix_g.shape[0]
    assert depth == 2
    alpha = (2.0 * depth) ** 0.25
    lp = FRONT + seq
    row2 = lambda t: t[None, :]

    h = jnp.concatenate([jnp.zeros((META_ROW0, d), F32), meta_tokens.astype(F32), x[0]], axis=0)

    w_in = w_in_ab[0].astype(BF16)
    s5w = s5_d.shape[1]
    wq, wk, wv, wu = (w_in[:, :Q_DIM], w_in[:, Q_DIM:Q_DIM + KV_DIM],
                      w_in[:, Q_DIM + KV_DIM:Q_DIM + 2 * KV_DIM], w_in[:, Q_DIM + 2 * KV_DIM:])
    cos2, sin2 = _rope_tables(lp)
    q, k, v, u = _inproj_ab(h, wq, wk, wv, wu, cos2, sin2)
    a_out = _window_attention(q, k, v, attn_sinks[0])
    mats = _s5_discretise(s5_lam_re[0], s5_lam_im[0], s5_log_step[0], s5_b_re[0], s5_b_im[0],
                          s5_c_re[0], s5_c_im[0])
    y_lo, y_hi = _s5_mixer_pre_glu(u, s5_d[0], mats)
    wo = w_out_ab[0].astype(BF16)
    w_route, b_route = _route_params(moe_w_group[0], moe_b_group[0], moe_w_expert[0], moe_b_expert[0])
    h, h_slabs, route = _mix_ab(h, a_out, y_lo, y_hi, s5_w_glu[0].astype(BF16), row2(s5_b_glu[0]),
                       wo[:Q_DIM], wo[Q_DIM:], row2(ln_mix_g[0]), row2(ln_mix_b[0]), w_route, b_route,
                       alpha)
    plan = _moe_plan(route, META_ROW0)
    y_buf = _moe_experts(h_slabs, plan, moe_w_gate_up, moe_w_down, 0)
    h, hb = _moe_combine(h, plan, y_buf, row2(ln_ffn_g[0]), row2(ln_ffn_b[0]), alpha, 0, with_bf16=True)

    lb_probs = jax.nn.softmax(hgrn_lb_logits.astype(F32), axis=0)
    lb = (jnp.cumsum(lb_probs, axis=0) - lb_probs[0])[1]
    plain, hk, hg = _inproj_c(hb, w_in_c[0].astype(BF16), row2(lb))
    o_fw, o_bw = _hgrn(plain, hk, hg)
    w_route, b_route = _route_params(moe_w_group[1], moe_b_group[1], moe_w_expert[1], moe_b_expert[1])
    h, h_slabs, route = _mix_c(h, o_fw, o_bw, plain, hgrn_norm_g[0].reshape(HG_HEADS, 1, HG_DK),
                      w_out_c[0].astype(BF16), row2(ln_mix_g[1]), row2(ln_mix_b[1]), w_route, b_route,
                      alpha)
    plan = _moe_plan(route, META_ROW0)
    y_buf = _moe_experts(h_slabs, plan, moe_w_gate_up, moe_w_down, 1)
    out = _moe_combine(h, plan, y_buf, row2(ln_ffn_g[1]), row2(ln_ffn_b[1]), alpha, FRONT)
    return out[None]
```

```python
import functools
import math

import jax
import jax.numpy as jnp
from jax import lax
from jax.experimental import pallas as pl
from jax.experimental.pallas import tpu as pltpu

F32 = jnp.float32
BF16 = jnp.bfloat16

N_META = 16
FRONT = 512
META_ROW0 = FRONT - N_META

ATTN_HEADS = 8
ATTN_KV_HEADS = 2
ATTN_GROUP = ATTN_HEADS // ATTN_KV_HEADS
HEAD_DIM = 128
WINDOW = 128
ATTN_BLOCK = 128
ROPE_THETA = 10000.0
Q_DIM = ATTN_HEADS * HEAD_DIM
KV_DIM = ATTN_KV_HEADS * HEAD_DIM

S5_GROUP = 16
S5_STATE = 64
S5_CHUNK = 16
S5_ROW = S5_CHUNK * S5_GROUP
S5_SCAN_BLOCK = 32
S5_CHUNK_BLOCKS = (176, 96, 48, 32, 16)

HG_HEADS = 16
HG_DK = 128
HG_CHUNK = 128
HG_SUB = 16
HG_NSUB = HG_CHUNK // HG_SUB
HG_EXP_CLAMP = 80.0

N_GROUPS = 8
EXPERTS_PER_GROUP = 8
N_EXPERTS = N_GROUPS * EXPERTS_PER_GROUP
TOP_K = 2
D_EXPERT = 512
MOE_TM = 256
ROUTE_LANES = 128

LN_EPS = 1e-5
RMS_EPS = 1e-6
NEG_INF = -1e30

ROW_TILE = 512
MIX_TILE = 256
MIX_SUB = 128
INPROJ_C_TILES = (1536, 1024, 512)
INPROJ_C_SUB = 512

VMEM_LIMIT = 56 * 1024 * 1024


def _cparams(*sem):
    return pltpu.CompilerParams(dimension_semantics=sem, vmem_limit_bytes=VMEM_LIMIT)


def _pick(n, candidates):
    return next(c for c in candidates if n % c == 0)


def _full(shape):
    nd = len(shape)
    return pl.BlockSpec(shape, lambda *_: (0,) * nd)


def _layer_norm(x, g, b):
    mu = jnp.mean(x, axis=-1, keepdims=True)
    xc = x - mu
    var = jnp.mean(xc * xc, axis=-1, keepdims=True)
    return xc * lax.rsqrt(var + LN_EPS) * g + b


def _inproj_ab_kernel(front_ref, x_ref, wq_ref, wk_ref, wv_ref, wu_ref, cos_ref, sin_ref,
                      q_ref, k_ref, v_ref, u_ref):
    xb = jnp.where(pl.program_id(0) == 0, front_ref[...], x_ref[...]).astype(BF16)
    cos = cos_ref[...]
    sin = sin_ref[...]

    def rope(t):
        return t * cos + pltpu.roll(t, HEAD_DIM // 2, axis=1) * sin

    q = jnp.dot(xb, wq_ref[...], preferred_element_type=F32)
    for h in range(ATTN_HEADS):
        sl = slice(h * HEAD_DIM, (h + 1) * HEAD_DIM)
        q_ref[:, sl] = rope(q[:, sl]).astype(BF16)
    k = jnp.dot(xb, wk_ref[...], preferred_element_type=F32)
    for h in range(ATTN_KV_HEADS):
        sl = slice(h * HEAD_DIM, (h + 1) * HEAD_DIM)
        k_ref[:, sl] = rope(k[:, sl]).astype(BF16)
    v_ref[...] = jnp.dot(xb, wv_ref[...], preferred_element_type=F32).astype(BF16)
    u_ref[...] = jnp.dot(xb, wu_ref[...], preferred_element_type=F32)


def _inproj_ab(front, x, wq, wk, wv, wu, cos2, sin2):
    d = x.shape[1]
    lp = front.shape[0] + x.shape[0]
    tm = ROW_TILE
    assert front.shape[0] == tm
    s5w = wu.shape[1]
    row = lambda w: pl.BlockSpec((tm, w), lambda i: (i, 0))
    return pl.pallas_call(
        _inproj_ab_kernel,
        grid=(lp // tm,),
        in_specs=[_full(front.shape), pl.BlockSpec((tm, d), lambda i: (jnp.maximum(i - 1, 0), 0)),
                  _full(wq.shape), _full(wk.shape), _full(wv.shape), _full(wu.shape),
                  row(HEAD_DIM), row(HEAD_DIM)],
        out_specs=[row(Q_DIM), row(KV_DIM), row(KV_DIM), row(s5w)],
        out_shape=[jax.ShapeDtypeStruct((lp, Q_DIM), BF16),
                   jax.ShapeDtypeStruct((lp, KV_DIM), BF16),
                   jax.ShapeDtypeStruct((lp, KV_DIM), BF16),
                   jax.ShapeDtypeStruct((lp, s5w), F32)],
        compiler_params=_cparams("parallel"),
        name="inproj_ab",
    )(front, x, wq, wk, wv, wu, cos2, sin2)


def _attn_kernel(sink_ref, q_ref, k0_ref, k1_ref, k2_ref, v0_ref, v1_ref, v2_ref, km_ref, vm_ref,
                 o_ref, *, lp):
    qb = pl.program_id(0)
    blk = ATTN_BLOCK
    rows = ATTN_GROUP * blk
    scale = HEAD_DIM ** -0.5
    q_row = qb * blk + lax.broadcasted_iota(jnp.int32, (rows, 3 * blk), 0) % blk
    k_row = (qb - 1) * blk + lax.broadcasted_iota(jnp.int32, (rows, 3 * blk), 1)
    vis = (k_row >= FRONT) & (k_row < lp) & (jnp.abs(q_row - k_row) <= WINDOW)
    head_of_row = lax.broadcasted_iota(jnp.int32, (rows, 1), 0) // blk
    for g in range(ATTN_KV_HEADS):
        gs = slice(g * HEAD_DIM, (g + 1) * HEAD_DIM)
        qs = jnp.concatenate(
            [q_ref[:, (g * ATTN_GROUP + r) * HEAD_DIM:(g * ATTN_GROUP + r + 1) * HEAD_DIM]
             for r in range(ATTN_GROUP)], axis=0)
        kband = jnp.concatenate([k0_ref[:, gs], k1_ref[:, gs], k2_ref[:, gs]], axis=0)
        vband = jnp.concatenate([v0_ref[:, gs], v1_ref[:, gs], v2_ref[:, gs]], axis=0)
        nt = (((1,), (1,)), ((), ()))
        s_band = lax.dot_general(qs, kband, nt, preferred_element_type=F32) * scale
        s_band = jnp.where(vis, s_band, NEG_INF)
        s_meta = lax.dot_general(qs, km_ref[:, gs], nt, preferred_element_type=F32) * scale
        sink = jnp.zeros((rows, 1), F32)
        for r in range(ATTN_GROUP):
            sink = jnp.where(head_of_row == r, sink_ref[g * ATTN_GROUP + r], sink)
        m = jnp.maximum(jnp.maximum(jnp.max(s_band, axis=-1, keepdims=True),
                                    jnp.max(s_meta, axis=-1, keepdims=True)), sink)
        e_band = jnp.exp(s_band - m)
        e_meta = jnp.exp(s_meta - m)
        denom = (jnp.sum(e_band, axis=-1, keepdims=True) + jnp.sum(e_meta, axis=-1, keepdims=True)
                 + jnp.exp(sink - m))
        o = (jnp.dot(e_band.astype(BF16), vband, preferred_element_type=F32)
             + jnp.dot(e_meta.astype(BF16), vm_ref[:, gs], preferred_element_type=F32)) / denom
        for r in range(ATTN_GROUP):
            hh = g * ATTN_GROUP + r
            o_ref[:, hh * HEAD_DIM:(hh + 1) * HEAD_DIM] = o[r * blk:(r + 1) * blk].astype(BF16)


def _window_attention(q, k, v, sinks):
    lp = q.shape[0]
    blk = ATTN_BLOCK
    nb = lp // blk
    qspec = pl.BlockSpec((blk, Q_DIM), lambda i, s: (i, 0))
    kv = lambda off: pl.BlockSpec((blk, KV_DIM), lambda i, s: (jnp.clip(i + off, 0, nb - 1), 0))
    meta = pl.BlockSpec((N_META, KV_DIM), lambda i, s: (META_ROW0 // N_META, 0))
    return pl.pallas_call(
        functools.partial(_attn_kernel, lp=lp),
        grid_spec=pltpu.PrefetchScalarGridSpec(
            num_scalar_prefetch=1,
            grid=(nb,),
            in_specs=[qspec, kv(-1), kv(0), kv(1), kv(-1), kv(0), kv(1), meta, meta],
            out_specs=qspec,
        ),
        out_shape=jax.ShapeDtypeStruct((lp, Q_DIM), BF16),
        compiler_params=_cparams("parallel"),
        name="window_attention",
    )(sinks, q, k, k, k, v, v, v, k, v)


def _s5_discretise(lam_re, lam_im, log_step, b_re, b_im, c_re, c_im):
    t = S5_CHUNK
    hi = lax.Precision.HIGHEST
    g = lam_re.shape[1]
    lr = jnp.minimum(lam_re, -1e-4)
    li = lam_im
    step = jnp.exp(log_step)[..., None]
    dr, di = lr * step, li * step
    lags = jnp.arange(t + 1, dtype=F32)[:, None]
    mag = jnp.exp(dr[..., None, :] * lags)
    pr, pi = mag * jnp.cos(di[..., None, :] * lags), mag * jnp.sin(di[..., None, :] * lags)
    ar, ai = pr[:, :, 1], pi[:, :, 1]
    den = lr * lr + li * li
    zr = ((ar - 1.0) * lr + ai * li) / den
    zi = (ai * lr - (ar - 1.0) * li) / den
    br = zr[..., None] * b_re - zi[..., None] * b_im
    bi = zr[..., None] * b_im + zi[..., None] * b_re
    brt, bit = br.transpose(0, 1, 3, 2), bi.transpose(0, 1, 3, 2)
    xr = c_re[:, :, None] * pr[:, :, :t, None] - c_im[:, :, None] * pi[:, :, :t, None]
    xi = c_re[:, :, None] * pi[:, :, :t, None] + c_im[:, :, None] * pr[:, :, :t, None]
    kern = (jnp.einsum('dglop,dgpi->dgilo', xr, br, precision=hi)
            - jnp.einsum('dglop,dgpi->dgilo', xi, bi, precision=hi))
    k_lag = jnp.concatenate([jnp.flip(kern[1][:, :, 1:], axis=2), kern[0][:, :, :1] + kern[1][:, :, :1],
                             kern[0][:, :, 1:]], axis=2)
    k_lag = k_lag.reshape(g, S5_GROUP, (2 * t - 1) * S5_GROUP)

    def flat_e(p_r, p_i, d):
        e_r = p_r[:, :, None] * brt[d][:, None] - p_i[:, :, None] * bit[d][:, None]
        e_i = p_r[:, :, None] * bit[d][:, None] + p_i[:, :, None] * brt[d][:, None]
        flat = lambda e: e.reshape(g, t * S5_GROUP, S5_STATE)
        return jnp.concatenate([flat(e_r), flat(e_i)], axis=-1)
    e_mat = jnp.concatenate([flat_e(pr[0][:, ::-1][:, 1:], pi[0][:, ::-1][:, 1:], 0),
                             flat_e(pr[1][:, :t], pi[1][:, :t], 1)], axis=-1)

    def flat_f(p_r, p_i, d):
        ct_r, ct_i = c_re[d].transpose(0, 2, 1), c_im[d].transpose(0, 2, 1)
        pt_r, pt_i = p_r.transpose(0, 2, 1)[..., None], p_i.transpose(0, 2, 1)[..., None]
        w_r = ct_r[:, :, None] * pt_r - ct_i[:, :, None] * pt_i
        w_i = ct_r[:, :, None] * pt_i + ct_i[:, :, None] * pt_r
        flat = lambda w: w.reshape(g, S5_STATE, t * S5_GROUP)
        return jnp.concatenate([flat(w_r), -flat(w_i)], axis=1)
    f_fw = flat_f(pr[0][:, 1:], pi[0][:, 1:], 0)
    f_bw = flat_f(pr[1][:, ::-1][:, :t], pi[1][:, ::-1][:, :t], 1)
    gh = g // 2
    upper = (jnp.arange(g) >= gh)[:, None, None]
    def place(m):
        z = jnp.zeros_like(m)
        return jnp.concatenate([jnp.where(upper, z, m), jnp.where(upper, m, z)], axis=1)
    p = S5_STATE
    f_mat = jnp.concatenate([place(f_fw[:, :p]), place(f_fw[:, p:]), place(f_bw[:, :p]), place(f_bw[:, p:])],
                            axis=1)
    pack = lambda v: jnp.concatenate([v[:gh], v[gh:]], axis=-1)
    coef = jnp.stack([pack(pr[0][:, t]), pack(pi[0][:, t]), pack(pr[1][:, t]), pack(pi[1][:, t])])
    split = lambda m: m.reshape((2, gh) + m.shape[1:])
    return split(k_lag), split(e_mat.astype(BF16)), split(f_mat.astype(BF16)), coef


def _chunk_rows_to_lanes(u_ref, ncb):
    lane_seg = lax.broadcasted_iota(jnp.int32, (ncb, 128), 1) // S5_GROUP
    rows = [u_ref[pl.ds(s, ncb, stride=S5_CHUNK), :] for s in range(S5_CHUNK)]
    segs = 128 // S5_GROUP
    out = []
    for j in range(segs):
        halves = []
        for h in range(S5_CHUNK // segs):
            acc = None
            for s8 in range(segs):
                r = rows[h * segs + s8]
                k = (s8 - j) % segs
                if k:
                    r = pltpu.roll(r, S5_GROUP * k, axis=1)
                acc = r if acc is None else jnp.where(lane_seg == s8, r, acc)
            halves.append(acc)
        out.append(jnp.concatenate(halves, axis=1))
    return out


def _lanes_to_chunk_rows(ys, y_ref, ncb, add_ref, scale_ref):
    lane_seg = lax.broadcasted_iota(jnp.int32, (ncb, 128), 1) // S5_GROUP
    segs = 128 // S5_GROUP
    for t in range(S5_CHUNK):
        h, t8 = divmod(t, segs)
        acc = None
        for j in range(segs):
            r = ys[j][:, h * 128:(h + 1) * 128]
            k = (j - t8) % segs
            if k:
                r = pltpu.roll(r, S5_GROUP * k, axis=1)
            acc = r if acc is None else jnp.where(lane_seg == j, r, acc)
        idx = pl.ds(t, ncb, stride=S5_CHUNK)
        y_ref[idx, :] = acc + add_ref[idx, :] * scale_ref[...]


def _pair_halves(a, b):
    lane = lax.broadcasted_iota(jnp.int32, a.shape, 1)
    lo = jnp.where(lane < S5_STATE, a, pltpu.roll(b, S5_STATE, axis=1))
    hi = jnp.where(lane < S5_STATE, pltpu.roll(a, S5_STATE, axis=1), b)
    return lo, hi


def _s5_local_kernel(ua_ref, ub_ref, e_ref, u2_ref, f1_ref, f2_ref, b1_ref, b2_ref):
    ncb = u2_ref.shape[2]
    p2 = 2 * S5_STATE
    chunks = [_chunk_rows_to_lanes(ua_ref, ncb), _chunk_rows_to_lanes(ub_ref, ncb)]
    for j in range(u2_ref.shape[1]):
        st = []
        for hh in range(2):
            u2 = chunks[hh][j].astype(BF16)
            u2_ref[hh, j] = u2
            st.append(jnp.dot(u2, e_ref[hh, j], preferred_element_type=F32))
        f1_ref[:, j, :], f2_ref[:, j, :] = _pair_halves(st[0][:, :p2], st[1][:, :p2])
        b1_ref[:, j, :], b2_ref[:, j, :] = _pair_halves(st[0][:, p2:], st[1][:, p2:])


def _s5_scan_kernel(sf1_ref, sf2_ref, sb1_ref, sb2_ref, coef_ref, xf1_ref, xf2_ref, xb1_ref, xb2_ref,
                    st_ref):
    @pl.when(pl.program_id(0) == 0)
    def _():
        st_ref[...] = jnp.zeros_like(st_ref)

    nb = sf1_ref.shape[0]
    crf, cif, crb, cib = coef_ref[0], coef_ref[1], coef_ref[2], coef_ref[3]

    def body(i, carry):
        f1, f2, b1, b2 = carry
        j = nb - 1 - i
        xf1_ref[i] = f1
        xf2_ref[i] = f2
        xb1_ref[j] = b1
        xb2_ref[j] = b2
        return (crf * f1 - cif * f2 + sf1_ref[i], crf * f2 + cif * f1 + sf2_ref[i],
                crb * b1 - cib * b2 + sb1_ref[j], crb * b2 + cib * b1 + sb2_ref[j])

    out = lax.fori_loop(0, nb, body, (st_ref[0], st_ref[1], st_ref[2], st_ref[3]))
    for q in range(4):
        st_ref[q] = out[q]


def _chunk_matrix(k_lag):
    t, c = S5_CHUNK, S5_GROUP
    return jnp.concatenate([k_lag[:, (t - 1 - s) * c:(2 * t - 1 - s) * c] for s in range(t)],
                           axis=0).astype(BF16)


def _s5_readout_kernel(ua_ref, ub_ref, d_ref, u2_ref, kl_ref, f_ref, xf1_ref, xf2_ref, xb1_ref, xb2_ref,
                       ya_ref, yb_ref):
    ncb = u2_ref.shape[2]
    ys = [[], []]
    for j in range(u2_ref.shape[1]):
        x = jnp.concatenate([xf1_ref[:, j, :], xf2_ref[:, j, :], xb1_ref[:, j, :], xb2_ref[:, j, :]],
                            axis=1).astype(BF16)
        for hh in range(2):
            y = jnp.dot(u2_ref[hh, j], _chunk_matrix(kl_ref[hh, j]), preferred_element_type=F32)
            ys[hh].append(y + jnp.dot(x, f_ref[hh, j], preferred_element_type=F32))
    _lanes_to_chunk_rows(ys[0], ya_ref, ncb, ua_ref, d_ref.at[0])
    _lanes_to_chunk_rows(ys[1], yb_ref, ncb, ub_ref, d_ref.at[1])


def _s5_mixer_pre_glu(u, d_skip, mats):
    k_lag, e_mat, f_mat, coef = mats
    lp, w = u.shape
    g = w // S5_GROUP
    gh = g // 2
    gpb = 128 // S5_GROUP
    nc = lp // S5_CHUNK
    ncb = _pick(nc, S5_CHUNK_BLOCKS)
    rows = ncb * S5_CHUNK
    nlb = w // 128
    grid = (nlb // 2, nc // ncb)
    u_lo = pl.BlockSpec((rows, 128), lambda b, r: (r, b))
    u_hi = pl.BlockSpec((rows, 128), lambda b, r: (r, b + nlb // 2))
    per_group = lambda k, n: pl.BlockSpec((2, gpb, k, n), lambda b, r: (0, b, 0, 0))
    u2_spec = pl.BlockSpec((2, gpb, ncb, S5_ROW), lambda b, r: (0, b, r, 0))
    st_spec = pl.BlockSpec((ncb, gpb, 2 * S5_STATE), lambda b, r: (r, b, 0))
    st_shape = jax.ShapeDtypeStruct((nc, gh, 2 * S5_STATE), F32)
    u2, sf1, sf2, sb1, sb2 = pl.pallas_call(
        _s5_local_kernel,
        grid=grid,
        in_specs=[u_lo, u_hi, per_group(S5_ROW, 4 * S5_STATE)],
        out_specs=[u2_spec, st_spec, st_spec, st_spec, st_spec],
        out_shape=[jax.ShapeDtypeStruct((2, gh, nc, S5_ROW), BF16)] + [st_shape] * 4,
        compiler_params=_cparams("parallel", "parallel"),
        name="s5_local_states",
    )(u, u, e_mat)
    sb = S5_SCAN_BLOCK
    nblk = nc // sb
    fwd = pl.BlockSpec((sb, gh, 2 * S5_STATE), lambda i: (i, 0, 0))
    bwd = pl.BlockSpec((sb, gh, 2 * S5_STATE), lambda i: (nblk - 1 - i, 0, 0))
    xf1, xf2, xb1, xb2 = pl.pallas_call(
        _s5_scan_kernel,
        grid=(nblk,),
        in_specs=[fwd, fwd, bwd, bwd, _full(coef.shape)],
        out_specs=[fwd, fwd, bwd, bwd],
        out_shape=[st_shape] * 4,
        scratch_shapes=[pltpu.VMEM((4, gh, 2 * S5_STATE), F32)],
        compiler_params=_cparams("arbitrary"),
        name="s5_chunk_scan",
    )(sf1, sf2, sb1, sb2, coef)
    d2 = d_skip.reshape(2, 1, nlb // 2 * 128)
    y_lo, y_hi = pl.pallas_call(
        _s5_readout_kernel,
        grid=grid,
        in_specs=[u_lo, u_hi, pl.BlockSpec((2, 1, 128), lambda b, r: (0, 0, b)), u2_spec,
                  per_group(S5_GROUP, k_lag.shape[-1]), per_group(8 * S5_STATE, S5_ROW),
                  st_spec, st_spec, st_spec, st_spec],
        out_specs=[pl.BlockSpec((rows, 128), lambda b, r: (r, b))] * 2,
        out_shape=[jax.ShapeDtypeStruct((lp, w // 2), F32)] * 2,
        compiler_params=_cparams("parallel", "parallel"),
        name="s5_readout",
    )(u, u, d2, u2, k_lag, f_mat, xf1, xf2, xb1, xb2)
    return y_lo, y_hi


def _route(hn, wr_ref, br_ref):
    hn_hi = hn.astype(BF16)
    hn_lo = (hn - hn_hi.astype(F32)).astype(BF16)
    p_hi = jnp.dot(hn_hi, wr_ref[...], preferred_element_type=F32)
    p_lo = jnp.dot(hn_lo, wr_ref[:, :ROUTE_LANES], preferred_element_type=F32)
    logits = p_hi[:, :ROUTE_LANES] + (p_hi[:, ROUTE_LANES:] + p_lo) + br_ref[...]
    lane_i = lax.broadcasted_iota(jnp.int32, logits.shape, 1)
    lane = lane_i.astype(F32)
    lane_grp = ((lane_i - N_GROUPS) // EXPERTS_PER_GROUP).astype(F32)
    big = float(ROUTE_LANES)
    g_log = jnp.where(lane_i < N_GROUPS, logits, -jnp.inf)
    g_max = jnp.max(g_log, axis=-1, keepdims=True)
    grp = jnp.min(jnp.where(g_log == g_max, lane, big), axis=-1, keepdims=True)
    p_grp = 1.0 / jnp.sum(jnp.exp(g_log - g_max), axis=-1, keepdims=True)
    in_grp = (lane_i >= N_GROUPS) & (lane_i < N_GROUPS + N_EXPERTS) & (lane_grp == grp)
    e_log = jnp.where(in_grp, logits, -jnp.inf)
    v1 = jnp.max(e_log, axis=-1, keepdims=True)
    i1 = jnp.min(jnp.where(e_log == v1, lane, big), axis=-1, keepdims=True)
    e_log2 = jnp.where(lane == i1, -jnp.inf, e_log)
    v2 = jnp.max(e_log2, axis=-1, keepdims=True)
    i2 = jnp.min(jnp.where(e_log2 == v2, lane, big), axis=-1, keepdims=True)
    e21 = jnp.exp(v2 - v1)
    w1 = p_grp / (1.0 + e21)
    w2 = p_grp * e21 / (1.0 + e21)
    route = jnp.where(lane_i == 0, w1, 0.0)
    route = jnp.where(lane_i == 1, w2, route)
    route = jnp.where(lane_i == 2, i1 - N_GROUPS, route)
    route = jnp.where(lane_i == 3, i2 - N_GROUPS, route)
    return route


def _slab_spec(tm, d):
    return pl.BlockSpec((tm * (d // 128), 128), lambda i: (i, 0))


def _slab_shape(rows, d):
    return jax.ShapeDtypeStruct((rows * (d // 128), 128), F32)


def _store_row_slabs(ref, rows, val):
    nl = val.shape[1] // 128
    for c in range(nl):
        ref[pl.ds(rows.start * nl + c, rows.stop - rows.start, stride=nl), :] = val[:, c * 128:(c + 1) * 128]


def _sub_rows(tm):
    return [slice(r, r + MIX_SUB) for r in range(0, tm, MIX_SUB)]


def _mix_ab_kernel(front_ref, x_ref, a_ref, ylo_ref, yhi_ref, wglu_ref, bglu_ref, woa_ref, wos_ref,
                   g_ref, b_ref, wr_ref, br_ref, hn_ref, hs_ref, route_ref, *, alpha):
    in_front = pl.program_id(0) < FRONT // MIX_TILE
    for rows in _sub_rows(x_ref.shape[0]):
        y = jnp.concatenate([ylo_ref[rows, :], yhi_ref[rows, :]], axis=1)
        z = 0.5 * y * (1.0 + jnp.tanh(math.sqrt(2.0 / math.pi) * (y + 0.044715 * (y * y * y))))
        gate = jnp.dot(z.astype(BF16), wglu_ref[...], preferred_element_type=F32) + bglu_ref[...]
        s_out = z * jax.nn.sigmoid(gate)
        mix = (jnp.dot(a_ref[rows, :], woa_ref[...], preferred_element_type=F32)
               + jnp.dot(s_out.astype(BF16), wos_ref[...], preferred_element_type=F32))
        h_in = jnp.where(in_front, front_ref[rows, :], x_ref[rows, :])
        hn = _layer_norm(alpha * h_in + mix, g_ref[...], b_ref[...])
        hn_ref[rows, :] = hn
        _store_row_slabs(hs_ref, rows, hn)
        route_ref[rows, :] = _route(hn, wr_ref, br_ref)


def _mix_ab(front, x, a_out, y_lo, y_hi, w_glu, b_glu, wo_a, wo_s, ln_g, ln_b, w_route, b_route, alpha):
    d = x.shape[1]
    lp = front.shape[0] + x.shape[0]
    tm = MIX_TILE
    nf = front.shape[0] // tm
    row = lambda c: pl.BlockSpec((tm, c), lambda i: (i, 0))
    return pl.pallas_call(
        functools.partial(_mix_ab_kernel, alpha=alpha),
        grid=(lp // tm,),
        in_specs=[pl.BlockSpec((tm, d), lambda i: (jnp.minimum(i, nf - 1), 0)),
                  pl.BlockSpec((tm, d), lambda i: (jnp.maximum(i - nf, 0), 0)), row(Q_DIM), row(y_lo.shape[1]), row(y_hi.shape[1]), _full(w_glu.shape),
                  _full(b_glu.shape), _full(wo_a.shape), _full(wo_s.shape), _full(ln_g.shape),
                  _full(ln_b.shape), _full(w_route.shape), _full(b_route.shape)],
        out_specs=[row(d), _slab_spec(tm, d), row(ROUTE_LANES)],
        out_shape=[jax.ShapeDtypeStruct((lp, d), F32), _slab_shape(lp, d),
                   jax.ShapeDtypeStruct((lp, ROUTE_LANES), F32)],
        compiler_params=_cparams("parallel"),
        name="mix_ab",
    )(front, x, a_out, y_lo, y_hi, w_glu, b_glu, wo_a, wo_s, ln_g, ln_b, w_route, b_route)


def _mix_c_kernel(h_ref, ofw_ref, obw_ref, gate_ref, ng_ref, wo_ref, g_ref, b_ref, wr_ref, br_ref,
                  hn_ref, hs_ref, route_ref, *, alpha):
    for rows in _sub_rows(h_ref.shape[0]):
        parts = []
        for hh in range(HG_HEADS):
            o = ofw_ref[hh, rows, :] + obw_ref[hh, rows, :]
            o = o * lax.rsqrt(jnp.mean(o * o, axis=-1, keepdims=True) + RMS_EPS)
            o = o * ng_ref[hh] * jax.nn.sigmoid(gate_ref[0, hh, rows, :].astype(F32))
            parts.append(o.astype(BF16))
        mix = jnp.dot(jnp.concatenate(parts, axis=1), wo_ref[...], preferred_element_type=F32)
        hn = _layer_norm(alpha * h_ref[rows, :] + mix, g_ref[...], b_ref[...])
        hn_ref[rows, :] = hn
        _store_row_slabs(hs_ref, rows, hn)
        route_ref[rows, :] = _route(hn, wr_ref, br_ref)


def _mix_c(h, o_fw, o_bw, plain, norm_g, wo, ln_g, ln_b, w_route, b_route, alpha):
    lp, d = h.shape
    tm = MIX_TILE
    row = lambda c: pl.BlockSpec((tm, c), lambda i: (i, 0))
    hm = pl.BlockSpec((HG_HEADS, tm, HG_DK), lambda i: (0, i, 0))
    gate = pl.BlockSpec((1, HG_HEADS, tm, HG_DK), lambda i: (2, 0, i, 0))
    return pl.pallas_call(
        functools.partial(_mix_c_kernel, alpha=alpha),
        grid=(lp // tm,),
        in_specs=[row(d), hm, hm, gate, _full(norm_g.shape), _full(wo.shape), _full(ln_g.shape),
                  _full(ln_b.shape), _full(w_route.shape), _full(b_route.shape)],
        out_specs=[row(d), _slab_spec(tm, d), row(ROUTE_LANES)],
        out_shape=[jax.ShapeDtypeStruct((lp, d), F32), _slab_shape(lp, d),
                   jax.ShapeDtypeStruct((lp, ROUTE_LANES), F32)],
        compiler_params=_cparams("parallel"),
        name="mix_c",
    )(h, o_fw, o_bw, plain, norm_g, wo, ln_g, ln_b, w_route, b_route)


def _moe_plan(route, n_rows_valid_from):
    lp = route.shape[0]
    tm = MOE_TM
    valid = (jnp.arange(lp) >= n_rows_valid_from)
    expert = route[:, 2:4].astype(jnp.int32)
    flat_e = jnp.where(valid[:, None], expert, N_EXPERTS).reshape(-1)
    onehot = (flat_e[:, None] == jnp.arange(N_EXPERTS)[None, :]).astype(jnp.int32)
    csum = jnp.cumsum(onehot, axis=0)
    counts = csum[-1]
    rank = jnp.sum(jnp.where(onehot > 0, csum - 1, 0), axis=1)
    padded = (counts + tm - 1) // tm * tm
    pend = jnp.cumsum(padded)
    pstart = pend - padded
    e_safe = jnp.minimum(flat_e, N_EXPERTS - 1)
    dest = jnp.where(flat_e < N_EXPERTS, pstart[e_safe] + rank, 0)
    n_tokens = lp - n_rows_valid_from
    n_blocks = -(-(n_tokens * TOP_K + N_EXPERTS * (tm - 1)) // tm)
    rows = n_blocks * tm
    tok = jnp.repeat(jnp.arange(lp, dtype=jnp.int32), TOP_K)
    scatter_to = jnp.where(flat_e < N_EXPERTS, dest, rows)
    tok_of_row = jnp.zeros((rows,), jnp.int32).at[scatter_to].set(tok, mode='drop')
    n_used = (pend[-1] // tm).astype(jnp.int32)
    blk = jnp.minimum(jnp.arange(n_blocks, dtype=jnp.int32), jnp.maximum(n_used - 1, 0))
    block_expert = jnp.minimum(jnp.searchsorted(pend, blk * tm, side='right'),
                               N_EXPERTS - 1).astype(jnp.int32)
    used = counts > 0
    pos = jnp.cumsum(used.astype(jnp.int32)) - 1
    eid = jnp.arange(N_EXPERTS, dtype=jnp.int32)
    later = jnp.where(used[None, :] & (eid[None, :] > eid[:, None]), eid[None, :], N_EXPERTS)
    nxt_e = jnp.min(later, axis=1)
    nxt_e = jnp.where(nxt_e == N_EXPERTS, -1, nxt_e).astype(jnp.int32)
    w_slot = (pos[block_expert] % 2).astype(jnp.int32)
    nxt = nxt_e[block_expert]
    weights = jnp.where(valid[:, None], route[:, 0:2], 0.0)
    return (block_expert, n_used.reshape(1), w_slot, nxt, tok_of_row.reshape(n_blocks, 1, tm),
            dest.reshape(lp, TOP_K).astype(jnp.int32), weights)


def _moe_expert_kernel(be_ref, nu_ref, ws_ref, nx_ref, tok0_ref, tokn_ref, h_hbm, wgu_hbm, wd_hbm,
                       y_ref, xbuf, xsem, wgu_f, wd_f, wsem, wgu_bf, wd_bf, *, layer):
    i = pl.program_id(0)
    nl = wgu_f.shape[1] // 128
    tm = xbuf.shape[1] // nl
    n_used = nu_ref[0]

    def gather(tok_ref, slot):
        def issue(r, carry):
            src = h_hbm.at[pl.ds(pl.multiple_of(tok_ref[0, 0, r], nl), nl)]
            pltpu.make_async_copy(src, xbuf.at[slot, pl.ds(pl.multiple_of(r * nl, nl), nl)],
                                  xsem.at[slot]).start()
            return carry
        lax.fori_loop(0, tm, issue, 0, unroll=8)

    def gather_wait(slot):
        pltpu.make_async_copy(h_hbm.at[pl.ds(0, tm * nl)], xbuf.at[slot], xsem.at[slot]).wait()

    def weight_copies(e, s):
        return (pltpu.make_async_copy(wgu_hbm.at[layer, e], wgu_f.at[s], wsem.at[0, s]),
                pltpu.make_async_copy(wd_hbm.at[layer, e], wd_f.at[s], wsem.at[1, s]))

    @pl.when(i == 0)
    def _():
        for cp in weight_copies(be_ref[0], 0):
            cp.start()
        gather(tok0_ref, 0)

    @pl.when(i < n_used)
    def _():
        slot = i % 2
        e = be_ref[i]
        first = jnp.logical_or(i == 0, be_ref[jnp.maximum(i - 1, 0)] != e)

        gather_wait(slot)
        gather(tokn_ref, 1 - slot)

        @pl.when(first)
        def _():
            s = ws_ref[i]
            for cp in weight_copies(e, s):
                cp.wait()
            nxt = nx_ref[i]

            @pl.when(nxt >= 0)
            def _():
                for cp in weight_copies(nxt, 1 - s):
                    cp.start(priority=1)

            wgu_bf[...] = wgu_f[s].astype(BF16)
            wd_bf[...] = wd_f[s].astype(BF16)

        x = jnp.concatenate([xbuf[slot, pl.ds(c, tm, stride=nl), :] for c in range(nl)], axis=1)
        hgu = jnp.dot(x.astype(BF16), wgu_bf[...], preferred_element_type=F32)
        act = jax.nn.silu(hgu[:, :D_EXPERT]) * hgu[:, D_EXPERT:]
        y = jnp.dot(act.astype(BF16), wd_bf[...], preferred_element_type=F32)
        for c in range(y.shape[1] // 128):
            y_ref[pl.ds(c, tm, stride=y.shape[1] // 128), :] = y[:, c * 128:(c + 1) * 128]

        @pl.when(i == n_used - 1)
        def _():
            gather_wait(1 - slot)

    @pl.when(i >= n_used)
    def _():
        y_ref[...] = jnp.zeros_like(y_ref)


def _moe_experts(h_slabs, plan, w_gate_up, w_down, layer):
    block_expert, n_used, w_slot, nxt, tok_of_row, _, _ = plan
    d = w_gate_up.shape[2]
    tok_of_row = tok_of_row * (d // 128)
    tm = MOE_TM
    n_blocks = tok_of_row.shape[0]
    nxt_blk = lambda i, be, nu, ws, nx: (jnp.minimum(i + 1, jnp.maximum(nu[0] - 1, 0)), 0, 0)
    smem = lambda imap: pl.BlockSpec((1, 1, tm), imap, memory_space=pltpu.SMEM)
    anyspace = pl.BlockSpec(memory_space=pl.ANY)
    return pl.pallas_call(
        functools.partial(_moe_expert_kernel, layer=layer),
        grid_spec=pltpu.PrefetchScalarGridSpec(
            num_scalar_prefetch=4,
            grid=(n_blocks,),
            in_specs=[smem(lambda i, be, nu, ws, nx: (0, 0, 0)), smem(nxt_blk),
                      anyspace, anyspace, anyspace],
            out_specs=pl.BlockSpec((tm * (d // 128), 128), lambda i, be, nu, ws, nx: (i, 0)),
            scratch_shapes=[pltpu.VMEM((2, tm * (d // 128), 128), F32), pltpu.SemaphoreType.DMA((2,)),
                            pltpu.VMEM((2, d, 2 * D_EXPERT), F32), pltpu.VMEM((2, D_EXPERT, d), F32),
                            pltpu.SemaphoreType.DMA((2, 2)),
                            pltpu.VMEM((d, 2 * D_EXPERT), BF16), pltpu.VMEM((D_EXPERT, d), BF16)],
        ),
        out_shape=jax.ShapeDtypeStruct((n_blocks * tm * (d // 128), 128), F32),
        compiler_params=_cparams("arbitrary"),
        name="moe_experts",
    )(block_expert, n_used, w_slot, nxt, tok_of_row, tok_of_row, h_slabs, w_gate_up, w_down)


def _moe_combine_kernel(d0_ref, d1_ref, d0n_ref, d1n_ref, h_ref, w_ref, y_hbm, g_ref, b_ref, *rest,
                        alpha, with_bf16):
    if with_bf16:
        o_ref, ob_ref, buf, sem = rest
    else:
        o_ref, buf, sem = rest
    i = pl.program_id(0)
    tm, d = h_ref.shape
    nl = d // 128
    slot = i % 2

    def gather(a_ref, b_ref_, s):
        def issue(r, carry):
            for k, idx_ref in enumerate((a_ref, b_ref_)):
                src = y_hbm.at[pl.ds(pl.multiple_of(idx_ref[0, 0, r], nl), nl)]
                pltpu.make_async_copy(src, buf.at[s, k, pl.ds(pl.multiple_of(r * nl, nl), nl)],
                                      sem.at[s, k]).start()
            return carry
        lax.fori_loop(0, tm, issue, 0, unroll=8)

    @pl.when(i == 0)
    def _():
        gather(d0_ref, d1_ref, 0)

    for k in range(TOP_K):
        pltpu.make_async_copy(y_hbm.at[pl.ds(0, tm * nl)], buf.at[slot, k], sem.at[slot, k]).wait()

    @pl.when(i + 1 < pl.num_programs(0))
    def _():
        gather(d0n_ref, d1n_ref, 1 - slot)

    w = w_ref[...]
    cols = []
    for c in range(nl):
        rows = pl.ds(c, tm, stride=nl)
        ffn = w[:, 0:1] * buf[slot, 0, rows, :] + w[:, 1:2] * buf[slot, 1, rows, :]
        cols.append(alpha * h_ref[:, c * 128:(c + 1) * 128] + ffn)
    out = _layer_norm(jnp.concatenate(cols, axis=1), g_ref[...], b_ref[...])
    o_ref[...] = out
    if with_bf16:
        ob_ref[...] = out.astype(BF16)


def _moe_combine(h, plan, y_buf, ln_g, ln_b, alpha, first_row, with_bf16=False):
    dest, weights = plan[-2], plan[-1]
    lp, d = h.shape
    tm = MIX_TILE
    off = first_row // tm
    nt = lp // tm
    d0 = (dest[:, 0] * (d // 128)).reshape(nt, 1, tm)
    d1 = (dest[:, 1] * (d // 128)).reshape(nt, 1, tm)
    cur = pl.BlockSpec((1, 1, tm), lambda i: (off, 0, 0), memory_space=pltpu.SMEM)
    nxt = pl.BlockSpec((1, 1, tm), lambda i: (jnp.minimum(i + 1 + off, nt - 1), 0, 0),
                       memory_space=pltpu.SMEM)
    out_spec = pl.BlockSpec((tm, d), lambda i: (i, 0))
    out_shape = jax.ShapeDtypeStruct((lp - first_row, d), F32)
    if with_bf16:
        out_spec = [out_spec, out_spec]
        out_shape = [out_shape, jax.ShapeDtypeStruct((lp - first_row, d), BF16)]
    return pl.pallas_call(
        functools.partial(_moe_combine_kernel, alpha=alpha, with_bf16=with_bf16),
        grid=(nt - off,),
        in_specs=[cur, cur, nxt, nxt, pl.BlockSpec((tm, d), lambda i: (i + off, 0)),
                  pl.BlockSpec((tm, TOP_K), lambda i: (i + off, 0)),
                  pl.BlockSpec(memory_space=pl.ANY), _full(ln_g.shape), _full(ln_b.shape)],
        out_specs=out_spec,
        out_shape=out_shape,
        scratch_shapes=[pltpu.VMEM((2, TOP_K, tm * (d // 128), 128), F32),
                        pltpu.SemaphoreType.DMA((2, TOP_K))],
        compiler_params=_cparams("arbitrary"),
        name="moe_combine",
    )(d0, d1, d0, d1, h, weights, y_buf, ln_g, ln_b)


def _store_heads(ref, seg_rows, val):
    for hh in range(val.shape[1] // HG_DK):
        ref[0, hh, seg_rows, :] = val[:, hh * HG_DK:(hh + 1) * HG_DK].astype(ref.dtype)


def _inproj_plain_kernel(x_ref, w_ref, o_ref):
    for r in range(0, x_ref.shape[0], INPROJ_C_SUB):
        rows = slice(r, r + INPROJ_C_SUB)
        _store_heads(o_ref, rows, jnp.dot(x_ref[rows, :], w_ref[...], preferred_element_type=F32))


def _inproj_forget_kernel(x_ref, w_ref, lb_ref, k_ref, g_ref):
    i = pl.program_id(0)
    tm = x_ref.shape[0]
    lb = lb_ref[...]
    for r in range(0, tm, INPROJ_C_SUB):
        rows = slice(r, r + INPROJ_C_SUB)
        acc = jnp.dot(x_ref[rows, :], w_ref[...], preferred_element_type=F32)
        f = lb + (1.0 - lb) * jax.nn.sigmoid(acc)
        real = (i * tm + r + lax.broadcasted_iota(jnp.int32, (INPROJ_C_SUB, 1), 0)) >= META_ROW0
        _store_heads(k_ref, rows, jnp.where(real, 1.0 - f, 0.0))
        _store_heads(g_ref, rows, jnp.where(real, jnp.log(f), 0.0))


def _inproj_c(hb, w, lb):
    lp, d = hb.shape
    tm = _pick(lp, INPROJ_C_TILES)
    tn = 512
    tps = d // tn
    hpt = tn // HG_DK
    x_spec = pl.BlockSpec((tm, d), lambda i, j: (i, 0))
    out_spec = pl.BlockSpec((1, hpt, tm, HG_DK), lambda i, j: (j // tps, j % tps, i, 0))
    hm = lambda n, dt: jax.ShapeDtypeStruct((n, HG_HEADS, lp, HG_DK), dt)
    plain_col = lambda i, j: (0, jnp.where(j >= 2 * tps, j + 2 * tps, j))
    plain = pl.pallas_call(
        _inproj_plain_kernel,
        grid=(lp // tm, 3 * tps),
        in_specs=[x_spec, pl.BlockSpec((d, tn), plain_col)],
        out_specs=out_spec,
        out_shape=hm(3, BF16),
        compiler_params=_cparams("parallel", "arbitrary"),
        name="inproj_c_plain",
    )(hb, w)
    k, g = pl.pallas_call(
        _inproj_forget_kernel,
        grid=(lp // tm, 2 * tps),
        in_specs=[x_spec, pl.BlockSpec((d, tn), lambda i, j: (0, j + 2 * tps)),
                  pl.BlockSpec((1, tn), lambda i, j: (0, j % tps))],
        out_specs=[out_spec, out_spec],
        out_shape=[hm(2, BF16), hm(2, F32)],
        compiler_params=_cparams("parallel", "arbitrary"),
        name="inproj_c_forget",
    )(hb, w, lb)
    return plain, k, g


def _hgrn_chunk(q, k, v, g, state_t, reverse):
    c = HG_CHUNK
    row = lax.broadcasted_iota(jnp.int32, (c, HG_DK), 0)
    b = g
    sh = 1
    while sh < c:
        if not reverse:
            b = b + jnp.where(row >= sh, pltpu.roll(b, sh, axis=0), 0.0)
        else:
            b = b + jnp.where(row < c - sh, pltpu.roll(b, c - sh, axis=0), 0.0)
        sh *= 2
    r_i = lax.broadcasted_iota(jnp.int32, (c, c), 0)
    c_i = lax.broadcasted_iota(jnp.int32, (c, c), 1)
    causal = (c_i >= r_i) if reverse else (c_i <= r_i)
    sub = HG_SUB
    mid = sub // 2
    blocks = lambda x: [x[j * sub:(j + 1) * sub] for j in range(HG_NSUB)]
    ref = [b[j * sub + mid:j * sub + mid + 1, :] for j in range(HG_NSUB)]
    b_end = b[0:1, :] if reverse else b[c - 1:c, :]
    qd, ku, q0, k_end = [], [], [], []
    for j, (qj, kj, bj) in enumerate(zip(blocks(q), blocks(k), blocks(b))):
        dj = bj - ref[j]
        qd_j = qj * jnp.exp(jnp.minimum(dj, HG_EXP_CLAMP))
        ku_j = kj * jnp.exp(jnp.minimum(-dj, HG_EXP_CLAMP))
        qd.append(qd_j.astype(BF16))
        ku.append(ku_j)
        q0.append(qd_j * jnp.exp(ref[j]))
        k_end.append(ku_j * jnp.exp(b_end - ref[j]))
    nt = (((1,), (1,)), ((), ()))
    zero = jnp.zeros((sub, HG_DK), BF16)
    score_rows = []
    for j in range(HG_NSUB):
        parts = []
        for i in range(HG_NSUB):
            if i == j:
                parts.append(ku[i].astype(BF16))
            elif (i > j) if reverse else (i < j):
                parts.append((ku[i] * jnp.exp(ref[j] - ref[i])).astype(BF16))
            else:
                parts.append(zero)
        k_ext = jnp.concatenate(parts, axis=0)
        score_rows.append(lax.dot_general(qd[j], k_ext, nt, preferred_element_type=F32))
    scores = jnp.concatenate(score_rows, axis=0)
    scores = jnp.where(causal, scores, 0.0)
    o = jnp.dot(scores.astype(BF16), v.astype(BF16), preferred_element_type=F32)
    o += lax.dot_general(jnp.concatenate(q0, axis=0).astype(BF16), state_t.astype(BF16), nt,
                         preferred_element_type=F32)
    new_state = jnp.exp(b_end) * state_t + jnp.dot(
        v.T.astype(BF16), jnp.concatenate(k_end, axis=0).astype(BF16), preferred_element_type=F32)
    return o, new_state


def _hgrn_kernel(qf_ref, vf_ref, k1_ref, g1_ref, qb_ref, vb_ref, k2_ref, g2_ref,
                 ofw_ref, obw_ref, st_ref):
    @pl.when(pl.program_id(0) == 0)
    def _():
        st_ref[...] = jnp.zeros_like(st_ref)

    def head(hh, carry):
        o, s = _hgrn_chunk(qf_ref[0, hh].astype(F32), k1_ref[0, hh].astype(F32),
                           vf_ref[0, hh].astype(F32), g1_ref[0, hh], st_ref[0, hh], reverse=False)
        ofw_ref[hh] = o
        st_ref[0, hh] = s
        o, s = _hgrn_chunk(qb_ref[0, hh].astype(F32), k2_ref[0, hh].astype(F32),
                           vb_ref[0, hh].astype(F32), g2_ref[0, hh], st_ref[1, hh], reverse=True)
        obw_ref[hh] = o
        st_ref[1, hh] = s
        return carry

    lax.fori_loop(0, HG_HEADS, head, 0, unroll=4)


def _hgrn(plain, k, g):
    _, _, lp, dk = plain.shape
    c = HG_CHUNK
    nc = lp // c
    fwd = lambda seg: pl.BlockSpec((1, HG_HEADS, c, dk), lambda i: (seg, 0, i, 0))
    bwd = lambda seg: pl.BlockSpec((1, HG_HEADS, c, dk), lambda i: (seg, 0, nc - 1 - i, 0))
    out_f = pl.BlockSpec((HG_HEADS, c, dk), lambda i: (0, i, 0))
    out_b = pl.BlockSpec((HG_HEADS, c, dk), lambda i: (0, nc - 1 - i, 0))
    return pl.pallas_call(
        _hgrn_kernel,
        grid=(nc,),
        in_specs=[fwd(0), fwd(1), fwd(0), fwd(0), bwd(0), bwd(1), bwd(1), bwd(1)],
        out_specs=[out_f, out_b],
        out_shape=[jax.ShapeDtypeStruct((HG_HEADS, lp, dk), F32)] * 2,
        scratch_shapes=[pltpu.VMEM((2, HG_HEADS, dk, dk), F32)],
        compiler_params=_cparams("arbitrary"),
        name="hgrn2_recurrence",
    )(plain, plain, k, g, plain, plain, k, g)


def _route_params(w_group, b_group, w_expert, b_expert):
    d = w_group.shape[0]
    pad = ROUTE_LANES - N_GROUPS - N_EXPERTS
    w = jnp.concatenate([w_group, w_expert, jnp.zeros((d, pad), F32)], axis=1)
    b = jnp.concatenate([b_group, b_expert, jnp.zeros((pad,), F32)])[None, :]
    w_hi = w.astype(BF16)
    w_lo = (w - w_hi.astype(F32)).astype(BF16)
    return jnp.concatenate([w_hi, w_lo], axis=1), b


def _rope_tables(lp):
    half = HEAD_DIM // 2
    pos = jnp.maximum(jnp.arange(lp) - META_ROW0, 0).astype(F32)
    inv = ROPE_THETA ** (-jnp.arange(half, dtype=F32) * 2.0 / HEAD_DIM)
    ang = pos[:, None] * inv[None, :]
    cos, sin = jnp.cos(ang), jnp.sin(ang)
    return jnp.concatenate([cos, cos], axis=1), jnp.concatenate([-sin, sin], axis=1)


def kernel(x, meta_tokens, w_in_ab, w_out_ab, attn_sinks, s5_lam_re, s5_lam_im, s5_log_step, s5_b_re, s5_b_im, s5_c_re, s5_c_im, s5_d, s5_w_glu, s5_b_glu, w_in_c, w_out_c, hgrn_lb_logits, hgrn_norm_g, ln_mix_g, ln_mix_b, ln_ffn_g, ln_ffn_b, moe_w_group, moe_b_group, moe_w_expert, moe_b_expert, moe_w_gate_up, moe_w_down):
    batch, seq, d = x.shape
    assert batch == 1 and seq % FRONT == 0
    depth = ln_mix_g.shape[0]
    assert depth == 2
    alpha = (2.0 * depth) ** 0.25
    lp = FRONT + seq
    row2 = lambda t: t[None, :]

    front = jnp.concatenate([jnp.zeros((META_ROW0, d), F32), meta_tokens.astype(F32)], axis=0)
    xs = x[0]

    w_in = w_in_ab[0].astype(BF16)
    s5w = s5_d.shape[1]
    wq, wk, wv, wu = (w_in[:, :Q_DIM], w_in[:, Q_DIM:Q_DIM + KV_DIM],
                      w_in[:, Q_DIM + KV_DIM:Q_DIM + 2 * KV_DIM], w_in[:, Q_DIM + 2 * KV_DIM:])
    cos2, sin2 = _rope_tables(lp)
    q, k, v, u = _inproj_ab(front, xs, wq, wk, wv, wu, cos2, sin2)
    a_out = _window_attention(q, k, v, attn_sinks[0])
    mats = _s5_discretise(s5_lam_re[0], s5_lam_im[0], s5_log_step[0], s5_b_re[0], s5_b_im[0],
                          s5_c_re[0], s5_c_im[0])
    y_lo, y_hi = _s5_mixer_pre_glu(u, s5_d[0], mats)
    wo = w_out_ab[0].astype(BF16)
    w_route, b_route = _route_params(moe_w_group[0], moe_b_group[0], moe_w_expert[0], moe_b_expert[0])
    h, h_slabs, route = _mix_ab(front, xs, a_out, y_lo, y_hi, s5_w_glu[0].astype(BF16), row2(s5_b_glu[0]),
                       wo[:Q_DIM], wo[Q_DIM:], row2(ln_mix_g[0]), row2(ln_mix_b[0]), w_route, b_route,
                       alpha)
    plan = _moe_plan(route, META_ROW0)
    y_buf = _moe_experts(h_slabs, plan, moe_w_gate_up, moe_w_down, 0)
    h, hb = _moe_combine(h, plan, y_buf, row2(ln_ffn_g[0]), row2(ln_ffn_b[0]), alpha, 0, with_bf16=True)

    lb_probs = jax.nn.softmax(hgrn_lb_logits.astype(F32), axis=0)
    lb = (jnp.cumsum(lb_probs, axis=0) - lb_probs[0])[1]
    plain, hk, hg = _inproj_c(hb, w_in_c[0].astype(BF16), row2(lb))
    o_fw, o_bw = _hgrn(plain, hk, hg)
    w_route, b_route = _route_params(moe_w_group[1], moe_b_group[1], moe_w_expert[1], moe_b_expert[1])
    h, h_slabs, route = _mix_c(h, o_fw, o_bw, plain, hgrn_norm_g[0].reshape(HG_HEADS, 1, HG_DK),
                      w_out_c[0].astype(BF16), row2(ln_mix_g[1]), row2(ln_mix_b[1]), w_route, b_route,
                      alpha)
    plan = _moe_plan(route, META_ROW0)
    y_buf = _moe_experts(h_slabs, plan, moe_w_gate_up, moe_w_down, 1)
    out = _moe_combine(h, plan, y_buf, row2(ln_ffn_g[1]), row2(ln_ffn_b[1]), alpha, FRONT)
    return out[None]
```

```python
import functools
import math

import jax
import jax.numpy as jnp
from jax import lax
from jax.experimental import pallas as pl
from jax.experimental.pallas import tpu as pltpu

F32 = jnp.float32
BF16 = jnp.bfloat16

N_META = 16
FRONT = 512
META_ROW0 = FRONT - N_META

ATTN_HEADS = 8
ATTN_KV_HEADS = 2
ATTN_GROUP = ATTN_HEADS // ATTN_KV_HEADS
HEAD_DIM = 128
WINDOW = 128
ATTN_BLOCK = 128
ROPE_THETA = 10000.0
Q_DIM = ATTN_HEADS * HEAD_DIM
KV_DIM = ATTN_KV_HEADS * HEAD_DIM

S5_GROUP = 16
S5_STATE = 64
S5_CHUNK = 16
S5_ROW = S5_CHUNK * S5_GROUP
S5_SCAN_BLOCK = 32
S5_CHUNK_BLOCKS = (176, 96, 48, 32, 16)

HG_HEADS = 16
HG_DK = 128
HG_CHUNK = 128
HG_SUB = 32
HG_NSUB = HG_CHUNK // HG_SUB
HG_EXP_CLAMP = 80.0

N_GROUPS = 8
EXPERTS_PER_GROUP = 8
N_EXPERTS = N_GROUPS * EXPERTS_PER_GROUP
TOP_K = 2
D_EXPERT = 512
MOE_TM = 256
ROUTE_LANES = 128

LN_EPS = 1e-5
RMS_EPS = 1e-6
NEG_INF = -1e30

ROW_TILE = 512
MIX_TILE = 256
MIX_SUB = 128
INPROJ_C_TILES = (1536, 1024, 512)
INPROJ_C_SUB = 512

VMEM_LIMIT = 56 * 1024 * 1024


def _cparams(*sem):
    return pltpu.CompilerParams(dimension_semantics=sem, vmem_limit_bytes=VMEM_LIMIT)


def _pick(n, candidates):
    return next(c for c in candidates if n % c == 0)


def _full(shape):
    nd = len(shape)
    return pl.BlockSpec(shape, lambda *_: (0,) * nd)


def _layer_norm(x, g, b):
    mu = jnp.mean(x, axis=-1, keepdims=True)
    xc = x - mu
    var = jnp.mean(xc * xc, axis=-1, keepdims=True)
    return xc * lax.rsqrt(var + LN_EPS) * g + b


def _inproj_ab_kernel(front_ref, x_ref, wq_ref, wk_ref, wv_ref, wu_ref, cos_ref, sin_ref,
                      q_ref, k_ref, v_ref, u_ref):
    xb = jnp.where(pl.program_id(0) == 0, front_ref[...], x_ref[...]).astype(BF16)
    cos = cos_ref[...]
    sin = sin_ref[...]

    def rope(t):
        return t * cos + pltpu.roll(t, HEAD_DIM // 2, axis=1) * sin

    q = jnp.dot(xb, wq_ref[...], preferred_element_type=F32)
    for h in range(ATTN_HEADS):
        sl = slice(h * HEAD_DIM, (h + 1) * HEAD_DIM)
        q_ref[:, sl] = rope(q[:, sl]).astype(BF16)
    k = jnp.dot(xb, wk_ref[...], preferred_element_type=F32)
    for h in range(ATTN_KV_HEADS):
        sl = slice(h * HEAD_DIM, (h + 1) * HEAD_DIM)
        k_ref[:, sl] = rope(k[:, sl]).astype(BF16)
    v_ref[...] = jnp.dot(xb, wv_ref[...], preferred_element_type=F32).astype(BF16)
    u_ref[...] = jnp.dot(xb, wu_ref[...], preferred_element_type=F32)


def _inproj_ab(front, x, wq, wk, wv, wu, cos2, sin2):
    d = x.shape[1]
    lp = front.shape[0] + x.shape[0]
    tm = ROW_TILE
    assert front.shape[0] == tm
    s5w = wu.shape[1]
    row = lambda w: pl.BlockSpec((tm, w), lambda i: (i, 0))
    return pl.pallas_call(
        _inproj_ab_kernel,
        grid=(lp // tm,),
        in_specs=[_full(front.shape), pl.BlockSpec((tm, d), lambda i: (jnp.maximum(i - 1, 0), 0)),
                  _full(wq.shape), _full(wk.shape), _full(wv.shape), _full(wu.shape),
                  row(HEAD_DIM), row(HEAD_DIM)],
        out_specs=[row(Q_DIM), row(KV_DIM), row(KV_DIM), row(s5w)],
        out_shape=[jax.ShapeDtypeStruct((lp, Q_DIM), BF16),
                   jax.ShapeDtypeStruct((lp, KV_DIM), BF16),
                   jax.ShapeDtypeStruct((lp, KV_DIM), BF16),
                   jax.ShapeDtypeStruct((lp, s5w), F32)],
        compiler_params=_cparams("parallel"),
        name="inproj_ab",
    )(front, x, wq, wk, wv, wu, cos2, sin2)


def _attn_kernel(sink_ref, q_ref, k0_ref, k1_ref, k2_ref, v0_ref, v1_ref, v2_ref, km_ref, vm_ref,
                 o_ref, *, lp):
    qb = pl.program_id(0)
    blk = ATTN_BLOCK
    rows = ATTN_GROUP * blk
    scale = HEAD_DIM ** -0.5
    q_row = qb * blk + lax.broadcasted_iota(jnp.int32, (rows, 3 * blk), 0) % blk
    k_row = (qb - 1) * blk + lax.broadcasted_iota(jnp.int32, (rows, 3 * blk), 1)
    vis = (k_row >= FRONT) & (k_row < lp) & (jnp.abs(q_row - k_row) <= WINDOW)
    head_of_row = lax.broadcasted_iota(jnp.int32, (rows, 1), 0) // blk
    for g in range(ATTN_KV_HEADS):
        gs = slice(g * HEAD_DIM, (g + 1) * HEAD_DIM)
        qs = jnp.concatenate(
            [q_ref[:, (g * ATTN_GROUP + r) * HEAD_DIM:(g * ATTN_GROUP + r + 1) * HEAD_DIM]
             for r in range(ATTN_GROUP)], axis=0)
        kband = jnp.concatenate([k0_ref[:, gs], k1_ref[:, gs], k2_ref[:, gs]], axis=0)
        vband = jnp.concatenate([v0_ref[:, gs], v1_ref[:, gs], v2_ref[:, gs]], axis=0)
        nt = (((1,), (1,)), ((), ()))
        s_band = lax.dot_general(qs, kband, nt, preferred_element_type=F32) * scale
        s_band = jnp.where(vis, s_band, NEG_INF)
        s_meta = lax.dot_general(qs, km_ref[:, gs], nt, preferred_element_type=F32) * scale
        sink = jnp.zeros((rows, 1), F32)
        for r in range(ATTN_GROUP):
            sink = jnp.where(head_of_row == r, sink_ref[g * ATTN_GROUP + r], sink)
        m = jnp.maximum(jnp.maximum(jnp.max(s_band, axis=-1, keepdims=True),
                                    jnp.max(s_meta, axis=-1, keepdims=True)), sink)
        e_band = jnp.exp(s_band - m)
        e_meta = jnp.exp(s_meta - m)
        denom = (jnp.sum(e_band, axis=-1, keepdims=True) + jnp.sum(e_meta, axis=-1, keepdims=True)
                 + jnp.exp(sink - m))
        o = (jnp.dot(e_band.astype(BF16), vband, preferred_element_type=F32)
             + jnp.dot(e_meta.astype(BF16), vm_ref[:, gs], preferred_element_type=F32)) / denom
        for r in range(ATTN_GROUP):
            hh = g * ATTN_GROUP + r
            o_ref[:, hh * HEAD_DIM:(hh + 1) * HEAD_DIM] = o[r * blk:(r + 1) * blk].astype(BF16)


def _window_attention(q, k, v, sinks):
    lp = q.shape[0]
    blk = ATTN_BLOCK
    nb = lp // blk
    qspec = pl.BlockSpec((blk, Q_DIM), lambda i, s: (i, 0))
    kv = lambda off: pl.BlockSpec((blk, KV_DIM), lambda i, s: (jnp.clip(i + off, 0, nb - 1), 0))
    meta = pl.BlockSpec((N_META, KV_DIM), lambda i, s: (META_ROW0 // N_META, 0))
    return pl.pallas_call(
        functools.partial(_attn_kernel, lp=lp),
        grid_spec=pltpu.PrefetchScalarGridSpec(
            num_scalar_prefetch=1,
            grid=(nb,),
            in_specs=[qspec, kv(-1), kv(0), kv(1), kv(-1), kv(0), kv(1), meta, meta],
            out_specs=qspec,
        ),
        out_shape=jax.ShapeDtypeStruct((lp, Q_DIM), BF16),
        compiler_params=_cparams("parallel"),
        name="window_attention",
    )(sinks, q, k, k, k, v, v, v, k, v)


def _s5_discretise(lam_re, lam_im, log_step, b_re, b_im, c_re, c_im):
    t = S5_CHUNK
    hi = lax.Precision.HIGHEST
    g = lam_re.shape[1]
    lr = jnp.minimum(lam_re, -1e-4)
    li = lam_im
    step = jnp.exp(log_step)[..., None]
    dr, di = lr * step, li * step
    lags = jnp.arange(t + 1, dtype=F32)[:, None]
    mag = jnp.exp(dr[..., None, :] * lags)
    pr, pi = mag * jnp.cos(di[..., None, :] * lags), mag * jnp.sin(di[..., None, :] * lags)
    ar, ai = pr[:, :, 1], pi[:, :, 1]
    den = lr * lr + li * li
    zr = ((ar - 1.0) * lr + ai * li) / den
    zi = (ai * lr - (ar - 1.0) * li) / den
    br = zr[..., None] * b_re - zi[..., None] * b_im
    bi = zr[..., None] * b_im + zi[..., None] * b_re
    brt, bit = br.transpose(0, 1, 3, 2), bi.transpose(0, 1, 3, 2)
    ct_r, ct_i = c_re.transpose(0, 1, 3, 2), c_im.transpose(0, 1, 3, 2)
    pt_r, pt_i = pr[:, :, :t].transpose(0, 1, 3, 2), pi[:, :, :t].transpose(0, 1, 3, 2)
    xr = (ct_r[:, :, :, None] * pt_r[..., None] - ct_i[:, :, :, None] * pt_i[..., None])
    xi = (ct_r[:, :, :, None] * pt_i[..., None] + ct_i[:, :, :, None] * pt_r[..., None])
    x_cat = jnp.concatenate([xr, xi], axis=2).reshape(2, g, 2 * S5_STATE, t * S5_GROUP)
    b_cat = jnp.concatenate([brt, -bit], axis=3)
    kern = jnp.einsum('dgip,dgpn->dgin', b_cat, x_cat, precision=hi).reshape(2, g, S5_GROUP, t, S5_GROUP)
    k_lag = jnp.concatenate([jnp.flip(kern[1][:, :, 1:], axis=2), kern[0][:, :, :1] + kern[1][:, :, :1],
                             kern[0][:, :, 1:]], axis=2)
    k_lag = k_lag.reshape(g, S5_GROUP, (2 * t - 1) * S5_GROUP)

    def flat_e(p_r, p_i, d):
        e_r = p_r[:, :, None] * brt[d][:, None] - p_i[:, :, None] * bit[d][:, None]
        e_i = p_r[:, :, None] * bit[d][:, None] + p_i[:, :, None] * brt[d][:, None]
        flat = lambda e: e.reshape(g, t * S5_GROUP, S5_STATE)
        return jnp.concatenate([flat(e_r), flat(e_i)], axis=-1)
    e_mat = jnp.concatenate([flat_e(pr[0][:, ::-1][:, 1:], pi[0][:, ::-1][:, 1:], 0),
                             flat_e(pr[1][:, :t], pi[1][:, :t], 1)], axis=-1)

    def flat_f(p_r, p_i, d):
        ct_r, ct_i = c_re[d].transpose(0, 2, 1), c_im[d].transpose(0, 2, 1)
        pt_r, pt_i = p_r.transpose(0, 2, 1)[..., None], p_i.transpose(0, 2, 1)[..., None]
        w_r = ct_r[:, :, None] * pt_r - ct_i[:, :, None] * pt_i
        w_i = ct_r[:, :, None] * pt_i + ct_i[:, :, None] * pt_r
        flat = lambda w: w.reshape(g, S5_STATE, t * S5_GROUP)
        return jnp.concatenate([flat(w_r), -flat(w_i)], axis=1)
    f_fw = flat_f(pr[0][:, 1:], pi[0][:, 1:], 0)
    f_bw = flat_f(pr[1][:, ::-1][:, :t], pi[1][:, ::-1][:, :t], 1)
    gh = g // 2
    upper = (jnp.arange(g) >= gh)[:, None, None]
    def place(m):
        z = jnp.zeros_like(m)
        return jnp.concatenate([jnp.where(upper, z, m), jnp.where(upper, m, z)], axis=1)
    p = S5_STATE
    f_mat = jnp.concatenate([place(f_fw[:, :p]), place(f_fw[:, p:]), place(f_bw[:, :p]), place(f_bw[:, p:])],
                            axis=1)
    pack = lambda v: jnp.concatenate([v[:gh], v[gh:]], axis=-1)
    coef = jnp.stack([pack(pr[0][:, t]), pack(pi[0][:, t]), pack(pr[1][:, t]), pack(pi[1][:, t])])
    split = lambda m: m.reshape((2, gh) + m.shape[1:])
    return split(k_lag), split(e_mat.astype(BF16)), split(f_mat.astype(BF16)), coef


def _chunk_rows_to_lanes(u_ref, ncb):
    lane_seg = lax.broadcasted_iota(jnp.int32, (ncb, 128), 1) // S5_GROUP
    rows = [u_ref[pl.ds(s, ncb, stride=S5_CHUNK), :] for s in range(S5_CHUNK)]
    segs = 128 // S5_GROUP
    out = []
    for j in range(segs):
        halves = []
        for h in range(S5_CHUNK // segs):
            acc = None
            for s8 in range(segs):
                r = rows[h * segs + s8]
                k = (s8 - j) % segs
                if k:
                    r = pltpu.roll(r, S5_GROUP * k, axis=1)
                acc = r if acc is None else jnp.where(lane_seg == s8, r, acc)
            halves.append(acc)
        out.append(jnp.concatenate(halves, axis=1))
    return out


def _lanes_to_chunk_rows(ys, y_ref, ncb, add_ref, scale_ref):
    lane_seg = lax.broadcasted_iota(jnp.int32, (ncb, 128), 1) // S5_GROUP
    segs = 128 // S5_GROUP
    for t in range(S5_CHUNK):
        h, t8 = divmod(t, segs)
        acc = None
        for j in range(segs):
            r = ys[j][:, h * 128:(h + 1) * 128]
            k = (j - t8) % segs
            if k:
                r = pltpu.roll(r, S5_GROUP * k, axis=1)
            acc = r if acc is None else jnp.where(lane_seg == j, r, acc)
        idx = pl.ds(t, ncb, stride=S5_CHUNK)
        y_ref[idx, :] = acc + add_ref[idx, :] * scale_ref[...]


def _pair_halves(a, b):
    lane = lax.broadcasted_iota(jnp.int32, a.shape, 1)
    lo = jnp.where(lane < S5_STATE, a, pltpu.roll(b, S5_STATE, axis=1))
    hi = jnp.where(lane < S5_STATE, pltpu.roll(a, S5_STATE, axis=1), b)
    return lo, hi


def _s5_local_kernel(ua_ref, ub_ref, e_ref, u2_ref, f1_ref, f2_ref, b1_ref, b2_ref):
    ncb = u2_ref.shape[2]
    p2 = 2 * S5_STATE
    chunks = [_chunk_rows_to_lanes(ua_ref, ncb), _chunk_rows_to_lanes(ub_ref, ncb)]
    for j in range(u2_ref.shape[1]):
        st = []
        for hh in range(2):
            u2 = chunks[hh][j].astype(BF16)
            u2_ref[hh, j] = u2
            st.append(jnp.dot(u2, e_ref[hh, j], preferred_element_type=F32))
        f1_ref[:, j, :], f2_ref[:, j, :] = _pair_halves(st[0][:, :p2], st[1][:, :p2])
        b1_ref[:, j, :], b2_ref[:, j, :] = _pair_halves(st[0][:, p2:], st[1][:, p2:])


def _s5_scan_kernel(sf1_ref, sf2_ref, sb1_ref, sb2_ref, coef_ref, xf1_ref, xf2_ref, xb1_ref, xb2_ref,
                    st_ref):
    @pl.when(pl.program_id(0) == 0)
    def _():
        st_ref[...] = jnp.zeros_like(st_ref)

    nb = sf1_ref.shape[0]
    crf, cif, crb, cib = coef_ref[0], coef_ref[1], coef_ref[2], coef_ref[3]

    def body(i, carry):
        f1, f2, b1, b2 = carry
        j = nb - 1 - i
        xf1_ref[i] = f1
        xf2_ref[i] = f2
        xb1_ref[j] = b1
        xb2_ref[j] = b2
        return (crf * f1 - cif * f2 + sf1_ref[i], crf * f2 + cif * f1 + sf2_ref[i],
                crb * b1 - cib * b2 + sb1_ref[j], crb * b2 + cib * b1 + sb2_ref[j])

    out = lax.fori_loop(0, nb, body, (st_ref[0], st_ref[1], st_ref[2], st_ref[3]))
    for q in range(4):
        st_ref[q] = out[q]


def _chunk_matrix(k_lag):
    t, c = S5_CHUNK, S5_GROUP
    return jnp.concatenate([k_lag[:, (t - 1 - s) * c:(2 * t - 1 - s) * c] for s in range(t)],
                           axis=0).astype(BF16)


def _s5_readout_kernel(ua_ref, ub_ref, d_ref, u2_ref, kl_ref, f_ref, xf1_ref, xf2_ref, xb1_ref, xb2_ref,
                       ya_ref, yb_ref):
    ncb = u2_ref.shape[2]
    ys = [[], []]
    for j in range(u2_ref.shape[1]):
        x = jnp.concatenate([xf1_ref[:, j, :], xf2_ref[:, j, :], xb1_ref[:, j, :], xb2_ref[:, j, :]],
                            axis=1).astype(BF16)
        for hh in range(2):
            y = jnp.dot(u2_ref[hh, j], _chunk_matrix(kl_ref[hh, j]), preferred_element_type=F32)
            ys[hh].append(y + jnp.dot(x, f_ref[hh, j], preferred_element_type=F32))
    _lanes_to_chunk_rows(ys[0], ya_ref, ncb, ua_ref, d_ref.at[0])
    _lanes_to_chunk_rows(ys[1], yb_ref, ncb, ub_ref, d_ref.at[1])


def _s5_mixer_pre_glu(u, d_skip, mats):
    k_lag, e_mat, f_mat, coef = mats
    lp, w = u.shape
    g = w // S5_GROUP
    gh = g // 2
    gpb = 128 // S5_GROUP
    nc = lp // S5_CHUNK
    ncb = _pick(nc, S5_CHUNK_BLOCKS)
    rows = ncb * S5_CHUNK
    nlb = w // 128
    grid = (nlb // 2, nc // ncb)
    u_lo = pl.BlockSpec((rows, 128), lambda b, r: (r, b))
    u_hi = pl.BlockSpec((rows, 128), lambda b, r: (r, b + nlb // 2))
    per_group = lambda k, n: pl.BlockSpec((2, gpb, k, n), lambda b, r: (0, b, 0, 0))
    u2_spec = pl.BlockSpec((2, gpb, ncb, S5_ROW), lambda b, r: (0, b, r, 0))
    st_spec = pl.BlockSpec((ncb, gpb, 2 * S5_STATE), lambda b, r: (r, b, 0))
    st_shape = jax.ShapeDtypeStruct((nc, gh, 2 * S5_STATE), F32)
    u2, sf1, sf2, sb1, sb2 = pl.pallas_call(
        _s5_local_kernel,
        grid=grid,
        in_specs=[u_lo, u_hi, per_group(S5_ROW, 4 * S5_STATE)],
        out_specs=[u2_spec, st_spec, st_spec, st_spec, st_spec],
        out_shape=[jax.ShapeDtypeStruct((2, gh, nc, S5_ROW), BF16)] + [st_shape] * 4,
        compiler_params=_cparams("parallel", "parallel"),
        name="s5_local_states",
    )(u, u, e_mat)
    sb = S5_SCAN_BLOCK
    nblk = nc // sb
    fwd = pl.BlockSpec((sb, gh, 2 * S5_STATE), lambda i: (i, 0, 0))
    bwd = pl.BlockSpec((sb, gh, 2 * S5_STATE), lambda i: (nblk - 1 - i, 0, 0))
    xf1, xf2, xb1, xb2 = pl.pallas_call(
        _s5_scan_kernel,
        grid=(nblk,),
        in_specs=[fwd, fwd, bwd, bwd, _full(coef.shape)],
        out_specs=[fwd, fwd, bwd, bwd],
        out_shape=[st_shape] * 4,
        scratch_shapes=[pltpu.VMEM((4, gh, 2 * S5_STATE), F32)],
        compiler_params=_cparams("arbitrary"),
        name="s5_chunk_scan",
    )(sf1, sf2, sb1, sb2, coef)
    d2 = d_skip.reshape(2, 1, nlb // 2 * 128)
    y_lo, y_hi = pl.pallas_call(
        _s5_readout_kernel,
        grid=grid,
        in_specs=[u_lo, u_hi, pl.BlockSpec((2, 1, 128), lambda b, r: (0, 0, b)), u2_spec,
                  per_group(S5_GROUP, k_lag.shape[-1]), per_group(8 * S5_STATE, S5_ROW),
                  st_spec, st_spec, st_spec, st_spec],
        out_specs=[pl.BlockSpec((rows, 128), lambda b, r: (r, b))] * 2,
        out_shape=[jax.ShapeDtypeStruct((lp, w // 2), F32)] * 2,
        compiler_params=_cparams("parallel", "parallel"),
        name="s5_readout",
    )(u, u, d2, u2, k_lag, f_mat, xf1, xf2, xb1, xb2)
    return y_lo, y_hi


def _route(hn, wr_ref, br_ref):
    hn_hi = hn.astype(BF16)
    hn_lo = (hn - hn_hi.astype(F32)).astype(BF16)
    p_hi = jnp.dot(hn_hi, wr_ref[...], preferred_element_type=F32)
    p_lo = jnp.dot(hn_lo, wr_ref[:, :ROUTE_LANES], preferred_element_type=F32)
    logits = p_hi[:, :ROUTE_LANES] + (p_hi[:, ROUTE_LANES:] + p_lo) + br_ref[...]
    lane_i = lax.broadcasted_iota(jnp.int32, logits.shape, 1)
    lane = lane_i.astype(F32)
    lane_grp = ((lane_i - N_GROUPS) // EXPERTS_PER_GROUP).astype(F32)
    big = float(ROUTE_LANES)
    g_log = jnp.where(lane_i < N_GROUPS, logits, -jnp.inf)
    g_max = jnp.max(g_log, axis=-1, keepdims=True)
    grp = jnp.min(jnp.where(g_log == g_max, lane, big), axis=-1, keepdims=True)
    p_grp = 1.0 / jnp.sum(jnp.exp(g_log - g_max), axis=-1, keepdims=True)
    in_grp = (lane_i >= N_GROUPS) & (lane_i < N_GROUPS + N_EXPERTS) & (lane_grp == grp)
    e_log = jnp.where(in_grp, logits, -jnp.inf)
    v1 = jnp.max(e_log, axis=-1, keepdims=True)
    i1 = jnp.min(jnp.where(e_log == v1, lane, big), axis=-1, keepdims=True)
    e_log2 = jnp.where(lane == i1, -jnp.inf, e_log)
    v2 = jnp.max(e_log2, axis=-1, keepdims=True)
    i2 = jnp.min(jnp.where(e_log2 == v2, lane, big), axis=-1, keepdims=True)
    e21 = jnp.exp(v2 - v1)
    w1 = p_grp / (1.0 + e21)
    w2 = p_grp * e21 / (1.0 + e21)
    route = jnp.where(lane_i == 0, w1, 0.0)
    route = jnp.where(lane_i == 1, w2, route)
    route = jnp.where(lane_i == 2, i1 - N_GROUPS, route)
    route = jnp.where(lane_i == 3, i2 - N_GROUPS, route)
    return route


def _slab_spec(tm, d):
    return pl.BlockSpec((tm * (d // 128), 128), lambda i: (i, 0))


def _slab_shape(rows, d):
    return jax.ShapeDtypeStruct((rows * (d // 128), 128), F32)


def _store_row_slabs(ref, rows, val):
    nl = val.shape[1] // 128
    for c in range(nl):
        ref[pl.ds(rows.start * nl + c, rows.stop - rows.start, stride=nl), :] = val[:, c * 128:(c + 1) * 128]


def _sub_rows(tm):
    return [slice(r, r + MIX_SUB) for r in range(0, tm, MIX_SUB)]


def _mix_ab_kernel(front_ref, x_ref, a_ref, ylo_ref, yhi_ref, wglu_ref, bglu_ref, woa_ref, wos_ref,
                   g_ref, b_ref, wr_ref, br_ref, hn_ref, hs_ref, route_ref, *, alpha):
    in_front = pl.program_id(0) < FRONT // MIX_TILE
    for rows in _sub_rows(x_ref.shape[0]):
        y = jnp.concatenate([ylo_ref[rows, :], yhi_ref[rows, :]], axis=1)
        z = 0.5 * y * (1.0 + jnp.tanh(math.sqrt(2.0 / math.pi) * (y + 0.044715 * (y * y * y))))
        gate = jnp.dot(z.astype(BF16), wglu_ref[...], preferred_element_type=F32) + bglu_ref[...]
        s_out = z * jax.nn.sigmoid(gate)
        mix = (jnp.dot(a_ref[rows, :], woa_ref[...], preferred_element_type=F32)
               + jnp.dot(s_out.astype(BF16), wos_ref[...], preferred_element_type=F32))
        h_in = jnp.where(in_front, front_ref[rows, :], x_ref[rows, :])
        hn = _layer_norm(alpha * h_in + mix, g_ref[...], b_ref[...])
        hn_ref[rows, :] = hn
        _store_row_slabs(hs_ref, rows, hn)
        route_ref[rows, :] = _route(hn, wr_ref, br_ref)


def _mix_ab(front, x, a_out, y_lo, y_hi, w_glu, b_glu, wo_a, wo_s, ln_g, ln_b, w_route, b_route, alpha):
    d = x.shape[1]
    lp = front.shape[0] + x.shape[0]
    tm = MIX_TILE
    nf = front.shape[0] // tm
    row = lambda c: pl.BlockSpec((tm, c), lambda i: (i, 0))
    return pl.pallas_call(
        functools.partial(_mix_ab_kernel, alpha=alpha),
        grid=(lp // tm,),
        in_specs=[pl.BlockSpec((tm, d), lambda i: (jnp.minimum(i, nf - 1), 0)),
                  pl.BlockSpec((tm, d), lambda i: (jnp.maximum(i - nf, 0), 0)), row(Q_DIM), row(y_lo.shape[1]), row(y_hi.shape[1]), _full(w_glu.shape),
                  _full(b_glu.shape), _full(wo_a.shape), _full(wo_s.shape), _full(ln_g.shape),
                  _full(ln_b.shape), _full(w_route.shape), _full(b_route.shape)],
        out_specs=[row(d), _slab_spec(tm, d), row(ROUTE_LANES)],
        out_shape=[jax.ShapeDtypeStruct((lp, d), F32), _slab_shape(lp, d),
                   jax.ShapeDtypeStruct((lp, ROUTE_LANES), F32)],
        compiler_params=_cparams("parallel"),
        name="mix_ab",
    )(front, x, a_out, y_lo, y_hi, w_glu, b_glu, wo_a, wo_s, ln_g, ln_b, w_route, b_route)


def _mix_c_kernel(h_ref, ofw_ref, obw_ref, gate_ref, ng_ref, wo_ref, g_ref, b_ref, wr_ref, br_ref,
                  hn_ref, hs_ref, route_ref, *, alpha):
    for rows in _sub_rows(h_ref.shape[0]):
        parts = []
        for hh in range(HG_HEADS):
            o = ofw_ref[hh, rows, :] + obw_ref[hh, rows, :]
            o = o * lax.rsqrt(jnp.mean(o * o, axis=-1, keepdims=True) + RMS_EPS)
            o = o * ng_ref[hh] * jax.nn.sigmoid(gate_ref[0, hh, rows, :].astype(F32))
            parts.append(o.astype(BF16))
        mix = jnp.dot(jnp.concatenate(parts, axis=1), wo_ref[...], preferred_element_type=F32)
        hn = _layer_norm(alpha * h_ref[rows, :] + mix, g_ref[...], b_ref[...])
        hn_ref[rows, :] = hn
        _store_row_slabs(hs_ref, rows, hn)
        route_ref[rows, :] = _route(hn, wr_ref, br_ref)


def _mix_c(h, o_fw, o_bw, plain, norm_g, wo, ln_g, ln_b, w_route, b_route, alpha):
    lp, d = h.shape
    tm = MIX_TILE
    row = lambda c: pl.BlockSpec((tm, c), lambda i: (i, 0))
    hm = pl.BlockSpec((HG_HEADS, tm, HG_DK), lambda i: (0, i, 0))
    gate = pl.BlockSpec((1, HG_HEADS, tm, HG_DK), lambda i: (2, 0, i, 0))
    return pl.pallas_call(
        functools.partial(_mix_c_kernel, alpha=alpha),
        grid=(lp // tm,),
        in_specs=[row(d), hm, hm, gate, _full(norm_g.shape), _full(wo.shape), _full(ln_g.shape),
                  _full(ln_b.shape), _full(w_route.shape), _full(b_route.shape)],
        out_specs=[row(d), _slab_spec(tm, d), row(ROUTE_LANES)],
        out_shape=[jax.ShapeDtypeStruct((lp, d), F32), _slab_shape(lp, d),
                   jax.ShapeDtypeStruct((lp, ROUTE_LANES), F32)],
        compiler_params=_cparams("parallel"),
        name="mix_c",
    )(h, o_fw, o_bw, plain, norm_g, wo, ln_g, ln_b, w_route, b_route)


def _moe_plan(route, n_rows_valid_from):
    lp = route.shape[0]
    tm = MOE_TM
    valid = (jnp.arange(lp) >= n_rows_valid_from)
    expert = route[:, 2:4].astype(jnp.int32)
    flat_e = jnp.where(valid[:, None], expert, N_EXPERTS).reshape(-1)
    onehot = (flat_e[:, None] == jnp.arange(N_EXPERTS)[None, :]).astype(jnp.int32)
    csum = jnp.cumsum(onehot, axis=0)
    counts = csum[-1]
    rank = jnp.sum(jnp.where(onehot > 0, csum - 1, 0), axis=1)
    padded = (counts + tm - 1) // tm * tm
    pend = jnp.cumsum(padded)
    pstart = pend - padded
    e_safe = jnp.minimum(flat_e, N_EXPERTS - 1)
    dest = jnp.where(flat_e < N_EXPERTS, pstart[e_safe] + rank, 0)
    n_tokens = lp - n_rows_valid_from
    n_blocks = -(-(n_tokens * TOP_K + N_EXPERTS * (tm - 1)) // tm)
    rows = n_blocks * tm
    tok = jnp.repeat(jnp.arange(lp, dtype=jnp.int32), TOP_K)
    scatter_to = jnp.where(flat_e < N_EXPERTS, dest, rows)
    tok_of_row = jnp.zeros((rows,), jnp.int32).at[scatter_to].set(tok, mode='drop')
    n_used = (pend[-1] // tm).astype(jnp.int32)
    blk = jnp.minimum(jnp.arange(n_blocks, dtype=jnp.int32), jnp.maximum(n_used - 1, 0))
    block_expert = jnp.minimum(jnp.searchsorted(pend, blk * tm, side='right'),
                               N_EXPERTS - 1).astype(jnp.int32)
    used = counts > 0
    pos = jnp.cumsum(used.astype(jnp.int32)) - 1
    eid = jnp.arange(N_EXPERTS, dtype=jnp.int32)
    later = jnp.where(used[None, :] & (eid[None, :] > eid[:, None]), eid[None, :], N_EXPERTS)
    nxt_e = jnp.min(later, axis=1)
    nxt_e = jnp.where(nxt_e == N_EXPERTS, -1, nxt_e).astype(jnp.int32)
    w_slot = (pos[block_expert] % 2).astype(jnp.int32)
    nxt = nxt_e[block_expert]
    weights = jnp.where(valid[:, None], route[:, 0:2], 0.0)
    return (block_expert, n_used.reshape(1), w_slot, nxt, tok_of_row.reshape(n_blocks, 1, tm),
            dest.reshape(lp, TOP_K).astype(jnp.int32), weights)


def _moe_expert_kernel(be_ref, nu_ref, ws_ref, nx_ref, tok0_ref, tokn_ref, h_hbm, wgu_hbm, wd_hbm,
                       y_ref, xbuf, xsem, wgu_f, wd_f, wsem, wgu_bf, wd_bf, *, layer):
    i = pl.program_id(0)
    nl = wgu_f.shape[1] // 128
    tm = xbuf.shape[1] // nl
    n_used = nu_ref[0]

    def gather(tok_ref, slot):
        def issue(r, carry):
            src = h_hbm.at[pl.ds(pl.multiple_of(tok_ref[0, 0, r], nl), nl)]
            pltpu.make_async_copy(src, xbuf.at[slot, pl.ds(pl.multiple_of(r * nl, nl), nl)],
                                  xsem.at[slot]).start()
            return carry
        lax.fori_loop(0, tm, issue, 0, unroll=8)

    def gather_wait(slot):
        pltpu.make_async_copy(h_hbm.at[pl.ds(0, tm * nl)], xbuf.at[slot], xsem.at[slot]).wait()

    def weight_copies(e, s):
        return (pltpu.make_async_copy(wgu_hbm.at[layer, e], wgu_f.at[s], wsem.at[0, s]),
                pltpu.make_async_copy(wd_hbm.at[layer, e], wd_f.at[s], wsem.at[1, s]))

    @pl.when(i == 0)
    def _():
        for cp in weight_copies(be_ref[0], 0):
            cp.start()
        gather(tok0_ref, 0)

    @pl.when(i < n_used)
    def _():
        slot = i % 2
        e = be_ref[i]
        first = jnp.logical_or(i == 0, be_ref[jnp.maximum(i - 1, 0)] != e)

        gather_wait(slot)
        gather(tokn_ref, 1 - slot)

        @pl.when(first)
        def _():
            s = ws_ref[i]
            for cp in weight_copies(e, s):
                cp.wait()
            nxt = nx_ref[i]

            @pl.when(nxt >= 0)
            def _():
                for cp in weight_copies(nxt, 1 - s):
                    cp.start(priority=1)

            wgu_bf[...] = wgu_f[s].astype(BF16)
            wd_bf[...] = wd_f[s].astype(BF16)

        x = jnp.concatenate([xbuf[slot, pl.ds(c, tm, stride=nl), :] for c in range(nl)], axis=1)
        hgu = jnp.dot(x.astype(BF16), wgu_bf[...], preferred_element_type=F32)
        act = jax.nn.silu(hgu[:, :D_EXPERT]) * hgu[:, D_EXPERT:]
        y = jnp.dot(act.astype(BF16), wd_bf[...], preferred_element_type=F32)
        for c in range(y.shape[1] // 128):
            y_ref[pl.ds(c, tm, stride=y.shape[1] // 128), :] = y[:, c * 128:(c + 1) * 128]

        @pl.when(i == n_used - 1)
        def _():
            gather_wait(1 - slot)

    @pl.when(i >= n_used)
    def _():
        y_ref[...] = jnp.zeros_like(y_ref)


def _moe_experts(h_slabs, plan, w_gate_up, w_down, layer):
    block_expert, n_used, w_slot, nxt, tok_of_row, _, _ = plan
    d = w_gate_up.shape[2]
    tok_of_row = tok_of_row * (d // 128)
    tm = MOE_TM
    n_blocks = tok_of_row.shape[0]
    nxt_blk = lambda i, be, nu, ws, nx: (jnp.minimum(i + 1, jnp.maximum(nu[0] - 1, 0)), 0, 0)
    smem = lambda imap: pl.BlockSpec((1, 1, tm), imap, memory_space=pltpu.SMEM)
    anyspace = pl.BlockSpec(memory_space=pl.ANY)
    return pl.pallas_call(
        functools.partial(_moe_expert_kernel, layer=layer),
        grid_spec=pltpu.PrefetchScalarGridSpec(
            num_scalar_prefetch=4,
            grid=(n_blocks,),
            in_specs=[smem(lambda i, be, nu, ws, nx: (0, 0, 0)), smem(nxt_blk),
                      anyspace, anyspace, anyspace],
            out_specs=pl.BlockSpec((tm * (d // 128), 128), lambda i, be, nu, ws, nx: (i, 0)),
            scratch_shapes=[pltpu.VMEM((2, tm * (d // 128), 128), F32), pltpu.SemaphoreType.DMA((2,)),
                            pltpu.VMEM((2, d, 2 * D_EXPERT), F32), pltpu.VMEM((2, D_EXPERT, d), F32),
                            pltpu.SemaphoreType.DMA((2, 2)),
                            pltpu.VMEM((d, 2 * D_EXPERT), BF16), pltpu.VMEM((D_EXPERT, d), BF16)],
        ),
        out_shape=jax.ShapeDtypeStruct((n_blocks * tm * (d // 128), 128), F32),
        compiler_params=_cparams("arbitrary"),
        name="moe_experts",
    )(block_expert, n_used, w_slot, nxt, tok_of_row, tok_of_row, h_slabs, w_gate_up, w_down)


def _moe_combine_kernel(d0_ref, d1_ref, d0n_ref, d1n_ref, h_ref, w_ref, y_hbm, g_ref, b_ref, *rest,
                        alpha, with_bf16):
    if with_bf16:
        o_ref, ob_ref, buf, sem = rest
    else:
        o_ref, buf, sem = rest
    i = pl.program_id(0)
    tm, d = h_ref.shape
    nl = d // 128
    slot = i % 2

    def gather(a_ref, b_ref_, s):
        def issue(r, carry):
            for k, idx_ref in enumerate((a_ref, b_ref_)):
                src = y_hbm.at[pl.ds(pl.multiple_of(idx_ref[0, 0, r], nl), nl)]
                pltpu.make_async_copy(src, buf.at[s, k, pl.ds(pl.multiple_of(r * nl, nl), nl)],
                                      sem.at[s, k]).start()
            return carry
        lax.fori_loop(0, tm, issue, 0, unroll=8)

    @pl.when(i == 0)
    def _():
        gather(d0_ref, d1_ref, 0)

    for k in range(TOP_K):
        pltpu.make_async_copy(y_hbm.at[pl.ds(0, tm * nl)], buf.at[slot, k], sem.at[slot, k]).wait()

    @pl.when(i + 1 < pl.num_programs(0))
    def _():
        gather(d0n_ref, d1n_ref, 1 - slot)

    w = w_ref[...]
    cols = []
    for c in range(nl):
        rows = pl.ds(c, tm, stride=nl)
        ffn = w[:, 0:1] * buf[slot, 0, rows, :] + w[:, 1:2] * buf[slot, 1, rows, :]
        cols.append(alpha * h_ref[:, c * 128:(c + 1) * 128] + ffn)
    out = _layer_norm(jnp.concatenate(cols, axis=1), g_ref[...], b_ref[...])
    o_ref[...] = out
    if with_bf16:
        ob_ref[...] = out.astype(BF16)


def _moe_combine(h, plan, y_buf, ln_g, ln_b, alpha, first_row, with_bf16=False):
    dest, weights = plan[-2], plan[-1]
    lp, d = h.shape
    tm = MIX_TILE
    off = first_row // tm
    nt = lp // tm
    d0 = (dest[:, 0] * (d // 128)).reshape(nt, 1, tm)
    d1 = (dest[:, 1] * (d // 128)).reshape(nt, 1, tm)
    cur = pl.BlockSpec((1, 1, tm), lambda i: (off, 0, 0), memory_space=pltpu.SMEM)
    nxt = pl.BlockSpec((1, 1, tm), lambda i: (jnp.minimum(i + 1 + off, nt - 1), 0, 0),
                       memory_space=pltpu.SMEM)
    out_spec = pl.BlockSpec((tm, d), lambda i: (i, 0))
    out_shape = jax.ShapeDtypeStruct((lp - first_row, d), F32)
    if with_bf16:
        out_spec = [out_spec, out_spec]
        out_shape = [out_shape, jax.ShapeDtypeStruct((lp - first_row, d), BF16)]
    return pl.pallas_call(
        functools.partial(_moe_combine_kernel, alpha=alpha, with_bf16=with_bf16),
        grid=(nt - off,),
        in_specs=[cur, cur, nxt, nxt, pl.BlockSpec((tm, d), lambda i: (i + off, 0)),
                  pl.BlockSpec((tm, TOP_K), lambda i: (i + off, 0)),
                  pl.BlockSpec(memory_space=pl.ANY), _full(ln_g.shape), _full(ln_b.shape)],
        out_specs=out_spec,
        out_shape=out_shape,
        scratch_shapes=[pltpu.VMEM((2, TOP_K, tm * (d // 128), 128), F32),
                        pltpu.SemaphoreType.DMA((2, TOP_K))],
        compiler_params=_cparams("arbitrary"),
        name="moe_combine",
    )(d0, d1, d0, d1, h, weights, y_buf, ln_g, ln_b)


def _store_heads(ref, seg_rows, val):
    for hh in range(val.shape[1] // HG_DK):
        ref[0, hh, seg_rows, :] = val[:, hh * HG_DK:(hh + 1) * HG_DK].astype(ref.dtype)


def _inproj_plain_kernel(x_ref, w_ref, o_ref):
    for r in range(0, x_ref.shape[0], INPROJ_C_SUB):
        rows = slice(r, r + INPROJ_C_SUB)
        _store_heads(o_ref, rows, jnp.dot(x_ref[rows, :], w_ref[...], preferred_element_type=F32))


def _inproj_forget_kernel(x_ref, w_ref, lb_ref, k_ref, g_ref):
    i = pl.program_id(0)
    tm = x_ref.shape[0]
    lb = lb_ref[...]
    for r in range(0, tm, INPROJ_C_SUB):
        rows = slice(r, r + INPROJ_C_SUB)
        acc = jnp.dot(x_ref[rows, :], w_ref[...], preferred_element_type=F32)
        f = lb + (1.0 - lb) * jax.nn.sigmoid(acc)
        real = (i * tm + r + lax.broadcasted_iota(jnp.int32, (INPROJ_C_SUB, 1), 0)) >= META_ROW0
        _store_heads(k_ref, rows, jnp.where(real, 1.0 - f, 0.0))
        _store_heads(g_ref, rows, jnp.where(real, jnp.log(f), 0.0))


def _inproj_c(hb, w, lb):
    lp, d = hb.shape
    tm = _pick(lp, INPROJ_C_TILES)
    tn = 512
    tps = d // tn
    hpt = tn // HG_DK
    x_spec = pl.BlockSpec((tm, d), lambda i, j: (i, 0))
    out_spec = pl.BlockSpec((1, hpt, tm, HG_DK), lambda i, j: (j // tps, j % tps, i, 0))
    hm = lambda n, dt: jax.ShapeDtypeStruct((n, HG_HEADS, lp, HG_DK), dt)
    plain_col = lambda i, j: (0, jnp.where(j >= 2 * tps, j + 2 * tps, j))
    plain = pl.pallas_call(
        _inproj_plain_kernel,
        grid=(lp // tm, 3 * tps),
        in_specs=[x_spec, pl.BlockSpec((d, tn), plain_col)],
        out_specs=out_spec,
        out_shape=hm(3, BF16),
        compiler_params=_cparams("parallel", "arbitrary"),
        name="inproj_c_plain",
    )(hb, w)
    k, g = pl.pallas_call(
        _inproj_forget_kernel,
        grid=(lp // tm, 2 * tps),
        in_specs=[x_spec, pl.BlockSpec((d, tn), lambda i, j: (0, j + 2 * tps)),
                  pl.BlockSpec((1, tn), lambda i, j: (0, j % tps))],
        out_specs=[out_spec, out_spec],
        out_shape=[hm(2, BF16), hm(2, F32)],
        compiler_params=_cparams("parallel", "arbitrary"),
        name="inproj_c_forget",
    )(hb, w, lb)
    return plain, k, g


def _hgrn_chunk(q, k, v, g, state_t, reverse):
    c = HG_CHUNK
    row = lax.broadcasted_iota(jnp.int32, (c, HG_DK), 0)
    b = g
    sh = 1
    while sh < c:
        if not reverse:
            b = b + jnp.where(row >= sh, pltpu.roll(b, sh, axis=0), 0.0)
        else:
            b = b + jnp.where(row < c - sh, pltpu.roll(b, c - sh, axis=0), 0.0)
        sh *= 2
    r_i = lax.broadcasted_iota(jnp.int32, (c, c), 0)
    c_i = lax.broadcasted_iota(jnp.int32, (c, c), 1)
    causal = (c_i >= r_i) if reverse else (c_i <= r_i)
    sub = HG_SUB
    mid = sub // 2
    blocks = lambda x: [x[j * sub:(j + 1) * sub] for j in range(HG_NSUB)]
    ref = [b[j * sub + mid:j * sub + mid + 1, :] for j in range(HG_NSUB)]
    b_end = b[0:1, :] if reverse else b[c - 1:c, :]
    qd, ku, q0, k_end = [], [], [], []
    for j, (qj, kj, bj) in enumerate(zip(blocks(q), blocks(k), blocks(b))):
        dj = bj - ref[j]
        qd_j = qj * jnp.exp(jnp.minimum(dj, HG_EXP_CLAMP))
        ku_j = kj * jnp.exp(jnp.minimum(-dj, HG_EXP_CLAMP))
        qd.append(qd_j.astype(BF16))
        ku.append(ku_j)
        q0.append(qd_j * jnp.exp(ref[j]))
        k_end.append(ku_j * jnp.exp(b_end - ref[j]))
    nt = (((1,), (1,)), ((), ()))
    zero = jnp.zeros((sub, HG_DK), BF16)
    score_rows = []
    for j in range(HG_NSUB):
        parts = []
        for i in range(HG_NSUB):
            if i == j:
                parts.append(ku[i].astype(BF16))
            elif (i > j) if reverse else (i < j):
                parts.append((ku[i] * jnp.exp(ref[j] - ref[i])).astype(BF16))
            else:
                parts.append(zero)
        k_ext = jnp.concatenate(parts, axis=0)
        score_rows.append(lax.dot_general(qd[j], k_ext, nt, preferred_element_type=F32))
    scores = jnp.concatenate(score_rows, axis=0)
    scores = jnp.where(causal, scores, 0.0)
    o = jnp.dot(scores.astype(BF16), v.astype(BF16), preferred_element_type=F32)
    o += lax.dot_general(jnp.concatenate(q0, axis=0).astype(BF16), state_t.astype(BF16), nt,
                         preferred_element_type=F32)
    new_state = jnp.exp(b_end) * state_t + jnp.dot(
        v.T.astype(BF16), jnp.concatenate(k_end, axis=0).astype(BF16), preferred_element_type=F32)
    return o, new_state


def _hgrn_kernel(qf_ref, vf_ref, k1_ref, g1_ref, qb_ref, vb_ref, k2_ref, g2_ref,
                 ofw_ref, obw_ref, st_ref):
    @pl.when(pl.program_id(0) == 0)
    def _():
        st_ref[...] = jnp.zeros_like(st_ref)

    def head(hh, carry):
        o, s = _hgrn_chunk(qf_ref[0, hh].astype(F32), k1_ref[0, hh].astype(F32),
                           vf_ref[0, hh].astype(F32), g1_ref[0, hh], st_ref[0, hh], reverse=False)
        ofw_ref[hh] = o
        st_ref[0, hh] = s
        o, s = _hgrn_chunk(qb_ref[0, hh].astype(F32), k2_ref[0, hh].astype(F32),
                           vb_ref[0, hh].astype(F32), g2_ref[0, hh], st_ref[1, hh], reverse=True)
        obw_ref[hh] = o
        st_ref[1, hh] = s
        return carry

    lax.fori_loop(0, HG_HEADS, head, 0, unroll=4)


def _hgrn(plain, k, g):
    _, _, lp, dk = plain.shape
    c = HG_CHUNK
    nc = lp // c
    fwd = lambda seg: pl.BlockSpec((1, HG_HEADS, c, dk), lambda i: (seg, 0, i, 0))
    bwd = lambda seg: pl.BlockSpec((1, HG_HEADS, c, dk), lambda i: (seg, 0, nc - 1 - i, 0))
    out_f = pl.BlockSpec((HG_HEADS, c, dk), lambda i: (0, i, 0))
    out_b = pl.BlockSpec((HG_HEADS, c, dk), lambda i: (0, nc - 1 - i, 0))
    return pl.pallas_call(
        _hgrn_kernel,
        grid=(nc,),
        in_specs=[fwd(0), fwd(1), fwd(0), fwd(0), bwd(0), bwd(1), bwd(1), bwd(1)],
        out_specs=[out_f, out_b],
        out_shape=[jax.ShapeDtypeStruct((HG_HEADS, lp, dk), F32)] * 2,
        scratch_shapes=[pltpu.VMEM((2, HG_HEADS, dk, dk), F32)],
        compiler_params=_cparams("arbitrary"),
        name="hgrn2_recurrence",
    )(plain, plain, k, g, plain, plain, k, g)


def _route_params(w_group, b_group, w_expert, b_expert):
    d = w_group.shape[0]
    pad = ROUTE_LANES - N_GROUPS - N_EXPERTS
    w = jnp.concatenate([w_group, w_expert, jnp.zeros((d, pad), F32)], axis=1)
    b = jnp.concatenate([b_group, b_expert, jnp.zeros((pad,), F32)])[None, :]
    w_hi = w.astype(BF16)
    w_lo = (w - w_hi.astype(F32)).astype(BF16)
    return jnp.concatenate([w_hi, w_lo], axis=1), b


def _rope_tables(lp):
    half = HEAD_DIM // 2
    pos = jnp.maximum(jnp.arange(lp) - META_ROW0, 0).astype(F32)
    inv = ROPE_THETA ** (-jnp.arange(half, dtype=F32) * 2.0 / HEAD_DIM)
    ang = pos[:, None] * inv[None, :]
    cos, sin = jnp.cos(ang), jnp.sin(ang)
    return jnp.concatenate([cos, cos], axis=1), jnp.concatenate([-sin, sin], axis=1)


def kernel(x, meta_tokens, w_in_ab, w_out_ab, attn_sinks, s5_lam_re, s5_lam_im, s5_log_step, s5_b_re, s5_b_im, s5_c_re, s5_c_im, s5_d, s5_w_glu, s5_b_glu, w_in_c, w_out_c, hgrn_lb_logits, hgrn_norm_g, ln_mix_g, ln_mix_b, ln_ffn_g, ln_ffn_b, moe_w_group, moe_b_group, moe_w_expert, moe_b_expert, moe_w_gate_up, moe_w_down):
    batch, seq, d = x.shape
    assert batch == 1 and seq % FRONT == 0
    depth = ln_mix_g.shape[0]
    assert depth == 2
    alpha = (2.0 * depth) ** 0.25
    lp = FRONT + seq
    row2 = lambda t: t[None, :]

    front = jnp.concatenate([jnp.zeros((META_ROW0, d), F32), meta_tokens.astype(F32)], axis=0)
    xs = x[0]

    w_in = w_in_ab[0].astype(BF16)
    s5w = s5_d.shape[1]
    wq, wk, wv, wu = (w_in[:, :Q_DIM], w_in[:, Q_DIM:Q_DIM + KV_DIM],
                      w_in[:, Q_DIM + KV_DIM:Q_DIM + 2 * KV_DIM], w_in[:, Q_DIM + 2 * KV_DIM:])
    cos2, sin2 = _rope_tables(lp)
    q, k, v, u = _inproj_ab(front, xs, wq, wk, wv, wu, cos2, sin2)
    a_out = _window_attention(q, k, v, attn_sinks[0])
    mats = _s5_discretise(s5_lam_re[0], s5_lam_im[0], s5_log_step[0], s5_b_re[0], s5_b_im[0],
                          s5_c_re[0], s5_c_im[0])
    y_lo, y_hi = _s5_mixer_pre_glu(u, s5_d[0], mats)
    wo = w_out_ab[0].astype(BF16)
    w_route, b_route = _route_params(moe_w_group[0], moe_b_group[0], moe_w_expert[0], moe_b_expert[0])
    h, h_slabs, route = _mix_ab(front, xs, a_out, y_lo, y_hi, s5_w_glu[0].astype(BF16), row2(s5_b_glu[0]),
                       wo[:Q_DIM], wo[Q_DIM:], row2(ln_mix_g[0]), row2(ln_mix_b[0]), w_route, b_route,
                       alpha)
    plan = _moe_plan(route, META_ROW0)
    y_buf = _moe_experts(h_slabs, plan, moe_w_gate_up, moe_w_down, 0)
    h, hb = _moe_combine(h, plan, y_buf, row2(ln_ffn_g[0]), row2(ln_ffn_b[0]), alpha, 0, with_bf16=True)

    lb_probs = jax.nn.softmax(hgrn_lb_logits.astype(F32), axis=0)
    lb = (jnp.cumsum(lb_probs, axis=0) - lb_probs[0])[1]
    plain, hk, hg = _inproj_c(hb, w_in_c[0].astype(BF16), row2(lb))
    o_fw, o_bw = _hgrn(plain, hk, hg)
    w_route, b_route = _route_params(moe_w_group[1], moe_b_group[1], moe_w_expert[1], moe_b_expert[1])
    h, h_slabs, route = _mix_c(h, o_fw, o_bw, plain, hgrn_norm_g[0].reshape(HG_HEADS, 1, HG_DK),
                      w_out_c[0].astype(BF16), row2(ln_mix_g[1]), row2(ln_mix_b[1]), w_route, b_route,
                      alpha)
    plan = _moe_plan(route, META_ROW0)
    y_buf = _moe_experts(h_slabs, plan, moe_w_gate_up, moe_w_down, 1)
    out = _moe_combine(h, plan, y_buf, row2(ln_ffn_g[1]), row2(ln_ffn_b[1]), alpha, FRONT)
    return out[None]
```

```python
import functools
import math

import jax
import jax.numpy as jnp
from jax import lax
from jax.experimental import pallas as pl
from jax.experimental.pallas import tpu as pltpu

F32 = jnp.float32
BF16 = jnp.bfloat16

N_META = 16
FRONT = 512
META_ROW0 = FRONT - N_META

ATTN_HEADS = 8
ATTN_KV_HEADS = 2
ATTN_GROUP = ATTN_HEADS // ATTN_KV_HEADS
HEAD_DIM = 128
WINDOW = 128
ATTN_BLOCK = 128
ROPE_THETA = 10000.0
Q_DIM = ATTN_HEADS * HEAD_DIM
KV_DIM = ATTN_KV_HEADS * HEAD_DIM

S5_GROUP = 16
S5_STATE = 64
S5_CHUNK = 16
S5_ROW = S5_CHUNK * S5_GROUP
S5_SCAN_BLOCK = 32
S5_CHUNK_BLOCKS = (176, 96, 48, 32, 16)

HG_HEADS = 16
HG_DK = 128
HG_CHUNK = 128
HG_SUB = 32
HG_NSUB = HG_CHUNK // HG_SUB
HG_EXP_CLAMP = 80.0

N_GROUPS = 8
EXPERTS_PER_GROUP = 8
N_EXPERTS = N_GROUPS * EXPERTS_PER_GROUP
TOP_K = 2
D_EXPERT = 512
MOE_TM = 256
ROUTE_LANES = 128

LN_EPS = 1e-5
RMS_EPS = 1e-6
NEG_INF = -1e30

ROW_TILE = 512
MIX_TILE = 256
MIX_SUB = 128
INPROJ_C_TILES = (1536, 1024, 512)
INPROJ_C_SUB = 512

VMEM_LIMIT = 56 * 1024 * 1024


def _cparams(*sem):
    return pltpu.CompilerParams(dimension_semantics=sem, vmem_limit_bytes=VMEM_LIMIT)


def _pick(n, candidates):
    return next(c for c in candidates if n % c == 0)


def _full(shape):
    nd = len(shape)
    return pl.BlockSpec(shape, lambda *_: (0,) * nd)


def _layer_norm(x, g, b):
    mu = jnp.mean(x, axis=-1, keepdims=True)
    xc = x - mu
    var = jnp.mean(xc * xc, axis=-1, keepdims=True)
    return xc * lax.rsqrt(var + LN_EPS) * g + b


def _inproj_ab_kernel(front_ref, x_ref, wq_ref, wk_ref, wv_ref, wu_ref, cos_ref, sin_ref,
                      q_ref, k_ref, v_ref, u_ref):
    xb = jnp.where(pl.program_id(0) == 0, front_ref[...], x_ref[...]).astype(BF16)
    cos = cos_ref[...]
    sin = sin_ref[...]

    def rope(t):
        return t * cos + pltpu.roll(t, HEAD_DIM // 2, axis=1) * sin

    q = jnp.dot(xb, wq_ref[...], preferred_element_type=F32)
    for h in range(ATTN_HEADS):
        sl = slice(h * HEAD_DIM, (h + 1) * HEAD_DIM)
        q_ref[:, sl] = rope(q[:, sl]).astype(BF16)
    k = jnp.dot(xb, wk_ref[...], preferred_element_type=F32)
    for h in range(ATTN_KV_HEADS):
        sl = slice(h * HEAD_DIM, (h + 1) * HEAD_DIM)
        k_ref[:, sl] = rope(k[:, sl]).astype(BF16)
    v_ref[...] = jnp.dot(xb, wv_ref[...], preferred_element_type=F32).astype(BF16)
    u_ref[...] = jnp.dot(xb, wu_ref[...], preferred_element_type=F32)


def _inproj_ab(front, x, wq, wk, wv, wu, cos2, sin2):
    d = x.shape[1]
    lp = front.shape[0] + x.shape[0]
    tm = ROW_TILE
    assert front.shape[0] == tm
    s5w = wu.shape[1]
    row = lambda w: pl.BlockSpec((tm, w), lambda i: (i, 0))
    return pl.pallas_call(
        _inproj_ab_kernel,
        grid=(lp // tm,),
        in_specs=[_full(front.shape), pl.BlockSpec((tm, d), lambda i: (jnp.maximum(i - 1, 0), 0)),
                  _full(wq.shape), _full(wk.shape), _full(wv.shape), _full(wu.shape),
                  row(HEAD_DIM), row(HEAD_DIM)],
        out_specs=[row(Q_DIM), row(KV_DIM), row(KV_DIM), row(s5w)],
        out_shape=[jax.ShapeDtypeStruct((lp, Q_DIM), BF16),
                   jax.ShapeDtypeStruct((lp, KV_DIM), BF16),
                   jax.ShapeDtypeStruct((lp, KV_DIM), BF16),
                   jax.ShapeDtypeStruct((lp, s5w), F32)],
        compiler_params=_cparams("parallel"),
        name="inproj_ab",
    )(front, x, wq, wk, wv, wu, cos2, sin2)


def _attn_kernel(sink_ref, q_ref, k0_ref, k1_ref, k2_ref, v0_ref, v1_ref, v2_ref, km_ref, vm_ref,
                 o_ref, *, lp):
    qb = pl.program_id(0)
    blk = ATTN_BLOCK
    rows = ATTN_GROUP * blk
    scale = HEAD_DIM ** -0.5
    q_row = qb * blk + lax.broadcasted_iota(jnp.int32, (rows, 3 * blk), 0) % blk
    k_row = (qb - 1) * blk + lax.broadcasted_iota(jnp.int32, (rows, 3 * blk), 1)
    vis = (k_row >= FRONT) & (k_row < lp) & (jnp.abs(q_row - k_row) <= WINDOW)
    head_of_row = lax.broadcasted_iota(jnp.int32, (rows, 1), 0) // blk
    for g in range(ATTN_KV_HEADS):
        gs = slice(g * HEAD_DIM, (g + 1) * HEAD_DIM)
        qs = jnp.concatenate(
            [q_ref[:, (g * ATTN_GROUP + r) * HEAD_DIM:(g * ATTN_GROUP + r + 1) * HEAD_DIM]
             for r in range(ATTN_GROUP)], axis=0)
        kband = jnp.concatenate([k0_ref[:, gs], k1_ref[:, gs], k2_ref[:, gs]], axis=0)
        vband = jnp.concatenate([v0_ref[:, gs], v1_ref[:, gs], v2_ref[:, gs]], axis=0)
        nt = (((1,), (1,)), ((), ()))
        s_band = lax.dot_general(qs, kband, nt, preferred_element_type=F32) * scale
        s_band = jnp.where(vis, s_band, NEG_INF)
        s_meta = lax.dot_general(qs, km_ref[:, gs], nt, preferred_element_type=F32) * scale
        sink = jnp.zeros((rows, 1), F32)
        for r in range(ATTN_GROUP):
            sink = jnp.where(head_of_row == r, sink_ref[g * ATTN_GROUP + r], sink)
        m = jnp.maximum(jnp.maximum(jnp.max(s_band, axis=-1, keepdims=True),
                                    jnp.max(s_meta, axis=-1, keepdims=True)), sink)
        e_band = jnp.exp(s_band - m)
        e_meta = jnp.exp(s_meta - m)
        denom = (jnp.sum(e_band, axis=-1, keepdims=True) + jnp.sum(e_meta, axis=-1, keepdims=True)
                 + jnp.exp(sink - m))
        o = (jnp.dot(e_band.astype(BF16), vband, preferred_element_type=F32)
             + jnp.dot(e_meta.astype(BF16), vm_ref[:, gs], preferred_element_type=F32)) / denom
        for r in range(ATTN_GROUP):
            hh = g * ATTN_GROUP + r
            o_ref[:, hh * HEAD_DIM:(hh + 1) * HEAD_DIM] = o[r * blk:(r + 1) * blk].astype(BF16)


def _window_attention(q, k, v, sinks):
    lp = q.shape[0]
    blk = ATTN_BLOCK
    nb = lp // blk
    qspec = pl.BlockSpec((blk, Q_DIM), lambda i, s: (i, 0))
    kv = lambda off: pl.BlockSpec((blk, KV_DIM), lambda i, s: (jnp.clip(i + off, 0, nb - 1), 0))
    meta = pl.BlockSpec((N_META, KV_DIM), lambda i, s: (META_ROW0 // N_META, 0))
    return pl.pallas_call(
        functools.partial(_attn_kernel, lp=lp),
        grid_spec=pltpu.PrefetchScalarGridSpec(
            num_scalar_prefetch=1,
            grid=(nb,),
            in_specs=[qspec, kv(-1), kv(0), kv(1), kv(-1), kv(0), kv(1), meta, meta],
            out_specs=qspec,
        ),
        out_shape=jax.ShapeDtypeStruct((lp, Q_DIM), BF16),
        compiler_params=_cparams("parallel"),
        name="window_attention",
    )(sinks, q, k, k, k, v, v, v, k, v)


def _s5_discretise(lam_re, lam_im, log_step, b_re, b_im, c_re, c_im):
    t = S5_CHUNK
    hi = lax.Precision.HIGHEST
    g = lam_re.shape[1]
    lr = jnp.minimum(lam_re, -1e-4)
    li = lam_im
    step = jnp.exp(log_step)[..., None]
    dr, di = lr * step, li * step
    lags = jnp.arange(t + 1, dtype=F32)[:, None]
    mag = jnp.exp(dr[..., None, :] * lags)
    pr, pi = mag * jnp.cos(di[..., None, :] * lags), mag * jnp.sin(di[..., None, :] * lags)
    ar, ai = pr[:, :, 1], pi[:, :, 1]
    den = lr * lr + li * li
    zr = ((ar - 1.0) * lr + ai * li) / den
    zi = (ai * lr - (ar - 1.0) * li) / den
    br = zr[..., None] * b_re - zi[..., None] * b_im
    bi = zr[..., None] * b_im + zi[..., None] * b_re
    brt, bit = br.transpose(0, 1, 3, 2), bi.transpose(0, 1, 3, 2)
    ct_r, ct_i = c_re.transpose(0, 1, 3, 2), c_im.transpose(0, 1, 3, 2)
    pt_r, pt_i = pr[:, :, :t].transpose(0, 1, 3, 2), pi[:, :, :t].transpose(0, 1, 3, 2)
    xr = (ct_r[:, :, :, None] * pt_r[..., None] - ct_i[:, :, :, None] * pt_i[..., None])
    xi = (ct_r[:, :, :, None] * pt_i[..., None] + ct_i[:, :, :, None] * pt_r[..., None])
    x_cat = jnp.concatenate([xr, xi], axis=2).reshape(2, g, 2 * S5_STATE, t * S5_GROUP)
    b_cat = jnp.concatenate([brt, -bit], axis=3)
    kern = jnp.einsum('dgip,dgpn->dgin', b_cat, x_cat, precision=hi).reshape(2, g, S5_GROUP, t, S5_GROUP)
    k_lag = jnp.concatenate([jnp.flip(kern[1][:, :, 1:], axis=2), kern[0][:, :, :1] + kern[1][:, :, :1],
                             kern[0][:, :, 1:]], axis=2)
    k_lag = k_lag.reshape(g, S5_GROUP, (2 * t - 1) * S5_GROUP)

    def flat_e(p_r, p_i, d):
        e_r = p_r[:, :, None] * brt[d][:, None] - p_i[:, :, None] * bit[d][:, None]
        e_i = p_r[:, :, None] * bit[d][:, None] + p_i[:, :, None] * brt[d][:, None]
        flat = lambda e: e.reshape(g, t * S5_GROUP, S5_STATE)
        return jnp.concatenate([flat(e_r), flat(e_i)], axis=-1)
    e_mat = jnp.concatenate([flat_e(pr[0][:, ::-1][:, 1:], pi[0][:, ::-1][:, 1:], 0),
                             flat_e(pr[1][:, :t], pi[1][:, :t], 1)], axis=-1)

    def flat_f(p_r, p_i, d):
        ct_r, ct_i = c_re[d].transpose(0, 2, 1), c_im[d].transpose(0, 2, 1)
        pt_r, pt_i = p_r.transpose(0, 2, 1)[..., None], p_i.transpose(0, 2, 1)[..., None]
        w_r = ct_r[:, :, None] * pt_r - ct_i[:, :, None] * pt_i
        w_i = ct_r[:, :, None] * pt_i + ct_i[:, :, None] * pt_r
        flat = lambda w: w.reshape(g, S5_STATE, t * S5_GROUP)
        return jnp.concatenate([flat(w_r), -flat(w_i)], axis=1)
    f_fw = flat_f(pr[0][:, 1:], pi[0][:, 1:], 0)
    f_bw = flat_f(pr[1][:, ::-1][:, :t], pi[1][:, ::-1][:, :t], 1)
    gh = g // 2
    upper = (jnp.arange(g) >= gh)[:, None, None]
    def place(m):
        z = jnp.zeros_like(m)
        return jnp.concatenate([jnp.where(upper, z, m), jnp.where(upper, m, z)], axis=1)
    p = S5_STATE
    f_mat = jnp.concatenate([place(f_fw[:, :p]), place(f_fw[:, p:]), place(f_bw[:, :p]), place(f_bw[:, p:])],
                            axis=1)
    pack = lambda v: jnp.concatenate([v[:gh], v[gh:]], axis=-1)
    coef = jnp.stack([pack(pr[0][:, t]), pack(pi[0][:, t]), pack(pr[1][:, t]), pack(pi[1][:, t])])
    split = lambda m: m.reshape((2, gh) + m.shape[1:])
    return split(k_lag), split(e_mat.astype(BF16)), split(f_mat.astype(BF16)), coef


def _chunk_rows_to_lanes(u_ref, ncb):
    lane_seg = lax.broadcasted_iota(jnp.int32, (ncb, 128), 1) // S5_GROUP
    rows = [u_ref[pl.ds(s, ncb, stride=S5_CHUNK), :] for s in range(S5_CHUNK)]
    segs = 128 // S5_GROUP
    out = []
    for j in range(segs):
        halves = []
        for h in range(S5_CHUNK // segs):
            acc = None
            for s8 in range(segs):
                r = rows[h * segs + s8]
                k = (s8 - j) % segs
                if k:
                    r = pltpu.roll(r, S5_GROUP * k, axis=1)
                acc = r if acc is None else jnp.where(lane_seg == s8, r, acc)
            halves.append(acc)
        out.append(jnp.concatenate(halves, axis=1))
    return out


def _lanes_to_chunk_rows(ys, y_ref, ncb, add_ref, scale_ref):
    lane_seg = lax.broadcasted_iota(jnp.int32, (ncb, 128), 1) // S5_GROUP
    segs = 128 // S5_GROUP
    for t in range(S5_CHUNK):
        h, t8 = divmod(t, segs)
        acc = None
        for j in range(segs):
            r = ys[j][:, h * 128:(h + 1) * 128]
            k = (j - t8) % segs
            if k:
                r = pltpu.roll(r, S5_GROUP * k, axis=1)
            acc = r if acc is None else jnp.where(lane_seg == j, r, acc)
        idx = pl.ds(t, ncb, stride=S5_CHUNK)
        y_ref[idx, :] = acc + add_ref[idx, :] * scale_ref[...]


def _pair_halves(a, b):
    lane = lax.broadcasted_iota(jnp.int32, a.shape, 1)
    lo = jnp.where(lane < S5_STATE, a, pltpu.roll(b, S5_STATE, axis=1))
    hi = jnp.where(lane < S5_STATE, pltpu.roll(a, S5_STATE, axis=1), b)
    return lo, hi


def _s5_local_kernel(ua_ref, ub_ref, e_ref, u2_ref, f1_ref, f2_ref, b1_ref, b2_ref):
    ncb = u2_ref.shape[2]
    p2 = 2 * S5_STATE
    chunks = [_chunk_rows_to_lanes(ua_ref, ncb), _chunk_rows_to_lanes(ub_ref, ncb)]
    for j in range(u2_ref.shape[1]):
        st = []
        for hh in range(2):
            u2 = chunks[hh][j].astype(BF16)
            u2_ref[hh, j] = u2
            st.append(jnp.dot(u2, e_ref[hh, j], preferred_element_type=F32))
        f1_ref[:, j, :], f2_ref[:, j, :] = _pair_halves(st[0][:, :p2], st[1][:, :p2])
        b1_ref[:, j, :], b2_ref[:, j, :] = _pair_halves(st[0][:, p2:], st[1][:, p2:])


def _s5_scan_kernel(sf1_ref, sf2_ref, sb1_ref, sb2_ref, coef_ref, xf1_ref, xf2_ref, xb1_ref, xb2_ref,
                    st_ref):
    @pl.when(pl.program_id(0) == 0)
    def _():
        st_ref[...] = jnp.zeros_like(st_ref)

    nb = sf1_ref.shape[0]
    crf, cif, crb, cib = coef_ref[0], coef_ref[1], coef_ref[2], coef_ref[3]

    def body(i, carry):
        f1, f2, b1, b2 = carry
        j = nb - 1 - i
        xf1_ref[i] = f1
        xf2_ref[i] = f2
        xb1_ref[j] = b1
        xb2_ref[j] = b2
        return (crf * f1 - cif * f2 + sf1_ref[i], crf * f2 + cif * f1 + sf2_ref[i],
                crb * b1 - cib * b2 + sb1_ref[j], crb * b2 + cib * b1 + sb2_ref[j])

    out = lax.fori_loop(0, nb, body, (st_ref[0], st_ref[1], st_ref[2], st_ref[3]))
    for q in range(4):
        st_ref[q] = out[q]


def _chunk_matrix(k_lag):
    t, c = S5_CHUNK, S5_GROUP
    return jnp.concatenate([k_lag[:, (t - 1 - s) * c:(2 * t - 1 - s) * c] for s in range(t)],
                           axis=0).astype(BF16)


def _s5_readout_kernel(ua_ref, ub_ref, d_ref, u2_ref, kl_ref, f_ref, xf1_ref, xf2_ref, xb1_ref, xb2_ref,
                       ya_ref, yb_ref):
    ncb = u2_ref.shape[2]
    ys = [[], []]
    for j in range(u2_ref.shape[1]):
        x = jnp.concatenate([xf1_ref[:, j, :], xf2_ref[:, j, :], xb1_ref[:, j, :], xb2_ref[:, j, :]],
                            axis=1).astype(BF16)
        for hh in range(2):
            y = jnp.dot(u2_ref[hh, j], _chunk_matrix(kl_ref[hh, j]), preferred_element_type=F32)
            ys[hh].append(y + jnp.dot(x, f_ref[hh, j], preferred_element_type=F32))
    _lanes_to_chunk_rows(ys[0], ya_ref, ncb, ua_ref, d_ref.at[0])
    _lanes_to_chunk_rows(ys[1], yb_ref, ncb, ub_ref, d_ref.at[1])


def _s5_mixer_pre_glu(u, d_skip, mats):
    k_lag, e_mat, f_mat, coef = mats
    lp, w = u.shape
    g = w // S5_GROUP
    gh = g // 2
    gpb = 128 // S5_GROUP
    nc = lp // S5_CHUNK
    ncb = _pick(nc, S5_CHUNK_BLOCKS)
    rows = ncb * S5_CHUNK
    nlb = w // 128
    grid = (nlb // 2, nc // ncb)
    u_lo = pl.BlockSpec((rows, 128), lambda b, r: (r, b))
    u_hi = pl.BlockSpec((rows, 128), lambda b, r: (r, b + nlb // 2))
    per_group = lambda k, n: pl.BlockSpec((2, gpb, k, n), lambda b, r: (0, b, 0, 0))
    u2_spec = pl.BlockSpec((2, gpb, ncb, S5_ROW), lambda b, r: (0, b, r, 0))
    st_spec = pl.BlockSpec((ncb, gpb, 2 * S5_STATE), lambda b, r: (r, b, 0))
    st_shape = jax.ShapeDtypeStruct((nc, gh, 2 * S5_STATE), F32)
    u2, sf1, sf2, sb1, sb2 = pl.pallas_call(
        _s5_local_kernel,
        grid=grid,
        in_specs=[u_lo, u_hi, per_group(S5_ROW, 4 * S5_STATE)],
        out_specs=[u2_spec, st_spec, st_spec, st_spec, st_spec],
        out_shape=[jax.ShapeDtypeStruct((2, gh, nc, S5_ROW), BF16)] + [st_shape] * 4,
        compiler_params=_cparams("parallel", "parallel"),
        name="s5_local_states",
    )(u, u, e_mat)
    sb = S5_SCAN_BLOCK
    nblk = nc // sb
    fwd = pl.BlockSpec((sb, gh, 2 * S5_STATE), lambda i: (i, 0, 0))
    bwd = pl.BlockSpec((sb, gh, 2 * S5_STATE), lambda i: (nblk - 1 - i, 0, 0))
    xf1, xf2, xb1, xb2 = pl.pallas_call(
        _s5_scan_kernel,
        grid=(nblk,),
        in_specs=[fwd, fwd, bwd, bwd, _full(coef.shape)],
        out_specs=[fwd, fwd, bwd, bwd],
        out_shape=[st_shape] * 4,
        scratch_shapes=[pltpu.VMEM((4, gh, 2 * S5_STATE), F32)],
        compiler_params=_cparams("arbitrary"),
        name="s5_chunk_scan",
    )(sf1, sf2, sb1, sb2, coef)
    d2 = d_skip.reshape(2, 1, nlb // 2 * 128)
    y_lo, y_hi = pl.pallas_call(
        _s5_readout_kernel,
        grid=grid,
        in_specs=[u_lo, u_hi, pl.BlockSpec((2, 1, 128), lambda b, r: (0, 0, b)), u2_spec,
                  per_group(S5_GROUP, k_lag.shape[-1]), per_group(8 * S5_STATE, S5_ROW),
                  st_spec, st_spec, st_spec, st_spec],
        out_specs=[pl.BlockSpec((rows, 128), lambda b, r: (r, b))] * 2,
        out_shape=[jax.ShapeDtypeStruct((lp, w // 2), F32)] * 2,
        compiler_params=_cparams("parallel", "parallel"),
        name="s5_readout",
    )(u, u, d2, u2, k_lag, f_mat, xf1, xf2, xb1, xb2)
    return y_lo, y_hi


def _route(hn, wr_ref, br_ref):
    hn_hi = hn.astype(BF16)
    hn_lo = (hn - hn_hi.astype(F32)).astype(BF16)
    p_hi = jnp.dot(hn_hi, wr_ref[...], preferred_element_type=F32)
    p_lo = jnp.dot(hn_lo, wr_ref[:, :ROUTE_LANES], preferred_element_type=F32)
    logits = p_hi[:, :ROUTE_LANES] + (p_hi[:, ROUTE_LANES:] + p_lo) + br_ref[...]
    lane_i = lax.broadcasted_iota(jnp.int32, logits.shape, 1)
    lane = lane_i.astype(F32)
    lane_grp = ((lane_i - N_GROUPS) // EXPERTS_PER_GROUP).astype(F32)
    big = float(ROUTE_LANES)
    g_log = jnp.where(lane_i < N_GROUPS, logits, -jnp.inf)
    g_max = jnp.max(g_log, axis=-1, keepdims=True)
    grp = jnp.min(jnp.where(g_log == g_max, lane, big), axis=-1, keepdims=True)
    p_grp = 1.0 / jnp.sum(jnp.exp(g_log - g_max), axis=-1, keepdims=True)
    in_grp = (lane_i >= N_GROUPS) & (lane_i < N_GROUPS + N_EXPERTS) & (lane_grp == grp)
    e_log = jnp.where(in_grp, logits, -jnp.inf)
    v1 = jnp.max(e_log, axis=-1, keepdims=True)
    i1 = jnp.min(jnp.where(e_log == v1, lane, big), axis=-1, keepdims=True)
    e_log2 = jnp.where(lane == i1, -jnp.inf, e_log)
    v2 = jnp.max(e_log2, axis=-1, keepdims=True)
    i2 = jnp.min(jnp.where(e_log2 == v2, lane, big), axis=-1, keepdims=True)
    e21 = jnp.exp(v2 - v1)
    w1 = p_grp / (1.0 + e21)
    w2 = p_grp * e21 / (1.0 + e21)
    route = jnp.where(lane_i == 0, w1, 0.0)
    route = jnp.where(lane_i == 1, w2, route)
    route = jnp.where(lane_i == 2, i1 - N_GROUPS, route)
    route = jnp.where(lane_i == 3, i2 - N_GROUPS, route)
    return route


U32 = jnp.uint32


def _slab_rows(d):
    return d // 256


def _slab_spec(tm, d):
    return pl.BlockSpec((tm * _slab_rows(d), 128), lambda i: (i, 0))


def _slab_shape(rows, d):
    return jax.ShapeDtypeStruct((rows * _slab_rows(d), 128), U32)


def _pack_pair(a, b):
    ua = lax.bitcast_convert_type(a.astype(BF16).astype(F32), U32)
    ub = lax.bitcast_convert_type(b.astype(BF16).astype(F32), U32)
    return ua | (ub >> 16)


def _unpack_pair(u):
    a = lax.bitcast_convert_type(u & jnp.uint32(0xFFFF0000), F32)
    b = lax.bitcast_convert_type(u << 16, F32)
    return a, b


def _store_row_slabs(ref, start, n, val):
    ns = _slab_rows(val.shape[1])
    for k in range(ns):
        word = _pack_pair(val[:, 2 * k * 128:(2 * k + 1) * 128], val[:, (2 * k + 1) * 128:(2 * k + 2) * 128])
        ref[pl.ds(start * ns + k, n, stride=ns), :] = word


def _load_row_slabs(ref_at, n, ns):
    cols = []
    for k in range(ns):
        cols.extend(_unpack_pair(ref_at(pl.ds(k, n, stride=ns))))
    return cols


def _sub_rows(tm):
    return [slice(r, r + MIX_SUB) for r in range(0, tm, MIX_SUB)]


def _mix_ab_kernel(front_ref, x_ref, a_ref, ylo_ref, yhi_ref, wglu_ref, bglu_ref, woa_ref, wos_ref,
                   g_ref, b_ref, wr_ref, br_ref, hn_ref, hs_ref, route_ref, *, alpha):
    in_front = pl.program_id(0) < FRONT // MIX_TILE
    for rows in _sub_rows(x_ref.shape[0]):
        y = jnp.concatenate([ylo_ref[rows, :], yhi_ref[rows, :]], axis=1)
        z = 0.5 * y * (1.0 + jnp.tanh(math.sqrt(2.0 / math.pi) * (y + 0.044715 * (y * y * y))))
        gate = jnp.dot(z.astype(BF16), wglu_ref[...], preferred_element_type=F32) + bglu_ref[...]
        s_out = z * jax.nn.sigmoid(gate)
        mix = (jnp.dot(a_ref[rows, :], woa_ref[...], preferred_element_type=F32)
               + jnp.dot(s_out.astype(BF16), wos_ref[...], preferred_element_type=F32))
        h_in = jnp.where(in_front, front_ref[rows, :], x_ref[rows, :])
        hn = _layer_norm(alpha * h_in + mix, g_ref[...], b_ref[...])
        hn_ref[rows, :] = hn
        _store_row_slabs(hs_ref, rows.start, rows.stop - rows.start, hn)
        route_ref[rows, :] = _route(hn, wr_ref, br_ref)


def _mix_ab(front, x, a_out, y_lo, y_hi, w_glu, b_glu, wo_a, wo_s, ln_g, ln_b, w_route, b_route, alpha):
    d = x.shape[1]
    lp = front.shape[0] + x.shape[0]
    tm = MIX_TILE
    nf = front.shape[0] // tm
    row = lambda c: pl.BlockSpec((tm, c), lambda i: (i, 0))
    return pl.pallas_call(
        functools.partial(_mix_ab_kernel, alpha=alpha),
        grid=(lp // tm,),
        in_specs=[pl.BlockSpec((tm, d), lambda i: (jnp.minimum(i, nf - 1), 0)),
                  pl.BlockSpec((tm, d), lambda i: (jnp.maximum(i - nf, 0), 0)), row(Q_DIM), row(y_lo.shape[1]), row(y_hi.shape[1]), _full(w_glu.shape),
                  _full(b_glu.shape), _full(wo_a.shape), _full(wo_s.shape), _full(ln_g.shape),
                  _full(ln_b.shape), _full(w_route.shape), _full(b_route.shape)],
        out_specs=[row(d), _slab_spec(tm, d), row(ROUTE_LANES)],
        out_shape=[jax.ShapeDtypeStruct((lp, d), F32), _slab_shape(lp, d),
                   jax.ShapeDtypeStruct((lp, ROUTE_LANES), F32)],
        compiler_params=_cparams("parallel"),
        name="mix_ab",
    )(front, x, a_out, y_lo, y_hi, w_glu, b_glu, wo_a, wo_s, ln_g, ln_b, w_route, b_route)


def _mix_c_kernel(h_ref, ofw_ref, obw_ref, gate_ref, ng_ref, wo_ref, g_ref, b_ref, wr_ref, br_ref,
                  hn_ref, hs_ref, route_ref, *, alpha):
    for rows in _sub_rows(h_ref.shape[0]):
        parts = []
        for hh in range(HG_HEADS):
            o = ofw_ref[hh, rows, :] + obw_ref[hh, rows, :]
            o = o * lax.rsqrt(jnp.mean(o * o, axis=-1, keepdims=True) + RMS_EPS)
            o = o * ng_ref[hh] * jax.nn.sigmoid(gate_ref[0, hh, rows, :].astype(F32))
            parts.append(o.astype(BF16))
        mix = jnp.dot(jnp.concatenate(parts, axis=1), wo_ref[...], preferred_element_type=F32)
        hn = _layer_norm(alpha * h_ref[rows, :] + mix, g_ref[...], b_ref[...])
        hn_ref[rows, :] = hn
        _store_row_slabs(hs_ref, rows.start, rows.stop - rows.start, hn)
        route_ref[rows, :] = _route(hn, wr_ref, br_ref)


def _mix_c(h, o_fw, o_bw, plain, norm_g, wo, ln_g, ln_b, w_route, b_route, alpha):
    lp, d = h.shape
    tm = MIX_TILE
    row = lambda c: pl.BlockSpec((tm, c), lambda i: (i, 0))
    hm = pl.BlockSpec((HG_HEADS, tm, HG_DK), lambda i: (0, i, 0))
    gate = pl.BlockSpec((1, HG_HEADS, tm, HG_DK), lambda i: (2, 0, i, 0))
    return pl.pallas_call(
        functools.partial(_mix_c_kernel, alpha=alpha),
        grid=(lp // tm,),
        in_specs=[row(d), hm, hm, gate, _full(norm_g.shape), _full(wo.shape), _full(ln_g.shape),
                  _full(ln_b.shape), _full(w_route.shape), _full(b_route.shape)],
        out_specs=[row(d), _slab_spec(tm, d), row(ROUTE_LANES)],
        out_shape=[jax.ShapeDtypeStruct((lp, d), F32), _slab_shape(lp, d),
                   jax.ShapeDtypeStruct((lp, ROUTE_LANES), F32)],
        compiler_params=_cparams("parallel"),
        name="mix_c",
    )(h, o_fw, o_bw, plain, norm_g, wo, ln_g, ln_b, w_route, b_route)


def _moe_plan(route, n_rows_valid_from):
    lp = route.shape[0]
    tm = MOE_TM
    valid = (jnp.arange(lp) >= n_rows_valid_from)
    expert = route[:, 2:4].astype(jnp.int32)
    flat_e = jnp.where(valid[:, None], expert, N_EXPERTS).reshape(-1)
    onehot = (flat_e[:, None] == jnp.arange(N_EXPERTS)[None, :]).astype(jnp.int32)
    csum = jnp.cumsum(onehot, axis=0)
    counts = csum[-1]
    rank = jnp.sum(jnp.where(onehot > 0, csum - 1, 0), axis=1)
    padded = (counts + tm - 1) // tm * tm
    pend = jnp.cumsum(padded)
    pstart = pend - padded
    e_safe = jnp.minimum(flat_e, N_EXPERTS - 1)
    dest = jnp.where(flat_e < N_EXPERTS, pstart[e_safe] + rank, 0)
    n_tokens = lp - n_rows_valid_from
    n_blocks = -(-(n_tokens * TOP_K + N_EXPERTS * (tm - 1)) // tm)
    rows = n_blocks * tm
    tok = jnp.repeat(jnp.arange(lp, dtype=jnp.int32), TOP_K)
    scatter_to = jnp.where(flat_e < N_EXPERTS, dest, rows)
    tok_of_row = jnp.zeros((rows,), jnp.int32).at[scatter_to].set(tok, mode='drop')
    n_used = (pend[-1] // tm).astype(jnp.int32)
    blk = jnp.minimum(jnp.arange(n_blocks, dtype=jnp.int32), jnp.maximum(n_used - 1, 0))
    block_expert = jnp.minimum(jnp.searchsorted(pend, blk * tm, side='right'),
                               N_EXPERTS - 1).astype(jnp.int32)
    used = counts > 0
    pos = jnp.cumsum(used.astype(jnp.int32)) - 1
    eid = jnp.arange(N_EXPERTS, dtype=jnp.int32)
    later = jnp.where(used[None, :] & (eid[None, :] > eid[:, None]), eid[None, :], N_EXPERTS)
    nxt_e = jnp.min(later, axis=1)
    nxt_e = jnp.where(nxt_e == N_EXPERTS, -1, nxt_e).astype(jnp.int32)
    w_slot = (pos[block_expert] % 2).astype(jnp.int32)
    nxt = nxt_e[block_expert]
    weights = jnp.where(valid[:, None], route[:, 0:2], 0.0)
    return (block_expert, n_used.reshape(1), w_slot, nxt, tok_of_row.reshape(n_blocks, 1, tm),
            dest.reshape(lp, TOP_K).astype(jnp.int32), weights)


def _moe_expert_kernel(be_ref, nu_ref, ws_ref, nx_ref, tok0_ref, tokn_ref, h_hbm, wgu_hbm, wd_hbm,
                       y_ref, xbuf, xsem, wgu_f, wd_f, wsem, wgu_bf, wd_bf, *, layer):
    i = pl.program_id(0)
    nl = _slab_rows(wgu_f.shape[1])
    tm = xbuf.shape[1] // nl
    n_used = nu_ref[0]

    def gather(tok_ref, slot):
        def issue(r, carry):
            src = h_hbm.at[pl.ds(pl.multiple_of(tok_ref[0, 0, r], nl), nl)]
            pltpu.make_async_copy(src, xbuf.at[slot, pl.ds(pl.multiple_of(r * nl, nl), nl)],
                                  xsem.at[slot]).start()
            return carry
        lax.fori_loop(0, tm, issue, 0, unroll=8)

    def gather_wait(slot):
        pltpu.make_async_copy(h_hbm.at[pl.ds(0, tm * nl)], xbuf.at[slot], xsem.at[slot]).wait()

    def weight_copies(e, s):
        return (pltpu.make_async_copy(wgu_hbm.at[layer, e], wgu_f.at[s], wsem.at[0, s]),
                pltpu.make_async_copy(wd_hbm.at[layer, e], wd_f.at[s], wsem.at[1, s]))

    @pl.when(i == 0)
    def _():
        for cp in weight_copies(be_ref[0], 0):
            cp.start()
        gather(tok0_ref, 0)

    @pl.when(i < n_used)
    def _():
        slot = i % 2
        e = be_ref[i]
        first = jnp.logical_or(i == 0, be_ref[jnp.maximum(i - 1, 0)] != e)

        gather_wait(slot)
        gather(tokn_ref, 1 - slot)

        @pl.when(first)
        def _():
            s = ws_ref[i]
            for cp in weight_copies(e, s):
                cp.wait()
            nxt = nx_ref[i]

            @pl.when(nxt >= 0)
            def _():
                for cp in weight_copies(nxt, 1 - s):
                    cp.start(priority=1)

            wgu_bf[...] = wgu_f[s].astype(BF16)
            wd_bf[...] = wd_f[s].astype(BF16)

        cols = _load_row_slabs(lambda idx: xbuf[slot, idx, :], tm, nl)
        x = jnp.concatenate([c.astype(BF16) for c in cols], axis=1)
        hgu = jnp.dot(x, wgu_bf[...], preferred_element_type=F32)
        act = jax.nn.silu(hgu[:, :D_EXPERT]) * hgu[:, D_EXPERT:]
        y = jnp.dot(act.astype(BF16), wd_bf[...], preferred_element_type=F32)
        _store_row_slabs(y_ref, 0, tm, y)

        @pl.when(i == n_used - 1)
        def _():
            gather_wait(1 - slot)

    @pl.when(i >= n_used)
    def _():
        y_ref[...] = jnp.zeros_like(y_ref)


def _moe_experts(h_slabs, plan, w_gate_up, w_down, layer):
    block_expert, n_used, w_slot, nxt, tok_of_row, _, _ = plan
    d = w_gate_up.shape[2]
    ns = _slab_rows(d)
    tok_of_row = tok_of_row * ns
    tm = MOE_TM
    n_blocks = tok_of_row.shape[0]
    nxt_blk = lambda i, be, nu, ws, nx: (jnp.minimum(i + 1, jnp.maximum(nu[0] - 1, 0)), 0, 0)
    smem = lambda imap: pl.BlockSpec((1, 1, tm), imap, memory_space=pltpu.SMEM)
    anyspace = pl.BlockSpec(memory_space=pl.ANY)
    return pl.pallas_call(
        functools.partial(_moe_expert_kernel, layer=layer),
        grid_spec=pltpu.PrefetchScalarGridSpec(
            num_scalar_prefetch=4,
            grid=(n_blocks,),
            in_specs=[smem(lambda i, be, nu, ws, nx: (0, 0, 0)), smem(nxt_blk),
                      anyspace, anyspace, anyspace],
            out_specs=pl.BlockSpec((tm * ns, 128), lambda i, be, nu, ws, nx: (i, 0)),
            scratch_shapes=[pltpu.VMEM((2, tm * ns, 128), U32), pltpu.SemaphoreType.DMA((2,)),
                            pltpu.VMEM((2, d, 2 * D_EXPERT), F32), pltpu.VMEM((2, D_EXPERT, d), F32),
                            pltpu.SemaphoreType.DMA((2, 2)),
                            pltpu.VMEM((d, 2 * D_EXPERT), BF16), pltpu.VMEM((D_EXPERT, d), BF16)],
        ),
        out_shape=jax.ShapeDtypeStruct((n_blocks * tm * ns, 128), U32),
        compiler_params=_cparams("arbitrary"),
        name="moe_experts",
    )(block_expert, n_used, w_slot, nxt, tok_of_row, tok_of_row, h_slabs, w_gate_up, w_down)


def _moe_combine_kernel(d0_ref, d1_ref, d0n_ref, d1n_ref, h_ref, w_ref, y_hbm, g_ref, b_ref, *rest,
                        alpha, with_bf16):
    if with_bf16:
        o_ref, ob_ref, buf, sem = rest
    else:
        o_ref, buf, sem = rest
    i = pl.program_id(0)
    tm, d = h_ref.shape
    nl = _slab_rows(d)
    slot = i % 2

    def gather(a_ref, b_ref_, s):
        def issue(r, carry):
            for k, idx_ref in enumerate((a_ref, b_ref_)):
                src = y_hbm.at[pl.ds(pl.multiple_of(idx_ref[0, 0, r], nl), nl)]
                pltpu.make_async_copy(src, buf.at[s, k, pl.ds(pl.multiple_of(r * nl, nl), nl)],
                                      sem.at[s, k]).start()
            return carry
        lax.fori_loop(0, tm, issue, 0, unroll=8)

    @pl.when(i == 0)
    def _():
        gather(d0_ref, d1_ref, 0)

    for k in range(TOP_K):
        pltpu.make_async_copy(y_hbm.at[pl.ds(0, tm * nl)], buf.at[slot, k], sem.at[slot, k]).wait()

    @pl.when(i + 1 < pl.num_programs(0))
    def _():
        gather(d0n_ref, d1n_ref, 1 - slot)

    w = w_ref[...]
    y0 = _load_row_slabs(lambda idx: buf[slot, 0, idx, :], tm, nl)
    y1 = _load_row_slabs(lambda idx: buf[slot, 1, idx, :], tm, nl)
    cols = [alpha * h_ref[:, c * 128:(c + 1) * 128] + (w[:, 0:1] * y0[c] + w[:, 1:2] * y1[c])
            for c in range(d // 128)]
    out = _layer_norm(jnp.concatenate(cols, axis=1), g_ref[...], b_ref[...])
    o_ref[...] = out
    if with_bf16:
        ob_ref[...] = out.astype(BF16)


def _moe_combine(h, plan, y_buf, ln_g, ln_b, alpha, first_row, with_bf16=False):
    dest, weights = plan[-2], plan[-1]
    lp, d = h.shape
    tm = MIX_TILE
    off = first_row // tm
    nt = lp // tm
    ns = _slab_rows(d)
    d0 = (dest[:, 0] * ns).reshape(nt, 1, tm)
    d1 = (dest[:, 1] * ns).reshape(nt, 1, tm)
    cur = pl.BlockSpec((1, 1, tm), lambda i: (off, 0, 0), memory_space=pltpu.SMEM)
    nxt = pl.BlockSpec((1, 1, tm), lambda i: (jnp.minimum(i + 1 + off, nt - 1), 0, 0),
                       memory_space=pltpu.SMEM)
    out_spec = pl.BlockSpec((tm, d), lambda i: (i, 0))
    out_shape = jax.ShapeDtypeStruct((lp - first_row, d), F32)
    if with_bf16:
        out_spec = [out_spec, out_spec]
        out_shape = [out_shape, jax.ShapeDtypeStruct((lp - first_row, d), BF16)]
    return pl.pallas_call(
        functools.partial(_moe_combine_kernel, alpha=alpha, with_bf16=with_bf16),
        grid=(nt - off,),
        in_specs=[cur, cur, nxt, nxt, pl.BlockSpec((tm, d), lambda i: (i + off, 0)),
                  pl.BlockSpec((tm, TOP_K), lambda i: (i + off, 0)),
                  pl.BlockSpec(memory_space=pl.ANY), _full(ln_g.shape), _full(ln_b.shape)],
        out_specs=out_spec,
        out_shape=out_shape,
        scratch_shapes=[pltpu.VMEM((2, TOP_K, tm * ns, 128), U32),
                        pltpu.SemaphoreType.DMA((2, TOP_K))],
        compiler_params=_cparams("arbitrary"),
        name="moe_combine",
    )(d0, d1, d0, d1, h, weights, y_buf, ln_g, ln_b)


def _store_heads(ref, seg_rows, val):
    for hh in range(val.shape[1] // HG_DK):
        ref[0, hh, seg_rows, :] = val[:, hh * HG_DK:(hh + 1) * HG_DK].astype(ref.dtype)


def _inproj_plain_kernel(x_ref, w_ref, o_ref):
    for r in range(0, x_ref.shape[0], INPROJ_C_SUB):
        rows = slice(r, r + INPROJ_C_SUB)
        _store_heads(o_ref, rows, jnp.dot(x_ref[rows, :], w_ref[...], preferred_element_type=F32))


def _inproj_forget_kernel(x_ref, w_ref, lb_ref, k_ref, g_ref):
    i = pl.program_id(0)
    tm = x_ref.shape[0]
    lb = lb_ref[...]
    for r in range(0, tm, INPROJ_C_SUB):
        rows = slice(r, r + INPROJ_C_SUB)
        acc = jnp.dot(x_ref[rows, :], w_ref[...], preferred_element_type=F32)
        f = lb + (1.0 - lb) * jax.nn.sigmoid(acc)
        real = (i * tm + r + lax.broadcasted_iota(jnp.int32, (INPROJ_C_SUB, 1), 0)) >= META_ROW0
        _store_heads(k_ref, rows, jnp.where(real, 1.0 - f, 0.0))
        _store_heads(g_ref, rows, jnp.where(real, jnp.log(f), 0.0))


def _inproj_c(hb, w, lb):
    lp, d = hb.shape
    tm = _pick(lp, INPROJ_C_TILES)
    tn = 512
    tps = d // tn
    hpt = tn // HG_DK
    x_spec = pl.BlockSpec((tm, d), lambda i, j: (i, 0))
    out_spec = pl.BlockSpec((1, hpt, tm, HG_DK), lambda i, j: (j // tps, j % tps, i, 0))
    hm = lambda n, dt: jax.ShapeDtypeStruct((n, HG_HEADS, lp, HG_DK), dt)
    plain_col = lambda i, j: (0, jnp.where(j >= 2 * tps, j + 2 * tps, j))
    plain = pl.pallas_call(
        _inproj_plain_kernel,
        grid=(lp // tm, 3 * tps),
        in_specs=[x_spec, pl.BlockSpec((d, tn), plain_col)],
        out_specs=out_spec,
        out_shape=hm(3, BF16),
        compiler_params=_cparams("parallel", "arbitrary"),
        name="inproj_c_plain",
    )(hb, w)
    k, g = pl.pallas_call(
        _inproj_forget_kernel,
        grid=(lp // tm, 2 * tps),
        in_specs=[x_spec, pl.BlockSpec((d, tn), lambda i, j: (0, j + 2 * tps)),
                  pl.BlockSpec((1, tn), lambda i, j: (0, j % tps))],
        out_specs=[out_spec, out_spec],
        out_shape=[hm(2, BF16), hm(2, F32)],
        compiler_params=_cparams("parallel", "arbitrary"),
        name="inproj_c_forget",
    )(hb, w, lb)
    return plain, k, g


def _hgrn_chunk(q, k, v, g, state_t, reverse):
    c = HG_CHUNK
    row = lax.broadcasted_iota(jnp.int32, (c, HG_DK), 0)
    b = g
    sh = 1
    while sh < c:
        if not reverse:
            b = b + jnp.where(row >= sh, pltpu.roll(b, sh, axis=0), 0.0)
        else:
            b = b + jnp.where(row < c - sh, pltpu.roll(b, c - sh, axis=0), 0.0)
        sh *= 2
    r_i = lax.broadcasted_iota(jnp.int32, (c, c), 0)
    c_i = lax.broadcasted_iota(jnp.int32, (c, c), 1)
    causal = (c_i >= r_i) if reverse else (c_i <= r_i)
    sub = HG_SUB
    mid = sub // 2
    blocks = lambda x: [x[j * sub:(j + 1) * sub] for j in range(HG_NSUB)]
    ref = [b[j * sub + mid:j * sub + mid + 1, :] for j in range(HG_NSUB)]
    b_end = b[0:1, :] if reverse else b[c - 1:c, :]
    qd, ku, q0, k_end = [], [], [], []
    for j, (qj, kj, bj) in enumerate(zip(blocks(q), blocks(k), blocks(b))):
        dj = bj - ref[j]
        qd_j = qj * jnp.exp(jnp.minimum(dj, HG_EXP_CLAMP))
        ku_j = kj * jnp.exp(jnp.minimum(-dj, HG_EXP_CLAMP))
        qd.append(qd_j.astype(BF16))
        ku.append(ku_j)
        q0.append(qd_j * jnp.exp(ref[j]))
        k_end.append(ku_j * jnp.exp(b_end - ref[j]))
    nt = (((1,), (1,)), ((), ()))
    zero = jnp.zeros((sub, HG_DK), BF16)
    score_rows = []
    for j in range(HG_NSUB):
        parts = []
        for i in range(HG_NSUB):
            if i == j:
                parts.append(ku[i].astype(BF16))
            elif (i > j) if reverse else (i < j):
                parts.append((ku[i] * jnp.exp(ref[j] - ref[i])).astype(BF16))
            else:
                parts.append(zero)
        k_ext = jnp.concatenate(parts, axis=0)
        score_rows.append(lax.dot_general(qd[j], k_ext, nt, preferred_element_type=F32))
    scores = jnp.concatenate(score_rows, axis=0)
    scores = jnp.where(causal, scores, 0.0)
    o = jnp.dot(scores.astype(BF16), v.astype(BF16), preferred_element_type=F32)
    o += lax.dot_general(jnp.concatenate(q0, axis=0).astype(BF16), state_t.astype(BF16), nt,
                         preferred_element_type=F32)
    new_state = jnp.exp(b_end) * state_t + jnp.dot(
        v.T.astype(BF16), jnp.concatenate(k_end, axis=0).astype(BF16), preferred_element_type=F32)
    return o, new_state


def _hgrn_kernel(qf_ref, vf_ref, k1_ref, g1_ref, qb_ref, vb_ref, k2_ref, g2_ref,
                 ofw_ref, obw_ref, st_ref):
    @pl.when(pl.program_id(0) == 0)
    def _():
        st_ref[...] = jnp.zeros_like(st_ref)

    def head(hh, carry):
        o, s = _hgrn_chunk(qf_ref[0, hh].astype(F32), k1_ref[0, hh].astype(F32),
                           vf_ref[0, hh].astype(F32), g1_ref[0, hh], st_ref[0, hh], reverse=False)
        ofw_ref[hh] = o
        st_ref[0, hh] = s
        o, s = _hgrn_chunk(qb_ref[0, hh].astype(F32), k2_ref[0, hh].astype(F32),
                           vb_ref[0, hh].astype(F32), g2_ref[0, hh], st_ref[1, hh], reverse=True)
        obw_ref[hh] = o
        st_ref[1, hh] = s
        return carry

    lax.fori_loop(0, HG_HEADS, head, 0, unroll=4)


def _hgrn(plain, k, g):
    _, _, lp, dk = plain.shape
    c = HG_CHUNK
    nc = lp // c
    fwd = lambda seg: pl.BlockSpec((1, HG_HEADS, c, dk), lambda i: (seg, 0, i, 0))
    bwd = lambda seg: pl.BlockSpec((1, HG_HEADS, c, dk), lambda i: (seg, 0, nc - 1 - i, 0))
    out_f = pl.BlockSpec((HG_HEADS, c, dk), lambda i: (0, i, 0))
    out_b = pl.BlockSpec((HG_HEADS, c, dk), lambda i: (0, nc - 1 - i, 0))
    return pl.pallas_call(
        _hgrn_kernel,
        grid=(nc,),
        in_specs=[fwd(0), fwd(1), fwd(0), fwd(0), bwd(0), bwd(1), bwd(1), bwd(1)],
        out_specs=[out_f, out_b],
        out_shape=[jax.ShapeDtypeStruct((HG_HEADS, lp, dk), F32)] * 2,
        scratch_shapes=[pltpu.VMEM((2, HG_HEADS, dk, dk), F32)],
        compiler_params=_cparams("arbitrary"),
        name="hgrn2_recurrence",
    )(plain, plain, k, g, plain, plain, k, g)


def _route_params(w_group, b_group, w_expert, b_expert):
    d = w_group.shape[0]
    pad = ROUTE_LANES - N_GROUPS - N_EXPERTS
    w = jnp.concatenate([w_group, w_expert, jnp.zeros((d, pad), F32)], axis=1)
    b = jnp.concatenate([b_group, b_expert, jnp.zeros((pad,), F32)])[None, :]
    w_hi = w.astype(BF16)
    w_lo = (w - w_hi.astype(F32)).astype(BF16)
    return jnp.concatenate([w_hi, w_lo], axis=1), b


def _rope_tables(lp):
    half = HEAD_DIM // 2
    pos = jnp.maximum(jnp.arange(lp) - META_ROW0, 0).astype(F32)
    inv = ROPE_THETA ** (-jnp.arange(half, dtype=F32) * 2.0 / HEAD_DIM)
    ang = pos[:, None] * inv[None, :]
    cos, sin = jnp.cos(ang), jnp.sin(ang)
    return jnp.concatenate([cos, cos], axis=1), jnp.concatenate([-sin, sin], axis=1)


def kernel(x, meta_tokens, w_in_ab, w_out_ab, attn_sinks, s5_lam_re, s5_lam_im, s5_log_step, s5_b_re, s5_b_im, s5_c_re, s5_c_im, s5_d, s5_w_glu, s5_b_glu, w_in_c, w_out_c, hgrn_lb_logits, hgrn_norm_g, ln_mix_g, ln_mix_b, ln_ffn_g, ln_ffn_b, moe_w_group, moe_b_group, moe_w_expert, moe_b_expert, moe_w_gate_up, moe_w_down):
    batch, seq, d = x.shape
    assert batch == 1 and seq % FRONT == 0
    depth = ln_mix_g.shape[0]
    assert depth == 2
    alpha = (2.0 * depth) ** 0.25
    lp = FRONT + seq
    row2 = lambda t: t[None, :]

    front = jnp.concatenate([jnp.zeros((META_ROW0, d), F32), meta_tokens.astype(F32)], axis=0)
    xs = x[0]

    w_in = w_in_ab[0].astype(BF16)
    s5w = s5_d.shape[1]
    wq, wk, wv, wu = (w_in[:, :Q_DIM], w_in[:, Q_DIM:Q_DIM + KV_DIM],
                      w_in[:, Q_DIM + KV_DIM:Q_DIM + 2 * KV_DIM], w_in[:, Q_DIM + 2 * KV_DIM:])
    cos2, sin2 = _rope_tables(lp)
    q, k, v, u = _inproj_ab(front, xs, wq, wk, wv, wu, cos2, sin2)
    a_out = _window_attention(q, k, v, attn_sinks[0])
    mats = _s5_discretise(s5_lam_re[0], s5_lam_im[0], s5_log_step[0], s5_b_re[0], s5_b_im[0],
                          s5_c_re[0], s5_c_im[0])
    y_lo, y_hi = _s5_mixer_pre_glu(u, s5_d[0], mats)
    wo = w_out_ab[0].astype(BF16)
    w_route, b_route = _route_params(moe_w_group[0], moe_b_group[0], moe_w_expert[0], moe_b_expert[0])
    h, h_slabs, route = _mix_ab(front, xs, a_out, y_lo, y_hi, s5_w_glu[0].astype(BF16), row2(s5_b_glu[0]),
                       wo[:Q_DIM], wo[Q_DIM:], row2(ln_mix_g[0]), row2(ln_mix_b[0]), w_route, b_route,
                       alpha)
    plan = _moe_plan(route, META_ROW0)
    y_buf = _moe_experts(h_slabs, plan, moe_w_gate_up, moe_w_down, 0)
    h, hb = _moe_combine(h, plan, y_buf, row2(ln_ffn_g[0]), row2(ln_ffn_b[0]), alpha, 0, with_bf16=True)

    lb_probs = jax.nn.softmax(hgrn_lb_logits.astype(F32), axis=0)
    lb = (jnp.cumsum(lb_probs, axis=0) - lb_probs[0])[1]
    plain, hk, hg = _inproj_c(hb, w_in_c[0].astype(BF16), row2(lb))
    o_fw, o_bw = _hgrn(plain, hk, hg)
    w_route, b_route = _route_params(moe_w_group[1], moe_b_group[1], moe_w_expert[1], moe_b_expert[1])
    h, h_slabs, route = _mix_c(h, o_fw, o_bw, plain, hgrn_norm_g[0].reshape(HG_HEADS, 1, HG_DK),
                      w_out_c[0].astype(BF16), row2(ln_mix_g[1]), row2(ln_mix_b[1]), w_route, b_route,
                      alpha)
    plan = _moe_plan(route, META_ROW0)
    y_buf = _moe_experts(h_slabs, plan, moe_w_gate_up, moe_w_down, 1)
    out = _moe_combine(h, plan, y_buf, row2(ln_ffn_g[1]), row2(ln_ffn_b[1]), alpha, FRONT)
    return out[None]
```

```python
import functools
import math

import jax
import jax.numpy as jnp
from jax import lax
from jax.experimental import pallas as pl
from jax.experimental.pallas import tpu as pltpu

F32 = jnp.float32
BF16 = jnp.bfloat16

N_META = 16
FRONT = 512
META_ROW0 = FRONT - N_META

ATTN_HEADS = 8
ATTN_KV_HEADS = 2
ATTN_GROUP = ATTN_HEADS // ATTN_KV_HEADS
HEAD_DIM = 128
WINDOW = 128
ATTN_BLOCK = 128
ROPE_THETA = 10000.0
Q_DIM = ATTN_HEADS * HEAD_DIM
KV_DIM = ATTN_KV_HEADS * HEAD_DIM

S5_GROUP = 16
S5_STATE = 64
S5_CHUNK = 16
S5_ROW = S5_CHUNK * S5_GROUP
S5_SCAN_BLOCK = 32
S5_CHUNK_BLOCKS = (176, 96, 48, 32, 16)

HG_HEADS = 16
HG_DK = 128
HG_CHUNK = 128
HG_SUB = 32
HG_NSUB = HG_CHUNK // HG_SUB
HG_EXP_CLAMP = 80.0

N_GROUPS = 8
EXPERTS_PER_GROUP = 8
N_EXPERTS = N_GROUPS * EXPERTS_PER_GROUP
TOP_K = 2
D_EXPERT = 512
MOE_TM = 256
ROUTE_LANES = 128

LN_EPS = 1e-5
RMS_EPS = 1e-6
NEG_INF = -1e30

ROW_TILE = 512
MIX_TILE = 256
MIX_SUB = 128
INPROJ_C_TILES = (1536, 1024, 512)
INPROJ_C_SUB = 512

VMEM_LIMIT = 56 * 1024 * 1024


def _cparams(*sem):
    return pltpu.CompilerParams(dimension_semantics=sem, vmem_limit_bytes=VMEM_LIMIT)


def _pick(n, candidates):
    return next(c for c in candidates if n % c == 0)


def _full(shape):
    nd = len(shape)
    return pl.BlockSpec(shape, lambda *_: (0,) * nd)


def _layer_norm(x, g, b):
    mu = jnp.mean(x, axis=-1, keepdims=True)
    xc = x - mu
    var = jnp.mean(xc * xc, axis=-1, keepdims=True)
    return xc * lax.rsqrt(var + LN_EPS) * g + b


def _inproj_ab_kernel(front_ref, x_ref, wq_ref, wk_ref, wv_ref, wu_ref, cos_ref, sin_ref,
                      q_ref, k_ref, v_ref, u_ref):
    xb = jnp.where(pl.program_id(0) == 0, front_ref[...], x_ref[...]).astype(BF16)
    cos = cos_ref[...]
    sin = sin_ref[...]

    def rope(t):
        return t * cos + pltpu.roll(t, HEAD_DIM // 2, axis=1) * sin

    q = jnp.dot(xb, wq_ref[...], preferred_element_type=F32)
    for h in range(ATTN_HEADS):
        sl = slice(h * HEAD_DIM, (h + 1) * HEAD_DIM)
        q_ref[:, sl] = rope(q[:, sl]).astype(BF16)
    k = jnp.dot(xb, wk_ref[...], preferred_element_type=F32)
    for h in range(ATTN_KV_HEADS):
        sl = slice(h * HEAD_DIM, (h + 1) * HEAD_DIM)
        k_ref[:, sl] = rope(k[:, sl]).astype(BF16)
    v_ref[...] = jnp.dot(xb, wv_ref[...], preferred_element_type=F32).astype(BF16)
    u_ref[...] = jnp.dot(xb, wu_ref[...], preferred_element_type=F32)


def _inproj_ab(front, x, wq, wk, wv, wu, cos2, sin2):
    d = x.shape[1]
    lp = front.shape[0] + x.shape[0]
    tm = ROW_TILE
    assert front.shape[0] == tm
    s5w = wu.shape[1]
    row = lambda w: pl.BlockSpec((tm, w), lambda i: (i, 0))
    return pl.pallas_call(
        _inproj_ab_kernel,
        grid=(lp // tm,),
        in_specs=[_full(front.shape), pl.BlockSpec((tm, d), lambda i: (jnp.maximum(i - 1, 0), 0)),
                  _full(wq.shape), _full(wk.shape), _full(wv.shape), _full(wu.shape),
                  row(HEAD_DIM), row(HEAD_DIM)],
        out_specs=[row(Q_DIM), row(KV_DIM), row(KV_DIM), row(s5w)],
        out_shape=[jax.ShapeDtypeStruct((lp, Q_DIM), BF16),
                   jax.ShapeDtypeStruct((lp, KV_DIM), BF16),
                   jax.ShapeDtypeStruct((lp, KV_DIM), BF16),
                   jax.ShapeDtypeStruct((lp, s5w), F32)],
        compiler_params=_cparams("parallel"),
        name="inproj_ab",
    )(front, x, wq, wk, wv, wu, cos2, sin2)


def _attn_kernel(sink_ref, q_ref, k0_ref, k1_ref, k2_ref, v0_ref, v1_ref, v2_ref, km_ref, vm_ref,
                 o_ref, *, lp):
    qb = pl.program_id(0)
    blk = ATTN_BLOCK
    rows = ATTN_GROUP * blk
    scale = HEAD_DIM ** -0.5
    q_row = qb * blk + lax.broadcasted_iota(jnp.int32, (rows, 3 * blk), 0) % blk
    k_row = (qb - 1) * blk + lax.broadcasted_iota(jnp.int32, (rows, 3 * blk), 1)
    vis = (k_row >= FRONT) & (k_row < lp) & (jnp.abs(q_row - k_row) <= WINDOW)
    head_of_row = lax.broadcasted_iota(jnp.int32, (rows, 1), 0) // blk
    for g in range(ATTN_KV_HEADS):
        gs = slice(g * HEAD_DIM, (g + 1) * HEAD_DIM)
        qs = jnp.concatenate(
            [q_ref[:, (g * ATTN_GROUP + r) * HEAD_DIM:(g * ATTN_GROUP + r + 1) * HEAD_DIM]
             for r in range(ATTN_GROUP)], axis=0)
        kband = jnp.concatenate([k0_ref[:, gs], k1_ref[:, gs], k2_ref[:, gs]], axis=0)
        vband = jnp.concatenate([v0_ref[:, gs], v1_ref[:, gs], v2_ref[:, gs]], axis=0)
        nt = (((1,), (1,)), ((), ()))
        s_band = lax.dot_general(qs, kband, nt, preferred_element_type=F32) * scale
        s_band = jnp.where(vis, s_band, NEG_INF)
        s_meta = lax.dot_general(qs, km_ref[:, gs], nt, preferred_element_type=F32) * scale
        sink = jnp.zeros((rows, 1), F32)
        for r in range(ATTN_GROUP):
            sink = jnp.where(head_of_row == r, sink_ref[g * ATTN_GROUP + r], sink)
        m = jnp.maximum(jnp.maximum(jnp.max(s_band, axis=-1, keepdims=True),
                                    jnp.max(s_meta, axis=-1, keepdims=True)), sink)
        e_band = jnp.exp(s_band - m)
        e_meta = jnp.exp(s_meta - m)
        denom = (jnp.sum(e_band, axis=-1, keepdims=True) + jnp.sum(e_meta, axis=-1, keepdims=True)
                 + jnp.exp(sink - m))
        o = (jnp.dot(e_band.astype(BF16), vband, preferred_element_type=F32)
             + jnp.dot(e_meta.astype(BF16), vm_ref[:, gs], preferred_element_type=F32)) / denom
        for r in range(ATTN_GROUP):
            hh = g * ATTN_GROUP + r
            o_ref[:, hh * HEAD_DIM:(hh + 1) * HEAD_DIM] = o[r * blk:(r + 1) * blk].astype(BF16)


def _window_attention(q, k, v, sinks):
    lp = q.shape[0]
    blk = ATTN_BLOCK
    nb = lp // blk
    qspec = pl.BlockSpec((blk, Q_DIM), lambda i, s: (i, 0))
    kv = lambda off: pl.BlockSpec((blk, KV_DIM), lambda i, s: (jnp.clip(i + off, 0, nb - 1), 0))
    meta = pl.BlockSpec((N_META, KV_DIM), lambda i, s: (META_ROW0 // N_META, 0))
    return pl.pallas_call(
        functools.partial(_attn_kernel, lp=lp),
        grid_spec=pltpu.PrefetchScalarGridSpec(
            num_scalar_prefetch=1,
            grid=(nb,),
            in_specs=[qspec, kv(-1), kv(0), kv(1), kv(-1), kv(0), kv(1), meta, meta],
            out_specs=qspec,
        ),
        out_shape=jax.ShapeDtypeStruct((lp, Q_DIM), BF16),
        compiler_params=_cparams("parallel"),
        name="window_attention",
    )(sinks, q, k, k, k, v, v, v, k, v)


def _s5_discretise(lam_re, lam_im, log_step, b_re, b_im, c_re, c_im):
    t = S5_CHUNK
    hi = lax.Precision.HIGHEST
    g = lam_re.shape[1]
    lr = jnp.minimum(lam_re, -1e-4)
    li = lam_im
    step = jnp.exp(log_step)[..., None]
    dr, di = lr * step, li * step
    lags = jnp.arange(t + 1, dtype=F32)[:, None]
    mag = jnp.exp(dr[..., None, :] * lags)
    pr, pi = mag * jnp.cos(di[..., None, :] * lags), mag * jnp.sin(di[..., None, :] * lags)
    ar, ai = pr[:, :, 1], pi[:, :, 1]
    den = lr * lr + li * li
    zr = ((ar - 1.0) * lr + ai * li) / den
    zi = (ai * lr - (ar - 1.0) * li) / den
    br = zr[..., None] * b_re - zi[..., None] * b_im
    bi = zr[..., None] * b_im + zi[..., None] * b_re
    brt, bit = br.transpose(0, 1, 3, 2), bi.transpose(0, 1, 3, 2)
    ct_r, ct_i = c_re.transpose(0, 1, 3, 2), c_im.transpose(0, 1, 3, 2)
    pt_r, pt_i = pr[:, :, :t].transpose(0, 1, 3, 2), pi[:, :, :t].transpose(0, 1, 3, 2)
    xr = (ct_r[:, :, :, None] * pt_r[..., None] - ct_i[:, :, :, None] * pt_i[..., None])
    xi = (ct_r[:, :, :, None] * pt_i[..., None] + ct_i[:, :, :, None] * pt_r[..., None])
    x_cat = jnp.concatenate([xr, xi], axis=2).reshape(2, g, 2 * S5_STATE, t * S5_GROUP)
    b_cat = jnp.concatenate([brt, -bit], axis=3)
    kern = jnp.einsum('dgip,dgpn->dgin', b_cat, x_cat, precision=hi).reshape(2, g, S5_GROUP, t, S5_GROUP)
    k_lag = jnp.concatenate([jnp.flip(kern[1][:, :, 1:], axis=2), kern[0][:, :, :1] + kern[1][:, :, :1],
                             kern[0][:, :, 1:]], axis=2)
    k_lag = k_lag.reshape(g, S5_GROUP, (2 * t - 1) * S5_GROUP)

    def flat_e(p_r, p_i, d):
        e_r = p_r[:, :, None] * brt[d][:, None] - p_i[:, :, None] * bit[d][:, None]
        e_i = p_r[:, :, None] * bit[d][:, None] + p_i[:, :, None] * brt[d][:, None]
        flat = lambda e: e.reshape(g, t * S5_GROUP, S5_STATE)
        return jnp.concatenate([flat(e_r), flat(e_i)], axis=-1)
    e_mat = jnp.concatenate([flat_e(pr[0][:, ::-1][:, 1:], pi[0][:, ::-1][:, 1:], 0),
                             flat_e(pr[1][:, :t], pi[1][:, :t], 1)], axis=-1)

    def flat_f(p_r, p_i, d):
        ct_r, ct_i = c_re[d].transpose(0, 2, 1), c_im[d].transpose(0, 2, 1)
        pt_r, pt_i = p_r.transpose(0, 2, 1)[..., None], p_i.transpose(0, 2, 1)[..., None]
        w_r = ct_r[:, :, None] * pt_r - ct_i[:, :, None] * pt_i
        w_i = ct_r[:, :, None] * pt_i + ct_i[:, :, None] * pt_r
        flat = lambda w: w.reshape(g, S5_STATE, t * S5_GROUP)
        return jnp.concatenate([flat(w_r), -flat(w_i)], axis=1)
    f_fw = flat_f(pr[0][:, 1:], pi[0][:, 1:], 0)
    f_bw = flat_f(pr[1][:, ::-1][:, :t], pi[1][:, ::-1][:, :t], 1)
    gh = g // 2
    upper = (jnp.arange(g) >= gh)[:, None, None]
    def place(m):
        z = jnp.zeros_like(m)
        return jnp.concatenate([jnp.where(upper, z, m), jnp.where(upper, m, z)], axis=1)
    p = S5_STATE
    f_mat = jnp.concatenate([place(f_fw[:, :p]), place(f_fw[:, p:]), place(f_bw[:, :p]), place(f_bw[:, p:])],
                            axis=1)
    pack = lambda v: jnp.concatenate([v[:gh], v[gh:]], axis=-1)
    coef = jnp.stack([pack(pr[0][:, t]), pack(pi[0][:, t]), pack(pr[1][:, t]), pack(pi[1][:, t])])
    split = lambda m: m.reshape((2, gh) + m.shape[1:])
    return split(k_lag), split(e_mat.astype(BF16)), split(f_mat.astype(BF16)), coef


def _chunk_rows_to_lanes(u_ref, ncb):
    lane_seg = lax.broadcasted_iota(jnp.int32, (ncb, 128), 1) // S5_GROUP
    rows = [u_ref[pl.ds(s, ncb, stride=S5_CHUNK), :] for s in range(S5_CHUNK)]
    segs = 128 // S5_GROUP
    out = []
    for j in range(segs):
        halves = []
        for h in range(S5_CHUNK // segs):
            acc = None
            for s8 in range(segs):
                r = rows[h * segs + s8]
                k = (s8 - j) % segs
                if k:
                    r = pltpu.roll(r, S5_GROUP * k, axis=1)
                acc = r if acc is None else jnp.where(lane_seg == s8, r, acc)
            halves.append(acc)
        out.append(jnp.concatenate(halves, axis=1))
    return out


def _lanes_to_chunk_rows(ys, y_ref, ncb, add_ref, scale_ref):
    lane_seg = lax.broadcasted_iota(jnp.int32, (ncb, 128), 1) // S5_GROUP
    segs = 128 // S5_GROUP
    for t in range(S5_CHUNK):
        h, t8 = divmod(t, segs)
        acc = None
        for j in range(segs):
            r = ys[j][:, h * 128:(h + 1) * 128]
            k = (j - t8) % segs
            if k:
                r = pltpu.roll(r, S5_GROUP * k, axis=1)
            acc = r if acc is None else jnp.where(lane_seg == j, r, acc)
        idx = pl.ds(t, ncb, stride=S5_CHUNK)
        y_ref[idx, :] = acc + add_ref[idx, :] * scale_ref[...]


def _pair_halves(a, b):
    lane = lax.broadcasted_iota(jnp.int32, a.shape, 1)
    lo = jnp.where(lane < S5_STATE, a, pltpu.roll(b, S5_STATE, axis=1))
    hi = jnp.where(lane < S5_STATE, pltpu.roll(a, S5_STATE, axis=1), b)
    return lo, hi


def _s5_local_kernel(ua_ref, ub_ref, e_ref, u2_ref, f1_ref, f2_ref, b1_ref, b2_ref):
    ncb = u2_ref.shape[2]
    p2 = 2 * S5_STATE
    chunks = [_chunk_rows_to_lanes(ua_ref, ncb), _chunk_rows_to_lanes(ub_ref, ncb)]
    for j in range(u2_ref.shape[1]):
        st = []
        for hh in range(2):
            u2 = chunks[hh][j].astype(BF16)
            u2_ref[hh, j] = u2
            st.append(jnp.dot(u2, e_ref[hh, j], preferred_element_type=F32))
        f1_ref[:, j, :], f2_ref[:, j, :] = _pair_halves(st[0][:, :p2], st[1][:, :p2])
        b1_ref[:, j, :], b2_ref[:, j, :] = _pair_halves(st[0][:, p2:], st[1][:, p2:])


def _s5_scan_kernel(sf1_ref, sf2_ref, sb1_ref, sb2_ref, coef_ref, xf1_ref, xf2_ref, xb1_ref, xb2_ref,
                    st_ref):
    @pl.when(pl.program_id(0) == 0)
    def _():
        st_ref[...] = jnp.zeros_like(st_ref)

    nb = sf1_ref.shape[0]
    crf, cif, crb, cib = coef_ref[0], coef_ref[1], coef_ref[2], coef_ref[3]

    def body(i, carry):
        f1, f2, b1, b2 = carry
        j = nb - 1 - i
        xf1_ref[i] = f1
        xf2_ref[i] = f2
        xb1_ref[j] = b1
        xb2_ref[j] = b2
        return (crf * f1 - cif * f2 + sf1_ref[i], crf * f2 + cif * f1 + sf2_ref[i],
                crb * b1 - cib * b2 + sb1_ref[j], crb * b2 + cib * b1 + sb2_ref[j])

    out = lax.fori_loop(0, nb, body, (st_ref[0], st_ref[1], st_ref[2], st_ref[3]))
    for q in range(4):
        st_ref[q] = out[q]


def _chunk_matrix(k_lag):
    t, c = S5_CHUNK, S5_GROUP
    return jnp.concatenate([k_lag[:, (t - 1 - s) * c:(2 * t - 1 - s) * c] for s in range(t)],
                           axis=0).astype(BF16)


def _s5_readout_kernel(ua_ref, ub_ref, d_ref, u2_ref, kl_ref, f_ref, xf1_ref, xf2_ref, xb1_ref, xb2_ref,
                       ya_ref, yb_ref):
    ncb = u2_ref.shape[2]
    ys = [[], []]
    for j in range(u2_ref.shape[1]):
        x = jnp.concatenate([xf1_ref[:, j, :], xf2_ref[:, j, :], xb1_ref[:, j, :], xb2_ref[:, j, :]],
                            axis=1).astype(BF16)
        for hh in range(2):
            y = jnp.dot(u2_ref[hh, j], _chunk_matrix(kl_ref[hh, j]), preferred_element_type=F32)
            ys[hh].append(y + jnp.dot(x, f_ref[hh, j], preferred_element_type=F32))
    _lanes_to_chunk_rows(ys[0], ya_ref, ncb, ua_ref, d_ref.at[0])
    _lanes_to_chunk_rows(ys[1], yb_ref, ncb, ub_ref, d_ref.at[1])


def _s5_mixer_pre_glu(u, d_skip, mats):
    k_lag, e_mat, f_mat, coef = mats
    lp, w = u.shape
    g = w // S5_GROUP
    gh = g // 2
    gpb = 128 // S5_GROUP
    nc = lp // S5_CHUNK
    ncb = _pick(nc, S5_CHUNK_BLOCKS)
    rows = ncb * S5_CHUNK
    nlb = w // 128
    grid = (nlb // 2, nc // ncb)
    u_lo = pl.BlockSpec((rows, 128), lambda b, r: (r, b))
    u_hi = pl.BlockSpec((rows, 128), lambda b, r: (r, b + nlb // 2))
    per_group = lambda k, n: pl.BlockSpec((2, gpb, k, n), lambda b, r: (0, b, 0, 0))
    u2_spec = pl.BlockSpec((2, gpb, ncb, S5_ROW), lambda b, r: (0, b, r, 0))
    st_spec = pl.BlockSpec((ncb, gpb, 2 * S5_STATE), lambda b, r: (r, b, 0))
    st_shape = jax.ShapeDtypeStruct((nc, gh, 2 * S5_STATE), F32)
    u2, sf1, sf2, sb1, sb2 = pl.pallas_call(
        _s5_local_kernel,
        grid=grid,
        in_specs=[u_lo, u_hi, per_group(S5_ROW, 4 * S5_STATE)],
        out_specs=[u2_spec, st_spec, st_spec, st_spec, st_spec],
        out_shape=[jax.ShapeDtypeStruct((2, gh, nc, S5_ROW), BF16)] + [st_shape] * 4,
        compiler_params=_cparams("parallel", "parallel"),
        name="s5_local_states",
    )(u, u, e_mat)
    sb = S5_SCAN_BLOCK
    nblk = nc // sb
    fwd = pl.BlockSpec((sb, gh, 2 * S5_STATE), lambda i: (i, 0, 0))
    bwd = pl.BlockSpec((sb, gh, 2 * S5_STATE), lambda i: (nblk - 1 - i, 0, 0))
    xf1, xf2, xb1, xb2 = pl.pallas_call(
        _s5_scan_kernel,
        grid=(nblk,),
        in_specs=[fwd, fwd, bwd, bwd, _full(coef.shape)],
        out_specs=[fwd, fwd, bwd, bwd],
        out_shape=[st_shape] * 4,
        scratch_shapes=[pltpu.VMEM((4, gh, 2 * S5_STATE), F32)],
        compiler_params=_cparams("arbitrary"),
        name="s5_chunk_scan",
    )(sf1, sf2, sb1, sb2, coef)
    d2 = d_skip.reshape(2, 1, nlb // 2 * 128)
    y_lo, y_hi = pl.pallas_call(
        _s5_readout_kernel,
        grid=grid,
        in_specs=[u_lo, u_hi, pl.BlockSpec((2, 1, 128), lambda b, r: (0, 0, b)), u2_spec,
                  per_group(S5_GROUP, k_lag.shape[-1]), per_group(8 * S5_STATE, S5_ROW),
                  st_spec, st_spec, st_spec, st_spec],
        out_specs=[pl.BlockSpec((rows, 128), lambda b, r: (r, b))] * 2,
        out_shape=[jax.ShapeDtypeStruct((lp, w // 2), F32)] * 2,
        compiler_params=_cparams("parallel", "parallel"),
        name="s5_readout",
    )(u, u, d2, u2, k_lag, f_mat, xf1, xf2, xb1, xb2)
    return y_lo, y_hi


def _route(hn, wr_ref, br_ref):
    hn_hi = hn.astype(BF16)
    hn_lo = (hn - hn_hi.astype(F32)).astype(BF16)
    p_hi = jnp.dot(hn_hi, wr_ref[...], preferred_element_type=F32)
    p_lo = jnp.dot(hn_lo, wr_ref[:, :ROUTE_LANES], preferred_element_type=F32)
    logits = p_hi[:, :ROUTE_LANES] + (p_hi[:, ROUTE_LANES:] + p_lo) + br_ref[...]
    lane_i = lax.broadcasted_iota(jnp.int32, logits.shape, 1)
    lane = lane_i.astype(F32)
    lane_grp = ((lane_i - N_GROUPS) // EXPERTS_PER_GROUP).astype(F32)
    big = float(ROUTE_LANES)
    g_log = jnp.where(lane_i < N_GROUPS, logits, -jnp.inf)
    g_max = jnp.max(g_log, axis=-1, keepdims=True)
    grp = jnp.min(jnp.where(g_log == g_max, lane, big), axis=-1, keepdims=True)
    p_grp = 1.0 / jnp.sum(jnp.exp(g_log - g_max), axis=-1, keepdims=True)
    in_grp = (lane_i >= N_GROUPS) & (lane_i < N_GROUPS + N_EXPERTS) & (lane_grp == grp)
    e_log = jnp.where(in_grp, logits, -jnp.inf)
    v1 = jnp.max(e_log, axis=-1, keepdims=True)
    i1 = jnp.min(jnp.where(e_log == v1, lane, big), axis=-1, keepdims=True)
    e_log2 = jnp.where(lane == i1, -jnp.inf, e_log)
    v2 = jnp.max(e_log2, axis=-1, keepdims=True)
    i2 = jnp.min(jnp.where(e_log2 == v2, lane, big), axis=-1, keepdims=True)
    e21 = jnp.exp(v2 - v1)
    w1 = p_grp / (1.0 + e21)
    w2 = p_grp * e21 / (1.0 + e21)
    route = jnp.where(lane_i == 0, w1, 0.0)
    route = jnp.where(lane_i == 1, w2, route)
    route = jnp.where(lane_i == 2, i1 - N_GROUPS, route)
    route = jnp.where(lane_i == 3, i2 - N_GROUPS, route)
    return route


U32 = jnp.uint32


def _slab_rows(d):
    return d // 256


def _slab_spec(tm, d):
    return pl.BlockSpec((tm * _slab_rows(d), 128), lambda i: (i, 0))


def _slab_shape(rows, d):
    return jax.ShapeDtypeStruct((rows * _slab_rows(d), 128), U32)


def _pack_pair(a, b):
    ua = lax.bitcast_convert_type(a.astype(BF16).astype(F32), U32)
    ub = lax.bitcast_convert_type(b.astype(BF16).astype(F32), U32)
    return ua | (ub >> 16)


def _unpack_pair(u):
    a = lax.bitcast_convert_type(u & jnp.uint32(0xFFFF0000), F32)
    b = lax.bitcast_convert_type(u << 16, F32)
    return a, b


def _store_row_slabs(ref, start, n, val):
    ns = _slab_rows(val.shape[1])
    for k in range(ns):
        word = _pack_pair(val[:, 2 * k * 128:(2 * k + 1) * 128], val[:, (2 * k + 1) * 128:(2 * k + 2) * 128])
        ref[pl.ds(start * ns + k, n, stride=ns), :] = word


def _load_row_slabs(ref_at, n, ns):
    cols = []
    for k in range(ns):
        cols.extend(_unpack_pair(ref_at(pl.ds(k, n, stride=ns))))
    return cols


def _sub_rows(tm):
    return [slice(r, r + MIX_SUB) for r in range(0, tm, MIX_SUB)]


def _mix_ab_kernel(front_ref, x_ref, a_ref, ylo_ref, yhi_ref, wglu_ref, bglu_ref, woa_ref, wos_ref,
                   g_ref, b_ref, wr_ref, br_ref, hn_ref, hs_ref, route_ref, *, alpha):
    in_front = pl.program_id(0) < FRONT // MIX_TILE
    for rows in _sub_rows(x_ref.shape[0]):
        y = jnp.concatenate([ylo_ref[rows, :], yhi_ref[rows, :]], axis=1)
        z = 0.5 * y * (1.0 + jnp.tanh(math.sqrt(2.0 / math.pi) * (y + 0.044715 * (y * y * y))))
        gate = jnp.dot(z.astype(BF16), wglu_ref[...], preferred_element_type=F32) + bglu_ref[...]
        s_out = z * jax.nn.sigmoid(gate)
        mix = (jnp.dot(a_ref[rows, :], woa_ref[...], preferred_element_type=F32)
               + jnp.dot(s_out.astype(BF16), wos_ref[...], preferred_element_type=F32))
        h_in = jnp.where(in_front, front_ref[rows, :], x_ref[rows, :])
        hn = _layer_norm(alpha * h_in + mix, g_ref[...], b_ref[...])
        hn_ref[rows, :] = hn
        _store_row_slabs(hs_ref, rows.start, rows.stop - rows.start, hn)
        route_ref[rows, :] = _route(hn, wr_ref, br_ref)


def _mix_ab(front, x, a_out, y_lo, y_hi, w_glu, b_glu, wo_a, wo_s, ln_g, ln_b, w_route, b_route, alpha):
    d = x.shape[1]
    lp = front.shape[0] + x.shape[0]
    tm = MIX_TILE
    nf = front.shape[0] // tm
    row = lambda c: pl.BlockSpec((tm, c), lambda i: (i, 0))
    return pl.pallas_call(
        functools.partial(_mix_ab_kernel, alpha=alpha),
        grid=(lp // tm,),
        in_specs=[pl.BlockSpec((tm, d), lambda i: (jnp.minimum(i, nf - 1), 0)),
                  pl.BlockSpec((tm, d), lambda i: (jnp.maximum(i - nf, 0), 0)), row(Q_DIM), row(y_lo.shape[1]), row(y_hi.shape[1]), _full(w_glu.shape),
                  _full(b_glu.shape), _full(wo_a.shape), _full(wo_s.shape), _full(ln_g.shape),
                  _full(ln_b.shape), _full(w_route.shape), _full(b_route.shape)],
        out_specs=[row(d), _slab_spec(tm, d), row(ROUTE_LANES)],
        out_shape=[jax.ShapeDtypeStruct((lp, d), F32), _slab_shape(lp, d),
                   jax.ShapeDtypeStruct((lp, ROUTE_LANES), F32)],
        compiler_params=_cparams("parallel"),
        name="mix_ab",
    )(front, x, a_out, y_lo, y_hi, w_glu, b_glu, wo_a, wo_s, ln_g, ln_b, w_route, b_route)


def _mix_c_kernel(h_ref, ofw_ref, obw_ref, gate_ref, ng_ref, wo_ref, g_ref, b_ref, wr_ref, br_ref,
                  hn_ref, hs_ref, route_ref, *, alpha):
    for rows in _sub_rows(h_ref.shape[0]):
        parts = []
        for hh in range(HG_HEADS):
            o = ofw_ref[hh, rows, :] + obw_ref[hh, rows, :]
            o = o * lax.rsqrt(jnp.mean(o * o, axis=-1, keepdims=True) + RMS_EPS)
            o = o * ng_ref[hh] * jax.nn.sigmoid(gate_ref[0, hh, rows, :].astype(F32))
            parts.append(o.astype(BF16))
        mix = jnp.dot(jnp.concatenate(parts, axis=1), wo_ref[...], preferred_element_type=F32)
        hn = _layer_norm(alpha * h_ref[rows, :] + mix, g_ref[...], b_ref[...])
        hn_ref[rows, :] = hn
        _store_row_slabs(hs_ref, rows.start, rows.stop - rows.start, hn)
        route_ref[rows, :] = _route(hn, wr_ref, br_ref)


def _mix_c(h, o_fw, o_bw, plain, norm_g, wo, ln_g, ln_b, w_route, b_route, alpha):
    lp, d = h.shape
    tm = MIX_TILE
    row = lambda c: pl.BlockSpec((tm, c), lambda i: (i, 0))
    hm = pl.BlockSpec((HG_HEADS, tm, HG_DK), lambda i: (0, i, 0))
    gate = pl.BlockSpec((1, HG_HEADS, tm, HG_DK), lambda i: (2, 0, i, 0))
    return pl.pallas_call(
        functools.partial(_mix_c_kernel, alpha=alpha),
        grid=(lp // tm,),
        in_specs=[row(d), hm, hm, gate, _full(norm_g.shape), _full(wo.shape), _full(ln_g.shape),
                  _full(ln_b.shape), _full(w_route.shape), _full(b_route.shape)],
        out_specs=[row(d), _slab_spec(tm, d), row(ROUTE_LANES)],
        out_shape=[jax.ShapeDtypeStruct((lp, d), F32), _slab_shape(lp, d),
                   jax.ShapeDtypeStruct((lp, ROUTE_LANES), F32)],
        compiler_params=_cparams("parallel"),
        name="mix_c",
    )(h, o_fw, o_bw, plain, norm_g, wo, ln_g, ln_b, w_route, b_route)


def _moe_plan(route, n_rows_valid_from):
    lp = route.shape[0]
    tm = MOE_TM
    valid = (jnp.arange(lp) >= n_rows_valid_from)
    expert = route[:, 2:4].astype(jnp.int32)
    flat_e = jnp.where(valid[:, None], expert, N_EXPERTS).reshape(-1)
    onehot = (flat_e[:, None] == jnp.arange(N_EXPERTS)[None, :]).astype(jnp.int32)
    csum = jnp.cumsum(onehot, axis=0)
    counts = csum[-1]
    rank = jnp.sum(jnp.where(onehot > 0, csum - 1, 0), axis=1)
    padded = (counts + tm - 1) // tm * tm
    pend = jnp.cumsum(padded)
    pstart = pend - padded
    e_safe = jnp.minimum(flat_e, N_EXPERTS - 1)
    dest = jnp.where(flat_e < N_EXPERTS, pstart[e_safe] + rank, 0)
    n_tokens = lp - n_rows_valid_from
    n_blocks = -(-(n_tokens * TOP_K + N_EXPERTS * (tm - 1)) // tm)
    rows = n_blocks * tm
    tok = jnp.repeat(jnp.arange(lp, dtype=jnp.int32), TOP_K)
    scatter_to = jnp.where(flat_e < N_EXPERTS, dest, rows)
    tok_of_row = jnp.zeros((rows,), jnp.int32).at[scatter_to].set(tok, mode='drop')
    n_used = (pend[-1] // tm).astype(jnp.int32)
    blk = jnp.minimum(jnp.arange(n_blocks, dtype=jnp.int32), jnp.maximum(n_used - 1, 0))
    block_expert = jnp.minimum(jnp.searchsorted(pend, blk * tm, side='right'),
                               N_EXPERTS - 1).astype(jnp.int32)
    used = counts > 0
    pos = jnp.cumsum(used.astype(jnp.int32)) - 1
    eid = jnp.arange(N_EXPERTS, dtype=jnp.int32)
    later = jnp.where(used[None, :] & (eid[None, :] > eid[:, None]), eid[None, :], N_EXPERTS)
    nxt_e = jnp.min(later, axis=1)
    nxt_e = jnp.where(nxt_e == N_EXPERTS, -1, nxt_e).astype(jnp.int32)
    w_slot = (pos[block_expert] % 2).astype(jnp.int32)
    nxt = nxt_e[block_expert]
    weights = jnp.where(valid[:, None], route[:, 0:2], 0.0)
    return (block_expert, n_used.reshape(1), w_slot, nxt, tok_of_row.reshape(n_blocks, 1, tm),
            dest.reshape(lp, TOP_K).astype(jnp.int32), weights)


def _moe_expert_kernel(be_ref, nu_ref, ws_ref, nx_ref, tok0_ref, tokn_ref, h_hbm, wgu_hbm, wd_hbm,
                       y_ref, xbuf, xsem, wgu_f, wd_f, wsem, wgu_bf, wd_bf, *, layer):
    i = pl.program_id(0)
    nl = _slab_rows(wgu_f.shape[1])
    tm = xbuf.shape[1] // nl
    n_used = nu_ref[0]

    def gather(tok_ref, slot):
        def issue(r, carry):
            src = h_hbm.at[pl.ds(pl.multiple_of(tok_ref[0, 0, r], nl), nl)]
            pltpu.make_async_copy(src, xbuf.at[slot, pl.ds(pl.multiple_of(r * nl, nl), nl)],
                                  xsem.at[slot]).start()
            return carry
        lax.fori_loop(0, tm, issue, 0, unroll=8)

    def gather_wait(slot):
        pltpu.make_async_copy(h_hbm.at[pl.ds(0, tm * nl)], xbuf.at[slot], xsem.at[slot]).wait()

    def weight_copies(e, s):
        return (pltpu.make_async_copy(wgu_hbm.at[layer, e], wgu_f.at[s], wsem.at[0, s]),
                pltpu.make_async_copy(wd_hbm.at[layer, e], wd_f.at[s], wsem.at[1, s]))

    @pl.when(i == 0)
    def _():
        for cp in weight_copies(be_ref[0], 0):
            cp.start()
        gather(tok0_ref, 0)

    @pl.when(i < n_used)
    def _():
        slot = i % 2
        e = be_ref[i]
        first = jnp.logical_or(i == 0, be_ref[jnp.maximum(i - 1, 0)] != e)

        gather_wait(slot)
        gather(tokn_ref, 1 - slot)

        @pl.when(first)
        def _():
            s = ws_ref[i]
            for cp in weight_copies(e, s):
                cp.wait()
            nxt = nx_ref[i]

            @pl.when(nxt >= 0)
            def _():
                for cp in weight_copies(nxt, 1 - s):
                    cp.start(priority=1)

            wgu_bf[...] = wgu_f[s].astype(BF16)
            wd_bf[...] = wd_f[s].astype(BF16)

        cols = _load_row_slabs(lambda idx: xbuf[slot, idx, :], tm, nl)
        x = jnp.concatenate([c.astype(BF16) for c in cols], axis=1)
        hgu = jnp.dot(x, wgu_bf[...], preferred_element_type=F32)
        act = jax.nn.silu(hgu[:, :D_EXPERT]) * hgu[:, D_EXPERT:]
        y = jnp.dot(act.astype(BF16), wd_bf[...], preferred_element_type=F32)
        _store_row_slabs(y_ref, 0, tm, y)

        @pl.when(i == n_used - 1)
        def _():
            gather_wait(1 - slot)

    @pl.when(i >= n_used)
    def _():
        y_ref[...] = jnp.zeros_like(y_ref)


def _moe_experts(h_slabs, plan, w_gate_up, w_down, layer):
    block_expert, n_used, w_slot, nxt, tok_of_row, _, _ = plan
    d = w_gate_up.shape[2]
    ns = _slab_rows(d)
    tok_of_row = tok_of_row * ns
    tm = MOE_TM
    n_blocks = tok_of_row.shape[0]
    nxt_blk = lambda i, be, nu, ws, nx: (jnp.minimum(i + 1, jnp.maximum(nu[0] - 1, 0)), 0, 0)
    smem = lambda imap: pl.BlockSpec((1, 1, tm), imap, memory_space=pltpu.SMEM)
    anyspace = pl.BlockSpec(memory_space=pl.ANY)
    return pl.pallas_call(
        functools.partial(_moe_expert_kernel, layer=layer),
        grid_spec=pltpu.PrefetchScalarGridSpec(
            num_scalar_prefetch=4,
            grid=(n_blocks,),
            in_specs=[smem(lambda i, be, nu, ws, nx: (0, 0, 0)), smem(nxt_blk),
                      anyspace, anyspace, anyspace],
            out_specs=pl.BlockSpec((tm * ns, 128), lambda i, be, nu, ws, nx: (i, 0)),
            scratch_shapes=[pltpu.VMEM((2, tm * ns, 128), U32), pltpu.SemaphoreType.DMA((2,)),
                            pltpu.VMEM((2, d, 2 * D_EXPERT), F32), pltpu.VMEM((2, D_EXPERT, d), F32),
                            pltpu.SemaphoreType.DMA((2, 2)),
                            pltpu.VMEM((d, 2 * D_EXPERT), BF16), pltpu.VMEM((D_EXPERT, d), BF16)],
        ),
        out_shape=jax.ShapeDtypeStruct((n_blocks * tm * ns, 128), U32),
        compiler_params=_cparams("arbitrary"),
        name="moe_experts",
    )(block_expert, n_used, w_slot, nxt, tok_of_row, tok_of_row, h_slabs, w_gate_up, w_down)


def _moe_combine_kernel(d0_ref, d1_ref, d0n_ref, d1n_ref, h_ref, w_ref, y_hbm, g_ref, b_ref, *rest,
                        alpha, with_bf16):
    if with_bf16:
        o_ref, ob_ref, buf, sem = rest
    else:
        o_ref, buf, sem = rest
    i = pl.program_id(0)
    tm, d = h_ref.shape
    nl = _slab_rows(d)
    slot = i % 2

    def gather(a_ref, b_ref_, s):
        def issue(r, carry):
            for k, idx_ref in enumerate((a_ref, b_ref_)):
                src = y_hbm.at[pl.ds(pl.multiple_of(idx_ref[0, 0, r], nl), nl)]
                pltpu.make_async_copy(src, buf.at[s, k, pl.ds(pl.multiple_of(r * nl, nl), nl)],
                                      sem.at[s, k]).start()
            return carry
        lax.fori_loop(0, tm, issue, 0, unroll=8)

    @pl.when(i == 0)
    def _():
        gather(d0_ref, d1_ref, 0)

    for k in range(TOP_K):
        pltpu.make_async_copy(y_hbm.at[pl.ds(0, tm * nl)], buf.at[slot, k], sem.at[slot, k]).wait()

    @pl.when(i + 1 < pl.num_programs(0))
    def _():
        gather(d0n_ref, d1n_ref, 1 - slot)

    w = w_ref[...]
    y0 = _load_row_slabs(lambda idx: buf[slot, 0, idx, :], tm, nl)
    y1 = _load_row_slabs(lambda idx: buf[slot, 1, idx, :], tm, nl)
    cols = [alpha * h_ref[:, c * 128:(c + 1) * 128] + (w[:, 0:1] * y0[c] + w[:, 1:2] * y1[c])
            for c in range(d // 128)]
    out = _layer_norm(jnp.concatenate(cols, axis=1), g_ref[...], b_ref[...])
    o_ref[...] = out
    if with_bf16:
        ob_ref[...] = out.astype(BF16)


def _moe_combine(h, plan, y_buf, ln_g, ln_b, alpha, first_row, with_bf16=False):
    dest, weights = plan[-2], plan[-1]
    lp, d = h.shape
    tm = MIX_TILE
    off = first_row // tm
    nt = lp // tm
    ns = _slab_rows(d)
    d0 = (dest[:, 0] * ns).reshape(nt, 1, tm)
    d1 = (dest[:, 1] * ns).reshape(nt, 1, tm)
    cur = pl.BlockSpec((1, 1, tm), lambda i: (off, 0, 0), memory_space=pltpu.SMEM)
    nxt = pl.BlockSpec((1, 1, tm), lambda i: (jnp.minimum(i + 1 + off, nt - 1), 0, 0),
                       memory_space=pltpu.SMEM)
    out_spec = pl.BlockSpec((tm, d), lambda i: (i, 0))
    out_shape = jax.ShapeDtypeStruct((lp - first_row, d), F32)
    if with_bf16:
        out_spec = [out_spec, out_spec]
        out_shape = [out_shape, jax.ShapeDtypeStruct((lp - first_row, d), BF16)]
    return pl.pallas_call(
        functools.partial(_moe_combine_kernel, alpha=alpha, with_bf16=with_bf16),
        grid=(nt - off,),
        in_specs=[cur, cur, nxt, nxt, pl.BlockSpec((tm, d), lambda i: (i + off, 0)),
                  pl.BlockSpec((tm, TOP_K), lambda i: (i + off, 0)),
                  pl.BlockSpec(memory_space=pl.ANY), _full(ln_g.shape), _full(ln_b.shape)],
        out_specs=out_spec,
        out_shape=out_shape,
        scratch_shapes=[pltpu.VMEM((2, TOP_K, tm * ns, 128), U32),
                        pltpu.SemaphoreType.DMA((2, TOP_K))],
        compiler_params=_cparams("arbitrary"),
        name="moe_combine",
    )(d0, d1, d0, d1, h, weights, y_buf, ln_g, ln_b)


def _store_heads(ref, seg_rows, val):
    for hh in range(val.shape[1] // HG_DK):
        ref[0, hh, seg_rows, :] = val[:, hh * HG_DK:(hh + 1) * HG_DK].astype(ref.dtype)


def _inproj_plain_kernel(x_ref, w_ref, o_ref):
    for r in range(0, x_ref.shape[0], INPROJ_C_SUB):
        rows = slice(r, r + INPROJ_C_SUB)
        _store_heads(o_ref, rows, jnp.dot(x_ref[rows, :], w_ref[...], preferred_element_type=F32))


def _inproj_forget_kernel(x_ref, w_ref, lb_ref, k_ref, g_ref):
    i = pl.program_id(0)
    tm = x_ref.shape[0]
    lb = lb_ref[...]
    for r in range(0, tm, INPROJ_C_SUB):
        rows = slice(r, r + INPROJ_C_SUB)
        acc = jnp.dot(x_ref[rows, :], w_ref[...], preferred_element_type=F32)
        f = lb + (1.0 - lb) * jax.nn.sigmoid(acc)
        real = (i * tm + r + lax.broadcasted_iota(jnp.int32, (INPROJ_C_SUB, 1), 0)) >= META_ROW0
        _store_heads(k_ref, rows, jnp.where(real, 1.0 - f, 0.0))
        _store_heads(g_ref, rows, jnp.where(real, jnp.log(f), 0.0))


def _inproj_c(hb, w, lb):
    lp, d = hb.shape
    tm = _pick(lp, INPROJ_C_TILES)
    tn = 1024
    tps = d // tn
    hpt = tn // HG_DK
    x_spec = pl.BlockSpec((tm, d), lambda i, j: (i, 0))
    out_spec = pl.BlockSpec((1, hpt, tm, HG_DK), lambda i, j: (j // tps, j % tps, i, 0))
    hm = lambda n, dt: jax.ShapeDtypeStruct((n, HG_HEADS, lp, HG_DK), dt)
    plain_col = lambda i, j: (0, jnp.where(j >= 2 * tps, j + 2 * tps, j))
    plain = pl.pallas_call(
        _inproj_plain_kernel,
        grid=(lp // tm, 3 * tps),
        in_specs=[x_spec, pl.BlockSpec((d, tn), plain_col)],
        out_specs=out_spec,
        out_shape=hm(3, BF16),
        compiler_params=_cparams("parallel", "arbitrary"),
        name="inproj_c_plain",
    )(hb, w)
    k, g = pl.pallas_call(
        _inproj_forget_kernel,
        grid=(lp // tm, 2 * tps),
        in_specs=[x_spec, pl.BlockSpec((d, tn), lambda i, j: (0, j + 2 * tps)),
                  pl.BlockSpec((1, tn), lambda i, j: (0, j % tps))],
        out_specs=[out_spec, out_spec],
        out_shape=[hm(2, BF16), hm(2, F32)],
        compiler_params=_cparams("parallel", "arbitrary"),
        name="inproj_c_forget",
    )(hb, w, lb)
    return plain, k, g


def _hgrn_chunk(q, k, v, g, state_t, reverse):
    c = HG_CHUNK
    row = lax.broadcasted_iota(jnp.int32, (c, HG_DK), 0)
    b = g
    sh = 1
    while sh < c:
        if not reverse:
            b = b + jnp.where(row >= sh, pltpu.roll(b, sh, axis=0), 0.0)
        else:
            b = b + jnp.where(row < c - sh, pltpu.roll(b, c - sh, axis=0), 0.0)
        sh *= 2
    r_i = lax.broadcasted_iota(jnp.int32, (c, c), 0)
    c_i = lax.broadcasted_iota(jnp.int32, (c, c), 1)
    causal = (c_i >= r_i) if reverse else (c_i <= r_i)
    sub = HG_SUB
    mid = sub // 2
    blocks = lambda x: [x[j * sub:(j + 1) * sub] for j in range(HG_NSUB)]
    ref = [b[j * sub + mid:j * sub + mid + 1, :] for j in range(HG_NSUB)]
    b_end = b[0:1, :] if reverse else b[c - 1:c, :]
    qd, ku, q0, k_end = [], [], [], []
    for j, (qj, kj, bj) in enumerate(zip(blocks(q), blocks(k), blocks(b))):
        dj = bj - ref[j]
        qd_j = qj * jnp.exp(jnp.minimum(dj, HG_EXP_CLAMP))
        ku_j = kj * jnp.exp(jnp.minimum(-dj, HG_EXP_CLAMP))
        qd.append(qd_j.astype(BF16))
        ku.append(ku_j)
        q0.append(qd_j * jnp.exp(ref[j]))
        k_end.append(ku_j * jnp.exp(b_end - ref[j]))
    nt = (((1,), (1,)), ((), ()))
    zero = jnp.zeros((sub, HG_DK), BF16)
    score_rows = []
    for j in range(HG_NSUB):
        parts = []
        for i in range(HG_NSUB):
            if i == j:
                parts.append(ku[i].astype(BF16))
            elif (i > j) if reverse else (i < j):
                parts.append((ku[i] * jnp.exp(ref[j] - ref[i])).astype(BF16))
            else:
                parts.append(zero)
        k_ext = jnp.concatenate(parts, axis=0)
        score_rows.append(lax.dot_general(qd[j], k_ext, nt, preferred_element_type=F32))
    scores = jnp.concatenate(score_rows, axis=0)
    scores = jnp.where(causal, scores, 0.0)
    o = jnp.dot(scores.astype(BF16), v.astype(BF16), preferred_element_type=F32)
    o += lax.dot_general(jnp.concatenate(q0, axis=0).astype(BF16), state_t.astype(BF16), nt,
                         preferred_element_type=F32)
    new_state = jnp.exp(b_end) * state_t + jnp.dot(
        v.T.astype(BF16), jnp.concatenate(k_end, axis=0).astype(BF16), preferred_element_type=F32)
    return o, new_state


def _hgrn_kernel(qf_ref, vf_ref, k1_ref, g1_ref, qb_ref, vb_ref, k2_ref, g2_ref,
                 ofw_ref, obw_ref, st_ref):
    @pl.when(pl.program_id(0) == 0)
    def _():
        st_ref[...] = jnp.zeros_like(st_ref)

    def head(hh, carry):
        o, s = _hgrn_chunk(qf_ref[0, hh].astype(F32), k1_ref[0, hh].astype(F32),
                           vf_ref[0, hh].astype(F32), g1_ref[0, hh], st_ref[0, hh], reverse=False)
        ofw_ref[hh] = o
        st_ref[0, hh] = s
        o, s = _hgrn_chunk(qb_ref[0, hh].astype(F32), k2_ref[0, hh].astype(F32),
                           vb_ref[0, hh].astype(F32), g2_ref[0, hh], st_ref[1, hh], reverse=True)
        obw_ref[hh] = o
        st_ref[1, hh] = s
        return carry

    lax.fori_loop(0, HG_HEADS, head, 0, unroll=4)


def _hgrn(plain, k, g):
    _, _, lp, dk = plain.shape
    c = HG_CHUNK
    nc = lp // c
    fwd = lambda seg: pl.BlockSpec((1, HG_HEADS, c, dk), lambda i: (seg, 0, i, 0))
    bwd = lambda seg: pl.BlockSpec((1, HG_HEADS, c, dk), lambda i: (seg, 0, nc - 1 - i, 0))
    out_f = pl.BlockSpec((HG_HEADS, c, dk), lambda i: (0, i, 0))
    out_b = pl.BlockSpec((HG_HEADS, c, dk), lambda i: (0, nc - 1 - i, 0))
    return pl.pallas_call(
        _hgrn_kernel,
        grid=(nc,),
        in_specs=[fwd(0), fwd(1), fwd(0), fwd(0), bwd(0), bwd(1), bwd(1), bwd(1)],
        out_specs=[out_f, out_b],
        out_shape=[jax.ShapeDtypeStruct((HG_HEADS, lp, dk), F32)] * 2,
        scratch_shapes=[pltpu.VMEM((2, HG_HEADS, dk, dk), F32)],
        compiler_params=_cparams("arbitrary"),
        name="hgrn2_recurrence",
    )(plain, plain, k, g, plain, plain, k, g)


def _route_params(w_group, b_group, w_expert, b_expert):
    d = w_group.shape[0]
    pad = ROUTE_LANES - N_GROUPS - N_EXPERTS
    w = jnp.concatenate([w_group, w_expert, jnp.zeros((d, pad), F32)], axis=1)
    b = jnp.concatenate([b_group, b_expert, jnp.zeros((pad,), F32)])[None, :]
    w_hi = w.astype(BF16)
    w_lo = (w - w_hi.astype(F32)).astype(BF16)
    return jnp.concatenate([w_hi, w_lo], axis=1), b


def _rope_tables(lp):
    half = HEAD_DIM // 2
    pos = jnp.maximum(jnp.arange(lp) - META_ROW0, 0).astype(F32)
    inv = ROPE_THETA ** (-jnp.arange(half, dtype=F32) * 2.0 / HEAD_DIM)
    ang = pos[:, None] * inv[None, :]
    cos, sin = jnp.cos(ang), jnp.sin(ang)
    return jnp.concatenate([cos, cos], axis=1), jnp.concatenate([-sin, sin], axis=1)


def kernel(x, meta_tokens, w_in_ab, w_out_ab, attn_sinks, s5_lam_re, s5_lam_im, s5_log_step, s5_b_re, s5_b_im, s5_c_re, s5_c_im, s5_d, s5_w_glu, s5_b_glu, w_in_c, w_out_c, hgrn_lb_logits, hgrn_norm_g, ln_mix_g, ln_mix_b, ln_ffn_g, ln_ffn_b, moe_w_group, moe_b_group, moe_w_expert, moe_b_expert, moe_w_gate_up, moe_w_down):
    batch, seq, d = x.shape
    assert batch == 1 and seq % FRONT == 0
    depth = ln_mix_g.shape[0]
    assert depth == 2
    alpha = (2.0 * depth) ** 0.25
    lp = FRONT + seq
    row2 = lambda t: t[None, :]

    front = jnp.concatenate([jnp.zeros((META_ROW0, d), F32), meta_tokens.astype(F32)], axis=0)
    xs = x[0]

    w_in = w_in_ab[0].astype(BF16)
    s5w = s5_d.shape[1]
    wq, wk, wv, wu = (w_in[:, :Q_DIM], w_in[:, Q_DIM:Q_DIM + KV_DIM],
                      w_in[:, Q_DIM + KV_DIM:Q_DIM + 2 * KV_DIM], w_in[:, Q_DIM + 2 * KV_DIM:])
    cos2, sin2 = _rope_tables(lp)
    q, k, v, u = _inproj_ab(front, xs, wq, wk, wv, wu, cos2, sin2)
    a_out = _window_attention(q, k, v, attn_sinks[0])
    mats = _s5_discretise(s5_lam_re[0], s5_lam_im[0], s5_log_step[0], s5_b_re[0], s5_b_im[0],
                          s5_c_re[0], s5_c_im[0])
    y_lo, y_hi = _s5_mixer_pre_glu(u, s5_d[0], mats)
    wo = w_out_ab[0].astype(BF16)
    w_route, b_route = _route_params(moe_w_group[0], moe_b_group[0], moe_w_expert[0], moe_b_expert[0])
    h, h_slabs, route = _mix_ab(front, xs, a_out, y_lo, y_hi, s5_w_glu[0].astype(BF16), row2(s5_b_glu[0]),
                       wo[:Q_DIM], wo[Q_DIM:], row2(ln_mix_g[0]), row2(ln_mix_b[0]), w_route, b_route,
                       alpha)
    plan = _moe_plan(route, META_ROW0)
    y_buf = _moe_experts(h_slabs, plan, moe_w_gate_up, moe_w_down, 0)
    h, hb = _moe_combine(h, plan, y_buf, row2(ln_ffn_g[0]), row2(ln_ffn_b[0]), alpha, 0, with_bf16=True)

    lb_probs = jax.nn.softmax(hgrn_lb_logits.astype(F32), axis=0)
    lb = (jnp.cumsum(lb_probs, axis=0) - lb_probs[0])[1]
    plain, hk, hg = _inproj_c(hb, w_in_c[0].astype(BF16), row2(lb))
    o_fw, o_bw = _hgrn(plain, hk, hg)
    w_route, b_route = _route_params(moe_w_group[1], moe_b_group[1], moe_w_expert[1], moe_b_expert[1])
    h, h_slabs, route = _mix_c(h, o_fw, o_bw, plain, hgrn_norm_g[0].reshape(HG_HEADS, 1, HG_DK),
                      w_out_c[0].astype(BF16), row2(ln_mix_g[1]), row2(ln_mix_b[1]), w_route, b_route,
                      alpha)
    plan = _moe_plan(route, META_ROW0)
    y_buf = _moe_experts(h_slabs, plan, moe_w_gate_up, moe_w_down, 1)
    out = _moe_combine(h, plan, y_buf, row2(ln_ffn_g[1]), row2(ln_ffn_b[1]), alpha, FRONT)
    return out[None]
```

```python
import functools
import math

import jax
import jax.numpy as jnp
from jax import lax
from jax.experimental import pallas as pl
from jax.experimental.pallas import tpu as pltpu

F32 = jnp.float32
BF16 = jnp.bfloat16

N_META = 16
FRONT = 512
META_ROW0 = FRONT - N_META

ATTN_HEADS = 8
ATTN_KV_HEADS = 2
ATTN_GROUP = ATTN_HEADS // ATTN_KV_HEADS
HEAD_DIM = 128
WINDOW = 128
ATTN_BLOCK = 128
ROPE_THETA = 10000.0
Q_DIM = ATTN_HEADS * HEAD_DIM
KV_DIM = ATTN_KV_HEADS * HEAD_DIM

S5_GROUP = 16
S5_STATE = 64
S5_CHUNK = 16
S5_ROW = S5_CHUNK * S5_GROUP
S5_SCAN_BLOCK = 32
S5_CHUNK_BLOCKS = (176, 96, 48, 32, 16)

HG_HEADS = 16
HG_DK = 128
HG_CHUNK = 256
HG_SUB = 32
HG_NSUB = HG_CHUNK // HG_SUB
HG_EXP_CLAMP = 80.0

N_GROUPS = 8
EXPERTS_PER_GROUP = 8
N_EXPERTS = N_GROUPS * EXPERTS_PER_GROUP
TOP_K = 2
D_EXPERT = 512
MOE_TM = 256
ROUTE_LANES = 128

LN_EPS = 1e-5
RMS_EPS = 1e-6
NEG_INF = -1e30

ROW_TILE = 512
MIX_TILE = 256
MIX_SUB = 128
INPROJ_C_TILES = (1536, 1024, 512)
INPROJ_C_SUB = 512

VMEM_LIMIT = 56 * 1024 * 1024


def _cparams(*sem):
    return pltpu.CompilerParams(dimension_semantics=sem, vmem_limit_bytes=VMEM_LIMIT)


def _pick(n, candidates):
    return next(c for c in candidates if n % c == 0)


def _full(shape):
    nd = len(shape)
    return pl.BlockSpec(shape, lambda *_: (0,) * nd)


def _layer_norm(x, g, b):
    mu = jnp.mean(x, axis=-1, keepdims=True)
    xc = x - mu
    var = jnp.mean(xc * xc, axis=-1, keepdims=True)
    return xc * lax.rsqrt(var + LN_EPS) * g + b


def _inproj_ab_kernel(front_ref, x_ref, wq_ref, wk_ref, wv_ref, wu_ref, cos_ref, sin_ref,
                      q_ref, k_ref, v_ref, u_ref):
    xb = jnp.where(pl.program_id(0) == 0, front_ref[...], x_ref[...]).astype(BF16)
    cos = cos_ref[...]
    sin = sin_ref[...]

    def rope(t):
        return t * cos + pltpu.roll(t, HEAD_DIM // 2, axis=1) * sin

    q = jnp.dot(xb, wq_ref[...], preferred_element_type=F32)
    for h in range(ATTN_HEADS):
        sl = slice(h * HEAD_DIM, (h + 1) * HEAD_DIM)
        q_ref[:, sl] = rope(q[:, sl]).astype(BF16)
    k = jnp.dot(xb, wk_ref[...], preferred_element_type=F32)
    for h in range(ATTN_KV_HEADS):
        sl = slice(h * HEAD_DIM, (h + 1) * HEAD_DIM)
        k_ref[:, sl] = rope(k[:, sl]).astype(BF16)
    v_ref[...] = jnp.dot(xb, wv_ref[...], preferred_element_type=F32).astype(BF16)
    u_ref[...] = jnp.dot(xb, wu_ref[...], preferred_element_type=F32)


def _inproj_ab(front, x, wq, wk, wv, wu, cos2, sin2):
    d = x.shape[1]
    lp = front.shape[0] + x.shape[0]
    tm = ROW_TILE
    assert front.shape[0] == tm
    s5w = wu.shape[1]
    row = lambda w: pl.BlockSpec((tm, w), lambda i: (i, 0))
    return pl.pallas_call(
        _inproj_ab_kernel,
        grid=(lp // tm,),
        in_specs=[_full(front.shape), pl.BlockSpec((tm, d), lambda i: (jnp.maximum(i - 1, 0), 0)),
                  _full(wq.shape), _full(wk.shape), _full(wv.shape), _full(wu.shape),
                  row(HEAD_DIM), row(HEAD_DIM)],
        out_specs=[row(Q_DIM), row(KV_DIM), row(KV_DIM), row(s5w)],
        out_shape=[jax.ShapeDtypeStruct((lp, Q_DIM), BF16),
                   jax.ShapeDtypeStruct((lp, KV_DIM), BF16),
                   jax.ShapeDtypeStruct((lp, KV_DIM), BF16),
                   jax.ShapeDtypeStruct((lp, s5w), F32)],
        compiler_params=_cparams("parallel"),
        name="inproj_ab",
    )(front, x, wq, wk, wv, wu, cos2, sin2)


def _attn_kernel(sink_ref, q_ref, k0_ref, k1_ref, k2_ref, v0_ref, v1_ref, v2_ref, km_ref, vm_ref,
                 o_ref, *, lp):
    qb = pl.program_id(0)
    blk = ATTN_BLOCK
    rows = ATTN_GROUP * blk
    scale = HEAD_DIM ** -0.5
    q_row = qb * blk + lax.broadcasted_iota(jnp.int32, (rows, 3 * blk), 0) % blk
    k_row = (qb - 1) * blk + lax.broadcasted_iota(jnp.int32, (rows, 3 * blk), 1)
    vis = (k_row >= FRONT) & (k_row < lp) & (jnp.abs(q_row - k_row) <= WINDOW)
    head_of_row = lax.broadcasted_iota(jnp.int32, (rows, 1), 0) // blk
    for g in range(ATTN_KV_HEADS):
        gs = slice(g * HEAD_DIM, (g + 1) * HEAD_DIM)
        qs = jnp.concatenate(
            [q_ref[:, (g * ATTN_GROUP + r) * HEAD_DIM:(g * ATTN_GROUP + r + 1) * HEAD_DIM]
             for r in range(ATTN_GROUP)], axis=0)
        kband = jnp.concatenate([k0_ref[:, gs], k1_ref[:, gs], k2_ref[:, gs]], axis=0)
        vband = jnp.concatenate([v0_ref[:, gs], v1_ref[:, gs], v2_ref[:, gs]], axis=0)
        nt = (((1,), (1,)), ((), ()))
        s_band = lax.dot_general(qs, kband, nt, preferred_element_type=F32) * scale
        s_band = jnp.where(vis, s_band, NEG_INF)
        s_meta = lax.dot_general(qs, km_ref[:, gs], nt, preferred_element_type=F32) * scale
        sink = jnp.zeros((rows, 1), F32)
        for r in range(ATTN_GROUP):
            sink = jnp.where(head_of_row == r, sink_ref[g * ATTN_GROUP + r], sink)
        m = jnp.maximum(jnp.maximum(jnp.max(s_band, axis=-1, keepdims=True),
                                    jnp.max(s_meta, axis=-1, keepdims=True)), sink)
        e_band = jnp.exp(s_band - m)
        e_meta = jnp.exp(s_meta - m)
        denom = (jnp.sum(e_band, axis=-1, keepdims=True) + jnp.sum(e_meta, axis=-1, keepdims=True)
                 + jnp.exp(sink - m))
        o = (jnp.dot(e_band.astype(BF16), vband, preferred_element_type=F32)
             + jnp.dot(e_meta.astype(BF16), vm_ref[:, gs], preferred_element_type=F32)) / denom
        for r in range(ATTN_GROUP):
            hh = g * ATTN_GROUP + r
            o_ref[:, hh * HEAD_DIM:(hh + 1) * HEAD_DIM] = o[r * blk:(r + 1) * blk].astype(BF16)


def _window_attention(q, k, v, sinks):
    lp = q.shape[0]
    blk = ATTN_BLOCK
    nb = lp // blk
    qspec = pl.BlockSpec((blk, Q_DIM), lambda i, s: (i, 0))
    kv = lambda off: pl.BlockSpec((blk, KV_DIM), lambda i, s: (jnp.clip(i + off, 0, nb - 1), 0))
    meta = pl.BlockSpec((N_META, KV_DIM), lambda i, s: (META_ROW0 // N_META, 0))
    return pl.pallas_call(
        functools.partial(_attn_kernel, lp=lp),
        grid_spec=pltpu.PrefetchScalarGridSpec(
            num_scalar_prefetch=1,
            grid=(nb,),
            in_specs=[qspec, kv(-1), kv(0), kv(1), kv(-1), kv(0), kv(1), meta, meta],
            out_specs=qspec,
        ),
        out_shape=jax.ShapeDtypeStruct((lp, Q_DIM), BF16),
        compiler_params=_cparams("parallel"),
        name="window_attention",
    )(sinks, q, k, k, k, v, v, v, k, v)


def _s5_discretise(lam_re, lam_im, log_step, b_re, b_im, c_re, c_im):
    t = S5_CHUNK
    hi = lax.Precision.HIGHEST
    g = lam_re.shape[1]
    lr = jnp.minimum(lam_re, -1e-4)
    li = lam_im
    step = jnp.exp(log_step)[..., None]
    dr, di = lr * step, li * step
    lags = jnp.arange(t + 1, dtype=F32)[:, None]
    mag = jnp.exp(dr[..., None, :] * lags)
    pr, pi = mag * jnp.cos(di[..., None, :] * lags), mag * jnp.sin(di[..., None, :] * lags)
    ar, ai = pr[:, :, 1], pi[:, :, 1]
    den = lr * lr + li * li
    zr = ((ar - 1.0) * lr + ai * li) / den
    zi = (ai * lr - (ar - 1.0) * li) / den
    br = zr[..., None] * b_re - zi[..., None] * b_im
    bi = zr[..., None] * b_im + zi[..., None] * b_re
    brt, bit = br.transpose(0, 1, 3, 2), bi.transpose(0, 1, 3, 2)
    ct_r, ct_i = c_re.transpose(0, 1, 3, 2), c_im.transpose(0, 1, 3, 2)
    pt_r, pt_i = pr[:, :, :t].transpose(0, 1, 3, 2), pi[:, :, :t].transpose(0, 1, 3, 2)
    xr = (ct_r[:, :, :, None] * pt_r[..., None] - ct_i[:, :, :, None] * pt_i[..., None])
    xi = (ct_r[:, :, :, None] * pt_i[..., None] + ct_i[:, :, :, None] * pt_r[..., None])
    x_cat = jnp.concatenate([xr, xi], axis=2).reshape(2, g, 2 * S5_STATE, t * S5_GROUP)
    b_cat = jnp.concatenate([brt, -bit], axis=3)
    kern = jnp.einsum('dgip,dgpn->dgin', b_cat, x_cat, precision=hi).reshape(2, g, S5_GROUP, t, S5_GROUP)
    k_lag = jnp.concatenate([jnp.flip(kern[1][:, :, 1:], axis=2), kern[0][:, :, :1] + kern[1][:, :, :1],
                             kern[0][:, :, 1:]], axis=2)
    k_lag = k_lag.reshape(g, S5_GROUP, (2 * t - 1) * S5_GROUP)

    def flat_e(p_r, p_i, d):
        e_r = p_r[:, :, None] * brt[d][:, None] - p_i[:, :, None] * bit[d][:, None]
        e_i = p_r[:, :, None] * bit[d][:, None] + p_i[:, :, None] * brt[d][:, None]
        flat = lambda e: e.reshape(g, t * S5_GROUP, S5_STATE)
        return jnp.concatenate([flat(e_r), flat(e_i)], axis=-1)
    e_mat = jnp.concatenate([flat_e(pr[0][:, ::-1][:, 1:], pi[0][:, ::-1][:, 1:], 0),
                             flat_e(pr[1][:, :t], pi[1][:, :t], 1)], axis=-1)

    def flat_f(p_r, p_i, d):
        ct_r, ct_i = c_re[d].transpose(0, 2, 1), c_im[d].transpose(0, 2, 1)
        pt_r, pt_i = p_r.transpose(0, 2, 1)[..., None], p_i.transpose(0, 2, 1)[..., None]
        w_r = ct_r[:, :, None] * pt_r - ct_i[:, :, None] * pt_i
        w_i = ct_r[:, :, None] * pt_i + ct_i[:, :, None] * pt_r
        flat = lambda w: w.reshape(g, S5_STATE, t * S5_GROUP)
        return jnp.concatenate([flat(w_r), -flat(w_i)], axis=1)
    f_fw = flat_f(pr[0][:, 1:], pi[0][:, 1:], 0)
    f_bw = flat_f(pr[1][:, ::-1][:, :t], pi[1][:, ::-1][:, :t], 1)
    gh = g // 2
    upper = (jnp.arange(g) >= gh)[:, None, None]
    def place(m):
        z = jnp.zeros_like(m)
        return jnp.concatenate([jnp.where(upper, z, m), jnp.where(upper, m, z)], axis=1)
    p = S5_STATE
    f_mat = jnp.concatenate([place(f_fw[:, :p]), place(f_fw[:, p:]), place(f_bw[:, :p]), place(f_bw[:, p:])],
                            axis=1)
    pack = lambda v: jnp.concatenate([v[:gh], v[gh:]], axis=-1)
    coef = jnp.stack([pack(pr[0][:, t]), pack(pi[0][:, t]), pack(pr[1][:, t]), pack(pi[1][:, t])])
    split = lambda m: m.reshape((2, gh) + m.shape[1:])
    return split(k_lag), split(e_mat.astype(BF16)), split(f_mat.astype(BF16)), coef


def _chunk_rows_to_lanes(u_ref, ncb):
    lane_seg = lax.broadcasted_iota(jnp.int32, (ncb, 128), 1) // S5_GROUP
    rows = [u_ref[pl.ds(s, ncb, stride=S5_CHUNK), :] for s in range(S5_CHUNK)]
    segs = 128 // S5_GROUP
    out = []
    for j in range(segs):
        halves = []
        for h in range(S5_CHUNK // segs):
            acc = None
            for s8 in range(segs):
                r = rows[h * segs + s8]
                k = (s8 - j) % segs
                if k:
                    r = pltpu.roll(r, S5_GROUP * k, axis=1)
                acc = r if acc is None else jnp.where(lane_seg == s8, r, acc)
            halves.append(acc)
        out.append(jnp.concatenate(halves, axis=1))
    return out


def _lanes_to_chunk_rows(ys, y_ref, ncb, add_ref, scale_ref):
    lane_seg = lax.broadcasted_iota(jnp.int32, (ncb, 128), 1) // S5_GROUP
    segs = 128 // S5_GROUP
    for t in range(S5_CHUNK):
        h, t8 = divmod(t, segs)
        acc = None
        for j in range(segs):
            r = ys[j][:, h * 128:(h + 1) * 128]
            k = (j - t8) % segs
            if k:
                r = pltpu.roll(r, S5_GROUP * k, axis=1)
            acc = r if acc is None else jnp.where(lane_seg == j, r, acc)
        idx = pl.ds(t, ncb, stride=S5_CHUNK)
        y_ref[idx, :] = acc + add_ref[idx, :] * scale_ref[...]


def _pair_halves(a, b):
    lane = lax.broadcasted_iota(jnp.int32, a.shape, 1)
    lo = jnp.where(lane < S5_STATE, a, pltpu.roll(b, S5_STATE, axis=1))
    hi = jnp.where(lane < S5_STATE, pltpu.roll(a, S5_STATE, axis=1), b)
    return lo, hi


def _s5_local_kernel(ua_ref, ub_ref, e_ref, u2_ref, f1_ref, f2_ref, b1_ref, b2_ref):
    ncb = u2_ref.shape[2]
    p2 = 2 * S5_STATE
    chunks = [_chunk_rows_to_lanes(ua_ref, ncb), _chunk_rows_to_lanes(ub_ref, ncb)]
    for j in range(u2_ref.shape[1]):
        st = []
        for hh in range(2):
            u2 = chunks[hh][j].astype(BF16)
            u2_ref[hh, j] = u2
            st.append(jnp.dot(u2, e_ref[hh, j], preferred_element_type=F32))
        f1_ref[:, j, :], f2_ref[:, j, :] = _pair_halves(st[0][:, :p2], st[1][:, :p2])
        b1_ref[:, j, :], b2_ref[:, j, :] = _pair_halves(st[0][:, p2:], st[1][:, p2:])


def _s5_scan_kernel(sf1_ref, sf2_ref, sb1_ref, sb2_ref, coef_ref, xf1_ref, xf2_ref, xb1_ref, xb2_ref,
                    st_ref):
    @pl.when(pl.program_id(0) == 0)
    def _():
        st_ref[...] = jnp.zeros_like(st_ref)

    nb = sf1_ref.shape[0]
    crf, cif, crb, cib = coef_ref[0], coef_ref[1], coef_ref[2], coef_ref[3]

    def body(i, carry):
        f1, f2, b1, b2 = carry
        j = nb - 1 - i
        xf1_ref[i] = f1
        xf2_ref[i] = f2
        xb1_ref[j] = b1
        xb2_ref[j] = b2
        return (crf * f1 - cif * f2 + sf1_ref[i], crf * f2 + cif * f1 + sf2_ref[i],
                crb * b1 - cib * b2 + sb1_ref[j], crb * b2 + cib * b1 + sb2_ref[j])

    out = lax.fori_loop(0, nb, body, (st_ref[0], st_ref[1], st_ref[2], st_ref[3]))
    for q in range(4):
        st_ref[q] = out[q]


def _chunk_matrix(k_lag):
    t, c = S5_CHUNK, S5_GROUP
    return jnp.concatenate([k_lag[:, (t - 1 - s) * c:(2 * t - 1 - s) * c] for s in range(t)],
                           axis=0).astype(BF16)


def _s5_readout_kernel(ua_ref, ub_ref, d_ref, u2_ref, kl_ref, f_ref, xf1_ref, xf2_ref, xb1_ref, xb2_ref,
                       ya_ref, yb_ref):
    ncb = u2_ref.shape[2]
    ys = [[], []]
    for j in range(u2_ref.shape[1]):
        x = jnp.concatenate([xf1_ref[:, j, :], xf2_ref[:, j, :], xb1_ref[:, j, :], xb2_ref[:, j, :]],
                            axis=1).astype(BF16)
        for hh in range(2):
            y = jnp.dot(u2_ref[hh, j], _chunk_matrix(kl_ref[hh, j]), preferred_element_type=F32)
            ys[hh].append(y + jnp.dot(x, f_ref[hh, j], preferred_element_type=F32))
    _lanes_to_chunk_rows(ys[0], ya_ref, ncb, ua_ref, d_ref.at[0])
    _lanes_to_chunk_rows(ys[1], yb_ref, ncb, ub_ref, d_ref.at[1])


def _s5_mixer_pre_glu(u, d_skip, mats):
    k_lag, e_mat, f_mat, coef = mats
    lp, w = u.shape
    g = w // S5_GROUP
    gh = g // 2
    gpb = 128 // S5_GROUP
    nc = lp // S5_CHUNK
    ncb = _pick(nc, S5_CHUNK_BLOCKS)
    rows = ncb * S5_CHUNK
    nlb = w // 128
    grid = (nlb // 2, nc // ncb)
    u_lo = pl.BlockSpec((rows, 128), lambda b, r: (r, b))
    u_hi = pl.BlockSpec((rows, 128), lambda b, r: (r, b + nlb // 2))
    per_group = lambda k, n: pl.BlockSpec((2, gpb, k, n), lambda b, r: (0, b, 0, 0))
    u2_spec = pl.BlockSpec((2, gpb, ncb, S5_ROW), lambda b, r: (0, b, r, 0))
    st_spec = pl.BlockSpec((ncb, gpb, 2 * S5_STATE), lambda b, r: (r, b, 0))
    st_shape = jax.ShapeDtypeStruct((nc, gh, 2 * S5_STATE), F32)
    u2, sf1, sf2, sb1, sb2 = pl.pallas_call(
        _s5_local_kernel,
        grid=grid,
        in_specs=[u_lo, u_hi, per_group(S5_ROW, 4 * S5_STATE)],
        out_specs=[u2_spec, st_spec, st_spec, st_spec, st_spec],
        out_shape=[jax.ShapeDtypeStruct((2, gh, nc, S5_ROW), BF16)] + [st_shape] * 4,
        compiler_params=_cparams("parallel", "parallel"),
        name="s5_local_states",
    )(u, u, e_mat)
    sb = S5_SCAN_BLOCK
    nblk = nc // sb
    fwd = pl.BlockSpec((sb, gh, 2 * S5_STATE), lambda i: (i, 0, 0))
    bwd = pl.BlockSpec((sb, gh, 2 * S5_STATE), lambda i: (nblk - 1 - i, 0, 0))
    xf1, xf2, xb1, xb2 = pl.pallas_call(
        _s5_scan_kernel,
        grid=(nblk,),
        in_specs=[fwd, fwd, bwd, bwd, _full(coef.shape)],
        out_specs=[fwd, fwd, bwd, bwd],
        out_shape=[st_shape] * 4,
        scratch_shapes=[pltpu.VMEM((4, gh, 2 * S5_STATE), F32)],
        compiler_params=_cparams("arbitrary"),
        name="s5_chunk_scan",
    )(sf1, sf2, sb1, sb2, coef)
    d2 = d_skip.reshape(2, 1, nlb // 2 * 128)
    y_lo, y_hi = pl.pallas_call(
        _s5_readout_kernel,
        grid=grid,
        in_specs=[u_lo, u_hi, pl.BlockSpec((2, 1, 128), lambda b, r: (0, 0, b)), u2_spec,
                  per_group(S5_GROUP, k_lag.shape[-1]), per_group(8 * S5_STATE, S5_ROW),
                  st_spec, st_spec, st_spec, st_spec],
        out_specs=[pl.BlockSpec((rows, 128), lambda b, r: (r, b))] * 2,
        out_shape=[jax.ShapeDtypeStruct((lp, w // 2), F32)] * 2,
        compiler_params=_cparams("parallel", "parallel"),
        name="s5_readout",
    )(u, u, d2, u2, k_lag, f_mat, xf1, xf2, xb1, xb2)
    return y_lo, y_hi


def _route(hn, wr_ref, br_ref):
    hn_hi = hn.astype(BF16)
    hn_lo = (hn - hn_hi.astype(F32)).astype(BF16)
    p_hi = jnp.dot(hn_hi, wr_ref[...], preferred_element_type=F32)
    p_lo = jnp.dot(hn_lo, wr_ref[:, :ROUTE_LANES], preferred_element_type=F32)
    logits = p_hi[:, :ROUTE_LANES] + (p_hi[:, ROUTE_LANES:] + p_lo) + br_ref[...]
    lane_i = lax.broadcasted_iota(jnp.int32, logits.shape, 1)
    lane = lane_i.astype(F32)
    lane_grp = ((lane_i - N_GROUPS) // EXPERTS_PER_GROUP).astype(F32)
    big = float(ROUTE_LANES)
    g_log = jnp.where(lane_i < N_GROUPS, logits, -jnp.inf)
    g_max = jnp.max(g_log, axis=-1, keepdims=True)
    grp = jnp.min(jnp.where(g_log == g_max, lane, big), axis=-1, keepdims=True)
    p_grp = 1.0 / jnp.sum(jnp.exp(g_log - g_max), axis=-1, keepdims=True)
    in_grp = (lane_i >= N_GROUPS) & (lane_i < N_GROUPS + N_EXPERTS) & (lane_grp == grp)
    e_log = jnp.where(in_grp, logits, -jnp.inf)
    v1 = jnp.max(e_log, axis=-1, keepdims=True)
    i1 = jnp.min(jnp.where(e_log == v1, lane, big), axis=-1, keepdims=True)
    e_log2 = jnp.where(lane == i1, -jnp.inf, e_log)
    v2 = jnp.max(e_log2, axis=-1, keepdims=True)
    i2 = jnp.min(jnp.where(e_log2 == v2, lane, big), axis=-1, keepdims=True)
    e21 = jnp.exp(v2 - v1)
    w1 = p_grp / (1.0 + e21)
    w2 = p_grp * e21 / (1.0 + e21)
    route = jnp.where(lane_i == 0, w1, 0.0)
    route = jnp.where(lane_i == 1, w2, route)
    route = jnp.where(lane_i == 2, i1 - N_GROUPS, route)
    route = jnp.where(lane_i == 3, i2 - N_GROUPS, route)
    return route


U32 = jnp.uint32


def _slab_rows(d):
    return d // 256


def _slab_spec(tm, d):
    return pl.BlockSpec((tm * _slab_rows(d), 128), lambda i: (i, 0))


def _slab_shape(rows, d):
    return jax.ShapeDtypeStruct((rows * _slab_rows(d), 128), U32)


def _pack_pair(a, b):
    ua = lax.bitcast_convert_type(a.astype(BF16).astype(F32), U32)
    ub = lax.bitcast_convert_type(b.astype(BF16).astype(F32), U32)
    return ua | (ub >> 16)


def _unpack_pair(u):
    a = lax.bitcast_convert_type(u & jnp.uint32(0xFFFF0000), F32)
    b = lax.bitcast_convert_type(u << 16, F32)
    return a, b


def _store_row_slabs(ref, start, n, val):
    ns = _slab_rows(val.shape[1])
    for k in range(ns):
        word = _pack_pair(val[:, 2 * k * 128:(2 * k + 1) * 128], val[:, (2 * k + 1) * 128:(2 * k + 2) * 128])
        ref[pl.ds(start * ns + k, n, stride=ns), :] = word


def _load_row_slabs(ref_at, n, ns):
    cols = []
    for k in range(ns):
        cols.extend(_unpack_pair(ref_at(pl.ds(k, n, stride=ns))))
    return cols


def _sub_rows(tm):
    return [slice(r, r + MIX_SUB) for r in range(0, tm, MIX_SUB)]


def _mix_ab_kernel(front_ref, x_ref, a_ref, ylo_ref, yhi_ref, wglu_ref, bglu_ref, woa_ref, wos_ref,
                   g_ref, b_ref, wr_ref, br_ref, hn_ref, hs_ref, route_ref, *, alpha):
    in_front = pl.program_id(0) < FRONT // MIX_TILE
    for rows in _sub_rows(x_ref.shape[0]):
        y = jnp.concatenate([ylo_ref[rows, :], yhi_ref[rows, :]], axis=1)
        z = 0.5 * y * (1.0 + jnp.tanh(math.sqrt(2.0 / math.pi) * (y + 0.044715 * (y * y * y))))
        gate = jnp.dot(z.astype(BF16), wglu_ref[...], preferred_element_type=F32) + bglu_ref[...]
        s_out = z * jax.nn.sigmoid(gate)
        mix = (jnp.dot(a_ref[rows, :], woa_ref[...], preferred_element_type=F32)
               + jnp.dot(s_out.astype(BF16), wos_ref[...], preferred_element_type=F32))
        h_in = jnp.where(in_front, front_ref[rows, :], x_ref[rows, :])
        hn = _layer_norm(alpha * h_in + mix, g_ref[...], b_ref[...])
        hn_ref[rows, :] = hn
        _store_row_slabs(hs_ref, rows.start, rows.stop - rows.start, hn)
        route_ref[rows, :] = _route(hn, wr_ref, br_ref)


def _mix_ab(front, x, a_out, y_lo, y_hi, w_glu, b_glu, wo_a, wo_s, ln_g, ln_b, w_route, b_route, alpha):
    d = x.shape[1]
    lp = front.shape[0] + x.shape[0]
    tm = MIX_TILE
    nf = front.shape[0] // tm
    row = lambda c: pl.BlockSpec((tm, c), lambda i: (i, 0))
    return pl.pallas_call(
        functools.partial(_mix_ab_kernel, alpha=alpha),
        grid=(lp // tm,),
        in_specs=[pl.BlockSpec((tm, d), lambda i: (jnp.minimum(i, nf - 1), 0)),
                  pl.BlockSpec((tm, d), lambda i: (jnp.maximum(i - nf, 0), 0)), row(Q_DIM), row(y_lo.shape[1]), row(y_hi.shape[1]), _full(w_glu.shape),
                  _full(b_glu.shape), _full(wo_a.shape), _full(wo_s.shape), _full(ln_g.shape),
                  _full(ln_b.shape), _full(w_route.shape), _full(b_route.shape)],
        out_specs=[row(d), _slab_spec(tm, d), row(ROUTE_LANES)],
        out_shape=[jax.ShapeDtypeStruct((lp, d), F32), _slab_shape(lp, d),
                   jax.ShapeDtypeStruct((lp, ROUTE_LANES), F32)],
        compiler_params=_cparams("parallel"),
        name="mix_ab",
    )(front, x, a_out, y_lo, y_hi, w_glu, b_glu, wo_a, wo_s, ln_g, ln_b, w_route, b_route)


def _mix_c_kernel(h_ref, ofw_ref, obw_ref, gate_ref, ng_ref, wo_ref, g_ref, b_ref, wr_ref, br_ref,
                  hn_ref, hs_ref, route_ref, *, alpha):
    for rows in _sub_rows(h_ref.shape[0]):
        parts = []
        for hh in range(HG_HEADS):
            o = ofw_ref[hh, rows, :] + obw_ref[hh, rows, :]
            o = o * lax.rsqrt(jnp.mean(o * o, axis=-1, keepdims=True) + RMS_EPS)
            o = o * ng_ref[hh] * jax.nn.sigmoid(gate_ref[0, hh, rows, :].astype(F32))
            parts.append(o.astype(BF16))
        mix = jnp.dot(jnp.concatenate(parts, axis=1), wo_ref[...], preferred_element_type=F32)
        hn = _layer_norm(alpha * h_ref[rows, :] + mix, g_ref[...], b_ref[...])
        hn_ref[rows, :] = hn
        _store_row_slabs(hs_ref, rows.start, rows.stop - rows.start, hn)
        route_ref[rows, :] = _route(hn, wr_ref, br_ref)


def _mix_c(h, o_fw, o_bw, plain, norm_g, wo, ln_g, ln_b, w_route, b_route, alpha):
    lp, d = h.shape
    tm = MIX_TILE
    row = lambda c: pl.BlockSpec((tm, c), lambda i: (i, 0))
    hm = pl.BlockSpec((HG_HEADS, tm, HG_DK), lambda i: (0, i, 0))
    gate = pl.BlockSpec((1, HG_HEADS, tm, HG_DK), lambda i: (2, 0, i, 0))
    return pl.pallas_call(
        functools.partial(_mix_c_kernel, alpha=alpha),
        grid=(lp // tm,),
        in_specs=[row(d), hm, hm, gate, _full(norm_g.shape), _full(wo.shape), _full(ln_g.shape),
                  _full(ln_b.shape), _full(w_route.shape), _full(b_route.shape)],
        out_specs=[row(d), _slab_spec(tm, d), row(ROUTE_LANES)],
        out_shape=[jax.ShapeDtypeStruct((lp, d), F32), _slab_shape(lp, d),
                   jax.ShapeDtypeStruct((lp, ROUTE_LANES), F32)],
        compiler_params=_cparams("parallel"),
        name="mix_c",
    )(h, o_fw, o_bw, plain, norm_g, wo, ln_g, ln_b, w_route, b_route)


def _moe_plan(route, n_rows_valid_from):
    lp = route.shape[0]
    tm = MOE_TM
    valid = (jnp.arange(lp) >= n_rows_valid_from)
    expert = route[:, 2:4].astype(jnp.int32)
    flat_e = jnp.where(valid[:, None], expert, N_EXPERTS).reshape(-1)
    onehot = (flat_e[:, None] == jnp.arange(N_EXPERTS)[None, :]).astype(jnp.int32)
    csum = jnp.cumsum(onehot, axis=0)
    counts = csum[-1]
    rank = jnp.sum(jnp.where(onehot > 0, csum - 1, 0), axis=1)
    padded = (counts + tm - 1) // tm * tm
    pend = jnp.cumsum(padded)
    pstart = pend - padded
    e_safe = jnp.minimum(flat_e, N_EXPERTS - 1)
    dest = jnp.where(flat_e < N_EXPERTS, pstart[e_safe] + rank, 0)
    n_tokens = lp - n_rows_valid_from
    n_blocks = -(-(n_tokens * TOP_K + N_EXPERTS * (tm - 1)) // tm)
    rows = n_blocks * tm
    tok = jnp.repeat(jnp.arange(lp, dtype=jnp.int32), TOP_K)
    scatter_to = jnp.where(flat_e < N_EXPERTS, dest, rows)
    tok_of_row = jnp.zeros((rows,), jnp.int32).at[scatter_to].set(tok, mode='drop')
    n_used = (pend[-1] // tm).astype(jnp.int32)
    blk = jnp.minimum(jnp.arange(n_blocks, dtype=jnp.int32), jnp.maximum(n_used - 1, 0))
    block_expert = jnp.minimum(jnp.searchsorted(pend, blk * tm, side='right'),
                               N_EXPERTS - 1).astype(jnp.int32)
    used = counts > 0
    pos = jnp.cumsum(used.astype(jnp.int32)) - 1
    eid = jnp.arange(N_EXPERTS, dtype=jnp.int32)
    later = jnp.where(used[None, :] & (eid[None, :] > eid[:, None]), eid[None, :], N_EXPERTS)
    nxt_e = jnp.min(later, axis=1)
    nxt_e = jnp.where(nxt_e == N_EXPERTS, -1, nxt_e).astype(jnp.int32)
    w_slot = (pos[block_expert] % 2).astype(jnp.int32)
    nxt = nxt_e[block_expert]
    weights = jnp.where(valid[:, None], route[:, 0:2], 0.0)
    return (block_expert, n_used.reshape(1), w_slot, nxt, tok_of_row.reshape(n_blocks, 1, tm),
            dest.reshape(lp, TOP_K).astype(jnp.int32), weights)


def _moe_expert_kernel(be_ref, nu_ref, ws_ref, nx_ref, tok0_ref, tokn_ref, h_hbm, wgu_hbm, wd_hbm,
                       y_ref, xbuf, xsem, wgu_f, wd_f, wsem, wgu_bf, wd_bf, *, layer):
    i = pl.program_id(0)
    nl = _slab_rows(wgu_f.shape[1])
    tm = xbuf.shape[1] // nl
    n_used = nu_ref[0]

    def gather(tok_ref, slot):
        def issue(r, carry):
            src = h_hbm.at[pl.ds(pl.multiple_of(tok_ref[0, 0, r], nl), nl)]
            pltpu.make_async_copy(src, xbuf.at[slot, pl.ds(pl.multiple_of(r * nl, nl), nl)],
                                  xsem.at[slot]).start()
            return carry
        lax.fori_loop(0, tm, issue, 0, unroll=8)

    def gather_wait(slot):
        pltpu.make_async_copy(h_hbm.at[pl.ds(0, tm * nl)], xbuf.at[slot], xsem.at[slot]).wait()

    def weight_copies(e, s):
        return (pltpu.make_async_copy(wgu_hbm.at[layer, e], wgu_f.at[s], wsem.at[0, s]),
                pltpu.make_async_copy(wd_hbm.at[layer, e], wd_f.at[s], wsem.at[1, s]))

    @pl.when(i == 0)
    def _():
        for cp in weight_copies(be_ref[0], 0):
            cp.start()
        gather(tok0_ref, 0)

    @pl.when(i < n_used)
    def _():
        slot = i % 2
        e = be_ref[i]
        first = jnp.logical_or(i == 0, be_ref[jnp.maximum(i - 1, 0)] != e)

        gather_wait(slot)
        gather(tokn_ref, 1 - slot)

        @pl.when(first)
        def _():
            s = ws_ref[i]
            for cp in weight_copies(e, s):
                cp.wait()
            nxt = nx_ref[i]

            @pl.when(nxt >= 0)
            def _():
                for cp in weight_copies(nxt, 1 - s):
                    cp.start(priority=1)

            wgu_bf[...] = wgu_f[s].astype(BF16)
            wd_bf[...] = wd_f[s].astype(BF16)

        cols = _load_row_slabs(lambda idx: xbuf[slot, idx, :], tm, nl)
        x = jnp.concatenate([c.astype(BF16) for c in cols], axis=1)
        hgu = jnp.dot(x, wgu_bf[...], preferred_element_type=F32)
        act = jax.nn.silu(hgu[:, :D_EXPERT]) * hgu[:, D_EXPERT:]
        y = jnp.dot(act.astype(BF16), wd_bf[...], preferred_element_type=F32)
        _store_row_slabs(y_ref, 0, tm, y)

        @pl.when(i == n_used - 1)
        def _():
            gather_wait(1 - slot)

    @pl.when(i >= n_used)
    def _():
        y_ref[...] = jnp.zeros_like(y_ref)


def _moe_experts(h_slabs, plan, w_gate_up, w_down, layer):
    block_expert, n_used, w_slot, nxt, tok_of_row, _, _ = plan
    d = w_gate_up.shape[2]
    ns = _slab_rows(d)
    tok_of_row = tok_of_row * ns
    tm = MOE_TM
    n_blocks = tok_of_row.shape[0]
    nxt_blk = lambda i, be, nu, ws, nx: (jnp.minimum(i + 1, jnp.maximum(nu[0] - 1, 0)), 0, 0)
    smem = lambda imap: pl.BlockSpec((1, 1, tm), imap, memory_space=pltpu.SMEM)
    anyspace = pl.BlockSpec(memory_space=pl.ANY)
    return pl.pallas_call(
        functools.partial(_moe_expert_kernel, layer=layer),
        grid_spec=pltpu.PrefetchScalarGridSpec(
            num_scalar_prefetch=4,
            grid=(n_blocks,),
            in_specs=[smem(lambda i, be, nu, ws, nx: (0, 0, 0)), smem(nxt_blk),
                      anyspace, anyspace, anyspace],
            out_specs=pl.BlockSpec((tm * ns, 128), lambda i, be, nu, ws, nx: (i, 0)),
            scratch_shapes=[pltpu.VMEM((2, tm * ns, 128), U32), pltpu.SemaphoreType.DMA((2,)),
                            pltpu.VMEM((2, d, 2 * D_EXPERT), F32), pltpu.VMEM((2, D_EXPERT, d), F32),
                            pltpu.SemaphoreType.DMA((2, 2)),
                            pltpu.VMEM((d, 2 * D_EXPERT), BF16), pltpu.VMEM((D_EXPERT, d), BF16)],
        ),
        out_shape=jax.ShapeDtypeStruct((n_blocks * tm * ns, 128), U32),
        compiler_params=_cparams("arbitrary"),
        name="moe_experts",
    )(block_expert, n_used, w_slot, nxt, tok_of_row, tok_of_row, h_slabs, w_gate_up, w_down)


def _moe_combine_kernel(d0_ref, d1_ref, d0n_ref, d1n_ref, h_ref, w_ref, y_hbm, g_ref, b_ref, *rest,
                        alpha, with_bf16):
    if with_bf16:
        o_ref, ob_ref, buf, sem = rest
    else:
        o_ref, buf, sem = rest
    i = pl.program_id(0)
    tm, d = h_ref.shape
    nl = _slab_rows(d)
    slot = i % 2

    def gather(a_ref, b_ref_, s):
        def issue(r, carry):
            for k, idx_ref in enumerate((a_ref, b_ref_)):
                src = y_hbm.at[pl.ds(pl.multiple_of(idx_ref[0, 0, r], nl), nl)]
                pltpu.make_async_copy(src, buf.at[s, k, pl.ds(pl.multiple_of(r * nl, nl), nl)],
                                      sem.at[s, k]).start()
            return carry
        lax.fori_loop(0, tm, issue, 0, unroll=8)

    @pl.when(i == 0)
    def _():
        gather(d0_ref, d1_ref, 0)

    for k in range(TOP_K):
        pltpu.make_async_copy(y_hbm.at[pl.ds(0, tm * nl)], buf.at[slot, k], sem.at[slot, k]).wait()

    @pl.when(i + 1 < pl.num_programs(0))
    def _():
        gather(d0n_ref, d1n_ref, 1 - slot)

    w = w_ref[...]
    y0 = _load_row_slabs(lambda idx: buf[slot, 0, idx, :], tm, nl)
    y1 = _load_row_slabs(lambda idx: buf[slot, 1, idx, :], tm, nl)
    cols = [alpha * h_ref[:, c * 128:(c + 1) * 128] + (w[:, 0:1] * y0[c] + w[:, 1:2] * y1[c])
            for c in range(d // 128)]
    out = _layer_norm(jnp.concatenate(cols, axis=1), g_ref[...], b_ref[...])
    o_ref[...] = out
    if with_bf16:
        ob_ref[...] = out.astype(BF16)


def _moe_combine(h, plan, y_buf, ln_g, ln_b, alpha, first_row, with_bf16=False):
    dest, weights = plan[-2], plan[-1]
    lp, d = h.shape
    tm = MIX_TILE
    off = first_row // tm
    nt = lp // tm
    ns = _slab_rows(d)
    d0 = (dest[:, 0] * ns).reshape(nt, 1, tm)
    d1 = (dest[:, 1] * ns).reshape(nt, 1, tm)
    cur = pl.BlockSpec((1, 1, tm), lambda i: (off, 0, 0), memory_space=pltpu.SMEM)
    nxt = pl.BlockSpec((1, 1, tm), lambda i: (jnp.minimum(i + 1 + off, nt - 1), 0, 0),
                       memory_space=pltpu.SMEM)
    out_spec = pl.BlockSpec((tm, d), lambda i: (i, 0))
    out_shape = jax.ShapeDtypeStruct((lp - first_row, d), F32)
    if with_bf16:
        out_spec = [out_spec, out_spec]
        out_shape = [out_shape, jax.ShapeDtypeStruct((lp - first_row, d), BF16)]
    return pl.pallas_call(
        functools.partial(_moe_combine_kernel, alpha=alpha, with_bf16=with_bf16),
        grid=(nt - off,),
        in_specs=[cur, cur, nxt, nxt, pl.BlockSpec((tm, d), lambda i: (i + off, 0)),
                  pl.BlockSpec((tm, TOP_K), lambda i: (i + off, 0)),
                  pl.BlockSpec(memory_space=pl.ANY), _full(ln_g.shape), _full(ln_b.shape)],
        out_specs=out_spec,
        out_shape=out_shape,
        scratch_shapes=[pltpu.VMEM((2, TOP_K, tm * ns, 128), U32),
                        pltpu.SemaphoreType.DMA((2, TOP_K))],
        compiler_params=_cparams("arbitrary"),
        name="moe_combine",
    )(d0, d1, d0, d1, h, weights, y_buf, ln_g, ln_b)


def _store_heads(ref, seg_rows, val):
    for hh in range(val.shape[1] // HG_DK):
        ref[0, hh, seg_rows, :] = val[:, hh * HG_DK:(hh + 1) * HG_DK].astype(ref.dtype)


def _inproj_plain_kernel(x_ref, w_ref, o_ref):
    for r in range(0, x_ref.shape[0], INPROJ_C_SUB):
        rows = slice(r, r + INPROJ_C_SUB)
        _store_heads(o_ref, rows, jnp.dot(x_ref[rows, :], w_ref[...], preferred_element_type=F32))


def _inproj_forget_kernel(x_ref, w_ref, lb_ref, k_ref, g_ref):
    i = pl.program_id(0)
    tm = x_ref.shape[0]
    lb = lb_ref[...]
    for r in range(0, tm, INPROJ_C_SUB):
        rows = slice(r, r + INPROJ_C_SUB)
        acc = jnp.dot(x_ref[rows, :], w_ref[...], preferred_element_type=F32)
        f = lb + (1.0 - lb) * jax.nn.sigmoid(acc)
        real = (i * tm + r + lax.broadcasted_iota(jnp.int32, (INPROJ_C_SUB, 1), 0)) >= META_ROW0
        _store_heads(k_ref, rows, jnp.where(real, 1.0 - f, 0.0))
        _store_heads(g_ref, rows, jnp.where(real, jnp.log(f), 0.0))


def _inproj_c(hb, w, lb):
    lp, d = hb.shape
    tm = _pick(lp, INPROJ_C_TILES)
    tn = 1024
    tps = d // tn
    hpt = tn // HG_DK
    x_spec = pl.BlockSpec((tm, d), lambda i, j: (i, 0))
    out_spec = pl.BlockSpec((1, hpt, tm, HG_DK), lambda i, j: (j // tps, j % tps, i, 0))
    hm = lambda n, dt: jax.ShapeDtypeStruct((n, HG_HEADS, lp, HG_DK), dt)
    plain_col = lambda i, j: (0, jnp.where(j >= 2 * tps, j + 2 * tps, j))
    plain = pl.pallas_call(
        _inproj_plain_kernel,
        grid=(lp // tm, 3 * tps),
        in_specs=[x_spec, pl.BlockSpec((d, tn), plain_col)],
        out_specs=out_spec,
        out_shape=hm(3, BF16),
        compiler_params=_cparams("parallel", "arbitrary"),
        name="inproj_c_plain",
    )(hb, w)
    k, g = pl.pallas_call(
        _inproj_forget_kernel,
        grid=(lp // tm, 2 * tps),
        in_specs=[x_spec, pl.BlockSpec((d, tn), lambda i, j: (0, j + 2 * tps)),
                  pl.BlockSpec((1, tn), lambda i, j: (0, j % tps))],
        out_specs=[out_spec, out_spec],
        out_shape=[hm(2, BF16), hm(2, F32)],
        compiler_params=_cparams("parallel", "arbitrary"),
        name="inproj_c_forget",
    )(hb, w, lb)
    return plain, k, g


def _hgrn_chunk(q, k, v, g, state_t, reverse):
    c = HG_CHUNK
    row = lax.broadcasted_iota(jnp.int32, (c, HG_DK), 0)
    b = g
    sh = 1
    while sh < c:
        if not reverse:
            b = b + jnp.where(row >= sh, pltpu.roll(b, sh, axis=0), 0.0)
        else:
            b = b + jnp.where(row < c - sh, pltpu.roll(b, c - sh, axis=0), 0.0)
        sh *= 2
    r_i = lax.broadcasted_iota(jnp.int32, (c, c), 0)
    c_i = lax.broadcasted_iota(jnp.int32, (c, c), 1)
    causal = (c_i >= r_i) if reverse else (c_i <= r_i)
    sub = HG_SUB
    mid = sub // 2
    blocks = lambda x: [x[j * sub:(j + 1) * sub] for j in range(HG_NSUB)]
    ref = [b[j * sub + mid:j * sub + mid + 1, :] for j in range(HG_NSUB)]
    b_end = b[0:1, :] if reverse else b[c - 1:c, :]
    qd, ku, q0, k_end = [], [], [], []
    for j, (qj, kj, bj) in enumerate(zip(blocks(q), blocks(k), blocks(b))):
        dj = bj - ref[j]
        qd_j = qj * jnp.exp(jnp.minimum(dj, HG_EXP_CLAMP))
        ku_j = kj * jnp.exp(jnp.minimum(-dj, HG_EXP_CLAMP))
        qd.append(qd_j.astype(BF16))
        ku.append(ku_j)
        q0.append(qd_j * jnp.exp(ref[j]))
        k_end.append(ku_j * jnp.exp(b_end - ref[j]))
    nt = (((1,), (1,)), ((), ()))
    zero = jnp.zeros((sub, HG_DK), BF16)
    score_rows = []
    for j in range(HG_NSUB):
        parts = []
        for i in range(HG_NSUB):
            if i == j:
                parts.append(ku[i].astype(BF16))
            elif (i > j) if reverse else (i < j):
                parts.append((ku[i] * jnp.exp(ref[j] - ref[i])).astype(BF16))
            else:
                parts.append(zero)
        k_ext = jnp.concatenate(parts, axis=0)
        score_rows.append(lax.dot_general(qd[j], k_ext, nt, preferred_element_type=F32))
    scores = jnp.concatenate(score_rows, axis=0)
    scores = jnp.where(causal, scores, 0.0)
    o = jnp.dot(scores.astype(BF16), v.astype(BF16), preferred_element_type=F32)
    o += lax.dot_general(jnp.concatenate(q0, axis=0).astype(BF16), state_t.astype(BF16), nt,
                         preferred_element_type=F32)
    new_state = jnp.exp(b_end) * state_t + jnp.dot(
        v.T.astype(BF16), jnp.concatenate(k_end, axis=0).astype(BF16), preferred_element_type=F32)
    return o, new_state


def _hgrn_kernel(qf_ref, vf_ref, k1_ref, g1_ref, qb_ref, vb_ref, k2_ref, g2_ref,
                 ofw_ref, obw_ref, st_ref):
    @pl.when(pl.program_id(0) == 0)
    def _():
        st_ref[...] = jnp.zeros_like(st_ref)

    def head(hh, carry):
        o, s = _hgrn_chunk(qf_ref[0, hh].astype(F32), k1_ref[0, hh].astype(F32),
                           vf_ref[0, hh].astype(F32), g1_ref[0, hh], st_ref[0, hh], reverse=False)
        ofw_ref[hh] = o
        st_ref[0, hh] = s
        o, s = _hgrn_chunk(qb_ref[0, hh].astype(F32), k2_ref[0, hh].astype(F32),
                           vb_ref[0, hh].astype(F32), g2_ref[0, hh], st_ref[1, hh], reverse=True)
        obw_ref[hh] = o
        st_ref[1, hh] = s
        return carry

    lax.fori_loop(0, HG_HEADS, head, 0, unroll=4)


def _hgrn(plain, k, g):
    _, _, lp, dk = plain.shape
    c = HG_CHUNK
    nc = lp // c
    fwd = lambda seg: pl.BlockSpec((1, HG_HEADS, c, dk), lambda i: (seg, 0, i, 0))
    bwd = lambda seg: pl.BlockSpec((1, HG_HEADS, c, dk), lambda i: (seg, 0, nc - 1 - i, 0))
    out_f = pl.BlockSpec((HG_HEADS, c, dk), lambda i: (0, i, 0))
    out_b = pl.BlockSpec((HG_HEADS, c, dk), lambda i: (0, nc - 1 - i, 0))
    return pl.pallas_call(
        _hgrn_kernel,
        grid=(nc,),
        in_specs=[fwd(0), fwd(1), fwd(0), fwd(0), bwd(0), bwd(1), bwd(1), bwd(1)],
        out_specs=[out_f, out_b],
        out_shape=[jax.ShapeDtypeStruct((HG_HEADS, lp, dk), F32)] * 2,
        scratch_shapes=[pltpu.VMEM((2, HG_HEADS, dk, dk), F32)],
        compiler_params=_cparams("arbitrary"),
        name="hgrn2_recurrence",
    )(plain, plain, k, g, plain, plain, k, g)


def _route_params(w_group, b_group, w_expert, b_expert):
    d = w_group.shape[0]
    pad = ROUTE_LANES - N_GROUPS - N_EXPERTS
    w = jnp.concatenate([w_group, w_expert, jnp.zeros((d, pad), F32)], axis=1)
    b = jnp.concatenate([b_group, b_expert, jnp.zeros((pad,), F32)])[None, :]
    w_hi = w.astype(BF16)
    w_lo = (w - w_hi.astype(F32)).astype(BF16)
    return jnp.concatenate([w_hi, w_lo], axis=1), b


def _rope_tables(lp):
    half = HEAD_DIM // 2
    pos = jnp.maximum(jnp.arange(lp) - META_ROW0, 0).astype(F32)
    inv = ROPE_THETA ** (-jnp.arange(half, dtype=F32) * 2.0 / HEAD_DIM)
    ang = pos[:, None] * inv[None, :]
    cos, sin = jnp.cos(ang), jnp.sin(ang)
    return jnp.concatenate([cos, cos], axis=1), jnp.concatenate([-sin, sin], axis=1)


def kernel(x, meta_tokens, w_in_ab, w_out_ab, attn_sinks, s5_lam_re, s5_lam_im, s5_log_step, s5_b_re, s5_b_im, s5_c_re, s5_c_im, s5_d, s5_w_glu, s5_b_glu, w_in_c, w_out_c, hgrn_lb_logits, hgrn_norm_g, ln_mix_g, ln_mix_b, ln_ffn_g, ln_ffn_b, moe_w_group, moe_b_group, moe_w_expert, moe_b_expert, moe_w_gate_up, moe_w_down):
    batch, seq, d = x.shape
    assert batch == 1 and seq % FRONT == 0
    depth = ln_mix_g.shape[0]
    assert depth == 2
    alpha = (2.0 * depth) ** 0.25
    lp = FRONT + seq
    row2 = lambda t: t[None, :]

    front = jnp.concatenate([jnp.zeros((META_ROW0, d), F32), meta_tokens.astype(F32)], axis=0)
    xs = x[0]

    w_in = w_in_ab[0].astype(BF16)
    s5w = s5_d.shape[1]
    wq, wk, wv, wu = (w_in[:, :Q_DIM], w_in[:, Q_DIM:Q_DIM + KV_DIM],
                      w_in[:, Q_DIM + KV_DIM:Q_DIM + 2 * KV_DIM], w_in[:, Q_DIM + 2 * KV_DIM:])
    cos2, sin2 = _rope_tables(lp)
    q, k, v, u = _inproj_ab(front, xs, wq, wk, wv, wu, cos2, sin2)
    a_out = _window_attention(q, k, v, attn_sinks[0])
    mats = _s5_discretise(s5_lam_re[0], s5_lam_im[0], s5_log_step[0], s5_b_re[0], s5_b_im[0],
                          s5_c_re[0], s5_c_im[0])
    y_lo, y_hi = _s5_mixer_pre_glu(u, s5_d[0], mats)
    wo = w_out_ab[0].astype(BF16)
    w_route, b_route = _route_params(moe_w_group[0], moe_b_group[0], moe_w_expert[0], moe_b_expert[0])
    h, h_slabs, route = _mix_ab(front, xs, a_out, y_lo, y_hi, s5_w_glu[0].astype(BF16), row2(s5_b_glu[0]),
                       wo[:Q_DIM], wo[Q_DIM:], row2(ln_mix_g[0]), row2(ln_mix_b[0]), w_route, b_route,
                       alpha)
    plan = _moe_plan(route, META_ROW0)
    y_buf = _moe_experts(h_slabs, plan, moe_w_gate_up, moe_w_down, 0)
    h, hb = _moe_combine(h, plan, y_buf, row2(ln_ffn_g[0]), row2(ln_ffn_b[0]), alpha, 0, with_bf16=True)

    lb_probs = jax.nn.softmax(hgrn_lb_logits.astype(F32), axis=0)
    lb = (jnp.cumsum(lb_probs, axis=0) - lb_probs[0])[1]
    plain, hk, hg = _inproj_c(hb, w_in_c[0].astype(BF16), row2(lb))
    o_fw, o_bw = _hgrn(plain, hk, hg)
    w_route, b_route = _route_params(moe_w_group[1], moe_b_group[1], moe_w_expert[1], moe_b_expert[1])
    h, h_slabs, route = _mix_c(h, o_fw, o_bw, plain, hgrn_norm_g[0].reshape(HG_HEADS, 1, HG_DK),
                      w_out_c[0].astype(BF16), row2(ln_mix_g[1]), row2(ln_mix_b[1]), w_route, b_route,
                      alpha)
    plan = _moe_plan(route, META_ROW0)
    y_buf = _moe_experts(h_slabs, plan, moe_w_gate_up, moe_w_down, 1)
    out = _moe_combine(h, plan, y_buf, row2(ln_ffn_g[1]), row2(ln_ffn_b[1]), alpha, FRONT)
    return out[None]
```

```python
import functools
import math

import jax
import jax.numpy as jnp
from jax import lax
from jax.experimental import pallas as pl
from jax.experimental.pallas import tpu as pltpu

F32 = jnp.float32
BF16 = jnp.bfloat16

N_META = 16
FRONT = 512
META_ROW0 = FRONT - N_META

ATTN_HEADS = 8
ATTN_KV_HEADS = 2
ATTN_GROUP = ATTN_HEADS // ATTN_KV_HEADS
HEAD_DIM = 128
WINDOW = 128
ATTN_BLOCK = 128
ROPE_THETA = 10000.0
Q_DIM = ATTN_HEADS * HEAD_DIM
KV_DIM = ATTN_KV_HEADS * HEAD_DIM

S5_GROUP = 16
S5_STATE = 64
S5_CHUNK = 16
S5_ROW = S5_CHUNK * S5_GROUP
S5_SCAN_BLOCK = 32
S5_CHUNK_BLOCKS = (176, 96, 48, 32, 16)

HG_HEADS = 16
HG_DK = 128
HG_CHUNK = 128
HG_SUB = 32
HG_NSUB = HG_CHUNK // HG_SUB
HG_EXP_CLAMP = 80.0

N_GROUPS = 8
EXPERTS_PER_GROUP = 8
N_EXPERTS = N_GROUPS * EXPERTS_PER_GROUP
TOP_K = 2
D_EXPERT = 512
MOE_TM = 512
ROUTE_LANES = 128

LN_EPS = 1e-5
RMS_EPS = 1e-6
NEG_INF = -1e30

ROW_TILE = 512
MIX_TILE = 256
MIX_SUB = 128
INPROJ_C_TILES = (1536, 1024, 512)
INPROJ_C_SUB = 512

VMEM_LIMIT = 56 * 1024 * 1024


def _cparams(*sem):
    return pltpu.CompilerParams(dimension_semantics=sem, vmem_limit_bytes=VMEM_LIMIT)


def _pick(n, candidates):
    return next(c for c in candidates if n % c == 0)


def _full(shape):
    nd = len(shape)
    return pl.BlockSpec(shape, lambda *_: (0,) * nd)


def _layer_norm(x, g, b):
    mu = jnp.mean(x, axis=-1, keepdims=True)
    xc = x - mu
    var = jnp.mean(xc * xc, axis=-1, keepdims=True)
    return xc * lax.rsqrt(var + LN_EPS) * g + b


def _inproj_ab_kernel(front_ref, x_ref, wq_ref, wk_ref, wv_ref, wu_ref, cos_ref, sin_ref,
                      q_ref, k_ref, v_ref, u_ref):
    xb = jnp.where(pl.program_id(0) == 0, front_ref[...], x_ref[...]).astype(BF16)
    cos = cos_ref[...]
    sin = sin_ref[...]

    def rope(t):
        return t * cos + pltpu.roll(t, HEAD_DIM // 2, axis=1) * sin

    q = jnp.dot(xb, wq_ref[...], preferred_element_type=F32)
    for h in range(ATTN_HEADS):
        sl = slice(h * HEAD_DIM, (h + 1) * HEAD_DIM)
        q_ref[:, sl] = rope(q[:, sl]).astype(BF16)
    k = jnp.dot(xb, wk_ref[...], preferred_element_type=F32)
    for h in range(ATTN_KV_HEADS):
        sl = slice(h * HEAD_DIM, (h + 1) * HEAD_DIM)
        k_ref[:, sl] = rope(k[:, sl]).astype(BF16)
    v_ref[...] = jnp.dot(xb, wv_ref[...], preferred_element_type=F32).astype(BF16)
    u_ref[...] = jnp.dot(xb, wu_ref[...], preferred_element_type=F32)


def _inproj_ab(front, x, wq, wk, wv, wu, cos2, sin2):
    d = x.shape[1]
    lp = front.shape[0] + x.shape[0]
    tm = ROW_TILE
    assert front.shape[0] == tm
    s5w = wu.shape[1]
    row = lambda w: pl.BlockSpec((tm, w), lambda i: (i, 0))
    return pl.pallas_call(
        _inproj_ab_kernel,
        grid=(lp // tm,),
        in_specs=[_full(front.shape), pl.BlockSpec((tm, d), lambda i: (jnp.maximum(i - 1, 0), 0)),
                  _full(wq.shape), _full(wk.shape), _full(wv.shape), _full(wu.shape),
                  row(HEAD_DIM), row(HEAD_DIM)],
        out_specs=[row(Q_DIM), row(KV_DIM), row(KV_DIM), row(s5w)],
        out_shape=[jax.ShapeDtypeStruct((lp, Q_DIM), BF16),
                   jax.ShapeDtypeStruct((lp, KV_DIM), BF16),
                   jax.ShapeDtypeStruct((lp, KV_DIM), BF16),
                   jax.ShapeDtypeStruct((lp, s5w), F32)],
        compiler_params=_cparams("parallel"),
        name="inproj_ab",
    )(front, x, wq, wk, wv, wu, cos2, sin2)


def _attn_kernel(sink_ref, q_ref, k0_ref, k1_ref, k2_ref, v0_ref, v1_ref, v2_ref, km_ref, vm_ref,
                 o_ref, *, lp):
    qb = pl.program_id(0)
    blk = ATTN_BLOCK
    rows = ATTN_GROUP * blk
    scale = HEAD_DIM ** -0.5
    q_row = qb * blk + lax.broadcasted_iota(jnp.int32, (rows, 3 * blk), 0) % blk
    k_row = (qb - 1) * blk + lax.broadcasted_iota(jnp.int32, (rows, 3 * blk), 1)
    vis = (k_row >= FRONT) & (k_row < lp) & (jnp.abs(q_row - k_row) <= WINDOW)
    head_of_row = lax.broadcasted_iota(jnp.int32, (rows, 1), 0) // blk
    for g in range(ATTN_KV_HEADS):
        gs = slice(g * HEAD_DIM, (g + 1) * HEAD_DIM)
        qs = jnp.concatenate(
            [q_ref[:, (g * ATTN_GROUP + r) * HEAD_DIM:(g * ATTN_GROUP + r + 1) * HEAD_DIM]
             for r in range(ATTN_GROUP)], axis=0)
        kband = jnp.concatenate([k0_ref[:, gs], k1_ref[:, gs], k2_ref[:, gs]], axis=0)
        vband = jnp.concatenate([v0_ref[:, gs], v1_ref[:, gs], v2_ref[:, gs]], axis=0)
        nt = (((1,), (1,)), ((), ()))
        s_band = lax.dot_general(qs, kband, nt, preferred_element_type=F32) * scale
        s_band = jnp.where(vis, s_band, NEG_INF)
        s_meta = lax.dot_general(qs, km_ref[:, gs], nt, preferred_element_type=F32) * scale
        sink = jnp.zeros((rows, 1), F32)
        for r in range(ATTN_GROUP):
            sink = jnp.where(head_of_row == r, sink_ref[g * ATTN_GROUP + r], sink)
        m = jnp.maximum(jnp.maximum(jnp.max(s_band, axis=-1, keepdims=True),
                                    jnp.max(s_meta, axis=-1, keepdims=True)), sink)
        e_band = jnp.exp(s_band - m)
        e_meta = jnp.exp(s_meta - m)
        denom = (jnp.sum(e_band, axis=-1, keepdims=True) + jnp.sum(e_meta, axis=-1, keepdims=True)
                 + jnp.exp(sink - m))
        o = (jnp.dot(e_band.astype(BF16), vband, preferred_element_type=F32)
             + jnp.dot(e_meta.astype(BF16), vm_ref[:, gs], preferred_element_type=F32)) / denom
        for r in range(ATTN_GROUP):
            hh = g * ATTN_GROUP + r
            o_ref[:, hh * HEAD_DIM:(hh + 1) * HEAD_DIM] = o[r * blk:(r + 1) * blk].astype(BF16)


def _window_attention(q, k, v, sinks):
    lp = q.shape[0]
    blk = ATTN_BLOCK
    nb = lp // blk
    qspec = pl.BlockSpec((blk, Q_DIM), lambda i, s: (i, 0))
    kv = lambda off: pl.BlockSpec((blk, KV_DIM), lambda i, s: (jnp.clip(i + off, 0, nb - 1), 0))
    meta = pl.BlockSpec((N_META, KV_DIM), lambda i, s: (META_ROW0 // N_META, 0))
    return pl.pallas_call(
        functools.partial(_attn_kernel, lp=lp),
        grid_spec=pltpu.PrefetchScalarGridSpec(
            num_scalar_prefetch=1,
            grid=(nb,),
            in_specs=[qspec, kv(-1), kv(0), kv(1), kv(-1), kv(0), kv(1), meta, meta],
            out_specs=qspec,
        ),
        out_shape=jax.ShapeDtypeStruct((lp, Q_DIM), BF16),
        compiler_params=_cparams("parallel"),
        name="window_attention",
    )(sinks, q, k, k, k, v, v, v, k, v)


def _s5_discretise(lam_re, lam_im, log_step, b_re, b_im, c_re, c_im):
    t = S5_CHUNK
    hi = lax.Precision.HIGHEST
    g = lam_re.shape[1]
    lr = jnp.minimum(lam_re, -1e-4)
    li = lam_im
    step = jnp.exp(log_step)[..., None]
    dr, di = lr * step, li * step
    lags = jnp.arange(t + 1, dtype=F32)[:, None]
    mag = jnp.exp(dr[..., None, :] * lags)
    pr, pi = mag * jnp.cos(di[..., None, :] * lags), mag * jnp.sin(di[..., None, :] * lags)
    ar, ai = pr[:, :, 1], pi[:, :, 1]
    den = lr * lr + li * li
    zr = ((ar - 1.0) * lr + ai * li) / den
    zi = (ai * lr - (ar - 1.0) * li) / den
    br = zr[..., None] * b_re - zi[..., None] * b_im
    bi = zr[..., None] * b_im + zi[..., None] * b_re
    brt, bit = br.transpose(0, 1, 3, 2), bi.transpose(0, 1, 3, 2)
    ct_r, ct_i = c_re.transpose(0, 1, 3, 2), c_im.transpose(0, 1, 3, 2)
    pt_r, pt_i = pr[:, :, :t].transpose(0, 1, 3, 2), pi[:, :, :t].transpose(0, 1, 3, 2)
    xr = (ct_r[:, :, :, None] * pt_r[..., None] - ct_i[:, :, :, None] * pt_i[..., None])
    xi = (ct_r[:, :, :, None] * pt_i[..., None] + ct_i[:, :, :, None] * pt_r[..., None])
    x_cat = jnp.concatenate([xr, xi], axis=2).reshape(2, g, 2 * S5_STATE, t * S5_GROUP)
    b_cat = jnp.concatenate([brt, -bit], axis=3)
    kern = jnp.einsum('dgip,dgpn->dgin', b_cat, x_cat, precision=hi).reshape(2, g, S5_GROUP, t, S5_GROUP)
    k_lag = jnp.concatenate([jnp.flip(kern[1][:, :, 1:], axis=2), kern[0][:, :, :1] + kern[1][:, :, :1],
                             kern[0][:, :, 1:]], axis=2)
    k_lag = k_lag.reshape(g, S5_GROUP, (2 * t - 1) * S5_GROUP)

    def flat_e(p_r, p_i, d):
        e_r = p_r[:, :, None] * brt[d][:, None] - p_i[:, :, None] * bit[d][:, None]
        e_i = p_r[:, :, None] * bit[d][:, None] + p_i[:, :, None] * brt[d][:, None]
        flat = lambda e: e.reshape(g, t * S5_GROUP, S5_STATE)
        return jnp.concatenate([flat(e_r), flat(e_i)], axis=-1)
    e_mat = jnp.concatenate([flat_e(pr[0][:, ::-1][:, 1:], pi[0][:, ::-1][:, 1:], 0),
                             flat_e(pr[1][:, :t], pi[1][:, :t], 1)], axis=-1)

    def flat_f(p_r, p_i, d):
        ct_r, ct_i = c_re[d].transpose(0, 2, 1), c_im[d].transpose(0, 2, 1)
        pt_r, pt_i = p_r.transpose(0, 2, 1)[..., None], p_i.transpose(0, 2, 1)[..., None]
        w_r = ct_r[:, :, None] * pt_r - ct_i[:, :, None] * pt_i
        w_i = ct_r[:, :, None] * pt_i + ct_i[:, :, None] * pt_r
        flat = lambda w: w.reshape(g, S5_STATE, t * S5_GROUP)
        return jnp.concatenate([flat(w_r), -flat(w_i)], axis=1)
    f_fw = flat_f(pr[0][:, 1:], pi[0][:, 1:], 0)
    f_bw = flat_f(pr[1][:, ::-1][:, :t], pi[1][:, ::-1][:, :t], 1)
    gh = g // 2
    upper = (jnp.arange(g) >= gh)[:, None, None]
    def place(m):
        z = jnp.zeros_like(m)
        return jnp.concatenate([jnp.where(upper, z, m), jnp.where(upper, m, z)], axis=1)
    p = S5_STATE
    f_mat = jnp.concatenate([place(f_fw[:, :p]), place(f_fw[:, p:]), place(f_bw[:, :p]), place(f_bw[:, p:])],
                            axis=1)
    pack = lambda v: jnp.concatenate([v[:gh], v[gh:]], axis=-1)
    coef = jnp.stack([pack(pr[0][:, t]), pack(pi[0][:, t]), pack(pr[1][:, t]), pack(pi[1][:, t])])
    split = lambda m: m.reshape((2, gh) + m.shape[1:])
    return split(k_lag), split(e_mat.astype(BF16)), split(f_mat.astype(BF16)), coef


def _chunk_rows_to_lanes(u_ref, ncb):
    lane_seg = lax.broadcasted_iota(jnp.int32, (ncb, 128), 1) // S5_GROUP
    rows = [u_ref[pl.ds(s, ncb, stride=S5_CHUNK), :] for s in range(S5_CHUNK)]
    segs = 128 // S5_GROUP
    out = []
    for j in range(segs):
        halves = []
        for h in range(S5_CHUNK // segs):
            acc = None
            for s8 in range(segs):
                r = rows[h * segs + s8]
                k = (s8 - j) % segs
                if k:
                    r = pltpu.roll(r, S5_GROUP * k, axis=1)
                acc = r if acc is None else jnp.where(lane_seg == s8, r, acc)
            halves.append(acc)
        out.append(jnp.concatenate(halves, axis=1))
    return out


def _lanes_to_chunk_rows(ys, y_ref, ncb, add_ref, scale_ref):
    lane_seg = lax.broadcasted_iota(jnp.int32, (ncb, 128), 1) // S5_GROUP
    segs = 128 // S5_GROUP
    for t in range(S5_CHUNK):
        h, t8 = divmod(t, segs)
        acc = None
        for j in range(segs):
            r = ys[j][:, h * 128:(h + 1) * 128]
            k = (j - t8) % segs
            if k:
                r = pltpu.roll(r, S5_GROUP * k, axis=1)
            acc = r if acc is None else jnp.where(lane_seg == j, r, acc)
        idx = pl.ds(t, ncb, stride=S5_CHUNK)
        y_ref[idx, :] = acc + add_ref[idx, :] * scale_ref[...]


def _pair_halves(a, b):
    lane = lax.broadcasted_iota(jnp.int32, a.shape, 1)
    lo = jnp.where(lane < S5_STATE, a, pltpu.roll(b, S5_STATE, axis=1))
    hi = jnp.where(lane < S5_STATE, pltpu.roll(a, S5_STATE, axis=1), b)
    return lo, hi


def _s5_local_kernel(ua_ref, ub_ref, e_ref, u2_ref, f1_ref, f2_ref, b1_ref, b2_ref):
    ncb = u2_ref.shape[2]
    p2 = 2 * S5_STATE
    chunks = [_chunk_rows_to_lanes(ua_ref, ncb), _chunk_rows_to_lanes(ub_ref, ncb)]
    for j in range(u2_ref.shape[1]):
        st = []
        for hh in range(2):
            u2 = chunks[hh][j].astype(BF16)
            u2_ref[hh, j] = u2
            st.append(jnp.dot(u2, e_ref[hh, j], preferred_element_type=F32))
        f1_ref[:, j, :], f2_ref[:, j, :] = _pair_halves(st[0][:, :p2], st[1][:, :p2])
        b1_ref[:, j, :], b2_ref[:, j, :] = _pair_halves(st[0][:, p2:], st[1][:, p2:])


def _s5_scan_kernel(sf1_ref, sf2_ref, sb1_ref, sb2_ref, coef_ref, xf1_ref, xf2_ref, xb1_ref, xb2_ref,
                    st_ref):
    @pl.when(pl.program_id(0) == 0)
    def _():
        st_ref[...] = jnp.zeros_like(st_ref)

    nb = sf1_ref.shape[0]
    crf, cif, crb, cib = coef_ref[0], coef_ref[1], coef_ref[2], coef_ref[3]

    def body(i, carry):
        f1, f2, b1, b2 = carry
        j = nb - 1 - i
        xf1_ref[i] = f1
        xf2_ref[i] = f2
        xb1_ref[j] = b1
        xb2_ref[j] = b2
        return (crf * f1 - cif * f2 + sf1_ref[i], crf * f2 + cif * f1 + sf2_ref[i],
                crb * b1 - cib * b2 + sb1_ref[j], crb * b2 + cib * b1 + sb2_ref[j])

    out = lax.fori_loop(0, nb, body, (st_ref[0], st_ref[1], st_ref[2], st_ref[3]))
    for q in range(4):
        st_ref[q] = out[q]


def _chunk_matrix(k_lag):
    t, c = S5_CHUNK, S5_GROUP
    return jnp.concatenate([k_lag[:, (t - 1 - s) * c:(2 * t - 1 - s) * c] for s in range(t)],
                           axis=0).astype(BF16)


def _s5_readout_kernel(ua_ref, ub_ref, d_ref, u2_ref, kl_ref, f_ref, xf1_ref, xf2_ref, xb1_ref, xb2_ref,
                       ya_ref, yb_ref):
    ncb = u2_ref.shape[2]
    ys = [[], []]
    for j in range(u2_ref.shape[1]):
        x = jnp.concatenate([xf1_ref[:, j, :], xf2_ref[:, j, :], xb1_ref[:, j, :], xb2_ref[:, j, :]],
                            axis=1).astype(BF16)
        for hh in range(2):
            y = jnp.dot(u2_ref[hh, j], _chunk_matrix(kl_ref[hh, j]), preferred_element_type=F32)
            ys[hh].append(y + jnp.dot(x, f_ref[hh, j], preferred_element_type=F32))
    _lanes_to_chunk_rows(ys[0], ya_ref, ncb, ua_ref, d_ref.at[0])
    _lanes_to_chunk_rows(ys[1], yb_ref, ncb, ub_ref, d_ref.at[1])


def _s5_mixer_pre_glu(u, d_skip, mats):
    k_lag, e_mat, f_mat, coef = mats
    lp, w = u.shape
    g = w // S5_GROUP
    gh = g // 2
    gpb = 128 // S5_GROUP
    nc = lp // S5_CHUNK
    ncb = _pick(nc, S5_CHUNK_BLOCKS)
    rows = ncb * S5_CHUNK
    nlb = w // 128
    grid = (nlb // 2, nc // ncb)
    u_lo = pl.BlockSpec((rows, 128), lambda b, r: (r, b))
    u_hi = pl.BlockSpec((rows, 128), lambda b, r: (r, b + nlb // 2))
    per_group = lambda k, n: pl.BlockSpec((2, gpb, k, n), lambda b, r: (0, b, 0, 0))
    u2_spec = pl.BlockSpec((2, gpb, ncb, S5_ROW), lambda b, r: (0, b, r, 0))
    st_spec = pl.BlockSpec((ncb, gpb, 2 * S5_STATE), lambda b, r: (r, b, 0))
    st_shape = jax.ShapeDtypeStruct((nc, gh, 2 * S5_STATE), F32)
    u2, sf1, sf2, sb1, sb2 = pl.pallas_call(
        _s5_local_kernel,
        grid=grid,
        in_specs=[u_lo, u_hi, per_group(S5_ROW, 4 * S5_STATE)],
        out_specs=[u2_spec, st_spec, st_spec, st_spec, st_spec],
        out_shape=[jax.ShapeDtypeStruct((2, gh, nc, S5_ROW), BF16)] + [st_shape] * 4,
        compiler_params=_cparams("parallel", "parallel"),
        name="s5_local_states",
    )(u, u, e_mat)
    sb = S5_SCAN_BLOCK
    nblk = nc // sb
    fwd = pl.BlockSpec((sb, gh, 2 * S5_STATE), lambda i: (i, 0, 0))
    bwd = pl.BlockSpec((sb, gh, 2 * S5_STATE), lambda i: (nblk - 1 - i, 0, 0))
    xf1, xf2, xb1, xb2 = pl.pallas_call(
        _s5_scan_kernel,
        grid=(nblk,),
        in_specs=[fwd, fwd, bwd, bwd, _full(coef.shape)],
        out_specs=[fwd, fwd, bwd, bwd],
        out_shape=[st_shape] * 4,
        scratch_shapes=[pltpu.VMEM((4, gh, 2 * S5_STATE), F32)],
        compiler_params=_cparams("arbitrary"),
        name="s5_chunk_scan",
    )(sf1, sf2, sb1, sb2, coef)
    d2 = d_skip.reshape(2, 1, nlb // 2 * 128)
    y_lo, y_hi = pl.pallas_call(
        _s5_readout_kernel,
        grid=grid,
        in_specs=[u_lo, u_hi, pl.BlockSpec((2, 1, 128), lambda b, r: (0, 0, b)), u2_spec,
                  per_group(S5_GROUP, k_lag.shape[-1]), per_group(8 * S5_STATE, S5_ROW),
                  st_spec, st_spec, st_spec, st_spec],
        out_specs=[pl.BlockSpec((rows, 128), lambda b, r: (r, b))] * 2,
        out_shape=[jax.ShapeDtypeStruct((lp, w // 2), F32)] * 2,
        compiler_params=_cparams("parallel", "parallel"),
        name="s5_readout",
    )(u, u, d2, u2, k_lag, f_mat, xf1, xf2, xb1, xb2)
    return y_lo, y_hi


def _route(hn, wr_ref, br_ref):
    hn_hi = hn.astype(BF16)
    hn_lo = (hn - hn_hi.astype(F32)).astype(BF16)
    p_hi = jnp.dot(hn_hi, wr_ref[...], preferred_element_type=F32)
    p_lo = jnp.dot(hn_lo, wr_ref[:, :ROUTE_LANES], preferred_element_type=F32)
    logits = p_hi[:, :ROUTE_LANES] + (p_hi[:, ROUTE_LANES:] + p_lo) + br_ref[...]
    lane_i = lax.broadcasted_iota(jnp.int32, logits.shape, 1)
    lane = lane_i.astype(F32)
    lane_grp = ((lane_i - N_GROUPS) // EXPERTS_PER_GROUP).astype(F32)
    big = float(ROUTE_LANES)
    g_log = jnp.where(lane_i < N_GROUPS, logits, -jnp.inf)
    g_max = jnp.max(g_log, axis=-1, keepdims=True)
    grp = jnp.min(jnp.where(g_log == g_max, lane, big), axis=-1, keepdims=True)
    p_grp = 1.0 / jnp.sum(jnp.exp(g_log - g_max), axis=-1, keepdims=True)
    in_grp = (lane_i >= N_GROUPS) & (lane_i < N_GROUPS + N_EXPERTS) & (lane_grp == grp)
    e_log = jnp.where(in_grp, logits, -jnp.inf)
    v1 = jnp.max(e_log, axis=-1, keepdims=True)
    i1 = jnp.min(jnp.where(e_log == v1, lane, big), axis=-1, keepdims=True)
    e_log2 = jnp.where(lane == i1, -jnp.inf, e_log)
    v2 = jnp.max(e_log2, axis=-1, keepdims=True)
    i2 = jnp.min(jnp.where(e_log2 == v2, lane, big), axis=-1, keepdims=True)
    e21 = jnp.exp(v2 - v1)
    w1 = p_grp / (1.0 + e21)
    w2 = p_grp * e21 / (1.0 + e21)
    route = jnp.where(lane_i == 0, w1, 0.0)
    route = jnp.where(lane_i == 1, w2, route)
    route = jnp.where(lane_i == 2, i1 - N_GROUPS, route)
    route = jnp.where(lane_i == 3, i2 - N_GROUPS, route)
    return route


U32 = jnp.uint32


def _slab_rows(d):
    return d // 256


def _slab_spec(tm, d):
    return pl.BlockSpec((tm * _slab_rows(d), 128), lambda i: (i, 0))


def _slab_shape(rows, d):
    return jax.ShapeDtypeStruct((rows * _slab_rows(d), 128), U32)


def _pack_pair(a, b):
    ua = lax.bitcast_convert_type(a.astype(BF16).astype(F32), U32)
    ub = lax.bitcast_convert_type(b.astype(BF16).astype(F32), U32)
    return ua | (ub >> 16)


def _unpack_pair(u):
    a = lax.bitcast_convert_type(u & jnp.uint32(0xFFFF0000), F32)
    b = lax.bitcast_convert_type(u << 16, F32)
    return a, b


def _store_row_slabs(ref, start, n, val):
    ns = _slab_rows(val.shape[1])
    for k in range(ns):
        word = _pack_pair(val[:, 2 * k * 128:(2 * k + 1) * 128], val[:, (2 * k + 1) * 128:(2 * k + 2) * 128])
        ref[pl.ds(start * ns + k, n, stride=ns), :] = word


def _load_row_slabs(ref_at, n, ns):
    cols = []
    for k in range(ns):
        cols.extend(_unpack_pair(ref_at(pl.ds(k, n, stride=ns))))
    return cols


def _sub_rows(tm):
    return [slice(r, r + MIX_SUB) for r in range(0, tm, MIX_SUB)]


def _mix_ab_kernel(front_ref, x_ref, a_ref, ylo_ref, yhi_ref, wglu_ref, bglu_ref, woa_ref, wos_ref,
                   g_ref, b_ref, wr_ref, br_ref, hn_ref, hs_ref, route_ref, *, alpha):
    in_front = pl.program_id(0) < FRONT // MIX_TILE
    for rows in _sub_rows(x_ref.shape[0]):
        y = jnp.concatenate([ylo_ref[rows, :], yhi_ref[rows, :]], axis=1)
        z = 0.5 * y * (1.0 + jnp.tanh(math.sqrt(2.0 / math.pi) * (y + 0.044715 * (y * y * y))))
        gate = jnp.dot(z.astype(BF16), wglu_ref[...], preferred_element_type=F32) + bglu_ref[...]
        s_out = z * jax.nn.sigmoid(gate)
        mix = (jnp.dot(a_ref[rows, :], woa_ref[...], preferred_element_type=F32)
               + jnp.dot(s_out.astype(BF16), wos_ref[...], preferred_element_type=F32))
        h_in = jnp.where(in_front, front_ref[rows, :], x_ref[rows, :])
        hn = _layer_norm(alpha * h_in + mix, g_ref[...], b_ref[...])
        hn_ref[rows, :] = hn
        _store_row_slabs(hs_ref, rows.start, rows.stop - rows.start, hn)
        route_ref[rows, :] = _route(hn, wr_ref, br_ref)


def _mix_ab(front, x, a_out, y_lo, y_hi, w_glu, b_glu, wo_a, wo_s, ln_g, ln_b, w_route, b_route, alpha):
    d = x.shape[1]
    lp = front.shape[0] + x.shape[0]
    tm = MIX_TILE
    nf = front.shape[0] // tm
    row = lambda c: pl.BlockSpec((tm, c), lambda i: (i, 0))
    return pl.pallas_call(
        functools.partial(_mix_ab_kernel, alpha=alpha),
        grid=(lp // tm,),
        in_specs=[pl.BlockSpec((tm, d), lambda i: (jnp.minimum(i, nf - 1), 0)),
                  pl.BlockSpec((tm, d), lambda i: (jnp.maximum(i - nf, 0), 0)), row(Q_DIM), row(y_lo.shape[1]), row(y_hi.shape[1]), _full(w_glu.shape),
                  _full(b_glu.shape), _full(wo_a.shape), _full(wo_s.shape), _full(ln_g.shape),
                  _full(ln_b.shape), _full(w_route.shape), _full(b_route.shape)],
        out_specs=[row(d), _slab_spec(tm, d), row(ROUTE_LANES)],
        out_shape=[jax.ShapeDtypeStruct((lp, d), F32), _slab_shape(lp, d),
                   jax.ShapeDtypeStruct((lp, ROUTE_LANES), F32)],
        compiler_params=_cparams("parallel"),
        name="mix_ab",
    )(front, x, a_out, y_lo, y_hi, w_glu, b_glu, wo_a, wo_s, ln_g, ln_b, w_route, b_route)


def _mix_c_kernel(h_ref, ofw_ref, obw_ref, gate_ref, ng_ref, wo_ref, g_ref, b_ref, wr_ref, br_ref,
                  hn_ref, hs_ref, route_ref, *, alpha):
    for rows in _sub_rows(h_ref.shape[0]):
        parts = []
        for hh in range(HG_HEADS):
            o = ofw_ref[hh, rows, :] + obw_ref[hh, rows, :]
            o = o * lax.rsqrt(jnp.mean(o * o, axis=-1, keepdims=True) + RMS_EPS)
            o = o * ng_ref[hh] * jax.nn.sigmoid(gate_ref[0, hh, rows, :].astype(F32))
            parts.append(o.astype(BF16))
        mix = jnp.dot(jnp.concatenate(parts, axis=1), wo_ref[...], preferred_element_type=F32)
        hn = _layer_norm(alpha * h_ref[rows, :] + mix, g_ref[...], b_ref[...])
        hn_ref[rows, :] = hn
        _store_row_slabs(hs_ref, rows.start, rows.stop - rows.start, hn)
        route_ref[rows, :] = _route(hn, wr_ref, br_ref)


def _mix_c(h, o_fw, o_bw, plain, norm_g, wo, ln_g, ln_b, w_route, b_route, alpha):
    lp, d = h.shape
    tm = MIX_TILE
    row = lambda c: pl.BlockSpec((tm, c), lambda i: (i, 0))
    hm = pl.BlockSpec((HG_HEADS, tm, HG_DK), lambda i: (0, i, 0))
    gate = pl.BlockSpec((1, HG_HEADS, tm, HG_DK), lambda i: (2, 0, i, 0))
    return pl.pallas_call(
        functools.partial(_mix_c_kernel, alpha=alpha),
        grid=(lp // tm,),
        in_specs=[row(d), hm, hm, gate, _full(norm_g.shape), _full(wo.shape), _full(ln_g.shape),
                  _full(ln_b.shape), _full(w_route.shape), _full(b_route.shape)],
        out_specs=[row(d), _slab_spec(tm, d), row(ROUTE_LANES)],
        out_shape=[jax.ShapeDtypeStruct((lp, d), F32), _slab_shape(lp, d),
                   jax.ShapeDtypeStruct((lp, ROUTE_LANES), F32)],
        compiler_params=_cparams("parallel"),
        name="mix_c",
    )(h, o_fw, o_bw, plain, norm_g, wo, ln_g, ln_b, w_route, b_route)


def _moe_plan(route, n_rows_valid_from):
    lp = route.shape[0]
    tm = MOE_TM
    valid = (jnp.arange(lp) >= n_rows_valid_from)
    expert = route[:, 2:4].astype(jnp.int32)
    flat_e = jnp.where(valid[:, None], expert, N_EXPERTS).reshape(-1)
    onehot = (flat_e[:, None] == jnp.arange(N_EXPERTS)[None, :]).astype(jnp.int32)
    csum = jnp.cumsum(onehot, axis=0)
    counts = csum[-1]
    rank = jnp.sum(jnp.where(onehot > 0, csum - 1, 0), axis=1)
    padded = (counts + tm - 1) // tm * tm
    pend = jnp.cumsum(padded)
    pstart = pend - padded
    e_safe = jnp.minimum(flat_e, N_EXPERTS - 1)
    dest = jnp.where(flat_e < N_EXPERTS, pstart[e_safe] + rank, 0)
    n_tokens = lp - n_rows_valid_from
    n_blocks = -(-(n_tokens * TOP_K + N_EXPERTS * (tm - 1)) // tm)
    rows = n_blocks * tm
    tok = jnp.repeat(jnp.arange(lp, dtype=jnp.int32), TOP_K)
    scatter_to = jnp.where(flat_e < N_EXPERTS, dest, rows)
    tok_of_row = jnp.zeros((rows,), jnp.int32).at[scatter_to].set(tok, mode='drop')
    n_used = (pend[-1] // tm).astype(jnp.int32)
    blk = jnp.minimum(jnp.arange(n_blocks, dtype=jnp.int32), jnp.maximum(n_used - 1, 0))
    block_expert = jnp.minimum(jnp.searchsorted(pend, blk * tm, side='right'),
                               N_EXPERTS - 1).astype(jnp.int32)
    used = counts > 0
    pos = jnp.cumsum(used.astype(jnp.int32)) - 1
    eid = jnp.arange(N_EXPERTS, dtype=jnp.int32)
    later = jnp.where(used[None, :] & (eid[None, :] > eid[:, None]), eid[None, :], N_EXPERTS)
    nxt_e = jnp.min(later, axis=1)
    nxt_e = jnp.where(nxt_e == N_EXPERTS, -1, nxt_e).astype(jnp.int32)
    w_slot = (pos[block_expert] % 2).astype(jnp.int32)
    nxt = nxt_e[block_expert]
    weights = jnp.where(valid[:, None], route[:, 0:2], 0.0)
    return (block_expert, n_used.reshape(1), w_slot, nxt, tok_of_row.reshape(n_blocks, 1, tm),
            dest.reshape(lp, TOP_K).astype(jnp.int32), weights)


def _moe_expert_kernel(be_ref, nu_ref, ws_ref, nx_ref, tok0_ref, tokn_ref, h_hbm, wgu_hbm, wd_hbm,
                       y_ref, xbuf, xsem, wgu_f, wd_f, wsem, wgu_bf, wd_bf, *, layer):
    i = pl.program_id(0)
    nl = _slab_rows(wgu_f.shape[1])
    tm = xbuf.shape[1] // nl
    n_used = nu_ref[0]

    def gather(tok_ref, slot):
        def issue(r, carry):
            src = h_hbm.at[pl.ds(pl.multiple_of(tok_ref[0, 0, r], nl), nl)]
            pltpu.make_async_copy(src, xbuf.at[slot, pl.ds(pl.multiple_of(r * nl, nl), nl)],
                                  xsem.at[slot]).start()
            return carry
        lax.fori_loop(0, tm, issue, 0, unroll=8)

    def gather_wait(slot):
        pltpu.make_async_copy(h_hbm.at[pl.ds(0, tm * nl)], xbuf.at[slot], xsem.at[slot]).wait()

    def weight_copies(e, s):
        return (pltpu.make_async_copy(wgu_hbm.at[layer, e], wgu_f.at[s], wsem.at[0, s]),
                pltpu.make_async_copy(wd_hbm.at[layer, e], wd_f.at[s], wsem.at[1, s]))

    @pl.when(i == 0)
    def _():
        for cp in weight_copies(be_ref[0], 0):
            cp.start()
        gather(tok0_ref, 0)

    @pl.when(i < n_used)
    def _():
        slot = i % 2
        e = be_ref[i]
        first = jnp.logical_or(i == 0, be_ref[jnp.maximum(i - 1, 0)] != e)

        gather_wait(slot)
        gather(tokn_ref, 1 - slot)

        @pl.when(first)
        def _():
            s = ws_ref[i]
            for cp in weight_copies(e, s):
                cp.wait()
            nxt = nx_ref[i]

            @pl.when(nxt >= 0)
            def _():
                for cp in weight_copies(nxt, 1 - s):
                    cp.start(priority=1)

            wgu_bf[...] = wgu_f[s].astype(BF16)
            wd_bf[...] = wd_f[s].astype(BF16)

        cols = _load_row_slabs(lambda idx: xbuf[slot, idx, :], tm, nl)
        x = jnp.concatenate([c.astype(BF16) for c in cols], axis=1)
        hgu = jnp.dot(x, wgu_bf[...], preferred_element_type=F32)
        act = jax.nn.silu(hgu[:, :D_EXPERT]) * hgu[:, D_EXPERT:]
        y = jnp.dot(act.astype(BF16), wd_bf[...], preferred_element_type=F32)
        _store_row_slabs(y_ref, 0, tm, y)

        @pl.when(i == n_used - 1)
        def _():
            gather_wait(1 - slot)

    @pl.when(i >= n_used)
    def _():
        y_ref[...] = jnp.zeros_like(y_ref)


def _moe_experts(h_slabs, plan, w_gate_up, w_down, layer):
    block_expert, n_used, w_slot, nxt, tok_of_row, _, _ = plan
    d = w_gate_up.shape[2]
    ns = _slab_rows(d)
    tok_of_row = tok_of_row * ns
    tm = MOE_TM
    n_blocks = tok_of_row.shape[0]
    nxt_blk = lambda i, be, nu, ws, nx: (jnp.minimum(i + 1, jnp.maximum(nu[0] - 1, 0)), 0, 0)
    smem = lambda imap: pl.BlockSpec((1, 1, tm), imap, memory_space=pltpu.SMEM)
    anyspace = pl.BlockSpec(memory_space=pl.ANY)
    return pl.pallas_call(
        functools.partial(_moe_expert_kernel, layer=layer),
        grid_spec=pltpu.PrefetchScalarGridSpec(
            num_scalar_prefetch=4,
            grid=(n_blocks,),
            in_specs=[smem(lambda i, be, nu, ws, nx: (0, 0, 0)), smem(nxt_blk),
                      anyspace, anyspace, anyspace],
            out_specs=pl.BlockSpec((tm * ns, 128), lambda i, be, nu, ws, nx: (i, 0)),
            scratch_shapes=[pltpu.VMEM((2, tm * ns, 128), U32), pltpu.SemaphoreType.DMA((2,)),
                            pltpu.VMEM((2, d, 2 * D_EXPERT), F32), pltpu.VMEM((2, D_EXPERT, d), F32),
                            pltpu.SemaphoreType.DMA((2, 2)),
                            pltpu.VMEM((d, 2 * D_EXPERT), BF16), pltpu.VMEM((D_EXPERT, d), BF16)],
        ),
        out_shape=jax.ShapeDtypeStruct((n_blocks * tm * ns, 128), U32),
        compiler_params=_cparams("arbitrary"),
        name="moe_experts",
    )(block_expert, n_used, w_slot, nxt, tok_of_row, tok_of_row, h_slabs, w_gate_up, w_down)


def _moe_combine_kernel(d0_ref, d1_ref, d0n_ref, d1n_ref, h_ref, w_ref, y_hbm, g_ref, b_ref, *rest,
                        alpha, with_bf16):
    if with_bf16:
        o_ref, ob_ref, buf, sem = rest
    else:
        o_ref, buf, sem = rest
    i = pl.program_id(0)
    tm, d = h_ref.shape
    nl = _slab_rows(d)
    slot = i % 2

    def gather(a_ref, b_ref_, s):
        def issue(r, carry):
            for k, idx_ref in enumerate((a_ref, b_ref_)):
                src = y_hbm.at[pl.ds(pl.multiple_of(idx_ref[0, 0, r], nl), nl)]
                pltpu.make_async_copy(src, buf.at[s, k, pl.ds(pl.multiple_of(r * nl, nl), nl)],
                                      sem.at[s, k]).start()
            return carry
        lax.fori_loop(0, tm, issue, 0, unroll=8)

    @pl.when(i == 0)
    def _():
        gather(d0_ref, d1_ref, 0)

    for k in range(TOP_K):
        pltpu.make_async_copy(y_hbm.at[pl.ds(0, tm * nl)], buf.at[slot, k], sem.at[slot, k]).wait()

    @pl.when(i + 1 < pl.num_programs(0))
    def _():
        gather(d0n_ref, d1n_ref, 1 - slot)

    w = w_ref[...]
    y0 = _load_row_slabs(lambda idx: buf[slot, 0, idx, :], tm, nl)
    y1 = _load_row_slabs(lambda idx: buf[slot, 1, idx, :], tm, nl)
    cols = [alpha * h_ref[:, c * 128:(c + 1) * 128] + (w[:, 0:1] * y0[c] + w[:, 1:2] * y1[c])
            for c in range(d // 128)]
    out = _layer_norm(jnp.concatenate(cols, axis=1), g_ref[...], b_ref[...])
    o_ref[...] = out
    if with_bf16:
        ob_ref[...] = out.astype(BF16)


def _moe_combine(h, plan, y_buf, ln_g, ln_b, alpha, first_row, with_bf16=False):
    dest, weights = plan[-2], plan[-1]
    lp, d = h.shape
    tm = MIX_TILE
    off = first_row // tm
    nt = lp // tm
    ns = _slab_rows(d)
    d0 = (dest[:, 0] * ns).reshape(nt, 1, tm)
    d1 = (dest[:, 1] * ns).reshape(nt, 1, tm)
    cur = pl.BlockSpec((1, 1, tm), lambda i: (off, 0, 0), memory_space=pltpu.SMEM)
    nxt = pl.BlockSpec((1, 1, tm), lambda i: (jnp.minimum(i + 1 + off, nt - 1), 0, 0),
                       memory_space=pltpu.SMEM)
    out_spec = pl.BlockSpec((tm, d), lambda i: (i, 0))
    out_shape = jax.ShapeDtypeStruct((lp - first_row, d), F32)
    if with_bf16:
        out_spec = [out_spec, out_spec]
        out_shape = [out_shape, jax.ShapeDtypeStruct((lp - first_row, d), BF16)]
    return pl.pallas_call(
        functools.partial(_moe_combine_kernel, alpha=alpha, with_bf16=with_bf16),
        grid=(nt - off,),
        in_specs=[cur, cur, nxt, nxt, pl.BlockSpec((tm, d), lambda i: (i + off, 0)),
                  pl.BlockSpec((tm, TOP_K), lambda i: (i + off, 0)),
                  pl.BlockSpec(memory_space=pl.ANY), _full(ln_g.shape), _full(ln_b.shape)],
        out_specs=out_spec,
        out_shape=out_shape,
        scratch_shapes=[pltpu.VMEM((2, TOP_K, tm * ns, 128), U32),
                        pltpu.SemaphoreType.DMA((2, TOP_K))],
        compiler_params=_cparams("arbitrary"),
        name="moe_combine",
    )(d0, d1, d0, d1, h, weights, y_buf, ln_g, ln_b)


def _store_heads(ref, seg_rows, val):
    for hh in range(val.shape[1] // HG_DK):
        ref[0, hh, seg_rows, :] = val[:, hh * HG_DK:(hh + 1) * HG_DK].astype(ref.dtype)


def _inproj_plain_kernel(x_ref, w_ref, o_ref):
    for r in range(0, x_ref.shape[0], INPROJ_C_SUB):
        rows = slice(r, r + INPROJ_C_SUB)
        _store_heads(o_ref, rows, jnp.dot(x_ref[rows, :], w_ref[...], preferred_element_type=F32))


def _inproj_forget_kernel(x_ref, w_ref, lb_ref, k_ref, g_ref):
    i = pl.program_id(0)
    tm = x_ref.shape[0]
    lb = lb_ref[...]
    for r in range(0, tm, INPROJ_C_SUB):
        rows = slice(r, r + INPROJ_C_SUB)
        acc = jnp.dot(x_ref[rows, :], w_ref[...], preferred_element_type=F32)
        f = lb + (1.0 - lb) * jax.nn.sigmoid(acc)
        real = (i * tm + r + lax.broadcasted_iota(jnp.int32, (INPROJ_C_SUB, 1), 0)) >= META_ROW0
        _store_heads(k_ref, rows, jnp.where(real, 1.0 - f, 0.0))
        _store_heads(g_ref, rows, jnp.where(real, jnp.log(f), 0.0))


def _inproj_c(hb, w, lb):
    lp, d = hb.shape
    tm = _pick(lp, INPROJ_C_TILES)
    tn = 1024
    tps = d // tn
    hpt = tn // HG_DK
    x_spec = pl.BlockSpec((tm, d), lambda i, j: (i, 0))
    out_spec = pl.BlockSpec((1, hpt, tm, HG_DK), lambda i, j: (j // tps, j % tps, i, 0))
    hm = lambda n, dt: jax.ShapeDtypeStruct((n, HG_HEADS, lp, HG_DK), dt)
    plain_col = lambda i, j: (0, jnp.where(j >= 2 * tps, j + 2 * tps, j))
    plain = pl.pallas_call(
        _inproj_plain_kernel,
        grid=(lp // tm, 3 * tps),
        in_specs=[x_spec, pl.BlockSpec((d, tn), plain_col)],
        out_specs=out_spec,
        out_shape=hm(3, BF16),
        compiler_params=_cparams("parallel", "arbitrary"),
        name="inproj_c_plain",
    )(hb, w)
    k, g = pl.pallas_call(
        _inproj_forget_kernel,
        grid=(lp // tm, 2 * tps),
        in_specs=[x_spec, pl.BlockSpec((d, tn), lambda i, j: (0, j + 2 * tps)),
                  pl.BlockSpec((1, tn), lambda i, j: (0, j % tps))],
        out_specs=[out_spec, out_spec],
        out_shape=[hm(2, BF16), hm(2, F32)],
        compiler_params=_cparams("parallel", "arbitrary"),
        name="inproj_c_forget",
    )(hb, w, lb)
    return plain, k, g


def _hgrn_chunk(q, k, v, g, state_t, reverse):
    c = HG_CHUNK
    row = lax.broadcasted_iota(jnp.int32, (c, HG_DK), 0)
    b = g
    sh = 1
    while sh < c:
        if not reverse:
            b = b + jnp.where(row >= sh, pltpu.roll(b, sh, axis=0), 0.0)
        else:
            b = b + jnp.where(row < c - sh, pltpu.roll(b, c - sh, axis=0), 0.0)
        sh *= 2
    r_i = lax.broadcasted_iota(jnp.int32, (c, c), 0)
    c_i = lax.broadcasted_iota(jnp.int32, (c, c), 1)
    causal = (c_i >= r_i) if reverse else (c_i <= r_i)
    sub = HG_SUB
    mid = sub // 2
    blocks = lambda x: [x[j * sub:(j + 1) * sub] for j in range(HG_NSUB)]
    ref = [b[j * sub + mid:j * sub + mid + 1, :] for j in range(HG_NSUB)]
    b_end = b[0:1, :] if reverse else b[c - 1:c, :]
    qd, ku, q0, k_end = [], [], [], []
    for j, (qj, kj, bj) in enumerate(zip(blocks(q), blocks(k), blocks(b))):
        dj = bj - ref[j]
        qd_j = qj * jnp.exp(jnp.minimum(dj, HG_EXP_CLAMP))
        ku_j = kj * jnp.exp(jnp.minimum(-dj, HG_EXP_CLAMP))
        qd.append(qd_j.astype(BF16))
        ku.append(ku_j)
        q0.append(qd_j * jnp.exp(ref[j]))
        k_end.append(ku_j * jnp.exp(b_end - ref[j]))
    nt = (((1,), (1,)), ((), ()))
    zero = jnp.zeros((sub, HG_DK), BF16)
    score_rows = []
    for j in range(HG_NSUB):
        parts = []
        for i in range(HG_NSUB):
            if i == j:
                parts.append(ku[i].astype(BF16))
            elif (i > j) if reverse else (i < j):
                parts.append((ku[i] * jnp.exp(ref[j] - ref[i])).astype(BF16))
            else:
                parts.append(zero)
        k_ext = jnp.concatenate(parts, axis=0)
        score_rows.append(lax.dot_general(qd[j], k_ext, nt, preferred_element_type=F32))
    scores = jnp.concatenate(score_rows, axis=0)
    scores = jnp.where(causal, scores, 0.0)
    o = jnp.dot(scores.astype(BF16), v.astype(BF16), preferred_element_type=F32)
    o += lax.dot_general(jnp.concatenate(q0, axis=0).astype(BF16), state_t.astype(BF16), nt,
                         preferred_element_type=F32)
    new_state = jnp.exp(b_end) * state_t + jnp.dot(
        v.T.astype(BF16), jnp.concatenate(k_end, axis=0).astype(BF16), preferred_element_type=F32)
    return o, new_state


def _hgrn_kernel(qf_ref, vf_ref, k1_ref, g1_ref, qb_ref, vb_ref, k2_ref, g2_ref,
                 ofw_ref, obw_ref, st_ref):
    @pl.when(pl.program_id(0) == 0)
    def _():
        st_ref[...] = jnp.zeros_like(st_ref)

    def head(hh, carry):
        o, s = _hgrn_chunk(qf_ref[0, hh].astype(F32), k1_ref[0, hh].astype(F32),
                           vf_ref[0, hh].astype(F32), g1_ref[0, hh], st_ref[0, hh], reverse=False)
        ofw_ref[hh] = o
        st_ref[0, hh] = s
        o, s = _hgrn_chunk(qb_ref[0, hh].astype(F32), k2_ref[0, hh].astype(F32),
                           vb_ref[0, hh].astype(F32), g2_ref[0, hh], st_ref[1, hh], reverse=True)
        obw_ref[hh] = o
        st_ref[1, hh] = s
        return carry

    lax.fori_loop(0, HG_HEADS, head, 0, unroll=4)


def _hgrn(plain, k, g):
    _, _, lp, dk = plain.shape
    c = HG_CHUNK
    nc = lp // c
    fwd = lambda seg: pl.BlockSpec((1, HG_HEADS, c, dk), lambda i: (seg, 0, i, 0))
    bwd = lambda seg: pl.BlockSpec((1, HG_HEADS, c, dk), lambda i: (seg, 0, nc - 1 - i, 0))
    out_f = pl.BlockSpec((HG_HEADS, c, dk), lambda i: (0, i, 0))
    out_b = pl.BlockSpec((HG_HEADS, c, dk), lambda i: (0, nc - 1 - i, 0))
    return pl.pallas_call(
        _hgrn_kernel,
        grid=(nc,),
        in_specs=[fwd(0), fwd(1), fwd(0), fwd(0), bwd(0), bwd(1), bwd(1), bwd(1)],
        out_specs=[out_f, out_b],
        out_shape=[jax.ShapeDtypeStruct((HG_HEADS, lp, dk), F32)] * 2,
        scratch_shapes=[pltpu.VMEM((2, HG_HEADS, dk, dk), F32)],
        compiler_params=_cparams("arbitrary"),
        name="hgrn2_recurrence",
    )(plain, plain, k, g, plain, plain, k, g)


def _route_params(w_group, b_group, w_expert, b_expert):
    d = w_group.shape[0]
    pad = ROUTE_LANES - N_GROUPS - N_EXPERTS
    w = jnp.concatenate([w_group, w_expert, jnp.zeros((d, pad), F32)], axis=1)
    b = jnp.concatenate([b_group, b_expert, jnp.zeros((pad,), F32)])[None, :]
    w_hi = w.astype(BF16)
    w_lo = (w - w_hi.astype(F32)).astype(BF16)
    return jnp.concatenate([w_hi, w_lo], axis=1), b


def _rope_tables(lp):
    half = HEAD_DIM // 2
    pos = jnp.maximum(jnp.arange(lp) - META_ROW0, 0).astype(F32)
    inv = ROPE_THETA ** (-jnp.arange(half, dtype=F32) * 2.0 / HEAD_DIM)
    ang = pos[:, None] * inv[None, :]
    cos, sin = jnp.cos(ang), jnp.sin(ang)
    return jnp.concatenate([cos, cos], axis=1), jnp.concatenate([-sin, sin], axis=1)


def kernel(x, meta_tokens, w_in_ab, w_out_ab, attn_sinks, s5_lam_re, s5_lam_im, s5_log_step, s5_b_re, s5_b_im, s5_c_re, s5_c_im, s5_d, s5_w_glu, s5_b_glu, w_in_c, w_out_c, hgrn_lb_logits, hgrn_norm_g, ln_mix_g, ln_mix_b, ln_ffn_g, ln_ffn_b, moe_w_group, moe_b_group, moe_w_expert, moe_b_expert, moe_w_gate_up, moe_w_down):
    batch, seq, d = x.shape
    assert batch == 1 and seq % FRONT == 0
    depth = ln_mix_g.shape[0]
    assert depth == 2
    alpha = (2.0 * depth) ** 0.25
    lp = FRONT + seq
    row2 = lambda t: t[None, :]

    front = jnp.concatenate([jnp.zeros((META_ROW0, d), F32), meta_tokens.astype(F32)], axis=0)
    xs = x[0]

    w_in = w_in_ab[0].astype(BF16)
    s5w = s5_d.shape[1]
    wq, wk, wv, wu = (w_in[:, :Q_DIM], w_in[:, Q_DIM:Q_DIM + KV_DIM],
                      w_in[:, Q_DIM + KV_DIM:Q_DIM + 2 * KV_DIM], w_in[:, Q_DIM + 2 * KV_DIM:])
    cos2, sin2 = _rope_tables(lp)
    q, k, v, u = _inproj_ab(front, xs, wq, wk, wv, wu, cos2, sin2)
    a_out = _window_attention(q, k, v, attn_sinks[0])
    mats = _s5_discretise(s5_lam_re[0], s5_lam_im[0], s5_log_step[0], s5_b_re[0], s5_b_im[0],
                          s5_c_re[0], s5_c_im[0])
    y_lo, y_hi = _s5_mixer_pre_glu(u, s5_d[0], mats)
    wo = w_out_ab[0].astype(BF16)
    w_route, b_route = _route_params(moe_w_group[0], moe_b_group[0], moe_w_expert[0], moe_b_expert[0])
    h, h_slabs, route = _mix_ab(front, xs, a_out, y_lo, y_hi, s5_w_glu[0].astype(BF16), row2(s5_b_glu[0]),
                       wo[:Q_DIM], wo[Q_DIM:], row2(ln_mix_g[0]), row2(ln_mix_b[0]), w_route, b_route,
                       alpha)
    plan = _moe_plan(route, META_ROW0)
    y_buf = _moe_experts(h_slabs, plan, moe_w_gate_up, moe_w_down, 0)
    h, hb = _moe_combine(h, plan, y_buf, row2(ln_ffn_g[0]), row2(ln_ffn_b[0]), alpha, 0, with_bf16=True)

    lb_probs = jax.nn.softmax(hgrn_lb_logits.astype(F32), axis=0)
    lb = (jnp.cumsum(lb_probs, axis=0) - lb_probs[0])[1]
    plain, hk, hg = _inproj_c(hb, w_in_c[0].astype(BF16), row2(lb))
    o_fw, o_bw = _hgrn(plain, hk, hg)
    w_route, b_route = _route_params(moe_w_group[1], moe_b_group[1], moe_w_expert[1], moe_b_expert[1])
    h, h_slabs, route = _mix_c(h, o_fw, o_bw, plain, hgrn_norm_g[0].reshape(HG_HEADS, 1, HG_DK),
                      w_out_c[0].astype(BF16), row2(ln_mix_g[1]), row2(ln_mix_b[1]), w_route, b_route,
                      alpha)
    plan = _moe_plan(route, META_ROW0)
    y_buf = _moe_experts(h_slabs, plan, moe_w_gate_up, moe_w_down, 1)
    out = _moe_combine(h, plan, y_buf, row2(ln_ffn_g[1]), row2(ln_ffn_b[1]), alpha, FRONT)
    return out[None]
```
